```python
import jax, jax.numpy as jnp
from jax import lax
import numpy as np

D_MODEL = 2048
BATCH = 8
SEQ = 8192
DEPTH = 4

GRID_W = 64
CTX_LEN = 256
N_MIXERS = 3
N_A = (DEPTH + 2) // 3
N_B = (DEPTH + 1) // 3
N_C = DEPTH // 3
CHUNK = 128
GMLP_WIDTH = D_MODEL
GMLP_GROUP_DIM = 128
GMLP_GROUPS = GMLP_WIDTH // GMLP_GROUP_DIM
NA_HEAD_DIM = 128
NA_HEADS = D_MODEL // NA_HEAD_DIM
NA_MAX_ROWS = 8
NA_COLS = 16
CONV_WIDTH = 31
FFN_HIDDEN = (((8 * D_MODEL + 2) // 3 + 255) // 256) * 256
EPS = 1e-6
NEG_INF = -1e30

kernel_name = "hybrid_interleaved_dit_backbone"


def rmsnorm(x, g):
    xf = x.astype(jnp.float32)
    y = xf * lax.rsqrt(jnp.mean(xf * xf, axis=-1, keepdims=True) + EPS)
    return (y * g.astype(jnp.float32)).astype(x.dtype)


def layernorm(x, g, b):
    xf = x.astype(jnp.float32)
    mu = jnp.mean(xf, axis=-1, keepdims=True)
    xc = xf - mu
    var = jnp.mean(xc * xc, axis=-1, keepdims=True)
    return (xc * lax.rsqrt(var + EPS) * g.astype(jnp.float32) + b.astype(jnp.float32)).astype(x.dtype)


def modulate(h, shift, scale):
    return h * (1 + scale) + shift


def swiglu(h, w1, w3, w2):
    return (jax.nn.silu(h @ w1) * (h @ w3)) @ w2


def gmlp_chunk_mixer(h, w_in, ln_g, ln_b, w_s, b_s, w_out):
    bsz, L, _ = h.shape
    z = jax.nn.gelu(h @ w_in)
    u, v = jnp.split(z, 2, axis=-1)
    v = layernorm(v, ln_g, ln_b)
    v = v.reshape(bsz, L // CHUNK, CHUNK, GMLP_GROUPS, GMLP_GROUP_DIM)
    v = jnp.einsum('gpq,bnqgc->bnpgc', w_s, v) + b_s.T[:, :, None]
    v = v.reshape(bsz, L, GMLP_WIDTH)
    return (u * v) @ w_out


def _heads(t):
    return t.reshape(t.shape[0], t.shape[1], NA_HEADS, NA_HEAD_DIM).transpose(0, 2, 1, 3)


def neighbourhood_attention(h_lat, h_ctx, w_qkv, rpb, w_out, need_ctx_out):
    bsz, L, _ = h_lat.shape
    rows = L // GRID_W
    kr = min(NA_MAX_ROWS, rows)
    scale = NA_HEAD_DIM ** -0.5

    q, k, v = jnp.split(h_lat @ w_qkv, 3, axis=-1)
    qc, kc, vc = jnp.split(h_ctx @ w_qkv, 3, axis=-1)
    q, k, v = _heads(q), _heads(k), _heads(v)
    qc, kc, vc = _heads(qc), _heads(kc), _heads(vc)
    grid = (bsz, NA_HEADS, rows, GRID_W, NA_HEAD_DIM)
    qg, kg, vg = q.reshape(grid), k.reshape(grid), v.reshape(grid)

    cols = jnp.arange(GRID_W)
    c_start = jnp.clip(cols - NA_COLS // 2, 0, GRID_W - NA_COLS)
    col_in = (cols[None, :] >= c_start[:, None]) & (cols[None, :] < c_start[:, None] + NA_COLS)
    mask = jnp.broadcast_to(col_in[:, None, :], (GRID_W, kr, GRID_W)).reshape(GRID_W, kr * GRID_W)
    dc_idx = jnp.clip(cols[None, :] - cols[:, None] + NA_COLS - 1, 0, 2 * NA_COLS - 2)

    def row_block(r):
        r_start = jnp.clip(r - kr // 2, 0, rows - kr)
        q_r = lax.dynamic_index_in_dim(qg, r, axis=2, keepdims=False)
        k_r = lax.dynamic_slice_in_dim(kg, r_start, kr, axis=2).reshape(bsz, NA_HEADS, kr * GRID_W, NA_HEAD_DIM)
        v_r = lax.dynamic_slice_in_dim(vg, r_start, kr, axis=2).reshape(bsz, NA_HEADS, kr * GRID_W, NA_HEAD_DIM)
        dr_idx = r_start + jnp.arange(kr) - r + (NA_MAX_ROWS - 1)
        bias = rpb[:, dr_idx][:, :, dc_idx]
        bias = bias.transpose(0, 2, 1, 3).reshape(NA_HEADS, GRID_W, kr * GRID_W)
        s_lat = jnp.einsum('bhqd,bhkd->bhqk', q_r, k_r).astype(jnp.float32) * scale + bias.astype(jnp.float32)
        s_lat = jnp.where(mask, s_lat, NEG_INF)
        s_ctx = jnp.einsum('bhqd,bhkd->bhqk', q_r, kc).astype(jnp.float32) * scale
        p = jax.nn.softmax(jnp.concatenate([s_lat, s_ctx], axis=-1), axis=-1).astype(v.dtype)
        return (jnp.einsum('bhqk,bhkd->bhqd', p[..., :kr * GRID_W], v_r)
                + jnp.einsum('bhqk,bhkd->bhqd', p[..., kr * GRID_W:], vc))

    o = lax.map(row_block, jnp.arange(rows))
    o = o.transpose(1, 0, 3, 2, 4).reshape(bsz, L, D_MODEL)
    out_lat = o @ w_out
    if not need_ctx_out:
        return out_lat, None
    s_c = jnp.einsum('bhqd,bhkd->bhqk', qc, kc).astype(jnp.float32) * scale
    p_c = jax.nn.softmax(s_c, axis=-1).astype(vc.dtype)
    oc = jnp.einsum('bhqk,bhkd->bhqd', p_c, vc).transpose(0, 2, 1, 3).reshape(bsz, h_ctx.shape[1], D_MODEL)
    return out_lat, oc @ w_out


def conformer_conv(h, w_pw1, w_dw, b_dw, ln_g, ln_b, w_pw2):
    a, g = jnp.split(h @ w_pw1, 2, axis=-1)
    y = a * jax.nn.sigmoid(g)
    y = lax.conv_general_dilated(
        y, w_dw[:, None, :], window_strides=(1,),
        padding=[(CONV_WIDTH // 2, CONV_WIDTH // 2)],
        dimension_numbers=('NWC', 'WIO', 'NWC'),
        feature_group_count=D_MODEL) + b_dw
    y = jax.nn.silu(layernorm(y, ln_g, ln_b))
    return y @ w_pw2


def _fwd_setup_inputs(seed: int = 0) -> dict:
    key = jax.random.key(seed)
    keys = iter(jax.random.split(key, 40))
    D, F, E = D_MODEL, FFN_HIDDEN, GMLP_WIDTH

    def nrm(shape, scale):
        return jax.random.normal(next(keys), shape, jnp.float32) * scale

    def gain(shape):
        return 1.0 + nrm(shape, 0.02)

    return {
        "x": nrm((BATCH, SEQ, D), 1.0),
        "c": nrm((BATCH, D), 1.0),
        "ctx": nrm((BATCH, CTX_LEN, D), 1.0),
        "c_ctx": nrm((D,), 1.0),
        "ada_w": nrm((DEPTH, D, 6 * D), 0.5 * D ** -0.5),
        "ada_b": nrm((DEPTH, 6 * D), 0.01),
        "g_mix": gain((DEPTH, D)),
        "g_ffn": gain((DEPTH, D)),
        "ffn_w1": nrm((DEPTH, D, F), D ** -0.5),
        "ffn_w3": nrm((DEPTH, D, F), D ** -0.5),
        "ffn_w2": nrm((DEPTH, F, D), F ** -0.5),
        "a_w_in": nrm((N_A, D, 2 * E), D ** -0.5),
        "a_ln_g": gain((N_A, E)),
        "a_ln_b": nrm((N_A, E), 0.01),
        "a_w_s": nrm((N_A, GMLP_GROUPS, CHUNK, CHUNK), CHUNK ** -0.5),
        "a_b_s": gain((N_A, GMLP_GROUPS, CHUNK)),
        "a_w_out": nrm((N_A, E, D), E ** -0.5),
        "b_w_qkv": nrm((N_B, D, 3 * D), D ** -0.5),
        "b_rpb": nrm((N_B, NA_HEADS, 2 * NA_MAX_ROWS - 1, 2 * NA_COLS - 1), 0.1),
        "b_w_out": nrm((N_B, D, D), D ** -0.5),
        "c_w_pw1": nrm((N_C, D, 2 * D), D ** -0.5),
        "c_w_dw": nrm((N_C, CONV_WIDTH, D), CONV_WIDTH ** -0.5),
        "c_b_dw": nrm((N_C, D), 0.01),
        "c_ln_g": gain((N_C, D)),
        "c_ln_b": nrm((N_C, D), 0.01),
        "c_w_pw2": nrm((N_C, D, D), D ** -0.5),
        "g_final": gain((D,)),
    }


def _fwd_reference(x, c, ctx, c_ctx, ada_w, ada_b, g_mix, g_ffn, ffn_w1, ffn_w3, ffn_w2,
              a_w_in, a_ln_g, a_ln_b, a_w_s, a_b_s, a_w_out,
              b_w_qkv, b_rpb, b_w_out,
              c_w_pw1, c_w_dw, c_b_dw, c_ln_g, c_ln_b, c_w_pw2, g_final):
    h_lat, h_ctx = x, ctx
    s_lat = jax.nn.silu(c)
    s_ctx = jax.nn.silu(c_ctx)
    for i in range(DEPTH):
        last = i == DEPTH - 1
        mixer = i % N_MIXERS
        j = i // N_MIXERS
        mod_l = (s_lat @ ada_w[i] + ada_b[i])[:, None, :]
        sh1, sc1, gt1, sh2, sc2, gt2 = jnp.split(mod_l, 6, axis=-1)
        ctx_needed = (not last) or mixer == 1
        if ctx_needed:
            mod_c = s_ctx @ ada_w[i] + ada_b[i]
            csh1, csc1, cgt1, csh2, csc2, cgt2 = jnp.split(mod_c, 6, axis=-1)
            hc = modulate(rmsnorm(h_ctx, g_mix[i]), csh1, csc1)
        hl = modulate(rmsnorm(h_lat, g_mix[i]), sh1, sc1)

        if mixer == 0:
            ml = gmlp_chunk_mixer(hl, a_w_in[j], a_ln_g[j], a_ln_b[j], a_w_s[j], a_b_s[j], a_w_out[j])
            mc = (gmlp_chunk_mixer(hc, a_w_in[j], a_ln_g[j], a_ln_b[j], a_w_s[j], a_b_s[j], a_w_out[j])
                  if not last else None)
        elif mixer == 1:
            ml, mc = neighbourhood_attention(hl, hc, b_w_qkv[j], b_rpb[j], b_w_out[j], not last)
        else:
            ml = conformer_conv(hl, c_w_pw1[j], c_w_dw[j], c_b_dw[j], c_ln_g[j], c_ln_b[j], c_w_pw2[j])
            mc = (conformer_conv(hc, c_w_pw1[j], c_w_dw[j], c_b_dw[j], c_ln_g[j], c_ln_b[j], c_w_pw2[j])
                  if not last else None)

        h_lat = h_lat + gt1 * ml
        hl = modulate(rmsnorm(h_lat, g_ffn[i]), sh2, sc2)
        h_lat = h_lat + gt2 * swiglu(hl, ffn_w1[i], ffn_w3[i], ffn_w2[i])
        if not last:
            h_ctx = h_ctx + cgt1 * mc
            hc = modulate(rmsnorm(h_ctx, g_ffn[i]), csh2, csc2)
            h_ctx = h_ctx + cgt2 * swiglu(hc, ffn_w1[i], ffn_w3[i], ffn_w2[i])
    return rmsnorm(h_lat, g_final)


import jax as _jax
import jax.numpy as _jnp

TWIN_FORMAT = 'train_step'
FWD_PARAMS = ['x', 'c', 'ctx', 'c_ctx', 'ada_w', 'ada_b', 'g_mix', 'g_ffn', 'ffn_w1', 'ffn_w3', 'ffn_w2', 'a_w_in', 'a_ln_g', 'a_ln_b', 'a_w_s', 'a_b_s', 'a_w_out', 'b_w_qkv', 'b_rpb', 'b_w_out', 'c_w_pw1', 'c_w_dw', 'c_b_dw', 'c_ln_g', 'c_ln_b', 'c_w_pw2', 'g_final']
TWIN_WEIGHTS = ['c_ctx', 'ada_w', 'ada_b', 'g_mix', 'g_ffn', 'ffn_w1', 'ffn_w3', 'ffn_w2', 'a_w_in', 'a_ln_g', 'a_ln_b', 'a_w_s', 'a_b_s', 'a_w_out', 'b_w_qkv', 'b_rpb', 'b_w_out', 'c_w_pw1', 'c_w_dw', 'c_b_dw', 'c_ln_g', 'c_ln_b', 'c_w_pw2', 'g_final']
TWIN_DIFF_INPUT = 'x'
TWIN_INPUTS = ['x', 'c', 'ctx', 'c_ctx', 'ada_w', 'ada_b', 'g_mix', 'g_ffn', 'ffn_w1', 'ffn_w3', 'ffn_w2', 'a_w_in', 'a_ln_g', 'a_ln_b', 'a_w_s', 'a_b_s', 'a_w_out', 'b_w_qkv', 'b_rpb', 'b_w_out', 'c_w_pw1', 'c_w_dw', 'c_b_dw', 'c_ln_g', 'c_ln_b', 'c_w_pw2', 'g_final', 'loss_target', 'm_c_ctx', 'm_ada_w', 'm_ada_b', 'm_g_mix', 'm_g_ffn', 'm_ffn_w1', 'm_ffn_w3', 'm_ffn_w2', 'm_a_w_in', 'm_a_ln_g', 'm_a_ln_b', 'm_a_w_s', 'm_a_b_s', 'm_a_w_out', 'm_b_w_qkv', 'm_b_rpb', 'm_b_w_out', 'm_c_w_pw1', 'm_c_w_dw', 'm_c_b_dw', 'm_c_ln_g', 'm_c_ln_b', 'm_c_w_pw2', 'm_g_final', 'v_c_ctx', 'v_ada_w', 'v_ada_b', 'v_g_mix', 'v_g_ffn', 'v_ffn_w1', 'v_ffn_w3', 'v_ffn_w2', 'v_a_w_in', 'v_a_ln_g', 'v_a_ln_b', 'v_a_w_s', 'v_a_b_s', 'v_a_w_out', 'v_b_w_qkv', 'v_b_rpb', 'v_b_w_out', 'v_c_w_pw1', 'v_c_w_dw', 'v_c_b_dw', 'v_c_ln_g', 'v_c_ln_b', 'v_c_w_pw2', 'v_g_final']
TWIN_OUTPUTS = ['loss', 'grad_x', 'grad_c_ctx', 'grad_ada_w', 'grad_ada_b', 'grad_g_mix', 'grad_g_ffn', 'grad_ffn_w1', 'grad_ffn_w3', 'grad_ffn_w2', 'grad_a_w_in', 'grad_a_ln_g', 'grad_a_ln_b', 'grad_a_w_s', 'grad_a_b_s', 'grad_a_w_out', 'grad_b_w_qkv', 'grad_b_rpb', 'grad_b_w_out', 'grad_c_w_pw1', 'grad_c_w_dw', 'grad_c_b_dw', 'grad_c_ln_g', 'grad_c_ln_b', 'grad_c_w_pw2', 'grad_g_final', 'delta_c_ctx', 'delta_ada_w', 'delta_ada_b', 'delta_g_mix', 'delta_g_ffn', 'delta_ffn_w1', 'delta_ffn_w3', 'delta_ffn_w2', 'delta_a_w_in', 'delta_a_ln_g', 'delta_a_ln_b', 'delta_a_w_s', 'delta_a_b_s', 'delta_a_w_out', 'delta_b_w_qkv', 'delta_b_rpb', 'delta_b_w_out', 'delta_c_w_pw1', 'delta_c_w_dw', 'delta_c_b_dw', 'delta_c_ln_g', 'delta_c_ln_b', 'delta_c_w_pw2', 'delta_g_final', 'new_m_c_ctx', 'new_m_ada_w', 'new_m_ada_b', 'new_m_g_mix', 'new_m_g_ffn', 'new_m_ffn_w1', 'new_m_ffn_w3', 'new_m_ffn_w2', 'new_m_a_w_in', 'new_m_a_ln_g', 'new_m_a_ln_b', 'new_m_a_w_s', 'new_m_a_b_s', 'new_m_a_w_out', 'new_m_b_w_qkv', 'new_m_b_rpb', 'new_m_b_w_out', 'new_m_c_w_pw1', 'new_m_c_w_dw', 'new_m_c_b_dw', 'new_m_c_ln_g', 'new_m_c_ln_b', 'new_m_c_w_pw2', 'new_m_g_final', 'new_v_c_ctx', 'new_v_ada_w', 'new_v_ada_b', 'new_v_g_mix', 'new_v_g_ffn', 'new_v_ffn_w1', 'new_v_ffn_w3', 'new_v_ffn_w2', 'new_v_a_w_in', 'new_v_a_ln_g', 'new_v_a_ln_b', 'new_v_a_w_s', 'new_v_a_b_s', 'new_v_a_w_out', 'new_v_b_w_qkv', 'new_v_b_rpb', 'new_v_b_w_out', 'new_v_c_w_pw1', 'new_v_c_w_dw', 'new_v_c_b_dw', 'new_v_c_ln_g', 'new_v_c_ln_b', 'new_v_c_w_pw2', 'new_v_g_final']
TWIN_LEAF_KINDS = {'loss': 'loss', 'grad_x': 'grad_x', 'grad_c_ctx': 'grad_w', 'grad_ada_w': 'grad_w', 'grad_ada_b': 'grad_w', 'grad_g_mix': 'grad_w', 'grad_g_ffn': 'grad_w', 'grad_ffn_w1': 'grad_w', 'grad_ffn_w3': 'grad_w', 'grad_ffn_w2': 'grad_w', 'grad_a_w_in': 'grad_w', 'grad_a_ln_g': 'grad_w', 'grad_a_ln_b': 'grad_w', 'grad_a_w_s': 'grad_w', 'grad_a_b_s': 'grad_w', 'grad_a_w_out': 'grad_w', 'grad_b_w_qkv': 'grad_w', 'grad_b_rpb': 'grad_w', 'grad_b_w_out': 'grad_w', 'grad_c_w_pw1': 'grad_w', 'grad_c_w_dw': 'grad_w', 'grad_c_b_dw': 'grad_w', 'grad_c_ln_g': 'grad_w', 'grad_c_ln_b': 'grad_w', 'grad_c_w_pw2': 'grad_w', 'grad_g_final': 'grad_w', 'delta_c_ctx': 'delta_w', 'delta_ada_w': 'delta_w', 'delta_ada_b': 'delta_w', 'delta_g_mix': 'delta_w', 'delta_g_ffn': 'delta_w', 'delta_ffn_w1': 'delta_w', 'delta_ffn_w3': 'delta_w', 'delta_ffn_w2': 'delta_w', 'delta_a_w_in': 'delta_w', 'delta_a_ln_g': 'delta_w', 'delta_a_ln_b': 'delta_w', 'delta_a_w_s': 'delta_w', 'delta_a_b_s': 'delta_w', 'delta_a_w_out': 'delta_w', 'delta_b_w_qkv': 'delta_w', 'delta_b_rpb': 'delta_w', 'delta_b_w_out': 'delta_w', 'delta_c_w_pw1': 'delta_w', 'delta_c_w_dw': 'delta_w', 'delta_c_b_dw': 'delta_w', 'delta_c_ln_g': 'delta_w', 'delta_c_ln_b': 'delta_w', 'delta_c_w_pw2': 'delta_w', 'delta_g_final': 'delta_w', 'new_m_c_ctx': 'new_m', 'new_m_ada_w': 'new_m', 'new_m_ada_b': 'new_m', 'new_m_g_mix': 'new_m', 'new_m_g_ffn': 'new_m', 'new_m_ffn_w1': 'new_m', 'new_m_ffn_w3': 'new_m', 'new_m_ffn_w2': 'new_m', 'new_m_a_w_in': 'new_m', 'new_m_a_ln_g': 'new_m', 'new_m_a_ln_b': 'new_m', 'new_m_a_w_s': 'new_m', 'new_m_a_b_s': 'new_m', 'new_m_a_w_out': 'new_m', 'new_m_b_w_qkv': 'new_m', 'new_m_b_rpb': 'new_m', 'new_m_b_w_out': 'new_m', 'new_m_c_w_pw1': 'new_m', 'new_m_c_w_dw': 'new_m', 'new_m_c_b_dw': 'new_m', 'new_m_c_ln_g': 'new_m', 'new_m_c_ln_b': 'new_m', 'new_m_c_w_pw2': 'new_m', 'new_m_g_final': 'new_m', 'new_v_c_ctx': 'new_v', 'new_v_ada_w': 'new_v', 'new_v_ada_b': 'new_v', 'new_v_g_mix': 'new_v', 'new_v_g_ffn': 'new_v', 'new_v_ffn_w1': 'new_v', 'new_v_ffn_w3': 'new_v', 'new_v_ffn_w2': 'new_v', 'new_v_a_w_in': 'new_v', 'new_v_a_ln_g': 'new_v', 'new_v_a_ln_b': 'new_v', 'new_v_a_w_s': 'new_v', 'new_v_a_b_s': 'new_v', 'new_v_a_w_out': 'new_v', 'new_v_b_w_qkv': 'new_v', 'new_v_b_rpb': 'new_v', 'new_v_b_w_out': 'new_v', 'new_v_c_w_pw1': 'new_v', 'new_v_c_w_dw': 'new_v', 'new_v_c_b_dw': 'new_v', 'new_v_c_ln_g': 'new_v', 'new_v_c_ln_b': 'new_v', 'new_v_c_w_pw2': 'new_v', 'new_v_g_final': 'new_v'}


def _forward(args):
    return _fwd_reference(*[args[k] for k in FWD_PARAMS])


def _output_shape():
    def fwd():
        inp = _fwd_setup_inputs(0)
        return _fwd_reference(*[inp[k] for k in FWD_PARAMS])
    out = _jax.eval_shape(fwd)
    return out.shape, out.dtype

N_MICROBATCH = 1
ADAM_LR = 0.001
ADAM_B1 = 0.9
ADAM_B2 = 0.999
ADAM_EPS = 1e-08
ADAM_WD = 0.01
ADAM_STEP = 10
PER_EXAMPLE_BATCH_AXIS = {'x': 0, 'c': 0, 'ctx': 0, 'loss_target': 0}
SHARED_INPUTS = []
_WEIGHT_DTYPES = {'c_ctx': _jnp.float32, 'ada_w': _jnp.float32, 'ada_b': _jnp.float32, 'g_mix': _jnp.float32, 'g_ffn': _jnp.float32, 'ffn_w1': _jnp.float32, 'ffn_w3': _jnp.float32, 'ffn_w2': _jnp.float32, 'a_w_in': _jnp.float32, 'a_ln_g': _jnp.float32, 'a_ln_b': _jnp.float32, 'a_w_s': _jnp.float32, 'a_b_s': _jnp.float32, 'a_w_out': _jnp.float32, 'b_w_qkv': _jnp.float32, 'b_rpb': _jnp.float32, 'b_w_out': _jnp.float32, 'c_w_pw1': _jnp.float32, 'c_w_dw': _jnp.float32, 'c_b_dw': _jnp.float32, 'c_ln_g': _jnp.float32, 'c_ln_b': _jnp.float32, 'c_w_pw2': _jnp.float32, 'g_final': _jnp.float32}
MOMENT_SCALE = {'c_ctx': 1.004693e-02, 'ada_w': 3.675366e-02, 'ada_b': 6.399615e-02, 'g_mix': 3.470951e-02, 'g_ffn': 3.533605e-02, 'ffn_w1': 1.558890e-02, 'ffn_w3': 1.507785e-02, 'ffn_w2': 2.501868e-02, 'a_w_in': 3.382176e-02, 'a_ln_g': 2.555527e-02, 'a_ln_b': 2.595443e-02, 'a_w_s': 2.562935e-02, 'a_b_s': 2.565139e-02, 'a_w_out': 3.693625e-02, 'b_w_qkv': 7.302304e-03, 'b_rpb': 1.512203e-03, 'b_w_out': 1.071294e-02, 'c_w_pw1': 1.719050e-02, 'c_w_dw': 2.234646e-02, 'c_b_dw': 3.983448e-02, 'c_ln_g': 2.646591e-02, 'c_ln_b': 2.424986e-02, 'c_w_pw2': 2.204175e-02, 'g_final': 3.202709e+01}


def _to_microbatches(a, axis):
    t = _jnp.moveaxis(a, axis, 0)
    t = t.reshape((N_MICROBATCH, t.shape[0] // N_MICROBATCH) + t.shape[1:])
    return _jnp.moveaxis(t, 1, axis + 1)


def setup_inputs(seed: int = 0) -> dict:
    inp = _fwd_setup_inputs(seed)
    key = _jax.random.fold_in(_jax.random.key(seed), 7919)
    shape, _ = _output_shape()
    out = dict(inp)
    out["loss_target"] = _jax.random.normal(_jax.random.fold_in(key, 0), shape, _jnp.float32)
    for i, name in enumerate(TWIN_WEIGHTS):
        w = inp[name].astype(_jnp.float32)
        if MOMENT_SCALE is None:
            s = _jnp.sqrt(_jnp.mean(_jnp.square(w)) + 1e-30)
        else:
            s = MOMENT_SCALE[name]
        km, kv = _jax.random.split(_jax.random.fold_in(key, i + 1))
        out[name] = w
        out["m_" + name] = s * _jax.random.normal(km, w.shape, _jnp.float32)
        out["v_" + name] = (s * s) * _jax.random.uniform(kv, w.shape, _jnp.float32, 0.5, 1.5)
    if N_MICROBATCH > 1:
        for name, axis in PER_EXAMPLE_BATCH_AXIS.items():
            out[name] = _to_microbatches(out[name], axis)
    return {'x': out['x'], 'c': out['c'], 'ctx': out['ctx'], 'c_ctx': out['c_ctx'], 'ada_w': out['ada_w'], 'ada_b': out['ada_b'], 'g_mix': out['g_mix'], 'g_ffn': out['g_ffn'], 'ffn_w1': out['ffn_w1'], 'ffn_w3': out['ffn_w3'], 'ffn_w2': out['ffn_w2'], 'a_w_in': out['a_w_in'], 'a_ln_g': out['a_ln_g'], 'a_ln_b': out['a_ln_b'], 'a_w_s': out['a_w_s'], 'a_b_s': out['a_b_s'], 'a_w_out': out['a_w_out'], 'b_w_qkv': out['b_w_qkv'], 'b_rpb': out['b_rpb'], 'b_w_out': out['b_w_out'], 'c_w_pw1': out['c_w_pw1'], 'c_w_dw': out['c_w_dw'], 'c_b_dw': out['c_b_dw'], 'c_ln_g': out['c_ln_g'], 'c_ln_b': out['c_ln_b'], 'c_w_pw2': out['c_w_pw2'], 'g_final': out['g_final'], 'loss_target': out['loss_target'], 'm_c_ctx': out['m_c_ctx'], 'm_ada_w': out['m_ada_w'], 'm_ada_b': out['m_ada_b'], 'm_g_mix': out['m_g_mix'], 'm_g_ffn': out['m_g_ffn'], 'm_ffn_w1': out['m_ffn_w1'], 'm_ffn_w3': out['m_ffn_w3'], 'm_ffn_w2': out['m_ffn_w2'], 'm_a_w_in': out['m_a_w_in'], 'm_a_ln_g': out['m_a_ln_g'], 'm_a_ln_b': out['m_a_ln_b'], 'm_a_w_s': out['m_a_w_s'], 'm_a_b_s': out['m_a_b_s'], 'm_a_w_out': out['m_a_w_out'], 'm_b_w_qkv': out['m_b_w_qkv'], 'm_b_rpb': out['m_b_rpb'], 'm_b_w_out': out['m_b_w_out'], 'm_c_w_pw1': out['m_c_w_pw1'], 'm_c_w_dw': out['m_c_w_dw'], 'm_c_b_dw': out['m_c_b_dw'], 'm_c_ln_g': out['m_c_ln_g'], 'm_c_ln_b': out['m_c_ln_b'], 'm_c_w_pw2': out['m_c_w_pw2'], 'm_g_final': out['m_g_final'], 'v_c_ctx': out['v_c_ctx'], 'v_ada_w': out['v_ada_w'], 'v_ada_b': out['v_ada_b'], 'v_g_mix': out['v_g_mix'], 'v_g_ffn': out['v_g_ffn'], 'v_ffn_w1': out['v_ffn_w1'], 'v_ffn_w3': out['v_ffn_w3'], 'v_ffn_w2': out['v_ffn_w2'], 'v_a_w_in': out['v_a_w_in'], 'v_a_ln_g': out['v_a_ln_g'], 'v_a_ln_b': out['v_a_ln_b'], 'v_a_w_s': out['v_a_w_s'], 'v_a_b_s': out['v_a_b_s'], 'v_a_w_out': out['v_a_w_out'], 'v_b_w_qkv': out['v_b_w_qkv'], 'v_b_rpb': out['v_b_rpb'], 'v_b_w_out': out['v_b_w_out'], 'v_c_w_pw1': out['v_c_w_pw1'], 'v_c_w_dw': out['v_c_w_dw'], 'v_c_b_dw': out['v_c_b_dw'], 'v_c_ln_g': out['v_c_ln_g'], 'v_c_ln_b': out['v_c_ln_b'], 'v_c_w_pw2': out['v_c_w_pw2'], 'v_g_final': out['v_g_final']}


def _loss(weights, diff, rest, loss_target):
    with _jax.named_scope("forward"):
        args = {**rest, TWIN_DIFF_INPUT: diff, **{k: w.astype(_WEIGHT_DTYPES[k]) for k, w in weights.items()}}
        y = _forward(args)
    with _jax.named_scope("loss_head"):
        err = _jnp.square(y.astype(_jnp.float32) - loss_target)
        return 0.5 * _jnp.sum(_jnp.mean(err, axis=-1)) if err.ndim else 0.5 * err


def _adamw(w, g, m, v):
    m = ADAM_B1 * m + (1.0 - ADAM_B1) * g
    v = ADAM_B2 * v + (1.0 - ADAM_B2) * _jnp.square(g)
    m_hat = m / (1.0 - ADAM_B1 ** ADAM_STEP)
    v_hat = v / (1.0 - ADAM_B2 ** ADAM_STEP)
    delta = -ADAM_LR * (m_hat / (_jnp.sqrt(v_hat) + ADAM_EPS) + ADAM_WD * w)
    return delta, m, v


def reference(x, c, ctx, c_ctx, ada_w, ada_b, g_mix, g_ffn, ffn_w1, ffn_w3, ffn_w2, a_w_in, a_ln_g, a_ln_b, a_w_s, a_b_s, a_w_out, b_w_qkv, b_rpb, b_w_out, c_w_pw1, c_w_dw, c_b_dw, c_ln_g, c_ln_b, c_w_pw2, g_final, loss_target, m_c_ctx, m_ada_w, m_ada_b, m_g_mix, m_g_ffn, m_ffn_w1, m_ffn_w3, m_ffn_w2, m_a_w_in, m_a_ln_g, m_a_ln_b, m_a_w_s, m_a_b_s, m_a_w_out, m_b_w_qkv, m_b_rpb, m_b_w_out, m_c_w_pw1, m_c_w_dw, m_c_b_dw, m_c_ln_g, m_c_ln_b, m_c_w_pw2, m_g_final, v_c_ctx, v_ada_w, v_ada_b, v_g_mix, v_g_ffn, v_ffn_w1, v_ffn_w3, v_ffn_w2, v_a_w_in, v_a_ln_g, v_a_ln_b, v_a_w_s, v_a_b_s, v_a_w_out, v_b_w_qkv, v_b_rpb, v_b_w_out, v_c_w_pw1, v_c_w_dw, v_c_b_dw, v_c_ln_g, v_c_ln_b, v_c_w_pw2, v_g_final):
    given = dict(x=x, c=c, ctx=ctx, c_ctx=c_ctx, ada_w=ada_w, ada_b=ada_b, g_mix=g_mix, g_ffn=g_ffn, ffn_w1=ffn_w1, ffn_w3=ffn_w3, ffn_w2=ffn_w2, a_w_in=a_w_in, a_ln_g=a_ln_g, a_ln_b=a_ln_b, a_w_s=a_w_s, a_b_s=a_b_s, a_w_out=a_w_out, b_w_qkv=b_w_qkv, b_rpb=b_rpb, b_w_out=b_w_out, c_w_pw1=c_w_pw1, c_w_dw=c_w_dw, c_b_dw=c_b_dw, c_ln_g=c_ln_g, c_ln_b=c_ln_b, c_w_pw2=c_w_pw2, g_final=g_final, loss_target=loss_target, m_c_ctx=m_c_ctx, m_ada_w=m_ada_w, m_ada_b=m_ada_b, m_g_mix=m_g_mix, m_g_ffn=m_g_ffn, m_ffn_w1=m_ffn_w1, m_ffn_w3=m_ffn_w3, m_ffn_w2=m_ffn_w2, m_a_w_in=m_a_w_in, m_a_ln_g=m_a_ln_g, m_a_ln_b=m_a_ln_b, m_a_w_s=m_a_w_s, m_a_b_s=m_a_b_s, m_a_w_out=m_a_w_out, m_b_w_qkv=m_b_w_qkv, m_b_rpb=m_b_rpb, m_b_w_out=m_b_w_out, m_c_w_pw1=m_c_w_pw1, m_c_w_dw=m_c_w_dw, m_c_b_dw=m_c_b_dw, m_c_ln_g=m_c_ln_g, m_c_ln_b=m_c_ln_b, m_c_w_pw2=m_c_w_pw2, m_g_final=m_g_final, v_c_ctx=v_c_ctx, v_ada_w=v_ada_w, v_ada_b=v_ada_b, v_g_mix=v_g_mix, v_g_ffn=v_g_ffn, v_ffn_w1=v_ffn_w1, v_ffn_w3=v_ffn_w3, v_ffn_w2=v_ffn_w2, v_a_w_in=v_a_w_in, v_a_ln_g=v_a_ln_g, v_a_ln_b=v_a_ln_b, v_a_w_s=v_a_w_s, v_a_b_s=v_a_b_s, v_a_w_out=v_a_w_out, v_b_w_qkv=v_b_w_qkv, v_b_rpb=v_b_rpb, v_b_w_out=v_b_w_out, v_c_w_pw1=v_c_w_pw1, v_c_w_dw=v_c_w_dw, v_c_b_dw=v_c_b_dw, v_c_ln_g=v_c_ln_g, v_c_ln_b=v_c_ln_b, v_c_w_pw2=v_c_w_pw2, v_g_final=v_g_final)
    weights = {n: given[n] for n in TWIN_WEIGHTS}
    shared = {n: given[n] for n in SHARED_INPUTS}
    per_example = {n: given[n] for n in ['x', 'c', 'ctx']}
    grad_fn = _jax.value_and_grad(_loss, argnums=(0, 1))

    def one_microbatch(ex, loss_target):
        ex = dict(ex)
        diff = ex.pop(TWIN_DIFF_INPUT)
        return grad_fn(weights, diff, {**shared, **ex}, loss_target)

    if N_MICROBATCH == 1:
        loss, (grad_w, grad_x) = one_microbatch(per_example, given["loss_target"])
    else:
        def body(carry, xs):
            loss_sum, grad_sum = carry
            l_k, (gw_k, gx_k) = one_microbatch(xs[0], xs[1])
            with _jax.named_scope("update"):
                return (loss_sum + l_k, _jax.tree.map(_jnp.add, grad_sum, gw_k)), gx_k

        init = (_jnp.zeros((), _jnp.float32), _jax.tree.map(_jnp.zeros_like, weights))
        (loss, grad_w), grad_x = _jax.lax.scan(body, init, (per_example, given["loss_target"]))
    with _jax.named_scope("update"):
        delta_w, new_m, new_v = {}, {}, {}
        for n in TWIN_WEIGHTS:
            delta_w[n], new_m[n], new_v[n] = _adamw(weights[n], grad_w[n], given["m_" + n], given["v_" + n])
    return (loss, grad_x, *[grad_w[n] for n in TWIN_WEIGHTS], *[delta_w[n] for n in TWIN_WEIGHTS],
            *[new_m[n] for n in TWIN_WEIGHTS], *[new_v[n] for n in TWIN_WEIGHTS])
```

```python
import functools
import math

import jax
import jax.numpy as jnp
from jax import lax
from jax.experimental import pallas as pl
from jax.experimental.pallas import tpu as pltpu

F32 = jnp.float32
BF16 = jnp.bfloat16
MESH = pl.DeviceIdType.MESH
ANY = pl.BlockSpec(memory_space=pl.ANY)

GRID_W = 64
CHUNK = 128
HEAD = 128
NA_ROWS = 8
NA_COLS = 16
CONV_W = 31
HALO = 16
EPS = 1e-6
NEG_INF = -1e30
N_LAYERS = 4
N_SHARD = 4
V7X_VMEM_BYTES = 64 * 1024 * 1024
VMEM_LIMIT = V7X_VMEM_BYTES - 6 * 1024 * 1024

ADAM_LR = 0.001
ADAM_B1 = 0.9
ADAM_B2 = 0.999
ADAM_EPS = 1e-08
ADAM_WD = 0.01
ADAM_STEP = 10

NN = (((1,), (0,)), ((), ()))
NT = (((1,), (1,)), ((), ()))
TN = (((0,), (0,)), ((), ()))


def _params(n_grid=0):
    sem = ("arbitrary",) * n_grid if n_grid else None
    return pltpu.CompilerParams(dimension_semantics=sem, vmem_limit_bytes=VMEM_LIMIT)


def _xyc():
    return lax.axis_index("x"), lax.axis_index("y"), lax.axis_index("c")


def _flip(v, f):
    return 1 - v if f else v


def _tile(n, pref, mult=128):
    if n <= pref:
        return n
    t = (pref // mult) * mult
    while t > mult and n % t:
        t -= mult
    assert n % t == 0, (n, pref, mult)
    return t


def _sigmoid(x):
    return 1.0 / (1.0 + jnp.exp(-x))


XY_FLIPS = ((1, 0), (0, 1), (1, 1))
ALL_FLIPS = tuple((fx, fy, fc) for fx in (0, 1) for fy in (0, 1) for fc in (0, 1) if fx or fy or fc)


def _remote(src, dst, ssem, rsem, dev):
    return pltpu.make_async_remote_copy(src_ref=src, dst_ref=dst, send_sem=ssem, recv_sem=rsem,
                                        device_id=dev, device_id_type=MESH)


def all_gather_xy(name, shard):
    def body(src, dst, ssem, rsem, lsem):
        x, y, c = _xyc()
        mine = pltpu.make_async_copy(src, dst.at[2 * x + y], lsem)
        mine.start()
        sends = []
        for k, (fx, fy) in enumerate(XY_FLIPS):
            cp = _remote(src, dst.at[2 * x + y], ssem.at[k], rsem.at[k], (_flip(x, fx), _flip(y, fy), c))
            cp.start()
            sends.append(cp)
        for k, (fx, fy) in enumerate(XY_FLIPS):
            px, py = _flip(x, fx), _flip(y, fy)
            _remote(src, dst.at[2 * px + py], ssem.at[k], rsem.at[k], (px, py, c)).wait_recv()
        for cp in sends:
            cp.wait_send()
        mine.wait()

    return pl.pallas_call(
        body, name=name, out_shape=jax.ShapeDtypeStruct((N_SHARD,) + shard.shape, shard.dtype),
        in_specs=[ANY], out_specs=ANY,
        scratch_shapes=[pltpu.SemaphoreType.DMA((3,)), pltpu.SemaphoreType.DMA((3,)), pltpu.SemaphoreType.DMA(())],
    )(shard)


def all_gather_8(name, blk):
    def body(src, dst, ssem, rsem, lsem):
        x, y, c = _xyc()
        me = 4 * x + 2 * y + c
        mine = pltpu.make_async_copy(src, dst.at[me], lsem)
        mine.start()
        sends = []
        for k, (fx, fy, fc) in enumerate(ALL_FLIPS):
            cp = _remote(src, dst.at[me], ssem.at[k], rsem.at[k], (_flip(x, fx), _flip(y, fy), _flip(c, fc)))
            cp.start()
            sends.append(cp)
        for k, (fx, fy, fc) in enumerate(ALL_FLIPS):
            px, py, pc = _flip(x, fx), _flip(y, fy), _flip(c, fc)
            _remote(src, dst.at[4 * px + 2 * py + pc], ssem.at[k], rsem.at[k], (px, py, pc)).wait_recv()
        for cp in sends:
            cp.wait_send()
        mine.wait()

    return pl.pallas_call(
        body, name=name, out_shape=jax.ShapeDtypeStruct((8,) + blk.shape, blk.dtype),
        in_specs=[ANY], out_specs=ANY,
        scratch_shapes=[pltpu.SemaphoreType.DMA((7,)), pltpu.SemaphoreType.DMA((7,)), pltpu.SemaphoreType.DMA(())],
    )(blk)


def reduce_scatter_parts(name, full):
    def body(src, land, land2, ssem, rsem, lsem):
        x, y, c = _xyc()
        me_s = 2 * x + y
        sib = (x, y, 1 - c)
        loc = pltpu.make_async_copy(src.at[me_s], land.at[me_s], lsem)
        loc.start()
        sends = []
        own = _remote(src.at[me_s], land2.at[me_s], ssem.at[3], rsem.at[3], sib)
        own.start()
        sends.append(own)
        for k, (fx, fy) in enumerate(XY_FLIPS):
            px, py = _flip(x, fx), _flip(y, fy)
            cp = _remote(src.at[2 * px + py], land.at[me_s], ssem.at[k], rsem.at[k], (px, py, c))
            cp.start()
            sends.append(cp)
        for k, (fx, fy) in enumerate(XY_FLIPS):
            px, py = _flip(x, fx), _flip(y, fy)
            ps = 2 * px + py
            _remote(src.at[ps], land.at[ps], ssem.at[k], rsem.at[k], (px, py, c)).wait_recv()
            fw = _remote(land.at[ps], land2.at[ps], ssem.at[4 + k], rsem.at[4 + k], sib)
            fw.start()
            sends.append(fw)
        _remote(src.at[me_s], land2.at[me_s], ssem.at[3], rsem.at[3], sib).wait_recv()
        for k, (fx, fy) in enumerate(XY_FLIPS):
            ps = 2 * _flip(x, fx) + _flip(y, fy)
            _remote(land.at[ps], land2.at[ps], ssem.at[4 + k], rsem.at[4 + k], sib).wait_recv()
        for cp in sends:
            cp.wait_send()
        loc.wait()

    shp = jax.ShapeDtypeStruct(full.shape, full.dtype)
    return pl.pallas_call(
        body, name=name, out_shape=(shp, shp), in_specs=[ANY], out_specs=(ANY, ANY),
        scratch_shapes=[pltpu.SemaphoreType.DMA((7,)), pltpu.SemaphoreType.DMA((7,)), pltpu.SemaphoreType.DMA(())],
    )(full)


def matmul(name, grid, order, pairs, pair_dims, acc_of_pair, acc_shapes, extras, outs, epilogue,
           alias_dst=None):
    ni, nj, nk = grid

    def wrap(m):
        if order == "ij":
            return lambda g0, g1, k: m(g0, g1, k)
        return lambda g0, g1, k: m(g1, g0, k)

    g = (ni, nj, nk) if order == "ij" else (nj, ni, nk)
    arrays, in_specs = [], []
    for a, b in pairs:
        for arr, blk, m in (a, b):
            arrays.append(arr)
            in_specs.append(pl.BlockSpec(blk, wrap(m)))
    for arr, blk, m in extras:
        arrays.append(arr)
        in_specs.append(pl.BlockSpec(blk, wrap(m)))
    n_in = len(arrays)
    aliases = {}
    if alias_dst is not None:
        arrays.append(alias_dst[0])
        in_specs.append(ANY)
        aliases = {n_in: alias_dst[1]}
    out_shape = tuple(o[0] for o in outs)
    out_specs = tuple(pl.BlockSpec(o[1], wrap(o[2])) for o in outs)
    n_pairs, n_ex, n_out, n_acc = len(pairs), len(extras), len(outs), len(acc_shapes)
    n_alias = 1 if alias_dst is not None else 0

    def body(*refs):
        ins = refs[:n_in]
        out_refs = refs[n_in + n_alias:n_in + n_alias + n_out]
        accs = refs[n_in + n_alias + n_out:]
        pid = (pl.program_id(0), pl.program_id(1))
        k = pl.program_id(2)

        def partial(p):
            return lax.dot_general(ins[2 * p][...], ins[2 * p + 1][...], pair_dims[p],
                                   preferred_element_type=F32)

        if nk == 1:
            vals = [None] * n_acc
            for p in range(n_pairs):
                d = partial(p)
                q = acc_of_pair[p]
                vals[q] = d if vals[q] is None else vals[q] + d
            epilogue(vals, ins[2 * n_pairs:2 * n_pairs + n_ex], out_refs, pid)
        else:
            @pl.when(k == 0)
            def _():
                for acc in accs:
                    acc[...] = jnp.zeros_like(acc)

            for p in range(n_pairs):
                accs[acc_of_pair[p]][...] += partial(p)

            @pl.when(k == nk - 1)
            def _():
                epilogue([acc[...] for acc in accs], ins[2 * n_pairs:2 * n_pairs + n_ex], out_refs, pid)

    scratch = [] if nk == 1 else [pltpu.VMEM(s, F32) for s in acc_shapes]
    res = pl.pallas_call(
        body, name=name, grid=g, in_specs=in_specs, out_specs=out_specs, out_shape=out_shape,
        scratch_shapes=scratch, input_output_aliases=aliases, compiler_params=_params(3),
    )(*arrays)
    return res


def _store_cast(vals, extras, out_refs, pid):
    out_refs[0][...] = vals[0].astype(out_refs[0].dtype)


def mm_cols(name, x, wg, j, out_dtype, tm):
    T, D = x.shape
    ns = wg.shape[3]
    tn = _tile(ns, 1536)
    nb = ns // tn
    return matmul(
        name, (T // tm, N_SHARD * nb, 1), "ji",
        [((x, (tm, D), lambda i, jj, k: (i, 0)),
          (wg, (None, None, D, tn), lambda i, jj, k: (jj // nb, j, 0, jj % nb)))],
        [NN], [0], [(tm, tn)], [],
        [(jax.ShapeDtypeStruct((T, N_SHARD * ns), out_dtype), (tm, tn), lambda i, jj, k: (i, jj))],
        _store_cast)[0]


def mm_ffn_up(name, x, w1g, w3g, j, tm):
    T, D = x.shape
    ns = w1g.shape[3]
    tn = _tile(ns, 1536)
    nb = ns // tn

    def epi(vals, extras, out_refs, pid):
        a, b = vals
        out_refs[0][...] = a.astype(BF16)
        out_refs[1][...] = b.astype(BF16)
        out_refs[2][...] = (a * _sigmoid(a) * b).astype(BF16)

    wmap = lambda i, jj, k: (jj // nb, j, 0, jj % nb)
    xa = (x, (tm, D), lambda i, jj, k: (i, 0))
    o = (jax.ShapeDtypeStruct((T, N_SHARD * ns), BF16), (tm, tn), lambda i, jj, k: (i, jj))
    return matmul(name, (T // tm, N_SHARD * nb, 1), "ji",
                  [(xa, (w1g, (None, None, D, tn), wmap)), (xa, (w3g, (None, None, D, tn), wmap))],
                  [NN, NN], [0, 1], [(tm, tn), (tm, tn)], [], [o, o, o], epi)


def mm_rows_residual(name, p, wg, j, h, mod, gate_row, seq, tm):
    T, kin = p.shape
    ks, D = wg.shape[2], wg.shape[3]
    tk = _tile(ks, 512)
    kb = ks // tk

    def epi(vals, extras, out_refs, pid):
        m = vals[0]
        h_ref, mod_ref = extras
        rows = pid[0] * tm + lax.broadcasted_iota(jnp.int32, (tm, 1), 0)
        gate = jnp.where(rows >= seq, mod_ref[6 + gate_row:7 + gate_row, :], mod_ref[gate_row:gate_row + 1, :])
        out_refs[0][...] = m.astype(BF16)
        out_refs[1][...] = h_ref[...] + gate * m

    return matmul(
        name, (T // tm, 1, N_SHARD * kb), "ij",
        [((p, (tm, tk), lambda i, jj, k: (i, k)),
          (wg, (None, None, tk, D), lambda i, jj, k: (k // kb, j, k % kb, 0)))],
        [NN], [0], [(tm, D)],
        [(h, (tm, D), lambda i, jj, k: (i, 0)), (mod, (16, D), lambda i, jj, k: (0, 0))],
        [(jax.ShapeDtypeStruct((T, D), BF16), (tm, D), lambda i, jj, k: (i, 0)),
         (jax.ShapeDtypeStruct((T, D), F32), (tm, D), lambda i, jj, k: (i, 0))],
        epi)


def mm_rows_dgrad(name, dm, wg, j, out_dtype, tm, ffn_ab=None):
    T, D = dm.shape
    ks = wg.shape[2]
    tn = _tile(ks, 1536)
    nb = ks // tn
    omap = lambda i, jj, k: (i, jj)
    o = (jax.ShapeDtypeStruct((T, N_SHARD * ks), out_dtype), (tm, tn), omap)
    pairs = [((dm, (tm, D), lambda i, jj, k: (i, 0)),
              (wg, (None, None, tn, D), lambda i, jj, k: (jj // nb, j, jj % nb, 0)))]
    if ffn_ab is None:
        return matmul(name, (T // tm, N_SHARD * nb, 1), "ji", pairs, [NT], [0], [(tm, tn)], [], [o],
                      _store_cast)[0]

    def epi(vals, extras, out_refs, pid):
        dact = vals[0]
        a = extras[0][...].astype(F32)
        b = extras[1][...].astype(F32)
        sig = _sigmoid(a)
        out_refs[0][...] = (dact * b * (sig * (1.0 + a * (1.0 - sig)))).astype(BF16)
        out_refs[1][...] = (dact * (a * sig)).astype(BF16)

    a, b = ffn_ab
    return matmul(name, (T // tm, N_SHARD * nb, 1), "ji", pairs, [NT], [0], [(tm, tn)],
                  [(a, (tm, tn), omap), (b, (tm, tn), omap)], [o, o], epi)


def mm_cols_dgrad(name, dys, wgs, j, tm):
    T = dys[0].shape[0]
    D, ns = wgs[0].shape[2], wgs[0].shape[3]
    tk = _tile(ns, 1536)
    kb = ns // tk
    pairs = [((dy, (tm, tk), lambda i, jj, k: (i, k)),
              (wg, (None, None, D, tk), lambda i, jj, k: (k // kb, j, 0, k % kb))) for dy, wg in zip(dys, wgs)]
    return matmul(name, (T // tm, 1, N_SHARD * kb), "ij", pairs, [NT] * len(dys), [0] * len(dys), [(tm, D)], [],
                  [(jax.ShapeDtypeStruct((T, D), F32), (tm, D), lambda i, jj, k: (i, 0))], _store_cast)[0]


def mm_wgrad_rows(name, p, dm, dst, j, tk):
    T, kin = p.shape
    D = dm.shape[1]
    ks = dst.shape[2]
    tm = _tile(ks, 1536)
    mb = ks // tm
    tn = _tile(D, 1024)
    return matmul(
        name, (N_SHARD * mb, D // tn, T // tk), "ij",
        [((p, (tk, tm), lambda i, jj, k: (k, i)), (dm, (tk, tn), lambda i, jj, k: (k, jj)))],
        [TN], [0], [(tm, tn)], [],
        [(jax.ShapeDtypeStruct(dst.shape, BF16), (None, None, tm, tn), lambda i, jj, k: (i // mb, j, i % mb, jj))],
        _store_cast, alias_dst=(dst, 0))[0]


def mm_wgrad_cols(name, x, dy, dst, j, tk):
    T, D = x.shape
    ns = dst.shape[3]
    tm = _tile(D, 1024)
    tn = _tile(ns, 1536)
    nb = ns // tn
    return matmul(
        name, (D // tm, N_SHARD * nb, T // tk), "ij",
        [((x, (tk, tm), lambda i, jj, k: (k, i)), (dy, (tk, tn), lambda i, jj, k: (k, jj)))],
        [TN], [0], [(tm, tn)], [],
        [(jax.ShapeDtypeStruct(dst.shape, BF16), (None, None, tm, tn), lambda i, jj, k: (jj // nb, j, i, jj % nb))],
        _store_cast, alias_dst=(dst, 0))[0]


def _row_tile(T, seq):
    if T == seq:
        return _tile(T, 256, 8)
    return math.gcd(256, math.gcd(seq, T - seq))


def _mod_row(mod_ref, row, is_ctx):
    return jnp.where(is_ctx, mod_ref[6 + row:7 + row, :], mod_ref[row:row + 1, :])


def _rowspec(tr, D):
    return pl.BlockSpec((tr, D), lambda i: (i, 0))


def _fullspec(shape):
    nd = len(shape)
    return pl.BlockSpec(shape, lambda i: (0,) * nd)


def _colsum(v):
    return jnp.sum(v, axis=0, keepdims=True)


def _acc_rows(st_ref, first, rows):
    @pl.when(first)
    def _():
        st_ref[...] = jnp.zeros_like(st_ref)
    for r, val in rows:
        st_ref[r:r + 1, :] += val


def _split_stats(is_ctx, val):
    zero = jnp.zeros_like(val)
    return jnp.where(is_ctx, zero, val), jnp.where(is_ctx, val, zero)


def nm_fwd(name, h, g, mod, r_sh, r_sc, seq):
    T, D = h.shape
    tr = _row_tile(T, seq)
    nlat = seq // tr

    def body(h_ref, g_ref, mod_ref, o_ref):
        is_ctx = pl.program_id(0) >= nlat
        x = h_ref[...]
        r = lax.rsqrt(jnp.mean(x * x, axis=-1, keepdims=True) + EPS)
        y = x * r * g_ref[...]
        o_ref[...] = (y * (1.0 + _mod_row(mod_ref, r_sc, is_ctx)) + _mod_row(mod_ref, r_sh, is_ctx)).astype(BF16)

    return pl.pallas_call(
        body, name=name, grid=(T // tr,), in_specs=[_rowspec(tr, D), _fullspec((1, D)), _fullspec((16, D))],
        out_specs=_rowspec(tr, D), out_shape=jax.ShapeDtypeStruct((T, D), BF16), compiler_params=_params(1),
    )(h, g, mod)


def nm_bwd(name, h, dhm, dres, g, mod, r_sc, seq):
    T, D = h.shape
    tr = _row_tile(T, seq)
    nlat = seq // tr

    def body(h_ref, d_ref, r_ref, g_ref, mod_ref, o_ref, st_ref):
        i = pl.program_id(0)
        is_ctx = i >= nlat
        x = h_ref[...]
        r = lax.rsqrt(jnp.mean(x * x, axis=-1, keepdims=True) + EPS)
        n = x * r
        gg = g_ref[...]
        dout = d_ref[...]
        dsh = _colsum(dout)
        dsc = _colsum(dout * (n * gg))
        dy = dout * (1.0 + _mod_row(mod_ref, r_sc, is_ctx))
        dn = dy * gg
        o_ref[...] = r_ref[...] + r * (dn - n * jnp.mean(dn * n, axis=-1, keepdims=True))
        dsh_l, dsh_c = _split_stats(is_ctx, dsh)
        dsc_l, dsc_c = _split_stats(is_ctx, dsc)
        _acc_rows(st_ref, i == 0, [(0, _colsum(dy * n)), (1, dsh_l), (2, dsc_l), (3, dsh_c), (4, dsc_c),
                                   (5, dsh), (6, dsc)])

    return pl.pallas_call(
        body, name=name, grid=(T // tr,),
        in_specs=[_rowspec(tr, D), _rowspec(tr, D), _rowspec(tr, D), _fullspec((1, D)), _fullspec((16, D))],
        out_specs=(_rowspec(tr, D), _fullspec((8, D))),
        out_shape=(jax.ShapeDtypeStruct((T, D), F32), jax.ShapeDtypeStruct((8, D), F32)),
        compiler_params=_params(1),
    )(h, dhm, dres, g, mod)


def gate_bwd(name, dh, m, mod, r_gt, seq):
    T, D = dh.shape
    tr = _row_tile(T, seq)
    nlat = seq // tr

    def body(d_ref, m_ref, mod_ref, o_ref, st_ref):
        i = pl.program_id(0)
        is_ctx = i >= nlat
        d = d_ref[...]
        o_ref[...] = (d * _mod_row(mod_ref, r_gt, is_ctx)).astype(BF16)
        dgt = _colsum(d * m_ref[...].astype(F32))
        dgt_l, dgt_c = _split_stats(is_ctx, dgt)
        _acc_rows(st_ref, i == 0, [(0, dgt_l), (1, dgt_c), (2, dgt)])

    return pl.pallas_call(
        body, name=name, grid=(T // tr,),
        in_specs=[_rowspec(tr, D), _rowspec(tr, D), _fullspec((16, D))],
        out_specs=(_rowspec(tr, D), _fullspec((8, D))),
        out_shape=(jax.ShapeDtypeStruct((T, D), BF16), jax.ShapeDtypeStruct((8, D), F32)),
        compiler_params=_params(1),
    )(dh, m, mod)


def loss_head(name, h, target, g):
    T, D = h.shape
    tr = _tile(T, 256, 8)

    def body(h_ref, t_ref, g_ref, o_ref, st_ref):
        i = pl.program_id(0)
        x = h_ref[...]
        r = lax.rsqrt(jnp.mean(x * x, axis=-1, keepdims=True) + EPS)
        n = x * r
        gg = g_ref[...]
        err = n * gg - t_ref[...]
        dy = err * (1.0 / D)
        dn = dy * gg
        o_ref[...] = r * (dn - n * jnp.mean(dn * n, axis=-1, keepdims=True))
        _acc_rows(st_ref, i == 0, [(0, _colsum(dy * n)), (1, _colsum(err * err))])

    return pl.pallas_call(
        body, name=name, grid=(T // tr,),
        in_specs=[_rowspec(tr, D), _rowspec(tr, D), _fullspec((1, D))],
        out_specs=(_rowspec(tr, D), _fullspec((8, D))),
        out_shape=(jax.ShapeDtypeStruct((T, D), F32), jax.ShapeDtypeStruct((8, D), F32)),
        compiler_params=_params(1),
    )(h, target, g)


GELU_C = math.sqrt(2.0 / math.pi)


def _gelu(x):
    t = jnp.tanh(GELU_C * (x + 0.044715 * (x * x * x)))
    return 0.5 * x * (1.0 + t), t


def _gelu_grad(x, t):
    return 0.5 * (1.0 + t) + 0.5 * x * (1.0 - t * t) * (GELU_C * (1.0 + 3.0 * 0.044715 * (x * x)))


def _layernorm_fwd(v, g, b):
    mu = jnp.mean(v, axis=-1, keepdims=True)
    xc = v - mu
    rs = lax.rsqrt(jnp.mean(xc * xc, axis=-1, keepdims=True) + EPS)
    xhat = xc * rs
    return xhat * g + b, xhat, rs


def _layernorm_bwd(dout, xhat, rs, g):
    dxh = dout * g
    return rs * (dxh - jnp.mean(dxh, axis=-1, keepdims=True) - xhat * jnp.mean(dxh * xhat, axis=-1, keepdims=True))


def gmlp_fwd(name, zp, ln_g, ln_b, ws, bfull):
    T, E2 = zp.shape
    E = E2 // 2
    G = E // CHUNK
    tr = 2 * CHUNK if T % (2 * CHUNK) == 0 else CHUNK

    def body(zp_ref, g_ref, b_ref, ws_ref, bf_ref, p_ref):
        for ch in range(tr // CHUNK):
            rs_ = slice(ch * CHUNK, (ch + 1) * CHUNK)
            u, _ = _gelu(zp_ref[rs_, 0:E].astype(F32))
            v, _ = _gelu(zp_ref[rs_, E:E2].astype(F32))
            vn, _, _ = _layernorm_fwd(v, g_ref[...], b_ref[...])
            vnb = vn.astype(BF16)
            for gi in range(G):
                cs = slice(gi * CHUNK, (gi + 1) * CHUNK)
                vs = jnp.dot(ws_ref[gi], vnb[:, cs], preferred_element_type=F32) + bf_ref[:, cs]
                p_ref[rs_, cs] = (u[:, cs] * vs).astype(BF16)

    return pl.pallas_call(
        body, name=name, grid=(T // tr,),
        in_specs=[_rowspec(tr, E2), _fullspec((1, E)), _fullspec((1, E)), _fullspec((G, CHUNK, CHUNK)),
                  _fullspec((CHUNK, E))],
        out_specs=_rowspec(tr, E), out_shape=jax.ShapeDtypeStruct((T, E), BF16), compiler_params=_params(1),
    )(zp, ln_g, ln_b, ws, bfull)


def gmlp_bwd(name, zp, dp, ln_g, ln_b, ws, bfull):
    T, E2 = zp.shape
    E = E2 // 2
    G = E // CHUNK
    tr = 2 * CHUNK if T % (2 * CHUNK) == 0 else CHUNK
    n = T // tr

    def body(zp_ref, dp_ref, g_ref, b_ref, ws_ref, bf_ref, dz_ref, dws_ref, dbs_ref, st_ref, dvn_ref, dbf_ref):
        i = pl.program_id(0)

        @pl.when(i == 0)
        def _():
            dws_ref[...] = jnp.zeros_like(dws_ref)
            dbf_ref[...] = jnp.zeros_like(dbf_ref)

        dlg = jnp.zeros((1, E), F32)
        dlb = jnp.zeros((1, E), F32)
        for ch in range(tr // CHUNK):
            rs_ = slice(ch * CHUNK, (ch + 1) * CHUNK)
            zu = zp_ref[rs_, 0:E].astype(F32)
            zv = zp_ref[rs_, E:E2].astype(F32)
            u, tu = _gelu(zu)
            v, tv = _gelu(zv)
            vn, xhat, rs = _layernorm_fwd(v, g_ref[...], b_ref[...])
            vnb = vn.astype(BF16)
            dpf = dp_ref[rs_, :].astype(F32)
            for gi in range(G):
                cs = slice(gi * CHUNK, (gi + 1) * CHUNK)
                w = ws_ref[gi]
                vs = jnp.dot(w, vnb[:, cs], preferred_element_type=F32) + bf_ref[:, cs]
                dz_ref[rs_, cs] = (dpf[:, cs] * vs * _gelu_grad(zu[:, cs], tu[:, cs])).astype(BF16)
                dvs = dpf[:, cs] * u[:, cs]
                dvsb = dvs.astype(BF16)
                dws_ref[gi] += lax.dot_general(dvsb, vnb[:, cs], NT, preferred_element_type=F32)
                dbf_ref[:, cs] += dvs
                dvn_ref[:, cs] = lax.dot_general(w, dvsb, TN, preferred_element_type=F32)
            dvn = dvn_ref[...]
            dlg = dlg + _colsum(dvn * xhat)
            dlb = dlb + _colsum(dvn)
            dv = _layernorm_bwd(dvn, xhat, rs, g_ref[...])
            dz_ref[rs_, E:E2] = (dv * _gelu_grad(zv, tv)).astype(BF16)
        _acc_rows(st_ref, i == 0, [(0, dlg), (1, dlb)])

        @pl.when(i == n - 1)
        def _():
            for gi in range(G):
                dbs_ref[:, gi:gi + 1] = jnp.sum(dbf_ref[:, gi * CHUNK:(gi + 1) * CHUNK], axis=1, keepdims=True)

    return pl.pallas_call(
        body, name=name, grid=(n,),
        in_specs=[_rowspec(tr, E2), _rowspec(tr, E), _fullspec((1, E)), _fullspec((1, E)),
                  _fullspec((G, CHUNK, CHUNK)), _fullspec((CHUNK, E))],
        out_specs=(_rowspec(tr, E2), _fullspec((G, CHUNK, CHUNK)), _fullspec((CHUNK, G)), _fullspec((8, E))),
        out_shape=(jax.ShapeDtypeStruct((T, E2), BF16), jax.ShapeDtypeStruct((G, CHUNK, CHUNK), F32),
                   jax.ShapeDtypeStruct((CHUNK, G), F32), jax.ShapeDtypeStruct((8, E), F32)),
        scratch_shapes=[pltpu.VMEM((CHUNK, E), F32), pltpu.VMEM((CHUNK, E), F32)],
        compiler_params=_params(1),
    )(zp, dp, ln_g, ln_b, ws, bfull)


def _halo_specs(tr, T, width):
    per = tr // HALO
    last = T // HALO - 1
    prev = pl.BlockSpec((HALO, width), lambda i: (jnp.maximum(i * per - 1, 0), 0))
    nxt = pl.BlockSpec((HALO, width), lambda i: (jnp.minimum((i + 1) * per, last), 0))
    return prev, nxt


def _halo_valid(i, n, nlat):
    return jnp.logical_and(i > 0, i != nlat), jnp.logical_and(i + 1 < n, i + 1 != nlat)


def _glu(tt, D):
    a = tt[:, 0:D].astype(F32)
    g = tt[:, D:2 * D].astype(F32)
    return a * _sigmoid(g)


def conv_fwd(name, t, wdw, bdw, ln_g, ln_b, seq):
    T, D2 = t.shape
    D = D2 // 2
    tr = _row_tile(T, seq)
    n, nlat = T // tr, seq // tr
    prev, nxt = _halo_specs(tr, T, D2)

    def body(tp_ref, tc_ref, tn_ref, w_ref, b_ref, g_ref, bb_ref, y_ref, yc_ref, s_ref, buf):
        i = pl.program_id(0)
        pv, nv = _halo_valid(i, n, nlat)
        y = _glu(tc_ref[...], D)
        y_ref[...] = y
        buf[0:HALO, :] = jnp.where(pv, _glu(tp_ref[...], D), 0.0)
        buf[HALO:HALO + tr, :] = y
        buf[HALO + tr:2 * HALO + tr, :] = jnp.where(nv, _glu(tn_ref[...], D), 0.0)
        acc = jnp.zeros((tr, D), F32) + b_ref[...]
        for k in range(CONV_W):
            acc = acc + w_ref[k:k + 1, :] * buf[pl.ds(k + 1, tr), :]
        yc_ref[...] = acc
        yl, _, _ = _layernorm_fwd(acc, g_ref[...], bb_ref[...])
        s_ref[...] = (yl * _sigmoid(yl)).astype(BF16)

    return pl.pallas_call(
        body, name=name, grid=(n,),
        in_specs=[prev, _rowspec(tr, D2), nxt, _fullspec((32, D)), _fullspec((1, D)), _fullspec((1, D)),
                  _fullspec((1, D))],
        out_specs=(_rowspec(tr, D), _rowspec(tr, D), _rowspec(tr, D)),
        out_shape=(jax.ShapeDtypeStruct((T, D), F32), jax.ShapeDtypeStruct((T, D), F32),
                   jax.ShapeDtypeStruct((T, D), BF16)),
        scratch_shapes=[pltpu.VMEM((tr + 2 * HALO, D), F32)], compiler_params=_params(1),
    )(t, t, t, wdw, bdw, ln_g, ln_b)


def conv_bwd_norm(name, ds, yc, ln_g, ln_b):
    T, D = yc.shape
    tr = _tile(T, 256, 8)

    def body(ds_ref, yc_ref, g_ref, b_ref, o_ref, st_ref):
        i = pl.program_id(0)
        yl, xhat, rs = _layernorm_fwd(yc_ref[...], g_ref[...], b_ref[...])
        sig = _sigmoid(yl)
        dyl = ds_ref[...] * (sig * (1.0 + yl * (1.0 - sig)))
        dyc = _layernorm_bwd(dyl, xhat, rs, g_ref[...])
        o_ref[...] = dyc
        _acc_rows(st_ref, i == 0, [(0, _colsum(dyl * xhat)), (1, _colsum(dyl)), (2, _colsum(dyc))])

    return pl.pallas_call(
        body, name=name, grid=(T // tr,),
        in_specs=[_rowspec(tr, D), _rowspec(tr, D), _fullspec((1, D)), _fullspec((1, D))],
        out_specs=(_rowspec(tr, D), _fullspec((8, D))),
        out_shape=(jax.ShapeDtypeStruct((T, D), F32), jax.ShapeDtypeStruct((8, D), F32)),
        compiler_params=_params(1),
    )(ds, yc, ln_g, ln_b)


def conv_bwd_taps(name, dyc, y, t, wdw, seq):
    T, D = y.shape
    tr = _row_tile(T, seq)
    n, nlat = T // tr, seq // tr
    prev, nxt = _halo_specs(tr, T, D)

    def body(dp_ref, dc_ref, dn_ref, yp_ref, ycur_ref, yn_ref, t_ref, w_ref, dt_ref, dw_ref, dbuf, ybuf):
        i = pl.program_id(0)
        pv, nv = _halo_valid(i, n, nlat)
        dcur = dc_ref[...]
        dbuf[0:HALO, :] = jnp.where(pv, dp_ref[...], 0.0)
        dbuf[HALO:HALO + tr, :] = dcur
        dbuf[HALO + tr:2 * HALO + tr, :] = jnp.where(nv, dn_ref[...], 0.0)
        ybuf[0:HALO, :] = jnp.where(pv, yp_ref[...], 0.0)
        ybuf[HALO:HALO + tr, :] = ycur_ref[...]
        ybuf[HALO + tr:2 * HALO + tr, :] = jnp.where(nv, yn_ref[...], 0.0)

        @pl.when(i == 0)
        def _():
            dw_ref[...] = jnp.zeros_like(dw_ref)

        dy = jnp.zeros((tr, D), F32)
        for k in range(CONV_W):
            dw_ref[k:k + 1, :] += _colsum(dcur * ybuf[pl.ds(k + 1, tr), :])
            dy = dy + w_ref[k:k + 1, :] * dbuf[pl.ds(CONV_W - k, tr), :]
        a = t_ref[:, 0:D].astype(F32)
        sig = _sigmoid(t_ref[:, D:2 * D].astype(F32))
        dt_ref[:, 0:D] = (dy * sig).astype(BF16)
        dt_ref[:, D:2 * D] = (dy * a * sig * (1.0 - sig)).astype(BF16)

    return pl.pallas_call(
        body, name=name, grid=(n,),
        in_specs=[prev, _rowspec(tr, D), nxt, prev, _rowspec(tr, D), nxt, _rowspec(tr, 2 * D), _fullspec((32, D))],
        out_specs=(_rowspec(tr, 2 * D), _fullspec((32, D))),
        out_shape=(jax.ShapeDtypeStruct((T, 2 * D), BF16), jax.ShapeDtypeStruct((32, D), F32)),
        scratch_shapes=[pltpu.VMEM((tr + 2 * HALO, D), F32), pltpu.VMEM((tr + 2 * HALO, D), F32)],
        compiler_params=_params(1),
    )(dyc, dyc, dyc, y, y, y, t, wdw)


WIN = NA_ROWS * GRID_W


def _na_specs(seq, ctx_rows, H):
    cb = seq // ctx_rows
    return [
        pl.BlockSpec((GRID_W, HEAD), lambda h, r: (r, h)),
        pl.BlockSpec((seq, HEAD), lambda h, r: (0, H + h)),
        pl.BlockSpec((seq, HEAD), lambda h, r: (0, 2 * H + h)),
        pl.BlockSpec((ctx_rows, HEAD), lambda h, r: (cb, H + h)),
        pl.BlockSpec((ctx_rows, HEAD), lambda h, r: (cb, 2 * H + h)),
        pl.BlockSpec((None, NA_ROWS, GRID_W, WIN), lambda h, r: (h, 0, 0, 0)),
    ]


def _na_scores(q_ref, k_ref, v_ref, kc_ref, vc_ref, b_ref, rows):
    r = pl.program_id(1)
    r_start = jnp.clip(r - NA_ROWS // 2, 0, rows - NA_ROWS)
    d0 = r_start - r + NA_ROWS - 1
    start = pl.multiple_of(r_start * GRID_W, GRID_W)
    scale = HEAD ** -0.5
    q = q_ref[...]
    kw = k_ref[pl.ds(start, WIN), :]
    vw = v_ref[pl.ds(start, WIN), :]
    kc = kc_ref[...]
    vc = vc_ref[...]
    s = lax.dot_general(q, kw, NT, preferred_element_type=F32) * scale + b_ref[d0]
    sc = lax.dot_general(q, kc, NT, preferred_element_type=F32) * scale
    m = jnp.maximum(jnp.max(s, axis=-1, keepdims=True), jnp.max(sc, axis=-1, keepdims=True))
    p = jnp.exp(s - m)
    pc = jnp.exp(sc - m)
    l = jnp.sum(p, axis=-1, keepdims=True) + jnp.sum(pc, axis=-1, keepdims=True)
    return q, kw, vw, kc, vc, p, pc, l, d0, start, scale


def na_fwd(name, qkv, bias, seq):
    T, D3 = qkv.shape
    D = D3 // 3
    H = D // HEAD
    rows = seq // GRID_W

    def body(q_ref, k_ref, v_ref, kc_ref, vc_ref, b_ref, o_ref):
        q, kw, vw, kc, vc, p, pc, l, d0, start, scale = _na_scores(q_ref, k_ref, v_ref, kc_ref, vc_ref, b_ref, rows)
        o = (jnp.dot(p.astype(BF16), vw, preferred_element_type=F32)
             + jnp.dot(pc.astype(BF16), vc, preferred_element_type=F32))
        o_ref[...] = (o / l).astype(BF16)

    return pl.pallas_call(
        body, name=name, grid=(H, rows), in_specs=_na_specs(seq, T - seq, H),
        out_specs=pl.BlockSpec((GRID_W, HEAD), lambda h, r: (r, h)),
        out_shape=jax.ShapeDtypeStruct((seq, D), BF16), compiler_params=_params(2),
    )(qkv, qkv, qkv, qkv, qkv, bias)


def na_bwd(name, qkv, bias, do, seq):
    T, D3 = qkv.shape
    D = D3 // 3
    H = D // HEAD
    rows = seq // GRID_W
    ctx_rows = T - seq

    def body(q_ref, k_ref, v_ref, kc_ref, vc_ref, b_ref, do_ref, dq_ref, dk_ref, dv_ref, dkc_ref, dvc_ref, db_ref):
        @pl.when(pl.program_id(1) == 0)
        def _():
            for ref in (dk_ref, dv_ref, dkc_ref, dvc_ref, db_ref):
                ref[...] = jnp.zeros_like(ref)

        q, kw, vw, kc, vc, p, pc, l, d0, start, scale = _na_scores(q_ref, k_ref, v_ref, kc_ref, vc_ref, b_ref, rows)
        inv = 1.0 / l
        pn = p * inv
        pcn = pc * inv
        do_ = do_ref[...]
        dp = lax.dot_general(do_, vw, NT, preferred_element_type=F32)
        dpc = lax.dot_general(do_, vc, NT, preferred_element_type=F32)
        delta = jnp.sum(pn * dp, axis=-1, keepdims=True) + jnp.sum(pcn * dpc, axis=-1, keepdims=True)
        ds = pn * (dp - delta)
        dsc = pcn * (dpc - delta)
        db_ref[d0] += ds
        dsb = (ds * scale).astype(BF16)
        dscb = (dsc * scale).astype(BF16)
        dq = jnp.dot(dsb, kw, preferred_element_type=F32) + jnp.dot(dscb, kc, preferred_element_type=F32)
        dq_ref[...] = dq.astype(BF16)
        dk_ref[pl.ds(start, WIN), :] += lax.dot_general(dsb, q, TN, preferred_element_type=F32)
        dv_ref[pl.ds(start, WIN), :] += lax.dot_general(pn.astype(BF16), do_, TN, preferred_element_type=F32)
        dkc_ref[...] += lax.dot_general(dscb, q, TN, preferred_element_type=F32)
        dvc_ref[...] += lax.dot_general(pcn.astype(BF16), do_, TN, preferred_element_type=F32)

    head_lat = pl.BlockSpec((seq, HEAD), lambda h, r: (0, h))
    head_ctx = pl.BlockSpec((ctx_rows, HEAD), lambda h, r: (0, h))
    return pl.pallas_call(
        body, name=name, grid=(H, rows),
        in_specs=_na_specs(seq, ctx_rows, H) + [pl.BlockSpec((GRID_W, HEAD), lambda h, r: (r, h))],
        out_specs=(pl.BlockSpec((GRID_W, HEAD), lambda h, r: (r, h)), head_lat, head_lat, head_ctx, head_ctx,
                   pl.BlockSpec((None, NA_ROWS, GRID_W, WIN), lambda h, r: (h, 0, 0, 0))),
        out_shape=(jax.ShapeDtypeStruct((seq, D), BF16), jax.ShapeDtypeStruct((seq, D), F32),
                   jax.ShapeDtypeStruct((seq, D), F32), jax.ShapeDtypeStruct((ctx_rows, D), F32),
                   jax.ShapeDtypeStruct((ctx_rows, D), F32), jax.ShapeDtypeStruct(bias.shape, F32)),
        compiler_params=_params(2),
    )(qkv, qkv, qkv, qkv, qkv, bias, do)


def _ctx_specs(seq, ctx_rows, H):
    cb = seq // ctx_rows
    return [pl.BlockSpec((ctx_rows, HEAD), lambda h: (cb, h)),
            pl.BlockSpec((ctx_rows, HEAD), lambda h: (cb, H + h)),
            pl.BlockSpec((ctx_rows, HEAD), lambda h: (cb, 2 * H + h))]


def _ctx_probs(q_ref, k_ref):
    s = lax.dot_general(q_ref[...], k_ref[...], NT, preferred_element_type=F32) * (HEAD ** -0.5)
    p = jnp.exp(s - jnp.max(s, axis=-1, keepdims=True))
    return p / jnp.sum(p, axis=-1, keepdims=True)


def ctx_attn_fwd(name, qkv, seq):
    T, D3 = qkv.shape
    D = D3 // 3
    H = D // HEAD
    ctx_rows = T - seq

    def body(q_ref, k_ref, v_ref, o_ref):
        p = _ctx_probs(q_ref, k_ref)
        o_ref[...] = jnp.dot(p.astype(BF16), v_ref[...], preferred_element_type=F32).astype(BF16)

    return pl.pallas_call(
        body, name=name, grid=(H,), in_specs=_ctx_specs(seq, ctx_rows, H),
        out_specs=pl.BlockSpec((ctx_rows, HEAD), lambda h: (0, h)),
        out_shape=jax.ShapeDtypeStruct((ctx_rows, D), BF16), compiler_params=_params(1),
    )(qkv, qkv, qkv)


def ctx_attn_bwd(name, qkv, do, dkc_lat, dvc_lat, seq):
    T, D3 = qkv.shape
    D = D3 // 3
    H = D // HEAD
    ctx_rows = T - seq
    cb = seq // ctx_rows
    scale = HEAD ** -0.5

    def body(q_ref, k_ref, v_ref, do_ref, dkl_ref, dvl_ref, dq_ref, dk_ref, dv_ref):
        p = _ctx_probs(q_ref, k_ref)
        do_ = do_ref[...]
        dp = lax.dot_general(do_, v_ref[...], NT, preferred_element_type=F32)
        ds = p * (dp - jnp.sum(p * dp, axis=-1, keepdims=True))
        dsb = (ds * scale).astype(BF16)
        dq_ref[...] = jnp.dot(dsb, k_ref[...], preferred_element_type=F32).astype(BF16)
        dk_ref[...] = (dkl_ref[...] + lax.dot_general(dsb, q_ref[...], TN, preferred_element_type=F32)).astype(BF16)
        dv_ref[...] = (dvl_ref[...]
                       + lax.dot_general(p.astype(BF16), do_, TN, preferred_element_type=F32)).astype(BF16)

    blk = pl.BlockSpec((ctx_rows, HEAD), lambda h: (0, h))
    shp = jax.ShapeDtypeStruct((ctx_rows, D), BF16)
    return pl.pallas_call(
        body, name=name, grid=(H,),
        in_specs=_ctx_specs(seq, ctx_rows, H) + [pl.BlockSpec((ctx_rows, HEAD), lambda h: (cb, h)), blk, blk],
        out_specs=(blk, blk, blk), out_shape=(shp, shp, shp), compiler_params=_params(1),
    )(qkv, qkv, qkv, do, dkc_lat, dvc_lat)


def _rpb_tables():
    qc = jnp.arange(GRID_W)[:, None]
    kc = jnp.arange(GRID_W)[None, :]
    rel = (kc - qc + NA_COLS - 1).reshape(1, GRID_W * GRID_W)
    onehot = (rel == jnp.arange(32)[:, None]).astype(F32)
    c_start = jnp.clip(qc - NA_COLS // 2, 0, GRID_W - NA_COLS)
    mask = jnp.logical_and(kc >= c_start, kc < c_start + NA_COLS).astype(F32).reshape(1, GRID_W * GRID_W)
    return onehot, mask


def rpb_expand(name, rpb2, onehot, mask):
    R = rpb2.shape[0]

    def body(r_ref, oh_ref, m_ref, o_ref):
        t = jnp.dot(r_ref[...], oh_ref[...], preferred_element_type=F32, precision=lax.Precision.HIGHEST)
        o_ref[...] = jnp.where(m_ref[...] > 0.5, t, NEG_INF)

    return pl.pallas_call(body, name=name, out_shape=jax.ShapeDtypeStruct((R, GRID_W * GRID_W), F32),
                          compiler_params=_params())(rpb2, onehot, mask)


def rpb_fold(name, x):
    H = x.shape[0]
    n_dr = 2 * NA_ROWS - 1

    def body(x_ref, y_ref):
        for dr in range(n_dr):
            acc = None
            for d0 in range(NA_ROWS):
                jj = dr - d0
                if 0 <= jj < NA_ROWS:
                    acc = x_ref[d0, jj] if acc is None else acc + x_ref[d0, jj]
            y_ref[dr] = acc

    return pl.pallas_call(
        body, name=name, grid=(H,),
        in_specs=[pl.BlockSpec((None, NA_ROWS, NA_ROWS, GRID_W, GRID_W), lambda h: (h, 0, 0, 0, 0))],
        out_specs=pl.BlockSpec((None, n_dr, GRID_W, GRID_W), lambda h: (h, 0, 0, 0)),
        out_shape=jax.ShapeDtypeStruct((H, n_dr, GRID_W, GRID_W), F32), compiler_params=_params(1),
    )(x)


def rpb_reduce(name, y2, onehot_t):
    R = y2.shape[0]

    def body(y_ref, oh_ref, o_ref):
        o_ref[...] = jnp.dot(y_ref[...], oh_ref[...], preferred_element_type=F32, precision=lax.Precision.HIGHEST)

    return pl.pallas_call(body, name=name, out_shape=jax.ShapeDtypeStruct((R, 32), F32),
                          compiler_params=_params())(y2, onehot_t)


def _adam(g, w, m, v):
    m2 = ADAM_B1 * m + (1.0 - ADAM_B1) * g
    v2 = ADAM_B2 * v + (1.0 - ADAM_B2) * (g * g)
    m_hat = m2 / (1.0 - ADAM_B1 ** ADAM_STEP)
    v_hat = v2 / (1.0 - ADAM_B2 ** ADAM_STEP)
    delta = -ADAM_LR * (m_hat / (jnp.sqrt(v_hat) + ADAM_EPS) + ADAM_WD * w)
    return delta, m2, v2


def adam_parts(name, land, land2, w, m, v):
    shape = w.shape
    C = shape[-1]
    R = math.prod(shape[:-1])
    tr = _tile(R, max(16, (256 * 1024 // C) // 16 * 16), 16)
    l1, l2 = land.reshape(N_SHARD, R, C), land2.reshape(N_SHARD, R, C)

    def body(l1_ref, l2_ref, w_ref, m_ref, v_ref, g_ref, d_ref, m2_ref, v2_ref):
        a = l1_ref[0].astype(F32)
        b = l2_ref[0].astype(F32)
        for k in range(1, N_SHARD):
            a = a + l1_ref[k].astype(F32)
            b = b + l2_ref[k].astype(F32)
        g = a + b
        g_ref[...] = g
        d_ref[...], m2_ref[...], v2_ref[...] = _adam(g, w_ref[...], m_ref[...], v_ref[...])

    part = pl.BlockSpec((N_SHARD, tr, C), lambda i: (0, i, 0))
    row = _rowspec(tr, C)
    shp = jax.ShapeDtypeStruct((R, C), F32)
    outs = pl.pallas_call(
        body, name=name, grid=(R // tr,), in_specs=[part, part, row, row, row], out_specs=(row,) * 4,
        out_shape=(shp,) * 4, compiler_params=_params(1),
    )(l1, l2, w.reshape(R, C), m.reshape(R, C), v.reshape(R, C))
    return tuple(o.reshape(shape) for o in outs)


def adam_flat(name, g, w, m, v):
    R, C = g.shape
    tr = _tile(R, 256, 8)

    def body(g_ref, w_ref, m_ref, v_ref, d_ref, m2_ref, v2_ref):
        d_ref[...], m2_ref[...], v2_ref[...] = _adam(g_ref[...], w_ref[...], m_ref[...], v_ref[...])

    row = _rowspec(tr, C)
    shp = jax.ShapeDtypeStruct((R, C), F32)
    return pl.pallas_call(body, name=name, grid=(R // tr,), in_specs=[row] * 4, out_specs=(row,) * 3,
                          out_shape=(shp,) * 3, compiler_params=_params(1))(g, w, m, v)


def reduce_8(name, gathered):
    _, R, D = gathered.shape
    tr = _tile(R, 64, 8)

    def body(x_ref, o_ref):
        acc = x_ref[0]
        for k in range(1, 8):
            acc = acc + x_ref[k]
        o_ref[...] = acc

    return pl.pallas_call(
        body, name=name, grid=(R // tr,), in_specs=[pl.BlockSpec((8, tr, D), lambda i: (0, i, 0))],
        out_specs=_rowspec(tr, D), out_shape=jax.ShapeDtypeStruct((R, D), F32), compiler_params=_params(1),
    )(gathered)


def ada_fwd(name, craw16, ada_w, ada_b3):
    L, D, Cs = ada_w.shape
    tn = _tile(Cs, 512)

    def body(c_ref, w_ref, b_ref, o_ref):
        cc = c_ref[...]
        s = cc * _sigmoid(cc)
        o_ref[...] = jnp.dot(s, w_ref[...], preferred_element_type=F32,
                             precision=lax.Precision.HIGHEST) + b_ref[...]

    return pl.pallas_call(
        body, name=name, grid=(L, Cs // tn),
        in_specs=[pl.BlockSpec((16, D), lambda l, j: (0, 0)), pl.BlockSpec((None, D, tn), lambda l, j: (l, 0, j)),
                  pl.BlockSpec((None, 1, tn), lambda l, j: (l, 0, j))],
        out_specs=pl.BlockSpec((None, 16, tn), lambda l, j: (l, 0, j)),
        out_shape=jax.ShapeDtypeStruct((L, 16, Cs), F32), compiler_params=_params(2),
    )(craw16, ada_w, ada_b3)


def ada_bwd_adam(name, craw16_t, dm16, dmc8, w, m, v):
    L, D, Cs = w.shape
    tn = _tile(Cs, 256)
    n = L * (Cs // tn)

    def body(c_ref, dm_ref, dc_ref, w_ref, m_ref, v_ref, g_ref, d_ref, m2_ref, v2_ref, ds_ref):
        step = pl.program_id(0) * (Cs // tn) + pl.program_id(1)
        cc = c_ref[...]
        s_t = cc * _sigmoid(cc)
        g = jnp.dot(s_t, dm_ref[...], preferred_element_type=F32, precision=lax.Precision.HIGHEST)
        ww = w_ref[...]
        g_ref[...] = g
        d_ref[...], m2_ref[...], v2_ref[...] = _adam(g, ww, m_ref[...], v_ref[...])

        @pl.when(step == 0)
        def _():
            ds_ref[...] = jnp.zeros_like(ds_ref)

        ds_ref[...] += lax.dot_general(dc_ref[...], ww, NT, preferred_element_type=F32,
                                       precision=lax.Precision.HIGHEST)

    wspec = pl.BlockSpec((None, D, tn), lambda l, j: (l, 0, j))
    shp = jax.ShapeDtypeStruct((L, D, Cs), F32)
    del n
    return pl.pallas_call(
        body, name=name, grid=(L, Cs // tn),
        in_specs=[pl.BlockSpec((D, 16), lambda l, j: (0, 0)), pl.BlockSpec((None, 16, tn), lambda l, j: (l, 0, j)),
                  pl.BlockSpec((None, 8, tn), lambda l, j: (l, 0, j)), wspec, wspec, wspec],
        out_specs=(wspec, wspec, wspec, wspec, pl.BlockSpec((8, D), lambda l, j: (0, 0))),
        out_shape=(shp, shp, shp, shp, jax.ShapeDtypeStruct((8, D), F32)), compiler_params=_params(2),
    )(craw16_t, dm16, dmc8, w, m, v)


def cctx_adam(name, ds_all, c_ctx, m, v):
    D = c_ctx.shape[1]

    def body(ds_ref, c_ref, m_ref, v_ref, g_ref, d_ref, m2_ref, v2_ref):
        ds = ds_ref[0, 0:1, :]
        for slot in (2, 4, 6):
            ds = ds + ds_ref[slot, 0:1, :]
        cc = c_ref[...]
        sig = _sigmoid(cc)
        g = ds * (sig * (1.0 + cc * (1.0 - sig)))
        g_ref[...] = g
        d_ref[...], m2_ref[...], v2_ref[...] = _adam(g, cc, m_ref[...], v_ref[...])

    shp = jax.ShapeDtypeStruct((1, D), F32)
    return pl.pallas_call(body, name=name, out_shape=(shp,) * 4, compiler_params=_params())(ds_all, c_ctx, m, v)


WEIGHT_NAMES = ['c_ctx', 'ada_w', 'ada_b', 'g_mix', 'g_ffn', 'ffn_w1', 'ffn_w3', 'ffn_w2', 'a_w_in', 'a_ln_g',
                'a_ln_b', 'a_w_s', 'a_b_s', 'a_w_out', 'b_w_qkv', 'b_rpb', 'b_w_out', 'c_w_pw1', 'c_w_dw', 'c_b_dw',
                'c_ln_g', 'c_ln_b', 'c_w_pw2', 'g_final']
BIG_NAMES = ['ffn_w1', 'ffn_w3', 'ffn_w2', 'a_w_in', 'a_w_out', 'b_w_qkv', 'b_w_out', 'c_w_pw1', 'c_w_pw2']
SMALL_NAMES = ['ada_b', 'g_mix', 'g_ffn', 'a_ln_g', 'a_ln_b', 'a_w_s', 'a_b_s', 'b_rpb', 'c_w_dw', 'c_b_dw',
               'c_ln_g', 'c_ln_b', 'g_final']
SMALL_PACK_COLS = 512


def _pad_rows(a, mult):
    r = (-a.shape[0]) % mult
    return a if r == 0 else jnp.concatenate([a, jnp.zeros((r,) + a.shape[1:], a.dtype)], axis=0)


def _rows_of(flat, D):
    n = flat.shape[0]
    r = -(-n // D)
    return jnp.concatenate([flat, jnp.zeros((r * D - n,), flat.dtype)]).reshape(r, D)


def _step(W, Mo, Vo, x, c, ctx, loss_target):
    seq, D = x.shape[1], x.shape[2]
    ctx_rows = ctx.shape[1]
    T = seq + ctx_rows
    L = N_LAYERS
    H = D // HEAD
    G = D // CHUNK
    xi, yi, ci = _xyc()
    e_idx = 4 * xi + 2 * yi + ci
    s_idx = 2 * xi + yi

    c_all = all_gather_8("ag_c", c)
    craw16 = jnp.concatenate([c_all.reshape(8, D), W['c_ctx'].reshape(1, D), jnp.zeros((7, D), F32)], axis=0)
    ada_w = W['ada_w']
    Cs = ada_w.shape[2]
    ada_b_s = lax.dynamic_slice_in_dim(W['ada_b'], s_idx * Cs, Cs, axis=1).reshape(L, 1, Cs)
    mod_s = ada_fwd("ada_fwd", craw16, ada_w, ada_b_s)
    mod_g = all_gather_xy("ag_mod", mod_s).transpose(1, 2, 0, 3).reshape(L, 16, N_SHARD * Cs)
    mod_lat = lax.dynamic_index_in_dim(mod_g, e_idx, axis=1, keepdims=False).reshape(L, 6, D)
    mod_all = jnp.concatenate([mod_lat, mod_g[:, 8].reshape(L, 6, D), jnp.zeros((L, 4, D), F32)], axis=1)

    Wg = {n: all_gather_xy("ag_" + n, W[n].astype(BF16)) for n in BIG_NAMES}
    dW = {n: lax.empty(Wg[n].shape, BF16) for n in BIG_NAMES}

    n_a, n_c = W['a_ln_g'].shape[0], W['c_ln_g'].shape[0]
    sh_rows = jnp.concatenate([W['a_ln_g'], W['a_ln_b'], W['c_w_dw'].reshape(n_c * CONV_W, -1), W['c_b_dw'],
                               W['c_ln_g'], W['c_ln_b']], axis=0)
    n_sh = sh_rows.shape[0]
    sh_full = all_gather_xy("ag_small", _pad_rows(sh_rows, 8)).transpose(1, 0, 2).reshape(-1, D)[:n_sh]
    o = 0
    a_ln_g_f, o = sh_full[o:o + n_a], o + n_a
    a_ln_b_f, o = sh_full[o:o + n_a], o + n_a
    c_w_dw_f, o = sh_full[o:o + n_c * CONV_W].reshape(n_c, CONV_W, D), o + n_c * CONV_W
    c_b_dw_f, o = sh_full[o:o + n_c], o + n_c
    c_ln_g_f, o = sh_full[o:o + n_c], o + n_c
    c_ln_b_f, o = sh_full[o:o + n_c], o + n_c

    onehot, colmask = _rpb_tables()
    n_dr = 2 * NA_ROWS - 1
    rpb2 = jnp.pad(W['b_rpb'][0].reshape(H * n_dr, 2 * NA_COLS - 1), ((0, 0), (0, 1)))
    toep = rpb_expand("rpb_expand", rpb2, onehot, colmask).reshape(H, n_dr, GRID_W, GRID_W)
    bias = jnp.stack([toep[:, d0:d0 + NA_ROWS] for d0 in range(NA_ROWS)], axis=1)
    bias = bias.transpose(0, 1, 3, 2, 4).reshape(H, NA_ROWS, GRID_W, WIN)

    def mixer_params(i):
        mixer, j = i % 3, i // 3
        if mixer == 0:
            return dict(ln_g=a_ln_g_f[j:j + 1], ln_b=a_ln_b_f[j:j + 1], ws=W['a_w_s'][j].astype(BF16),
                        bfull=jnp.repeat(W['a_b_s'][j].T, CHUNK, axis=1))
        if mixer == 2:
            return dict(wdw=_pad_rows(c_w_dw_f[j], 32), bdw=c_b_dw_f[j:j + 1], ln_g=c_ln_g_f[j:j + 1],
                        ln_b=c_ln_b_f[j:j + 1])
        return {}

    h = jnp.concatenate([x[0], ctx[0]], axis=0)
    saved = []
    for i in range(L):
        mixer, j = i % 3, i // 3
        if i == L - 1:
            h = h[:seq]
        Ti = h.shape[0]
        tm = _tile(Ti, 768)
        tmh = _tile(Ti, 384)
        mod = mod_all[i]
        mp = mixer_params(i)
        s = dict(h0=h, mp=mp)
        hm = nm_fwd(f"nm1_{i}", h, W['g_mix'][i:i + 1], mod, 0, 1, seq)
        s['hm'] = hm
        if mixer == 0:
            zp = mm_cols(f"a_in_{i}", hm, Wg['a_w_in'], j, BF16, tm)
            p = gmlp_fwd(f"a_mid_{i}", zp, mp['ln_g'], mp['ln_b'], mp['ws'], mp['bfull'])
            s.update(zp=zp, p=p)
            m1, h = mm_rows_residual(f"a_out_{i}", p, Wg['a_w_out'], j, h, mod, 2, seq, tm)
        elif mixer == 1:
            qkv = mm_cols(f"b_qkv_{i}", hm, Wg['b_w_qkv'], j, BF16, tm)
            p = jnp.concatenate([na_fwd(f"b_na_{i}", qkv, bias, seq), ctx_attn_fwd(f"b_ctx_{i}", qkv, seq)], axis=0)
            s.update(qkv=qkv, p=p)
            m1, h = mm_rows_residual(f"b_out_{i}", p, Wg['b_w_out'], j, h, mod, 2, seq, tm)
        else:
            t = mm_cols(f"c_pw1_{i}", hm, Wg['c_w_pw1'], j, BF16, tm)
            y, yc, p = conv_fwd(f"c_mid_{i}", t, mp['wdw'], mp['bdw'], mp['ln_g'], mp['ln_b'], seq)
            s.update(t=t, y=y, yc=yc, p=p)
            m1, h = mm_rows_residual(f"c_pw2_{i}", p, Wg['c_w_pw2'], j, h, mod, 2, seq, tm)
        s.update(m1=m1, h1=h)
        hf = nm_fwd(f"nm2_{i}", h, W['g_ffn'][i:i + 1], mod, 3, 4, seq)
        a, b, act = mm_ffn_up(f"ffn_up_{i}", hf, Wg['ffn_w1'], Wg['ffn_w3'], i, tmh)
        m2, h = mm_rows_residual(f"ffn_down_{i}", act, Wg['ffn_w2'], i, h, mod, 5, seq, tm)
        s.update(hf=hf, a=a, b=b, act=act, m2=m2)
        saved.append(s)

    dh, st_loss = loss_head("loss_head", h, loss_target[0], W['g_final'].reshape(1, D))

    dmod_lat, dmod_ctx, dmod_tot = [None] * L, [None] * L, [None] * L
    dg_mix, dg_ffn = [None] * L, [None] * L
    small = {}
    for i in reversed(range(L)):
        mixer, j = i % 3, i // 3
        s = saved[i]
        mp = s['mp']
        mod = mod_all[i]
        if i == L - 2:
            dh = jnp.concatenate([dh, jnp.zeros((ctx_rows, D), F32)], axis=0)
        Ti = dh.shape[0]
        tm = _tile(Ti, 768)
        tmh = _tile(Ti, 384)
        dm2, st_g2 = gate_bwd(f"gate2_bwd_{i}", dh, s['m2'], mod, 5, seq)
        da, db = mm_rows_dgrad(f"ffn_down_dx_{i}", dm2, Wg['ffn_w2'], i, BF16, tm, ffn_ab=(s['a'], s['b']))
        dW['ffn_w2'] = mm_wgrad_rows(f"ffn_w2_dw_{i}", s['act'], dm2, dW['ffn_w2'], i, tm)
        dW['ffn_w1'] = mm_wgrad_cols(f"ffn_w1_dw_{i}", s['hf'], da, dW['ffn_w1'], i, tm)
        dW['ffn_w3'] = mm_wgrad_cols(f"ffn_w3_dw_{i}", s['hf'], db, dW['ffn_w3'], i, tm)
        dhf = mm_cols_dgrad(f"ffn_up_dx_{i}", [da, db], [Wg['ffn_w1'], Wg['ffn_w3']], i, tmh)
        dh, st_n2 = nm_bwd(f"nm2_bwd_{i}", s['h1'], dhf, dh, W['g_ffn'][i:i + 1], mod, 4, seq)
        dm1, st_g1 = gate_bwd(f"gate1_bwd_{i}", dh, s['m1'], mod, 2, seq)
        if mixer == 0:
            dp = mm_rows_dgrad(f"a_out_dx_{i}", dm1, Wg['a_w_out'], j, BF16, tm)
            dW['a_w_out'] = mm_wgrad_rows(f"a_out_dw_{i}", s['p'], dm1, dW['a_w_out'], j, tm)
            dzp, dws, dbs, st_a = gmlp_bwd(f"a_mid_bwd_{i}", s['zp'], dp, mp['ln_g'], mp['ln_b'], mp['ws'], mp['bfull'])
            small[('a', j)] = (st_a[0:1], st_a[1:2], dws.reshape(-1, D), dbs.T.reshape(1, D))
            dW['a_w_in'] = mm_wgrad_cols(f"a_in_dw_{i}", s['hm'], dzp, dW['a_w_in'], j, tm)
            dhm = mm_cols_dgrad(f"a_in_dx_{i}", [dzp], [Wg['a_w_in']], j, tm)
        elif mixer == 1:
            do = mm_rows_dgrad(f"b_out_dx_{i}", dm1, Wg['b_w_out'], j, BF16, tm)
            dW['b_w_out'] = mm_wgrad_rows(f"b_out_dw_{i}", s['p'], dm1, dW['b_w_out'], j, tm)
            dq, dk, dv, dkc, dvc, dbias = na_bwd(f"b_na_bwd_{i}", s['qkv'], bias, do, seq)
            dqc, dkc, dvc = ctx_attn_bwd(f"b_ctx_bwd_{i}", s['qkv'], do, dkc, dvc, seq)
            dqkv = jnp.concatenate([jnp.concatenate([dq, dk.astype(BF16), dv.astype(BF16)], axis=1),
                                    jnp.concatenate([dqc, dkc, dvc], axis=1)], axis=0)
            folded = rpb_fold(f"rpb_fold_{i}", dbias.reshape(H, NA_ROWS, GRID_W, NA_ROWS, GRID_W).transpose(0, 1, 3, 2, 4))
            drpb = rpb_reduce(f"rpb_reduce_{i}", folded.reshape(H * n_dr, GRID_W * GRID_W), onehot.T)
            small[('b', j)] = _rows_of(drpb[:, :2 * NA_COLS - 1].reshape(-1), D)
            dW['b_w_qkv'] = mm_wgrad_cols(f"b_qkv_dw_{i}", s['hm'], dqkv, dW['b_w_qkv'], j, tm)
            dhm = mm_cols_dgrad(f"b_qkv_dx_{i}", [dqkv], [Wg['b_w_qkv']], j, tm)
        else:
            ds = mm_rows_dgrad(f"c_pw2_dx_{i}", dm1, Wg['c_w_pw2'], j, F32, tm)
            dW['c_w_pw2'] = mm_wgrad_rows(f"c_pw2_dw_{i}", s['p'], dm1, dW['c_w_pw2'], j, tm)
            dyc, st_c = conv_bwd_norm(f"c_norm_bwd_{i}", ds, s['yc'], mp['ln_g'], mp['ln_b'])
            dt, dwdw = conv_bwd_taps(f"c_taps_bwd_{i}", dyc, s['y'], s['t'], mp['wdw'], seq)
            small[('c', j)] = (dwdw, st_c[2:3], st_c[0:1], st_c[1:2])
            dW['c_w_pw1'] = mm_wgrad_cols(f"c_pw1_dw_{i}", s['hm'], dt, dW['c_w_pw1'], j, tm)
            dhm = mm_cols_dgrad(f"c_pw1_dx_{i}", [dt], [Wg['c_w_pw1']], j, tm)
        dh, st_n1 = nm_bwd(f"nm1_bwd_{i}", s['h0'], dhm, dh, W['g_mix'][i:i + 1], mod, 1, seq)
        dg_mix[i], dg_ffn[i] = st_n1[0:1], st_n2[0:1]
        for dst, r_n, r_g in ((dmod_lat, (1, 2), 0), (dmod_ctx, (3, 4), 1), (dmod_tot, (5, 6), 2)):
            dst[i] = jnp.concatenate([st_n1[r_n[0]:r_n[0] + 1], st_n1[r_n[1]:r_n[1] + 1], st_g1[r_g:r_g + 1],
                                      st_n2[r_n[0]:r_n[0] + 1], st_n2[r_n[1]:r_n[1] + 1], st_g2[r_g:r_g + 1]], axis=0)
    grad_x = dh[:seq].reshape(1, seq, D)

    a_parts = [small[('a', j)] for j in range(n_a)]
    c_parts = [small[('c', j)] for j in range(n_c)]
    entries = [
        ('dmod_lat', jnp.concatenate(dmod_lat, axis=0)), ('dmod_ctx', jnp.concatenate(dmod_ctx, axis=0)),
        ('ada_b', jnp.concatenate(dmod_tot, axis=0)),
        ('g_mix', jnp.concatenate(dg_mix, axis=0)), ('g_ffn', jnp.concatenate(dg_ffn, axis=0)),
        ('a_ln_g', jnp.concatenate([p[0] for p in a_parts], axis=0)),
        ('a_ln_b', jnp.concatenate([p[1] for p in a_parts], axis=0)),
        ('a_w_s', jnp.concatenate([p[2] for p in a_parts], axis=0)),
        ('a_b_s', jnp.concatenate([p[3] for p in a_parts], axis=0)),
        ('b_rpb', small[('b', 0)]),
        ('c_w_dw', jnp.concatenate([p[0] for p in c_parts], axis=0)),
        ('c_b_dw', jnp.concatenate([p[1] for p in c_parts], axis=0)),
        ('c_ln_g', jnp.concatenate([p[2] for p in c_parts], axis=0)),
        ('c_ln_b', jnp.concatenate([p[3] for p in c_parts], axis=0)),
        ('g_final', st_loss[0:1]), ('loss', st_loss[1:2]),
    ]
    offs, o = {}, 0
    for n, arr in entries:
        offs[n] = (o, arr.shape[0])
        o += arr.shape[0]
    pack = _pad_rows(jnp.concatenate([arr for _, arr in entries], axis=0), 64)
    gathered = all_gather_8("ag_small_grads", pack)
    sums = reduce_8("reduce_small_grads", gathered)

    def summed(n):
        return sums[offs[n][0]:offs[n][0] + offs[n][1]]

    loss = (0.5 / D) * jnp.sum(summed('loss'))

    lo, ln_ = offs['dmod_lat']
    dm_lat = gathered[:, lo:lo + ln_].reshape(8, L, 6 * D).transpose(1, 0, 2)
    dm_ctx = summed('dmod_ctx').reshape(L, 1, 6 * D)
    dm16 = jnp.concatenate([dm_lat, dm_ctx, jnp.zeros((L, 7, 6 * D), F32)], axis=1)
    dm16 = lax.dynamic_slice_in_dim(dm16, s_idx * Cs, Cs, axis=2)
    dmc8 = jnp.concatenate([dm16[:, 8:9], jnp.zeros((L, 7, Cs), F32)], axis=1)
    g_ada, d_ada, m_ada, v_ada, ds_part = ada_bwd_adam("ada_bwd_adam", craw16.T, dm16, dmc8, ada_w,
                                                       Mo['ada_w'], Vo['ada_w'])
    ds_all = all_gather_8("ag_ds_ctx", ds_part)
    cc = cctx_adam("cctx_adam", ds_all, W['c_ctx'].reshape(1, D), Mo['c_ctx'].reshape(1, D),
                   Vo['c_ctx'].reshape(1, D))
    out = {'c_ctx': tuple(t.reshape(D) for t in cc), 'ada_w': (g_ada, d_ada, m_ada, v_ada)}

    for n in BIG_NAMES:
        land, land2 = reduce_scatter_parts("rs_" + n, dW[n])
        out[n] = adam_parts("adam_" + n, land, land2, W[n], Mo[n], Vo[n])

    def own_cols(full):
        w = full.shape[-1] // N_SHARD
        return lax.dynamic_slice_in_dim(full, s_idx * w, w, axis=full.ndim - 1)

    small_g = {
        'ada_b': summed('ada_b').reshape(L, 6 * D), 'g_mix': summed('g_mix'), 'g_ffn': summed('g_ffn'),
        'a_ln_g': own_cols(summed('a_ln_g')), 'a_ln_b': own_cols(summed('a_ln_b')),
        'a_w_s': summed('a_w_s').reshape(n_a, G, CHUNK, CHUNK), 'a_b_s': summed('a_b_s').reshape(n_a, G, CHUNK),
        'b_rpb': summed('b_rpb').reshape(-1)[:H * n_dr * (2 * NA_COLS - 1)].reshape(W['b_rpb'].shape),
        'c_w_dw': own_cols(summed('c_w_dw').reshape(n_c, 32, D)[:, :CONV_W]),
        'c_b_dw': own_cols(summed('c_b_dw')), 'c_ln_g': own_cols(summed('c_ln_g')),
        'c_ln_b': own_cols(summed('c_ln_b')), 'g_final': summed('g_final').reshape(D),
    }

    def packed(d):
        flat = jnp.concatenate([d[n].reshape(-1) for n in SMALL_NAMES])
        return _pad_rows(_rows_of(flat, SMALL_PACK_COLS), 8)

    res = adam_flat("adam_small", packed(small_g), packed(W), packed(Mo), packed(Vo))
    o = 0
    for n in SMALL_NAMES:
        size, shape = W[n].size, W[n].shape
        out[n] = (small_g[n],) + tuple(r.reshape(-1)[o:o + size].reshape(shape) for r in res)
        o += size

    return (loss, grad_x) + tuple(out[n][k] for k in range(4) for n in WEIGHT_NAMES)


def kernel(x, c, ctx, c_ctx, ada_w, ada_b, g_mix, g_ffn, ffn_w1, ffn_w3, ffn_w2, a_w_in, a_ln_g, a_ln_b, a_w_s, a_b_s, a_w_out, b_w_qkv, b_rpb, b_w_out, c_w_pw1, c_w_dw, c_b_dw, c_ln_g, c_ln_b, c_w_pw2, g_final, loss_target, m_c_ctx, m_ada_w, m_ada_b, m_g_mix, m_g_ffn, m_ffn_w1, m_ffn_w3, m_ffn_w2, m_a_w_in, m_a_ln_g, m_a_ln_b, m_a_w_s, m_a_b_s, m_a_w_out, m_b_w_qkv, m_b_rpb, m_b_w_out, m_c_w_pw1, m_c_w_dw, m_c_b_dw, m_c_ln_g, m_c_ln_b, m_c_w_pw2, m_g_final, v_c_ctx, v_ada_w, v_ada_b, v_g_mix, v_g_ffn, v_ffn_w1, v_ffn_w3, v_ffn_w2, v_a_w_in, v_a_ln_g, v_a_ln_b, v_a_w_s, v_a_b_s, v_a_w_out, v_b_w_qkv, v_b_rpb, v_b_w_out, v_c_w_pw1, v_c_w_dw, v_c_b_dw, v_c_ln_g, v_c_ln_b, v_c_w_pw2, v_g_final):
    W = dict(zip(WEIGHT_NAMES, (c_ctx, ada_w, ada_b, g_mix, g_ffn, ffn_w1, ffn_w3, ffn_w2, a_w_in, a_ln_g, a_ln_b, a_w_s, a_b_s, a_w_out, b_w_qkv, b_rpb, b_w_out, c_w_pw1, c_w_dw, c_b_dw, c_ln_g, c_ln_b, c_w_pw2, g_final)))
    Mo = dict(zip(WEIGHT_NAMES, (m_c_ctx, m_ada_w, m_ada_b, m_g_mix, m_g_ffn, m_ffn_w1, m_ffn_w3, m_ffn_w2, m_a_w_in, m_a_ln_g, m_a_ln_b, m_a_w_s, m_a_b_s, m_a_w_out, m_b_w_qkv, m_b_rpb, m_b_w_out, m_c_w_pw1, m_c_w_dw, m_c_b_dw, m_c_ln_g, m_c_ln_b, m_c_w_pw2, m_g_final)))
    Vo = dict(zip(WEIGHT_NAMES, (v_c_ctx, v_ada_w, v_ada_b, v_g_mix, v_g_ffn, v_ffn_w1, v_ffn_w3, v_ffn_w2, v_a_w_in, v_a_ln_g, v_a_ln_b, v_a_w_s, v_a_b_s, v_a_w_out, v_b_w_qkv, v_b_rpb, v_b_w_out, v_c_w_pw1, v_c_w_dw, v_c_b_dw, v_c_ln_g, v_c_ln_b, v_c_w_pw2, v_g_final)))
    return _step(W, Mo, Vo, x, c, ctx, loss_target)
```

```python
import functools
import math

import jax
import jax.numpy as jnp
from jax import lax
from jax.experimental import pallas as pl
from jax.experimental.pallas import tpu as pltpu

F32 = jnp.float32
BF16 = jnp.bfloat16
MESH = pl.DeviceIdType.MESH
ANY = pl.BlockSpec(memory_space=pl.ANY)

GRID_W = 64
CHUNK = 128
HEAD = 128
NA_ROWS = 8
NA_COLS = 16
CONV_W = 31
HALO = 16
EPS = 1e-6
NEG_INF = -1e30
N_LAYERS = 4
N_SHARD = 4
V7X_VMEM_BYTES = 64 * 1024 * 1024
VMEM_LIMIT = V7X_VMEM_BYTES - 6 * 1024 * 1024

ADAM_LR = 0.001
ADAM_B1 = 0.9
ADAM_B2 = 0.999
ADAM_EPS = 1e-08
ADAM_WD = 0.01
ADAM_STEP = 10

NN = (((1,), (0,)), ((), ()))
NT = (((1,), (1,)), ((), ()))
TN = (((0,), (0,)), ((), ()))


def _params(n_grid=0):
    sem = ("arbitrary",) * n_grid if n_grid else None
    return pltpu.CompilerParams(dimension_semantics=sem, vmem_limit_bytes=VMEM_LIMIT)


def _xyc():
    return lax.axis_index("x"), lax.axis_index("y"), lax.axis_index("c")


def _flip(v, f):
    return 1 - v if f else v


def _tile(n, pref, mult=128):
    if n <= pref:
        return n
    t = (pref // mult) * mult
    while t > mult and n % t:
        t -= mult
    assert n % t == 0, (n, pref, mult)
    return t


def _sigmoid(x):
    return 1.0 / (1.0 + jnp.exp(-x))


XY_FLIPS = ((1, 0), (0, 1), (1, 1))
ALL_FLIPS = tuple((fx, fy, fc) for fx in (0, 1) for fy in (0, 1) for fc in (0, 1) if fx or fy or fc)


def _remote(src, dst, ssem, rsem, dev):
    return pltpu.make_async_remote_copy(src_ref=src, dst_ref=dst, send_sem=ssem, recv_sem=rsem,
                                        device_id=dev, device_id_type=MESH)


def all_gather_xy(name, shard):
    def body(src, dst, ssem, rsem, lsem):
        x, y, c = _xyc()
        mine = pltpu.make_async_copy(src, dst.at[2 * x + y], lsem)
        mine.start()
        sends = []
        for k, (fx, fy) in enumerate(XY_FLIPS):
            cp = _remote(src, dst.at[2 * x + y], ssem.at[k], rsem.at[k], (_flip(x, fx), _flip(y, fy), c))
            cp.start()
            sends.append(cp)
        for k, (fx, fy) in enumerate(XY_FLIPS):
            px, py = _flip(x, fx), _flip(y, fy)
            _remote(src, dst.at[2 * px + py], ssem.at[k], rsem.at[k], (px, py, c)).wait_recv()
        for cp in sends:
            cp.wait_send()
        mine.wait()

    return pl.pallas_call(
        body, name=name, out_shape=jax.ShapeDtypeStruct((N_SHARD,) + shard.shape, shard.dtype),
        in_specs=[ANY], out_specs=ANY,
        scratch_shapes=[pltpu.SemaphoreType.DMA((3,)), pltpu.SemaphoreType.DMA((3,)), pltpu.SemaphoreType.DMA(())],
    )(shard)


def all_gather_8(name, blk):
    def body(src, dst, ssem, rsem, lsem):
        x, y, c = _xyc()
        me = 4 * x + 2 * y + c
        mine = pltpu.make_async_copy(src, dst.at[me], lsem)
        mine.start()
        sends = []
        for k, (fx, fy, fc) in enumerate(ALL_FLIPS):
            cp = _remote(src, dst.at[me], ssem.at[k], rsem.at[k], (_flip(x, fx), _flip(y, fy), _flip(c, fc)))
            cp.start()
            sends.append(cp)
        for k, (fx, fy, fc) in enumerate(ALL_FLIPS):
            px, py, pc = _flip(x, fx), _flip(y, fy), _flip(c, fc)
            _remote(src, dst.at[4 * px + 2 * py + pc], ssem.at[k], rsem.at[k], (px, py, pc)).wait_recv()
        for cp in sends:
            cp.wait_send()
        mine.wait()

    return pl.pallas_call(
        body, name=name, out_shape=jax.ShapeDtypeStruct((8,) + blk.shape, blk.dtype),
        in_specs=[ANY], out_specs=ANY,
        scratch_shapes=[pltpu.SemaphoreType.DMA((7,)), pltpu.SemaphoreType.DMA((7,)), pltpu.SemaphoreType.DMA(())],
    )(blk)


class GatherTask:
    n_send, n_recv, n_local = 3, 3, 1
    alias = {}

    def __init__(self, shard):
        self.ins = [shard]
        self.outs = [jax.ShapeDtypeStruct((N_SHARD,) + shard.shape, shard.dtype)]

    def _copies(self, xyc, ins, outs, ssem, rsem):
        x, y, c = xyc
        for k, (fx, fy) in enumerate(XY_FLIPS):
            px, py = _flip(x, fx), _flip(y, fy)
            send = _remote(ins[0], outs[0].at[2 * x + y], ssem.at[k], rsem.at[k], (px, py, c))
            recv = _remote(ins[0], outs[0].at[2 * px + py], ssem.at[k], rsem.at[k], (px, py, c))
            yield send, recv

    def start(self, xyc, ins, outs, ssem, rsem, lsem):
        x, y, _ = xyc
        pltpu.make_async_copy(ins[0], outs[0].at[2 * x + y], lsem.at[0]).start()
        for send, _ in self._copies(xyc, ins, outs, ssem, rsem):
            send.start()

    def finish(self, xyc, ins, outs, ssem, rsem, lsem):
        x, y, _ = xyc
        for send, recv in self._copies(xyc, ins, outs, ssem, rsem):
            recv.wait_recv()
            send.wait_send()
        pltpu.make_async_copy(ins[0], outs[0].at[2 * x + y], lsem.at[0]).wait()


class ScatterTask:
    n_send, n_recv, n_local = 4, 4, 1
    alias = {1: 0, 2: 1}

    def __init__(self, part, land, land2, l):
        self.ins = [part, land, land2]
        self.outs = [jax.ShapeDtypeStruct(land.shape, land.dtype), jax.ShapeDtypeStruct(land2.shape, land2.dtype)]
        self.l = l

    def _copies(self, xyc, ins, outs, ssem, rsem):
        x, y, c = xyc
        me_s = 2 * x + y
        part, land, land2 = ins[0], outs[0], outs[1]
        sib = (x, y, 1 - c)
        own = _remote(part.at[me_s], land2.at[me_s, self.l], ssem.at[3], rsem.at[3], sib)
        yield own, own
        for k, (fx, fy) in enumerate(XY_FLIPS):
            px, py = _flip(x, fx), _flip(y, fy)
            ps = 2 * px + py
            send = _remote(part.at[ps], land.at[me_s, self.l], ssem.at[k], rsem.at[k], (px, py, c))
            recv = _remote(part.at[ps], land.at[ps, self.l], ssem.at[k], rsem.at[k], (px, py, c))
            yield send, recv

    def _local(self, xyc, ins, outs, lsem):
        me_s = 2 * xyc[0] + xyc[1]
        return pltpu.make_async_copy(ins[0].at[me_s], outs[0].at[me_s, self.l], lsem.at[0])

    def start(self, xyc, ins, outs, ssem, rsem, lsem):
        self._local(xyc, ins, outs, lsem).start()
        for send, _ in self._copies(xyc, ins, outs, ssem, rsem):
            send.start()

    def finish(self, xyc, ins, outs, ssem, rsem, lsem):
        for send, recv in self._copies(xyc, ins, outs, ssem, rsem):
            recv.wait_recv()
            send.wait_send()
        self._local(xyc, ins, outs, lsem).wait()


class ForwardTask:
    n_send, n_recv, n_local = 3, 3, 0
    alias = {0: 0, 1: 1}

    def __init__(self, land, land2, l):
        self.ins = [land, land2]
        self.outs = [jax.ShapeDtypeStruct(land.shape, land.dtype), jax.ShapeDtypeStruct(land2.shape, land2.dtype)]
        self.l = l

    def _copies(self, xyc, outs, ssem, rsem):
        x, y, c = xyc
        for k, (fx, fy) in enumerate(XY_FLIPS):
            ps = 2 * _flip(x, fx) + _flip(y, fy)
            yield _remote(outs[0].at[ps, self.l], outs[1].at[ps, self.l], ssem.at[k], rsem.at[k], (x, y, 1 - c))

    def start(self, xyc, ins, outs, ssem, rsem, lsem):
        for cp in self._copies(xyc, outs, ssem, rsem):
            cp.start()

    def finish(self, xyc, ins, outs, ssem, rsem, lsem):
        for cp in self._copies(xyc, outs, ssem, rsem):
            cp.wait_recv()
            cp.wait_send()


def host_call(name, body, grid, arrays, in_specs, out_shape, out_specs, scratch=(), tasks=()):
    out_shape, out_specs, scratch = tuple(out_shape), tuple(out_specs), list(scratch)
    n_in, n_out, n_scr, n_grid = len(arrays), len(out_shape), len(scratch), len(grid)
    t_arrays, t_outs, aliases, spans, sems = [], [], {}, [], []
    for t in tasks:
        i0, o0 = len(t_arrays), len(t_outs)
        t_arrays += t.ins
        t_outs += t.outs
        for a, b in t.alias.items():
            aliases[n_in + i0 + a] = n_out + o0 + b
        spans.append((i0, len(t_arrays), o0, len(t_outs)))
        sems += [pltpu.SemaphoreType.DMA((t.n_send,)), pltpu.SemaphoreType.DMA((t.n_recv,)),
                 pltpu.SemaphoreType.DMA((max(t.n_local, 1),))]
    n_tin, n_tout = len(t_arrays), len(t_outs)

    def full_body(*refs):
        ins = refs[:n_in]
        tin = refs[n_in:n_in + n_tin]
        outs = refs[n_in + n_tin:n_in + n_tin + n_out]
        tout = refs[n_in + n_tin + n_out:n_in + n_tin + n_out + n_tout]
        rest = refs[n_in + n_tin + n_out + n_tout:]
        scr, sm = rest[:n_scr], rest[n_scr:]
        if not tasks:
            body(*ins, *outs, *scr)
            return
        pids = [pl.program_id(d) for d in range(n_grid)]
        first = functools.reduce(jnp.logical_and, [p == 0 for p in pids])
        last = functools.reduce(jnp.logical_and, [p == n - 1 for p, n in zip(pids, grid)])
        xyc = _xyc()

        def each(method):
            for k, (t, (i0, i1, o0, o1)) in enumerate(zip(tasks, spans)):
                getattr(t, method)(xyc, tin[i0:i1], tout[o0:o1], sm[3 * k], sm[3 * k + 1], sm[3 * k + 2])

        @pl.when(first)
        def _():
            each("start")

        body(*ins, *outs, *scr)

        @pl.when(last)
        def _():
            each("finish")

    res = pl.pallas_call(
        full_body, name=name, grid=grid, in_specs=list(in_specs) + [ANY] * n_tin,
        out_specs=out_specs + (ANY,) * n_tout, out_shape=out_shape + tuple(t_outs),
        scratch_shapes=scratch + sems, input_output_aliases=aliases, compiler_params=_params(n_grid),
    )(*arrays, *t_arrays)
    return tuple(res[:n_out]), [tuple(res[n_out + o0:n_out + o1]) for (_, _, o0, o1) in spans]


def comm_only(name, tasks):
    def body(i_ref, o_ref):
        o_ref[...] = i_ref[...]

    spec = pl.BlockSpec((8, 128), lambda i: (0, 0))
    _, touts = host_call(name, body, (1,), [jnp.zeros((8, 128), F32)], [spec],
                         [jax.ShapeDtypeStruct((8, 128), F32)], [spec], tasks=tasks)
    return touts


def matmul(name, grid, order, pairs, pair_dims, acc_of_pair, acc_shapes, extras, outs, epilogue, tasks=()):
    ni, nj, nk = grid

    def wrap(m):
        if order == "ij":
            return lambda g0, g1, k: m(g0, g1, k)
        return lambda g0, g1, k: m(g1, g0, k)

    g = (ni, nj, nk) if order == "ij" else (nj, ni, nk)
    arrays, in_specs = [], []
    for a, b in pairs:
        for arr, blk, m in (a, b):
            arrays.append(arr)
            in_specs.append(pl.BlockSpec(blk, wrap(m)))
    for arr, blk, m in extras:
        arrays.append(arr)
        in_specs.append(pl.BlockSpec(blk, wrap(m)))
    n_in = len(arrays)
    out_shape = tuple(o[0] for o in outs)
    out_specs = tuple(pl.BlockSpec(o[1], wrap(o[2])) for o in outs)
    n_pairs, n_ex, n_out, n_acc = len(pairs), len(extras), len(outs), len(acc_shapes)

    def body(*refs):
        ins = refs[:n_in]
        out_refs = refs[n_in:n_in + n_out]
        accs = refs[n_in + n_out:]
        pid = (pl.program_id(0), pl.program_id(1)) if order == "ij" else (pl.program_id(1), pl.program_id(0))
        k = pl.program_id(2)

        def partial(p):
            return lax.dot_general(ins[2 * p][...], ins[2 * p + 1][...], pair_dims[p],
                                   preferred_element_type=F32)

        if nk == 1:
            vals = [None] * n_acc
            for p in range(n_pairs):
                d = partial(p)
                q = acc_of_pair[p]
                vals[q] = d if vals[q] is None else vals[q] + d
            epilogue(vals, ins[2 * n_pairs:2 * n_pairs + n_ex], out_refs, pid)
        else:
            @pl.when(k == 0)
            def _():
                for acc in accs:
                    acc[...] = jnp.zeros_like(acc)

            for p in range(n_pairs):
                accs[acc_of_pair[p]][...] += partial(p)

            @pl.when(k == nk - 1)
            def _():
                epilogue([acc[...] for acc in accs], ins[2 * n_pairs:2 * n_pairs + n_ex], out_refs, pid)

    scratch = [] if nk == 1 else [pltpu.VMEM(s, F32) for s in acc_shapes]
    return host_call(name, body, g, arrays, in_specs, out_shape, out_specs, scratch, tasks)


def _store_cast(vals, extras, out_refs, pid):
    out_refs[0][...] = vals[0].astype(out_refs[0].dtype)


def mm_cols(name, x, wg, out_dtype, tm, tasks=()):
    T, D = x.shape
    ns = wg.shape[2]
    tn = _tile(ns, 1536)
    nb = ns // tn
    outs, touts = matmul(
        name, (T // tm, N_SHARD * nb, 1), "ji",
        [((x, (tm, D), lambda i, jj, k: (i, 0)),
          (wg, (None, D, tn), lambda i, jj, k: (jj // nb, 0, jj % nb)))],
        [NN], [0], [(tm, tn)], [],
        [(jax.ShapeDtypeStruct((T, N_SHARD * ns), out_dtype), (tm, tn), lambda i, jj, k: (i, jj))],
        _store_cast, tasks)
    return outs[0], touts


def mm_ffn_up(name, x, w1g, w3g, tm, tasks=()):
    T, D = x.shape
    ns = w1g.shape[2]
    tn = _tile(ns, 1536)
    nb = ns // tn

    def epi(vals, extras, out_refs, pid):
        a, b = vals
        out_refs[0][...] = a.astype(BF16)
        out_refs[1][...] = b.astype(BF16)
        out_refs[2][...] = (a * _sigmoid(a) * b).astype(BF16)

    wmap = lambda i, jj, k: (jj // nb, 0, jj % nb)
    xa = (x, (tm, D), lambda i, jj, k: (i, 0))
    o = (jax.ShapeDtypeStruct((T, N_SHARD * ns), BF16), (tm, tn), lambda i, jj, k: (i, jj))
    return matmul(name, (T // tm, N_SHARD * nb, 1), "ji",
                  [(xa, (w1g, (None, D, tn), wmap)), (xa, (w3g, (None, D, tn), wmap))],
                  [NN, NN], [0, 1], [(tm, tn), (tm, tn)], [], [o, o, o], epi, tasks)


def mm_rows_residual(name, p, wg, h, mod, gate_row, seq, tm, tasks=()):
    T, kin = p.shape
    ks, D = wg.shape[1], wg.shape[2]
    tk = _tile(ks, 1536)
    kb = ks // tk

    def epi(vals, extras, out_refs, pid):
        m = vals[0]
        h_ref, mod_ref = extras
        rows = pid[0] * tm + lax.broadcasted_iota(jnp.int32, (tm, 1), 0)
        gate = jnp.where(rows >= seq, mod_ref[6 + gate_row:7 + gate_row, :], mod_ref[gate_row:gate_row + 1, :])
        out_refs[0][...] = m.astype(BF16)
        out_refs[1][...] = h_ref[...] + gate * m

    return matmul(
        name, (T // tm, 1, N_SHARD * kb), "ij",
        [((p, (tm, tk), lambda i, jj, k: (i, k)),
          (wg, (None, tk, D), lambda i, jj, k: (k // kb, k % kb, 0)))],
        [NN], [0], [(tm, D)],
        [(h, (tm, D), lambda i, jj, k: (i, 0)), (mod, (16, D), lambda i, jj, k: (0, 0))],
        [(jax.ShapeDtypeStruct((T, D), BF16), (tm, D), lambda i, jj, k: (i, 0)),
         (jax.ShapeDtypeStruct((T, D), F32), (tm, D), lambda i, jj, k: (i, 0))],
        epi, tasks)


def mm_rows_dgrad(name, dm, wg, out_dtype, tm, ffn_ab=None, tasks=()):
    T, D = dm.shape
    ks = wg.shape[1]
    tn = _tile(ks, 1536)
    nb = ks // tn
    omap = lambda i, jj, k: (i, jj)
    o = (jax.ShapeDtypeStruct((T, N_SHARD * ks), out_dtype), (tm, tn), omap)
    pairs = [((dm, (tm, D), lambda i, jj, k: (i, 0)),
              (wg, (None, tn, D), lambda i, jj, k: (jj // nb, jj % nb, 0)))]
    if ffn_ab is None:
        return matmul(name, (T // tm, N_SHARD * nb, 1), "ji", pairs, [NT], [0], [(tm, tn)], [], [o],
                      _store_cast, tasks)

    def epi(vals, extras, out_refs, pid):
        dact = vals[0]
        a = extras[0][...].astype(F32)
        b = extras[1][...].astype(F32)
        sig = _sigmoid(a)
        out_refs[0][...] = (dact * b * (sig * (1.0 + a * (1.0 - sig)))).astype(BF16)
        out_refs[1][...] = (dact * (a * sig)).astype(BF16)

    a, b = ffn_ab
    return matmul(name, (T // tm, N_SHARD * nb, 1), "ji", pairs, [NT], [0], [(tm, tn)],
                  [(a, (tm, tn), omap), (b, (tm, tn), omap)], [o, o], epi, tasks)


def mm_cols_dgrad(name, dys, wgs, tm, tasks=()):
    T = dys[0].shape[0]
    D, ns = wgs[0].shape[1], wgs[0].shape[2]
    tk = _tile(ns, 1536)
    kb = ns // tk
    pairs = [((dy, (tm, tk), lambda i, jj, k: (i, k)),
              (wg, (None, D, tk), lambda i, jj, k: (k // kb, 0, k % kb))) for dy, wg in zip(dys, wgs)]
    outs, touts = matmul(name, (T // tm, 1, N_SHARD * kb), "ij", pairs, [NT] * len(dys), [0] * len(dys),
                         [(tm, D)], [],
                         [(jax.ShapeDtypeStruct((T, D), F32), (tm, D), lambda i, jj, k: (i, 0))], _store_cast, tasks)
    return outs[0], touts


def mm_wgrad_rows(name, p, dm, tk, tasks=()):
    T, kin = p.shape
    D = dm.shape[1]
    ks = kin // N_SHARD
    tm = _tile(ks, 1536)
    mb = ks // tm
    tn = _tile(D, 1024)
    outs, touts = matmul(
        name, (N_SHARD * mb, D // tn, T // tk), "ij",
        [((p, (tk, tm), lambda i, jj, k: (k, i)), (dm, (tk, tn), lambda i, jj, k: (k, jj)))],
        [TN], [0], [(tm, tn)], [],
        [(jax.ShapeDtypeStruct((N_SHARD, ks, D), BF16), (None, tm, tn), lambda i, jj, k: (i // mb, i % mb, jj))],
        _store_cast, tasks)
    return outs[0], touts


def mm_wgrad_cols(name, x, dy, tk, tasks=()):
    T, D = x.shape
    ns = dy.shape[1] // N_SHARD
    tm = _tile(D, 1024)
    tn = _tile(ns, 1536)
    nb = ns // tn
    outs, touts = matmul(
        name, (D // tm, N_SHARD * nb, T // tk), "ij",
        [((x, (tk, tm), lambda i, jj, k: (k, i)), (dy, (tk, tn), lambda i, jj, k: (k, jj)))],
        [TN], [0], [(tm, tn)], [],
        [(jax.ShapeDtypeStruct((N_SHARD, D, ns), BF16), (None, tm, tn), lambda i, jj, k: (jj // nb, i, jj % nb))],
        _store_cast, tasks)
    return outs[0], touts


def _row_tile(T, seq):
    if T == seq:
        return _tile(T, 256, 8)
    return math.gcd(256, math.gcd(seq, T - seq))


def _mod_row(mod_ref, row, is_ctx):
    return jnp.where(is_ctx, mod_ref[6 + row:7 + row, :], mod_ref[row:row + 1, :])


def _rowspec(tr, D):
    return pl.BlockSpec((tr, D), lambda i: (i, 0))


def _fullspec(shape):
    nd = len(shape)
    return pl.BlockSpec(shape, lambda i: (0,) * nd)


def _colsum(v):
    return jnp.sum(v, axis=0, keepdims=True)


def _acc_rows(st_ref, first, rows):
    @pl.when(first)
    def _():
        st_ref[...] = jnp.zeros_like(st_ref)
    for r, val in rows:
        st_ref[r:r + 1, :] += val


def _split_stats(is_ctx, val):
    zero = jnp.zeros_like(val)
    return jnp.where(is_ctx, zero, val), jnp.where(is_ctx, val, zero)


def nm_fwd(name, h, g, mod, r_sh, r_sc, seq):
    T, D = h.shape
    tr = _row_tile(T, seq)
    nlat = seq // tr

    def body(h_ref, g_ref, mod_ref, o_ref):
        is_ctx = pl.program_id(0) >= nlat
        x = h_ref[...]
        r = lax.rsqrt(jnp.mean(x * x, axis=-1, keepdims=True) + EPS)
        y = x * r * g_ref[...]
        o_ref[...] = (y * (1.0 + _mod_row(mod_ref, r_sc, is_ctx)) + _mod_row(mod_ref, r_sh, is_ctx)).astype(BF16)

    return pl.pallas_call(
        body, name=name, grid=(T // tr,), in_specs=[_rowspec(tr, D), _fullspec((1, D)), _fullspec((16, D))],
        out_specs=_rowspec(tr, D), out_shape=jax.ShapeDtypeStruct((T, D), BF16), compiler_params=_params(1),
    )(h, g, mod)


def nm_bwd(name, h, dhm, dres, g, mod, r_sc, seq):
    T, D = h.shape
    tr = _row_tile(T, seq)
    nlat = seq // tr

    def body(h_ref, d_ref, r_ref, g_ref, mod_ref, o_ref, st_ref):
        i = pl.program_id(0)
        is_ctx = i >= nlat
        x = h_ref[...]
        r = lax.rsqrt(jnp.mean(x * x, axis=-1, keepdims=True) + EPS)
        n = x * r
        gg = g_ref[...]
        dout = d_ref[...]
        dsh = _colsum(dout)
        dsc = _colsum(dout * (n * gg))
        dy = dout * (1.0 + _mod_row(mod_ref, r_sc, is_ctx))
        dn = dy * gg
        o_ref[...] = r_ref[...] + r * (dn - n * jnp.mean(dn * n, axis=-1, keepdims=True))
        dsh_l, dsh_c = _split_stats(is_ctx, dsh)
        dsc_l, dsc_c = _split_stats(is_ctx, dsc)
        _acc_rows(st_ref, i == 0, [(0, _colsum(dy * n)), (1, dsh_l), (2, dsc_l), (3, dsh_c), (4, dsc_c),
                                   (5, dsh), (6, dsc)])

    return pl.pallas_call(
        body, name=name, grid=(T // tr,),
        in_specs=[_rowspec(tr, D), _rowspec(tr, D), _rowspec(tr, D), _fullspec((1, D)), _fullspec((16, D))],
        out_specs=(_rowspec(tr, D), _fullspec((8, D))),
        out_shape=(jax.ShapeDtypeStruct((T, D), F32), jax.ShapeDtypeStruct((8, D), F32)),
        compiler_params=_params(1),
    )(h, dhm, dres, g, mod)


def gate_bwd(name, dh, m, mod, r_gt, seq):
    T, D = dh.shape
    tr = _row_tile(T, seq)
    nlat = seq // tr

    def body(d_ref, m_ref, mod_ref, o_ref, st_ref):
        i = pl.program_id(0)
        is_ctx = i >= nlat
        d = d_ref[...]
        o_ref[...] = (d * _mod_row(mod_ref, r_gt, is_ctx)).astype(BF16)
        dgt = _colsum(d * m_ref[...].astype(F32))
        dgt_l, dgt_c = _split_stats(is_ctx, dgt)
        _acc_rows(st_ref, i == 0, [(0, dgt_l), (1, dgt_c), (2, dgt)])

    return pl.pallas_call(
        body, name=name, grid=(T // tr,),
        in_specs=[_rowspec(tr, D), _rowspec(tr, D), _fullspec((16, D))],
        out_specs=(_rowspec(tr, D), _fullspec((8, D))),
        out_shape=(jax.ShapeDtypeStruct((T, D), BF16), jax.ShapeDtypeStruct((8, D), F32)),
        compiler_params=_params(1),
    )(dh, m, mod)


def loss_head(name, h, target, g):
    T, D = h.shape
    tr = _tile(T, 256, 8)

    def body(h_ref, t_ref, g_ref, o_ref, st_ref):
        i = pl.program_id(0)
        x = h_ref[...]
        r = lax.rsqrt(jnp.mean(x * x, axis=-1, keepdims=True) + EPS)
        n = x * r
        gg = g_ref[...]
        err = n * gg - t_ref[...]
        dy = err * (1.0 / D)
        dn = dy * gg
        o_ref[...] = r * (dn - n * jnp.mean(dn * n, axis=-1, keepdims=True))
        _acc_rows(st_ref, i == 0, [(0, _colsum(dy * n)), (1, _colsum(err * err))])

    return pl.pallas_call(
        body, name=name, grid=(T // tr,),
        in_specs=[_rowspec(tr, D), _rowspec(tr, D), _fullspec((1, D))],
        out_specs=(_rowspec(tr, D), _fullspec((8, D))),
        out_shape=(jax.ShapeDtypeStruct((T, D), F32), jax.ShapeDtypeStruct((8, D), F32)),
        compiler_params=_params(1),
    )(h, target, g)


GELU_C = math.sqrt(2.0 / math.pi)


def _gelu(x):
    t = jnp.tanh(GELU_C * (x + 0.044715 * (x * x * x)))
    return 0.5 * x * (1.0 + t), t


def _gelu_grad(x, t):
    return 0.5 * (1.0 + t) + 0.5 * x * (1.0 - t * t) * (GELU_C * (1.0 + 3.0 * 0.044715 * (x * x)))


def _layernorm_fwd(v, g, b):
    mu = jnp.mean(v, axis=-1, keepdims=True)
    xc = v - mu
    rs = lax.rsqrt(jnp.mean(xc * xc, axis=-1, keepdims=True) + EPS)
    xhat = xc * rs
    return xhat * g + b, xhat, rs


def _layernorm_bwd(dout, xhat, rs, g):
    dxh = dout * g
    return rs * (dxh - jnp.mean(dxh, axis=-1, keepdims=True) - xhat * jnp.mean(dxh * xhat, axis=-1, keepdims=True))


def gmlp_fwd(name, zp, ln_g, ln_b, ws, bfull):
    T, E2 = zp.shape
    E = E2 // 2
    G = E // CHUNK
    tr = 2 * CHUNK if T % (2 * CHUNK) == 0 else CHUNK

    def body(zp_ref, g_ref, b_ref, ws_ref, bf_ref, p_ref):
        for ch in range(tr // CHUNK):
            rs_ = slice(ch * CHUNK, (ch + 1) * CHUNK)
            u, _ = _gelu(zp_ref[rs_, 0:E].astype(F32))
            v, _ = _gelu(zp_ref[rs_, E:E2].astype(F32))
            vn, _, _ = _layernorm_fwd(v, g_ref[...], b_ref[...])
            vnb = vn.astype(BF16)
            for gi in range(G):
                cs = slice(gi * CHUNK, (gi + 1) * CHUNK)
                vs = jnp.dot(ws_ref[gi], vnb[:, cs], preferred_element_type=F32) + bf_ref[:, cs]
                p_ref[rs_, cs] = (u[:, cs] * vs).astype(BF16)

    return pl.pallas_call(
        body, name=name, grid=(T // tr,),
        in_specs=[_rowspec(tr, E2), _fullspec((1, E)), _fullspec((1, E)), _fullspec((G, CHUNK, CHUNK)),
                  _fullspec((CHUNK, E))],
        out_specs=_rowspec(tr, E), out_shape=jax.ShapeDtypeStruct((T, E), BF16), compiler_params=_params(1),
    )(zp, ln_g, ln_b, ws, bfull)


def gmlp_bwd(name, zp, dp, ln_g, ln_b, ws, bfull):
    T, E2 = zp.shape
    E = E2 // 2
    G = E // CHUNK
    tr = 2 * CHUNK if T % (2 * CHUNK) == 0 else CHUNK
    n = T // tr

    def body(zp_ref, dp_ref, g_ref, b_ref, ws_ref, bf_ref, dz_ref, dws_ref, dbs_ref, st_ref, dvn_ref, dbf_ref):
        i = pl.program_id(0)

        @pl.when(i == 0)
        def _():
            dws_ref[...] = jnp.zeros_like(dws_ref)
            dbf_ref[...] = jnp.zeros_like(dbf_ref)

        dlg = jnp.zeros((1, E), F32)
        dlb = jnp.zeros((1, E), F32)
        for ch in range(tr // CHUNK):
            rs_ = slice(ch * CHUNK, (ch + 1) * CHUNK)
            zu = zp_ref[rs_, 0:E].astype(F32)
            zv = zp_ref[rs_, E:E2].astype(F32)
            u, tu = _gelu(zu)
            v, tv = _gelu(zv)
            vn, xhat, rs = _layernorm_fwd(v, g_ref[...], b_ref[...])
            vnb = vn.astype(BF16)
            dpf = dp_ref[rs_, :].astype(F32)
            for gi in range(G):
                cs = slice(gi * CHUNK, (gi + 1) * CHUNK)
                w = ws_ref[gi]
                vs = jnp.dot(w, vnb[:, cs], preferred_element_type=F32) + bf_ref[:, cs]
                dz_ref[rs_, cs] = (dpf[:, cs] * vs * _gelu_grad(zu[:, cs], tu[:, cs])).astype(BF16)
                dvs = dpf[:, cs] * u[:, cs]
                dvsb = dvs.astype(BF16)
                dws_ref[gi] += lax.dot_general(dvsb, vnb[:, cs], NT, preferred_element_type=F32)
                dbf_ref[:, cs] += dvs
                dvn_ref[:, cs] = lax.dot_general(w, dvsb, TN, preferred_element_type=F32)
            dvn = dvn_ref[...]
            dlg = dlg + _colsum(dvn * xhat)
            dlb = dlb + _colsum(dvn)
            dv = _layernorm_bwd(dvn, xhat, rs, g_ref[...])
            dz_ref[rs_, E:E2] = (dv * _gelu_grad(zv, tv)).astype(BF16)
        _acc_rows(st_ref, i == 0, [(0, dlg), (1, dlb)])

        @pl.when(i == n - 1)
        def _():
            for gi in range(G):
                dbs_ref[:, gi:gi + 1] = jnp.sum(dbf_ref[:, gi * CHUNK:(gi + 1) * CHUNK], axis=1, keepdims=True)

    return pl.pallas_call(
        body, name=name, grid=(n,),
        in_specs=[_rowspec(tr, E2), _rowspec(tr, E), _fullspec((1, E)), _fullspec((1, E)),
                  _fullspec((G, CHUNK, CHUNK)), _fullspec((CHUNK, E))],
        out_specs=(_rowspec(tr, E2), _fullspec((G, CHUNK, CHUNK)), _fullspec((CHUNK, G)), _fullspec((8, E))),
        out_shape=(jax.ShapeDtypeStruct((T, E2), BF16), jax.ShapeDtypeStruct((G, CHUNK, CHUNK), F32),
                   jax.ShapeDtypeStruct((CHUNK, G), F32), jax.ShapeDtypeStruct((8, E), F32)),
        scratch_shapes=[pltpu.VMEM((CHUNK, E), F32), pltpu.VMEM((CHUNK, E), F32)],
        compiler_params=_params(1),
    )(zp, dp, ln_g, ln_b, ws, bfull)


def _halo_specs(tr, T, width):
    per = tr // HALO
    last = T // HALO - 1
    prev = pl.BlockSpec((HALO, width), lambda i: (jnp.maximum(i * per - 1, 0), 0))
    nxt = pl.BlockSpec((HALO, width), lambda i: (jnp.minimum((i + 1) * per, last), 0))
    return prev, nxt


def _halo_valid(i, n, nlat):
    return jnp.logical_and(i > 0, i != nlat), jnp.logical_and(i + 1 < n, i + 1 != nlat)


def _glu(tt, D):
    a = tt[:, 0:D].astype(F32)
    g = tt[:, D:2 * D].astype(F32)
    return a * _sigmoid(g)


def conv_fwd(name, t, wdw, bdw, ln_g, ln_b, seq):
    T, D2 = t.shape
    D = D2 // 2
    tr = _row_tile(T, seq)
    n, nlat = T // tr, seq // tr
    prev, nxt = _halo_specs(tr, T, D2)

    def body(tp_ref, tc_ref, tn_ref, w_ref, b_ref, g_ref, bb_ref, y_ref, yc_ref, s_ref, buf):
        i = pl.program_id(0)
        pv, nv = _halo_valid(i, n, nlat)
        y = _glu(tc_ref[...], D)
        y_ref[...] = y
        buf[0:HALO, :] = jnp.where(pv, _glu(tp_ref[...], D), 0.0)
        buf[HALO:HALO + tr, :] = y
        buf[HALO + tr:2 * HALO + tr, :] = jnp.where(nv, _glu(tn_ref[...], D), 0.0)
        acc = jnp.zeros((tr, D), F32) + b_ref[...]
        for k in range(CONV_W):
            acc = acc + w_ref[k:k + 1, :] * buf[pl.ds(k + 1, tr), :]
        yc_ref[...] = acc
        yl, _, _ = _layernorm_fwd(acc, g_ref[...], bb_ref[...])
        s_ref[...] = (yl * _sigmoid(yl)).astype(BF16)

    return pl.pallas_call(
        body, name=name, grid=(n,),
        in_specs=[prev, _rowspec(tr, D2), nxt, _fullspec((32, D)), _fullspec((1, D)), _fullspec((1, D)),
                  _fullspec((1, D))],
        out_specs=(_rowspec(tr, D), _rowspec(tr, D), _rowspec(tr, D)),
        out_shape=(jax.ShapeDtypeStruct((T, D), F32), jax.ShapeDtypeStruct((T, D), F32),
                   jax.ShapeDtypeStruct((T, D), BF16)),
        scratch_shapes=[pltpu.VMEM((tr + 2 * HALO, D), F32)], compiler_params=_params(1),
    )(t, t, t, wdw, bdw, ln_g, ln_b)


def conv_bwd_norm(name, ds, yc, ln_g, ln_b):
    T, D = yc.shape
    tr = _tile(T, 256, 8)

    def body(ds_ref, yc_ref, g_ref, b_ref, o_ref, st_ref):
        i = pl.program_id(0)
        yl, xhat, rs = _layernorm_fwd(yc_ref[...], g_ref[...], b_ref[...])
        sig = _sigmoid(yl)
        dyl = ds_ref[...] * (sig * (1.0 + yl * (1.0 - sig)))
        dyc = _layernorm_bwd(dyl, xhat, rs, g_ref[...])
        o_ref[...] = dyc
        _acc_rows(st_ref, i == 0, [(0, _colsum(dyl * xhat)), (1, _colsum(dyl)), (2, _colsum(dyc))])

    return pl.pallas_call(
        body, name=name, grid=(T // tr,),
        in_specs=[_rowspec(tr, D), _rowspec(tr, D), _fullspec((1, D)), _fullspec((1, D))],
        out_specs=(_rowspec(tr, D), _fullspec((8, D))),
        out_shape=(jax.ShapeDtypeStruct((T, D), F32), jax.ShapeDtypeStruct((8, D), F32)),
        compiler_params=_params(1),
    )(ds, yc, ln_g, ln_b)


def conv_bwd_taps(name, dyc, y, t, wdw, seq):
    T, D = y.shape
    tr = _row_tile(T, seq)
    n, nlat = T // tr, seq // tr
    prev, nxt = _halo_specs(tr, T, D)

    def body(dp_ref, dc_ref, dn_ref, yp_ref, ycur_ref, yn_ref, t_ref, w_ref, dt_ref, dw_ref, dbuf, ybuf):
        i = pl.program_id(0)
        pv, nv = _halo_valid(i, n, nlat)
        dcur = dc_ref[...]
        dbuf[0:HALO, :] = jnp.where(pv, dp_ref[...], 0.0)
        dbuf[HALO:HALO + tr, :] = dcur
        dbuf[HALO + tr:2 * HALO + tr, :] = jnp.where(nv, dn_ref[...], 0.0)
        ybuf[0:HALO, :] = jnp.where(pv, yp_ref[...], 0.0)
        ybuf[HALO:HALO + tr, :] = ycur_ref[...]
        ybuf[HALO + tr:2 * HALO + tr, :] = jnp.where(nv, yn_ref[...], 0.0)

        @pl.when(i == 0)
        def _():
            dw_ref[...] = jnp.zeros_like(dw_ref)

        dy = jnp.zeros((tr, D), F32)
        for k in range(CONV_W):
            dw_ref[k:k + 1, :] += _colsum(dcur * ybuf[pl.ds(k + 1, tr), :])
            dy = dy + w_ref[k:k + 1, :] * dbuf[pl.ds(CONV_W - k, tr), :]
        a = t_ref[:, 0:D].astype(F32)
        sig = _sigmoid(t_ref[:, D:2 * D].astype(F32))
        dt_ref[:, 0:D] = (dy * sig).astype(BF16)
        dt_ref[:, D:2 * D] = (dy * a * sig * (1.0 - sig)).astype(BF16)

    return pl.pallas_call(
        body, name=name, grid=(n,),
        in_specs=[prev, _rowspec(tr, D), nxt, prev, _rowspec(tr, D), nxt, _rowspec(tr, 2 * D), _fullspec((32, D))],
        out_specs=(_rowspec(tr, 2 * D), _fullspec((32, D))),
        out_shape=(jax.ShapeDtypeStruct((T, 2 * D), BF16), jax.ShapeDtypeStruct((32, D), F32)),
        scratch_shapes=[pltpu.VMEM((tr + 2 * HALO, D), F32), pltpu.VMEM((tr + 2 * HALO, D), F32)],
        compiler_params=_params(1),
    )(dyc, dyc, dyc, y, y, y, t, wdw)


WIN = NA_ROWS * GRID_W


def _na_specs(seq, ctx_rows, H):
    cb = seq // ctx_rows
    return [
        pl.BlockSpec((GRID_W, HEAD), lambda h, r: (r, h)),
        pl.BlockSpec((seq, HEAD), lambda h, r: (0, H + h)),
        pl.BlockSpec((seq, HEAD), lambda h, r: (0, 2 * H + h)),
        pl.BlockSpec((ctx_rows, HEAD), lambda h, r: (cb, H + h)),
        pl.BlockSpec((ctx_rows, HEAD), lambda h, r: (cb, 2 * H + h)),
        pl.BlockSpec((None, NA_ROWS, GRID_W, WIN), lambda h, r: (h, 0, 0, 0)),
    ]


def _na_scores(q_ref, k_ref, v_ref, kc_ref, vc_ref, b_ref, rows):
    r = pl.program_id(1)
    r_start = jnp.clip(r - NA_ROWS // 2, 0, rows - NA_ROWS)
    d0 = r_start - r + NA_ROWS - 1
    start = pl.multiple_of(r_start * GRID_W, GRID_W)
    scale = HEAD ** -0.5
    q = q_ref[...]
    kw = k_ref[pl.ds(start, WIN), :]
    vw = v_ref[pl.ds(start, WIN), :]
    kc = kc_ref[...]
    vc = vc_ref[...]
    s = lax.dot_general(q, kw, NT, preferred_element_type=F32) * scale + b_ref[d0]
    sc = lax.dot_general(q, kc, NT, preferred_element_type=F32) * scale
    m = jnp.maximum(jnp.max(s, axis=-1, keepdims=True), jnp.max(sc, axis=-1, keepdims=True))
    p = jnp.exp(s - m)
    pc = jnp.exp(sc - m)
    l = jnp.sum(p, axis=-1, keepdims=True) + jnp.sum(pc, axis=-1, keepdims=True)
    return q, kw, vw, kc, vc, p, pc, l, d0, start, scale


def na_fwd(name, qkv, bias, seq, tasks=()):
    T, D3 = qkv.shape
    D = D3 // 3
    H = D // HEAD
    rows = seq // GRID_W

    def body(q_ref, k_ref, v_ref, kc_ref, vc_ref, b_ref, o_ref):
        q, kw, vw, kc, vc, p, pc, l, d0, start, scale = _na_scores(q_ref, k_ref, v_ref, kc_ref, vc_ref, b_ref, rows)
        o = (jnp.dot(p.astype(BF16), vw, preferred_element_type=F32)
             + jnp.dot(pc.astype(BF16), vc, preferred_element_type=F32))
        o_ref[...] = (o / l).astype(BF16)

    outs, touts = host_call(name, body, (H, rows), [qkv, qkv, qkv, qkv, qkv, bias], _na_specs(seq, T - seq, H),
                            [jax.ShapeDtypeStruct((seq, D), BF16)],
                            [pl.BlockSpec((GRID_W, HEAD), lambda h, r: (r, h))], tasks=tasks)
    return outs[0], touts


def na_bwd(name, qkv, bias, do, seq):
    T, D3 = qkv.shape
    D = D3 // 3
    H = D // HEAD
    rows = seq // GRID_W
    ctx_rows = T - seq

    def body(q_ref, k_ref, v_ref, kc_ref, vc_ref, b_ref, do_ref, dq_ref, dk_ref, dv_ref, dkc_ref, dvc_ref, db_ref):
        @pl.when(pl.program_id(1) == 0)
        def _():
            for ref in (dk_ref, dv_ref, dkc_ref, dvc_ref, db_ref):
                ref[...] = jnp.zeros_like(ref)

        q, kw, vw, kc, vc, p, pc, l, d0, start, scale = _na_scores(q_ref, k_ref, v_ref, kc_ref, vc_ref, b_ref, rows)
        inv = 1.0 / l
        pn = p * inv
        pcn = pc * inv
        do_ = do_ref[...]
        dp = lax.dot_general(do_, vw, NT, preferred_element_type=F32)
        dpc = lax.dot_general(do_, vc, NT, preferred_element_type=F32)
        delta = jnp.sum(pn * dp, axis=-1, keepdims=True) + jnp.sum(pcn * dpc, axis=-1, keepdims=True)
        ds = pn * (dp - delta)
        dsc = pcn * (dpc - delta)
        db_ref[d0] += ds
        dsb = (ds * scale).astype(BF16)
        dscb = (dsc * scale).astype(BF16)
        dq = jnp.dot(dsb, kw, preferred_element_type=F32) + jnp.dot(dscb, kc, preferred_element_type=F32)
        dq_ref[...] = dq.astype(BF16)
        dk_ref[pl.ds(start, WIN), :] += lax.dot_general(dsb, q, TN, preferred_element_type=F32)
        dv_ref[pl.ds(start, WIN), :] += lax.dot_general(pn.astype(BF16), do_, TN, preferred_element_type=F32)
        dkc_ref[...] += lax.dot_general(dscb, q, TN, preferred_element_type=F32)
        dvc_ref[...] += lax.dot_general(pcn.astype(BF16), do_, TN, preferred_element_type=F32)

    head_lat = pl.BlockSpec((seq, HEAD), lambda h, r: (0, h))
    head_ctx = pl.BlockSpec((ctx_rows, HEAD), lambda h, r: (0, h))
    return pl.pallas_call(
        body, name=name, grid=(H, rows),
        in_specs=_na_specs(seq, ctx_rows, H) + [pl.BlockSpec((GRID_W, HEAD), lambda h, r: (r, h))],
        out_specs=(pl.BlockSpec((GRID_W, HEAD), lambda h, r: (r, h)), head_lat, head_lat, head_ctx, head_ctx,
                   pl.BlockSpec((None, NA_ROWS, GRID_W, WIN), lambda h, r: (h, 0, 0, 0))),
        out_shape=(jax.ShapeDtypeStruct((seq, D), BF16), jax.ShapeDtypeStruct((seq, D), F32),
                   jax.ShapeDtypeStruct((seq, D), F32), jax.ShapeDtypeStruct((ctx_rows, D), F32),
                   jax.ShapeDtypeStruct((ctx_rows, D), F32), jax.ShapeDtypeStruct(bias.shape, F32)),
        compiler_params=_params(2),
    )(qkv, qkv, qkv, qkv, qkv, bias, do)


def _ctx_specs(seq, ctx_rows, H):
    cb = seq // ctx_rows
    return [pl.BlockSpec((ctx_rows, HEAD), lambda h: (cb, h)),
            pl.BlockSpec((ctx_rows, HEAD), lambda h: (cb, H + h)),
            pl.BlockSpec((ctx_rows, HEAD), lambda h: (cb, 2 * H + h))]


def _ctx_probs(q_ref, k_ref):
    s = lax.dot_general(q_ref[...], k_ref[...], NT, preferred_element_type=F32) * (HEAD ** -0.5)
    p = jnp.exp(s - jnp.max(s, axis=-1, keepdims=True))
    return p / jnp.sum(p, axis=-1, keepdims=True)


def ctx_attn_fwd(name, qkv, seq):
    T, D3 = qkv.shape
    D = D3 // 3
    H = D // HEAD
    ctx_rows = T - seq

    def body(q_ref, k_ref, v_ref, o_ref):
        p = _ctx_probs(q_ref, k_ref)
        o_ref[...] = jnp.dot(p.astype(BF16), v_ref[...], preferred_element_type=F32).astype(BF16)

    return pl.pallas_call(
        body, name=name, grid=(H,), in_specs=_ctx_specs(seq, ctx_rows, H),
        out_specs=pl.BlockSpec((ctx_rows, HEAD), lambda h: (0, h)),
        out_shape=jax.ShapeDtypeStruct((ctx_rows, D), BF16), compiler_params=_params(1),
    )(qkv, qkv, qkv)


def ctx_attn_bwd(name, qkv, do, dkc_lat, dvc_lat, seq):
    T, D3 = qkv.shape
    D = D3 // 3
    H = D // HEAD
    ctx_rows = T - seq
    cb = seq // ctx_rows
    scale = HEAD ** -0.5

    def body(q_ref, k_ref, v_ref, do_ref, dkl_ref, dvl_ref, dq_ref, dk_ref, dv_ref):
        p = _ctx_probs(q_ref, k_ref)
        do_ = do_ref[...]
        dp = lax.dot_general(do_, v_ref[...], NT, preferred_element_type=F32)
        ds = p * (dp - jnp.sum(p * dp, axis=-1, keepdims=True))
        dsb = (ds * scale).astype(BF16)
        dq_ref[...] = jnp.dot(dsb, k_ref[...], preferred_element_type=F32).astype(BF16)
        dk_ref[...] = (dkl_ref[...] + lax.dot_general(dsb, q_ref[...], TN, preferred_element_type=F32)).astype(BF16)
        dv_ref[...] = (dvl_ref[...]
                       + lax.dot_general(p.astype(BF16), do_, TN, preferred_element_type=F32)).astype(BF16)

    blk = pl.BlockSpec((ctx_rows, HEAD), lambda h: (0, h))
    shp = jax.ShapeDtypeStruct((ctx_rows, D), BF16)
    return pl.pallas_call(
        body, name=name, grid=(H,),
        in_specs=_ctx_specs(seq, ctx_rows, H) + [pl.BlockSpec((ctx_rows, HEAD), lambda h: (cb, h)), blk, blk],
        out_specs=(blk, blk, blk), out_shape=(shp, shp, shp), compiler_params=_params(1),
    )(qkv, qkv, qkv, do, dkc_lat, dvc_lat)


def _rpb_tables():
    qc = jnp.arange(GRID_W)[:, None]
    kc = jnp.arange(GRID_W)[None, :]
    rel = (kc - qc + NA_COLS - 1).reshape(1, GRID_W * GRID_W)
    onehot = (rel == jnp.arange(32)[:, None]).astype(F32)
    c_start = jnp.clip(qc - NA_COLS // 2, 0, GRID_W - NA_COLS)
    mask = jnp.logical_and(kc >= c_start, kc < c_start + NA_COLS).astype(F32).reshape(1, GRID_W * GRID_W)
    return onehot, mask


def rpb_expand(name, rpb2, onehot, mask):
    R = rpb2.shape[0]

    def body(r_ref, oh_ref, m_ref, o_ref):
        t = jnp.dot(r_ref[...], oh_ref[...], preferred_element_type=F32, precision=lax.Precision.HIGHEST)
        o_ref[...] = jnp.where(m_ref[...] > 0.5, t, NEG_INF)

    return pl.pallas_call(body, name=name, out_shape=jax.ShapeDtypeStruct((R, GRID_W * GRID_W), F32),
                          compiler_params=_params())(rpb2, onehot, mask)


def rpb_fold(name, x):
    H = x.shape[0]
    n_dr = 2 * NA_ROWS - 1

    def body(x_ref, y_ref):
        for dr in range(n_dr):
            acc = None
            for d0 in range(NA_ROWS):
                jj = dr - d0
                if 0 <= jj < NA_ROWS:
                    acc = x_ref[d0, jj] if acc is None else acc + x_ref[d0, jj]
            y_ref[dr] = acc

    return pl.pallas_call(
        body, name=name, grid=(H,),
        in_specs=[pl.BlockSpec((None, NA_ROWS, NA_ROWS, GRID_W, GRID_W), lambda h: (h, 0, 0, 0, 0))],
        out_specs=pl.BlockSpec((None, n_dr, GRID_W, GRID_W), lambda h: (h, 0, 0, 0)),
        out_shape=jax.ShapeDtypeStruct((H, n_dr, GRID_W, GRID_W), F32), compiler_params=_params(1),
    )(x)


def rpb_reduce(name, y2, onehot_t):
    R = y2.shape[0]

    def body(y_ref, oh_ref, o_ref):
        o_ref[...] = jnp.dot(y_ref[...], oh_ref[...], preferred_element_type=F32, precision=lax.Precision.HIGHEST)

    return pl.pallas_call(body, name=name, out_shape=jax.ShapeDtypeStruct((R, 32), F32),
                          compiler_params=_params())(y2, onehot_t)


def _adam(g, w, m, v):
    m2 = ADAM_B1 * m + (1.0 - ADAM_B1) * g
    v2 = ADAM_B2 * v + (1.0 - ADAM_B2) * (g * g)
    m_hat = m2 / (1.0 - ADAM_B1 ** ADAM_STEP)
    v_hat = v2 / (1.0 - ADAM_B2 ** ADAM_STEP)
    delta = -ADAM_LR * (m_hat / (jnp.sqrt(v_hat) + ADAM_EPS) + ADAM_WD * w)
    return delta, m2, v2


def adam_parts(name, land, land2, w, m, v):
    shape = w.shape
    C = shape[-1]
    R = math.prod(shape[:-1])
    tr = _tile(R, max(16, (256 * 1024 // C) // 16 * 16), 16)
    l1, l2 = land.reshape(N_SHARD, R, C), land2.reshape(N_SHARD, R, C)

    def body(l1_ref, l2_ref, w_ref, m_ref, v_ref, g_ref, d_ref, m2_ref, v2_ref):
        a = l1_ref[0].astype(F32)
        b = l2_ref[0].astype(F32)
        for k in range(1, N_SHARD):
            a = a + l1_ref[k].astype(F32)
            b = b + l2_ref[k].astype(F32)
        g = a + b
        g_ref[...] = g
        d_ref[...], m2_ref[...], v2_ref[...] = _adam(g, w_ref[...], m_ref[...], v_ref[...])

    part = pl.BlockSpec((N_SHARD, tr, C), lambda i: (0, i, 0))
    row = _rowspec(tr, C)
    shp = jax.ShapeDtypeStruct((R, C), F32)
    outs = pl.pallas_call(
        body, name=name, grid=(R // tr,), in_specs=[part, part, row, row, row], out_specs=(row,) * 4,
        out_shape=(shp,) * 4, compiler_params=_params(1),
    )(l1, l2, w.reshape(R, C), m.reshape(R, C), v.reshape(R, C))
    return tuple(o.reshape(shape) for o in outs)


def adam_flat(name, g, w, m, v):
    R, C = g.shape
    tr = _tile(R, 256, 8)

    def body(g_ref, w_ref, m_ref, v_ref, d_ref, m2_ref, v2_ref):
        d_ref[...], m2_ref[...], v2_ref[...] = _adam(g_ref[...], w_ref[...], m_ref[...], v_ref[...])

    row = _rowspec(tr, C)
    shp = jax.ShapeDtypeStruct((R, C), F32)
    return pl.pallas_call(body, name=name, grid=(R // tr,), in_specs=[row] * 4, out_specs=(row,) * 3,
                          out_shape=(shp,) * 3, compiler_params=_params(1))(g, w, m, v)


def reduce_8(name, gathered):
    _, R, D = gathered.shape
    tr = _tile(R, 64, 8)

    def body(x_ref, o_ref):
        acc = x_ref[0]
        for k in range(1, 8):
            acc = acc + x_ref[k]
        o_ref[...] = acc

    return pl.pallas_call(
        body, name=name, grid=(R // tr,), in_specs=[pl.BlockSpec((8, tr, D), lambda i: (0, i, 0))],
        out_specs=_rowspec(tr, D), out_shape=jax.ShapeDtypeStruct((R, D), F32), compiler_params=_params(1),
    )(gathered)


def ada_fwd(name, craw16, ada_w, ada_b3):
    L, D, Cs = ada_w.shape
    tn = _tile(Cs, 512)

    def body(c_ref, w_ref, b_ref, o_ref):
        cc = c_ref[...]
        s = cc * _sigmoid(cc)
        o_ref[...] = jnp.dot(s, w_ref[...], preferred_element_type=F32,
                             precision=lax.Precision.HIGHEST) + b_ref[...]

    return pl.pallas_call(
        body, name=name, grid=(L, Cs // tn),
        in_specs=[pl.BlockSpec((16, D), lambda l, j: (0, 0)), pl.BlockSpec((None, D, tn), lambda l, j: (l, 0, j)),
                  pl.BlockSpec((None, 1, tn), lambda l, j: (l, 0, j))],
        out_specs=pl.BlockSpec((None, 16, tn), lambda l, j: (l, 0, j)),
        out_shape=jax.ShapeDtypeStruct((L, 16, Cs), F32), compiler_params=_params(2),
    )(craw16, ada_w, ada_b3)


def ada_bwd_adam(name, craw16_t, dm16, dmc8, w, m, v):
    L, D, Cs = w.shape
    tn = _tile(Cs, 256)

    def body(c_ref, dm_ref, dc_ref, w_ref, m_ref, v_ref, g_ref, d_ref, m2_ref, v2_ref, ds_ref):
        step = pl.program_id(0) * (Cs // tn) + pl.program_id(1)
        cc = c_ref[...]
        s_t = cc * _sigmoid(cc)
        g = jnp.dot(s_t, dm_ref[...], preferred_element_type=F32, precision=lax.Precision.HIGHEST)
        ww = w_ref[...]
        g_ref[...] = g
        d_ref[...], m2_ref[...], v2_ref[...] = _adam(g, ww, m_ref[...], v_ref[...])

        @pl.when(step == 0)
        def _():
            ds_ref[...] = jnp.zeros_like(ds_ref)

        ds_ref[...] += lax.dot_general(dc_ref[...], ww, NT, preferred_element_type=F32,
                                       precision=lax.Precision.HIGHEST)

    wspec = pl.BlockSpec((None, D, tn), lambda l, j: (l, 0, j))
    shp = jax.ShapeDtypeStruct((L, D, Cs), F32)
    return pl.pallas_call(
        body, name=name, grid=(L, Cs // tn),
        in_specs=[pl.BlockSpec((D, 16), lambda l, j: (0, 0)), pl.BlockSpec((None, 16, tn), lambda l, j: (l, 0, j)),
                  pl.BlockSpec((None, 8, tn), lambda l, j: (l, 0, j)), wspec, wspec, wspec],
        out_specs=(wspec, wspec, wspec, wspec, pl.BlockSpec((8, D), lambda l, j: (0, 0))),
        out_shape=(shp, shp, shp, shp, jax.ShapeDtypeStruct((8, D), F32)), compiler_params=_params(2),
    )(craw16_t, dm16, dmc8, w, m, v)


def cctx_adam(name, ds_all, c_ctx, m, v):
    D = c_ctx.shape[1]

    def body(ds_ref, c_ref, m_ref, v_ref, g_ref, d_ref, m2_ref, v2_ref):
        ds = ds_ref[0, 0:1, :]
        for slot in (2, 4, 6):
            ds = ds + ds_ref[slot, 0:1, :]
        cc = c_ref[...]
        sig = _sigmoid(cc)
        g = ds * (sig * (1.0 + cc * (1.0 - sig)))
        g_ref[...] = g
        d_ref[...], m2_ref[...], v2_ref[...] = _adam(g, cc, m_ref[...], v_ref[...])

    shp = jax.ShapeDtypeStruct((1, D), F32)
    return pl.pallas_call(body, name=name, out_shape=(shp,) * 4, compiler_params=_params())(ds_all, c_ctx, m, v)


WEIGHT_NAMES = ['c_ctx', 'ada_w', 'ada_b', 'g_mix', 'g_ffn', 'ffn_w1', 'ffn_w3', 'ffn_w2', 'a_w_in', 'a_ln_g',
                'a_ln_b', 'a_w_s', 'a_b_s', 'a_w_out', 'b_w_qkv', 'b_rpb', 'b_w_out', 'c_w_pw1', 'c_w_dw', 'c_b_dw',
                'c_ln_g', 'c_ln_b', 'c_w_pw2', 'g_final']
BIG_NAMES = ['ffn_w1', 'ffn_w3', 'ffn_w2', 'a_w_in', 'a_w_out', 'b_w_qkv', 'b_w_out', 'c_w_pw1', 'c_w_pw2']
SMALL_NAMES = ['ada_b', 'g_mix', 'g_ffn', 'a_ln_g', 'a_ln_b', 'a_w_s', 'a_b_s', 'b_rpb', 'c_w_dw', 'c_b_dw',
               'c_ln_g', 'c_ln_b', 'g_final']
SMALL_PACK_COLS = 512
MIXER_IN = ('a_w_in', 'b_w_qkv', 'c_w_pw1')
MIXER_OUT = ('a_w_out', 'b_w_out', 'c_w_pw2')

FWD_PLAN = {
    "pre": [("a_w_in", 0)],
    "in_0": [("a_w_out", 0), ("ffn_w1", 0)],
    "out_0": [("ffn_w3", 0)],
    "ffn_up_0": [("ffn_w2", 0), ("b_w_qkv", 0)],
    "ffn_down_0": [("b_w_out", 0), ("ffn_w1", 1)],
    "in_1": [("ffn_w3", 1)],
    "b_na_1": [("ffn_w2", 1), ("c_w_pw1", 0), ("c_w_pw2", 0)],
    "out_1": [("ffn_w1", 2)],
    "ffn_up_1": [("ffn_w3", 2), ("ffn_w2", 2)],
    "ffn_down_1": [("a_w_in", 1), ("a_w_out", 1)],
    "in_2": [("ffn_w1", 3)],
    "ffn_up_2": [("ffn_w3", 3), ("ffn_w2", 3)],
}


def _bwd_plan():
    plan = {}
    for i in range(N_LAYERS):
        w_in, w_out = (MIXER_IN[i % 3], i // 3), (MIXER_OUT[i % 3], i // 3)
        if i + 1 < N_LAYERS:
            plan[f"ffn_down_dx_{i}"] = [("forward", MIXER_IN[(i + 1) % 3], (i + 1) // 3)]
        plan[f"ffn_w1_dw_{i}"] = [("scatter", "ffn_w2", i)]
        plan[f"ffn_w3_dw_{i}"] = [("forward", "ffn_w2", i), ("scatter", "ffn_w1", i)]
        plan[f"ffn_up_dx_{i}"] = [("forward", "ffn_w1", i), ("scatter", "ffn_w3", i)]
        plan[f"out_dx_{i}"] = [("forward", "ffn_w3", i)]
        plan[f"in_dw_{i}"] = [("scatter",) + w_out]
        plan[f"in_dx_{i}"] = [("forward",) + w_out, ("scatter",) + w_in]
    plan["rs_post"] = [("forward", MIXER_IN[0], 0)]
    return plan


BWD_PLAN = _bwd_plan()


def _pad_rows(a, mult):
    r = (-a.shape[0]) % mult
    return a if r == 0 else jnp.concatenate([a, jnp.zeros((r,) + a.shape[1:], a.dtype)], axis=0)


def _rows_of(flat, D):
    n = flat.shape[0]
    r = -(-n // D)
    return jnp.concatenate([flat, jnp.zeros((r * D - n,), flat.dtype)]).reshape(r, D)


def _step(W, Mo, Vo, x, c, ctx, loss_target):
    seq, D = x.shape[1], x.shape[2]
    ctx_rows = ctx.shape[1]
    T = seq + ctx_rows
    L = N_LAYERS
    H = D // HEAD
    G = D // CHUNK
    xi, yi, ci = _xyc()
    e_idx = 4 * xi + 2 * yi + ci
    s_idx = 2 * xi + yi

    c_all = all_gather_8("ag_c", c)
    craw16 = jnp.concatenate([c_all.reshape(8, D), W['c_ctx'].reshape(1, D), jnp.zeros((7, D), F32)], axis=0)
    ada_w = W['ada_w']
    Cs = ada_w.shape[2]
    ada_b_s = lax.dynamic_slice_in_dim(W['ada_b'], s_idx * Cs, Cs, axis=1).reshape(L, 1, Cs)
    mod_s = ada_fwd("ada_fwd", craw16, ada_w, ada_b_s)
    mod_g = all_gather_xy("ag_mod", mod_s).transpose(1, 2, 0, 3).reshape(L, 16, N_SHARD * Cs)
    mod_lat = lax.dynamic_index_in_dim(mod_g, e_idx, axis=1, keepdims=False).reshape(L, 6, D)
    mod_all = jnp.concatenate([mod_lat, mod_g[:, 8].reshape(L, 6, D), jnp.zeros((L, 4, D), F32)], axis=1)

    Wg = {}
    land = {n: lax.empty((N_SHARD,) + W[n].shape, BF16) for n in BIG_NAMES}
    land2 = {n: lax.empty((N_SHARD,) + W[n].shape, BF16) for n in BIG_NAMES}
    dW = {}

    def gather_tasks(host):
        return [GatherTask(W[n][l].astype(BF16)) for n, l in FWD_PLAN.get(host, ())]

    def gathered(host, touts):
        for (n, l), out in zip(FWD_PLAN.get(host, ()), touts):
            Wg[(n, l)] = out[0]

    def scatter_tasks(host):
        tasks = []
        for kind, n, l in BWD_PLAN.get(host, ()):
            if kind == "scatter":
                tasks.append(ScatterTask(dW[(n, l)], land[n], land2[n], l))
            else:
                tasks.append(ForwardTask(land[n], land2[n], l))
        return tasks

    def scattered(host, touts):
        for (kind, n, l), out in zip(BWD_PLAN.get(host, ()), touts):
            land[n], land2[n] = out

    gathered("pre", comm_only("ag_pre", gather_tasks("pre")))

    n_a, n_c = W['a_ln_g'].shape[0], W['c_ln_g'].shape[0]
    sh_rows = jnp.concatenate([W['a_ln_g'], W['a_ln_b'], W['c_w_dw'].reshape(n_c * CONV_W, -1), W['c_b_dw'],
                               W['c_ln_g'], W['c_ln_b']], axis=0)
    n_sh = sh_rows.shape[0]
    sh_full = all_gather_xy("ag_small", _pad_rows(sh_rows, 8)).transpose(1, 0, 2).reshape(-1, D)[:n_sh]
    o = 0
    a_ln_g_f, o = sh_full[o:o + n_a], o + n_a
    a_ln_b_f, o = sh_full[o:o + n_a], o + n_a
    c_w_dw_f, o = sh_full[o:o + n_c * CONV_W].reshape(n_c, CONV_W, D), o + n_c * CONV_W
    c_b_dw_f, o = sh_full[o:o + n_c], o + n_c
    c_ln_g_f, o = sh_full[o:o + n_c], o + n_c
    c_ln_b_f, o = sh_full[o:o + n_c], o + n_c

    onehot, colmask = _rpb_tables()
    n_dr = 2 * NA_ROWS - 1
    rpb2 = jnp.pad(W['b_rpb'][0].reshape(H * n_dr, 2 * NA_COLS - 1), ((0, 0), (0, 1)))
    toep = rpb_expand("rpb_expand", rpb2, onehot, colmask).reshape(H, n_dr, GRID_W, GRID_W)
    bias = jnp.stack([toep[:, d0:d0 + NA_ROWS] for d0 in range(NA_ROWS)], axis=1)
    bias = bias.transpose(0, 1, 3, 2, 4).reshape(H, NA_ROWS, GRID_W, WIN)

    def mixer_params(i):
        mixer, j = i % 3, i // 3
        if mixer == 0:
            return dict(ln_g=a_ln_g_f[j:j + 1], ln_b=a_ln_b_f[j:j + 1], ws=W['a_w_s'][j].astype(BF16),
                        bfull=jnp.repeat(W['a_b_s'][j].T, CHUNK, axis=1))
        if mixer == 2:
            return dict(wdw=_pad_rows(c_w_dw_f[j], 32), bdw=c_b_dw_f[j:j + 1], ln_g=c_ln_g_f[j:j + 1],
                        ln_b=c_ln_b_f[j:j + 1])
        return {}

    def fwd(fn, host, *args):
        res, touts = fn(host, *args, tasks=gather_tasks(host))
        gathered(host, touts)
        return res

    def bwd(fn, host, *args, **kw):
        res, touts = fn(host, *args, tasks=scatter_tasks(host), **kw)
        scattered(host, touts)
        return res

    h = jnp.concatenate([x[0], ctx[0]], axis=0)
    saved = []
    for i in range(L):
        mixer, j = i % 3, i // 3
        n_in, n_out = MIXER_IN[mixer], MIXER_OUT[mixer]
        if i == L - 1:
            h = h[:seq]
        Ti = h.shape[0]
        tm = _tile(Ti, 768)
        tmh = _tile(Ti, 384)
        mod = mod_all[i]
        mp = mixer_params(i)
        s = dict(h0=h, mp=mp)
        hm = nm_fwd(f"nm1_{i}", h, W['g_mix'][i:i + 1], mod, 0, 1, seq)
        s['hm'] = hm
        u = fwd(mm_cols, f"in_{i}", hm, Wg[(n_in, j)], BF16, tm)
        if mixer == 0:
            p = gmlp_fwd(f"a_mid_{i}", u, mp['ln_g'], mp['ln_b'], mp['ws'], mp['bfull'])
        elif mixer == 1:
            p = jnp.concatenate([fwd(na_fwd, f"b_na_{i}", u, bias, seq), ctx_attn_fwd(f"b_ctx_{i}", u, seq)], axis=0)
        else:
            y, yc, p = conv_fwd(f"c_mid_{i}", u, mp['wdw'], mp['bdw'], mp['ln_g'], mp['ln_b'], seq)
            s.update(y=y, yc=yc)
        s.update(u=u, p=p)
        m1, h = fwd(mm_rows_residual, f"out_{i}", p, Wg[(n_out, j)], h, mod, 2, seq, tm)
        s.update(m1=m1, h1=h)
        hf = nm_fwd(f"nm2_{i}", h, W['g_ffn'][i:i + 1], mod, 3, 4, seq)
        a, b, act = fwd(mm_ffn_up, f"ffn_up_{i}", hf, Wg[('ffn_w1', i)], Wg[('ffn_w3', i)], tmh)
        m2, h = fwd(mm_rows_residual, f"ffn_down_{i}", act, Wg[('ffn_w2', i)], h, mod, 5, seq, tmh)
        s.update(hf=hf, a=a, b=b, act=act, m2=m2)
        saved.append(s)

    dh, st_loss = loss_head("loss_head", h, loss_target[0], W['g_final'].reshape(1, D))

    dmod_lat, dmod_ctx, dmod_tot = [None] * L, [None] * L, [None] * L
    dg_mix, dg_ffn = [None] * L, [None] * L
    small = {}
    for i in reversed(range(L)):
        mixer, j = i % 3, i // 3
        n_in, n_out = MIXER_IN[mixer], MIXER_OUT[mixer]
        s = saved[i]
        mp = s['mp']
        mod = mod_all[i]
        if i == L - 2:
            dh = jnp.concatenate([dh, jnp.zeros((ctx_rows, D), F32)], axis=0)
        Ti = dh.shape[0]
        tm = _tile(Ti, 768)
        tmh = _tile(Ti, 384)
        dm2, st_g2 = gate_bwd(f"gate2_bwd_{i}", dh, s['m2'], mod, 5, seq)
        da, db = bwd(mm_rows_dgrad, f"ffn_down_dx_{i}", dm2, Wg[('ffn_w2', i)], BF16, tm, ffn_ab=(s['a'], s['b']))
        dW[('ffn_w2', i)] = bwd(mm_wgrad_rows, f"ffn_w2_dw_{i}", s['act'], dm2, tm)
        dW[('ffn_w1', i)] = bwd(mm_wgrad_cols, f"ffn_w1_dw_{i}", s['hf'], da, tm)
        dW[('ffn_w3', i)] = bwd(mm_wgrad_cols, f"ffn_w3_dw_{i}", s['hf'], db, tm)
        dhf = bwd(mm_cols_dgrad, f"ffn_up_dx_{i}", [da, db], [Wg[('ffn_w1', i)], Wg[('ffn_w3', i)]], tmh)
        dh, st_n2 = nm_bwd(f"nm2_bwd_{i}", s['h1'], dhf, dh, W['g_ffn'][i:i + 1], mod, 4, seq)
        dm1, st_g1 = gate_bwd(f"gate1_bwd_{i}", dh, s['m1'], mod, 2, seq)
        dp = bwd(mm_rows_dgrad, f"out_dx_{i}", dm1, Wg[(n_out, j)], F32 if mixer == 2 else BF16, tm)[0]
        dW[(n_out, j)] = bwd(mm_wgrad_rows, f"out_dw_{i}", s['p'], dm1, tm)
        if mixer == 0:
            du, dws, dbs, st_a = gmlp_bwd(f"a_mid_bwd_{i}", s['u'], dp, mp['ln_g'], mp['ln_b'], mp['ws'], mp['bfull'])
            small[('a', j)] = (st_a[0:1], st_a[1:2], dws.reshape(-1, D), dbs.T.reshape(1, D))
        elif mixer == 1:
            dq, dk, dv, dkc, dvc, dbias = na_bwd(f"b_na_bwd_{i}", s['u'], bias, dp, seq)
            dqc, dkc, dvc = ctx_attn_bwd(f"b_ctx_bwd_{i}", s['u'], dp, dkc, dvc, seq)
            du = jnp.concatenate([jnp.concatenate([dq, dk.astype(BF16), dv.astype(BF16)], axis=1),
                                  jnp.concatenate([dqc, dkc, dvc], axis=1)], axis=0)
            folded = rpb_fold(f"rpb_fold_{i}", dbias.reshape(H, NA_ROWS, GRID_W, NA_ROWS, GRID_W).transpose(0, 1, 3, 2, 4))
            drpb = rpb_reduce(f"rpb_reduce_{i}", folded.reshape(H * n_dr, GRID_W * GRID_W), onehot.T)
            small[('b', j)] = _rows_of(drpb[:, :2 * NA_COLS - 1].reshape(-1), D)
        else:
            dyc, st_c = conv_bwd_norm(f"c_norm_bwd_{i}", dp, s['yc'], mp['ln_g'], mp['ln_b'])
            du, dwdw = conv_bwd_taps(f"c_taps_bwd_{i}", dyc, s['y'], s['u'], mp['wdw'], seq)
            small[('c', j)] = (dwdw, st_c[2:3], st_c[0:1], st_c[1:2])
        dW[(n_in, j)] = bwd(mm_wgrad_cols, f"in_dw_{i}", s['hm'], du, tm)
        dhm = bwd(mm_cols_dgrad, f"in_dx_{i}", [du], [Wg[(n_in, j)]], tm)
        dh, st_n1 = nm_bwd(f"nm1_bwd_{i}", s['h0'], dhm, dh, W['g_mix'][i:i + 1], mod, 1, seq)
        dg_mix[i], dg_ffn[i] = st_n1[0:1], st_n2[0:1]
        for dst, r_n, r_g in ((dmod_lat, (1, 2), 0), (dmod_ctx, (3, 4), 1), (dmod_tot, (5, 6), 2)):
            dst[i] = jnp.concatenate([st_n1[r_n[0]:r_n[0] + 1], st_n1[r_n[1]:r_n[1] + 1], st_g1[r_g:r_g + 1],
                                      st_n2[r_n[0]:r_n[0] + 1], st_n2[r_n[1]:r_n[1] + 1], st_g2[r_g:r_g + 1]], axis=0)
    scattered("rs_post", comm_only("rs_post", scatter_tasks("rs_post")))
    grad_x = dh[:seq].reshape(1, seq, D)

    a_parts = [small[('a', j)] for j in range(n_a)]
    c_parts = [small[('c', j)] for j in range(n_c)]
    entries = [
        ('dmod_lat', jnp.concatenate(dmod_lat, axis=0)), ('dmod_ctx', jnp.concatenate(dmod_ctx, axis=0)),
        ('ada_b', jnp.concatenate(dmod_tot, axis=0)),
        ('g_mix', jnp.concatenate(dg_mix, axis=0)), ('g_ffn', jnp.concatenate(dg_ffn, axis=0)),
        ('a_ln_g', jnp.concatenate([p[0] for p in a_parts], axis=0)),
        ('a_ln_b', jnp.concatenate([p[1] for p in a_parts], axis=0)),
        ('a_w_s', jnp.concatenate([p[2] for p in a_parts], axis=0)),
        ('a_b_s', jnp.concatenate([p[3] for p in a_parts], axis=0)),
        ('b_rpb', small[('b', 0)]),
        ('c_w_dw', jnp.concatenate([p[0] for p in c_parts], axis=0)),
        ('c_b_dw', jnp.concatenate([p[1] for p in c_parts], axis=0)),
        ('c_ln_g', jnp.concatenate([p[2] for p in c_parts], axis=0)),
        ('c_ln_b', jnp.concatenate([p[3] for p in c_parts], axis=0)),
        ('g_final', st_loss[0:1]), ('loss', st_loss[1:2]),
    ]
    offs, o = {}, 0
    for n, arr in entries:
        offs[n] = (o, arr.shape[0])
        o += arr.shape[0]
    pack = _pad_rows(jnp.concatenate([arr for _, arr in entries], axis=0), 64)
    gathered = all_gather_8("ag_small_grads", pack)
    sums = reduce_8("reduce_small_grads", gathered)

    def summed(n):
        return sums[offs[n][0]:offs[n][0] + offs[n][1]]

    loss = (0.5 / D) * jnp.sum(summed('loss'))

    lo, ln_ = offs['dmod_lat']
    dm_lat = gathered[:, lo:lo + ln_].reshape(8, L, 6 * D).transpose(1, 0, 2)
    dm_ctx = summed('dmod_ctx').reshape(L, 1, 6 * D)
    dm16 = jnp.concatenate([dm_lat, dm_ctx, jnp.zeros((L, 7, 6 * D), F32)], axis=1)
    dm16 = lax.dynamic_slice_in_dim(dm16, s_idx * Cs, Cs, axis=2)
    dmc8 = jnp.concatenate([dm16[:, 8:9], jnp.zeros((L, 7, Cs), F32)], axis=1)
    g_ada, d_ada, m_ada, v_ada, ds_part = ada_bwd_adam("ada_bwd_adam", craw16.T, dm16, dmc8, ada_w,
                                                       Mo['ada_w'], Vo['ada_w'])
    ds_all = all_gather_8("ag_ds_ctx", ds_part)
    cc = cctx_adam("cctx_adam", ds_all, W['c_ctx'].reshape(1, D), Mo['c_ctx'].reshape(1, D),
                   Vo['c_ctx'].reshape(1, D))
    out = {'c_ctx': tuple(t.reshape(D) for t in cc), 'ada_w': (g_ada, d_ada, m_ada, v_ada)}

    for n in BIG_NAMES:
        out[n] = adam_parts("adam_" + n, land[n], land2[n], W[n], Mo[n], Vo[n])

    def own_cols(full):
        w = full.shape[-1] // N_SHARD
        return lax.dynamic_slice_in_dim(full, s_idx * w, w, axis=full.ndim - 1)

    small_g = {
        'ada_b': summed('ada_b').reshape(L, 6 * D), 'g_mix': summed('g_mix'), 'g_ffn': summed('g_ffn'),
        'a_ln_g': own_cols(summed('a_ln_g')), 'a_ln_b': own_cols(summed('a_ln_b')),
        'a_w_s': summed('a_w_s').reshape(n_a, G, CHUNK, CHUNK), 'a_b_s': summed('a_b_s').reshape(n_a, G, CHUNK),
        'b_rpb': summed('b_rpb').reshape(-1)[:H * n_dr * (2 * NA_COLS - 1)].reshape(W['b_rpb'].shape),
        'c_w_dw': own_cols(summed('c_w_dw').reshape(n_c, 32, D)[:, :CONV_W]),
        'c_b_dw': own_cols(summed('c_b_dw')), 'c_ln_g': own_cols(summed('c_ln_g')),
        'c_ln_b': own_cols(summed('c_ln_b')), 'g_final': summed('g_final').reshape(D),
    }

    def packed(d):
        flat = jnp.concatenate([d[n].reshape(-1) for n in SMALL_NAMES])
        return _pad_rows(_rows_of(flat, SMALL_PACK_COLS), 8)

    res = adam_flat("adam_small", packed(small_g), packed(W), packed(Mo), packed(Vo))
    o = 0
    for n in SMALL_NAMES:
        size, shape = W[n].size, W[n].shape
        out[n] = (small_g[n],) + tuple(r.reshape(-1)[o:o + size].reshape(shape) for r in res)
        o += size

    return (loss, grad_x) + tuple(out[n][k] for k in range(4) for n in WEIGHT_NAMES)


def kernel(x, c, ctx, c_ctx, ada_w, ada_b, g_mix, g_ffn, ffn_w1, ffn_w3, ffn_w2, a_w_in, a_ln_g, a_ln_b, a_w_s, a_b_s, a_w_out, b_w_qkv, b_rpb, b_w_out, c_w_pw1, c_w_dw, c_b_dw, c_ln_g, c_ln_b, c_w_pw2, g_final, loss_target, m_c_ctx, m_ada_w, m_ada_b, m_g_mix, m_g_ffn, m_ffn_w1, m_ffn_w3, m_ffn_w2, m_a_w_in, m_a_ln_g, m_a_ln_b, m_a_w_s, m_a_b_s, m_a_w_out, m_b_w_qkv, m_b_rpb, m_b_w_out, m_c_w_pw1, m_c_w_dw, m_c_b_dw, m_c_ln_g, m_c_ln_b, m_c_w_pw2, m_g_final, v_c_ctx, v_ada_w, v_ada_b, v_g_mix, v_g_ffn, v_ffn_w1, v_ffn_w3, v_ffn_w2, v_a_w_in, v_a_ln_g, v_a_ln_b, v_a_w_s, v_a_b_s, v_a_w_out, v_b_w_qkv, v_b_rpb, v_b_w_out, v_c_w_pw1, v_c_w_dw, v_c_b_dw, v_c_ln_g, v_c_ln_b, v_c_w_pw2, v_g_final):
    W = dict(zip(WEIGHT_NAMES, (c_ctx, ada_w, ada_b, g_mix, g_ffn, ffn_w1, ffn_w3, ffn_w2, a_w_in, a_ln_g, a_ln_b, a_w_s, a_b_s, a_w_out, b_w_qkv, b_rpb, b_w_out, c_w_pw1, c_w_dw, c_b_dw, c_ln_g, c_ln_b, c_w_pw2, g_final)))
    Mo = dict(zip(WEIGHT_NAMES, (m_c_ctx, m_ada_w, m_ada_b, m_g_mix, m_g_ffn, m_ffn_w1, m_ffn_w3, m_ffn_w2, m_a_w_in, m_a_ln_g, m_a_ln_b, m_a_w_s, m_a_b_s, m_a_w_out, m_b_w_qkv, m_b_rpb, m_b_w_out, m_c_w_pw1, m_c_w_dw, m_c_b_dw, m_c_ln_g, m_c_ln_b, m_c_w_pw2, m_g_final)))
    Vo = dict(zip(WEIGHT_NAMES, (v_c_ctx, v_ada_w, v_ada_b, v_g_mix, v_g_ffn, v_ffn_w1, v_ffn_w3, v_ffn_w2, v_a_w_in, v_a_ln_g, v_a_ln_b, v_a_w_s, v_a_b_s, v_a_w_out, v_b_w_qkv, v_b_rpb, v_b_w_out, v_c_w_pw1, v_c_w_dw, v_c_b_dw, v_c_ln_g, v_c_ln_b, v_c_w_pw2, v_g_final)))
    return _step(W, Mo, Vo, x, c, ctx, loss_target)
```

```python
import functools
import math

import jax
import jax.numpy as jnp
from jax import lax
from jax.experimental import pallas as pl
from jax.experimental.pallas import tpu as pltpu

F32 = jnp.float32
BF16 = jnp.bfloat16
MESH = pl.DeviceIdType.MESH
ANY = pl.BlockSpec(memory_space=pl.ANY)

GRID_W = 64
CHUNK = 128
HEAD = 128
NA_ROWS = 8
NA_COLS = 16
CONV_W = 31
HALO = 16
EPS = 1e-6
NEG_INF = -1e30
N_LAYERS = 4
N_SHARD = 4
V7X_VMEM_BYTES = 64 * 1024 * 1024
VMEM_LIMIT = V7X_VMEM_BYTES - 6 * 1024 * 1024

ADAM_LR = 0.001
ADAM_B1 = 0.9
ADAM_B2 = 0.999
ADAM_EPS = 1e-08
ADAM_WD = 0.01
ADAM_STEP = 10

NN = (((1,), (0,)), ((), ()))
NT = (((1,), (1,)), ((), ()))
TN = (((0,), (0,)), ((), ()))


def _params(n_grid=0):
    sem = ("arbitrary",) * n_grid if n_grid else None
    return pltpu.CompilerParams(dimension_semantics=sem, vmem_limit_bytes=VMEM_LIMIT)


def _xyc():
    return lax.axis_index("x"), lax.axis_index("y"), lax.axis_index("c")


def _flip(v, f):
    return 1 - v if f else v


def _tile(n, pref, mult=128):
    if n <= pref:
        return n
    t = (pref // mult) * mult
    while t > mult and n % t:
        t -= mult
    assert n % t == 0, (n, pref, mult)
    return t


def _sigmoid(x):
    return 1.0 / (1.0 + jnp.exp(-x))


XY_FLIPS = ((1, 0), (0, 1), (1, 1))
ALL_FLIPS = tuple((fx, fy, fc) for fx in (0, 1) for fy in (0, 1) for fc in (0, 1) if fx or fy or fc)


def _remote(src, dst, ssem, rsem, dev):
    return pltpu.make_async_remote_copy(src_ref=src, dst_ref=dst, send_sem=ssem, recv_sem=rsem,
                                        device_id=dev, device_id_type=MESH)


def all_gather_xy(name, shard):
    def body(src, dst, ssem, rsem, lsem):
        x, y, c = _xyc()
        mine = pltpu.make_async_copy(src, dst.at[2 * x + y], lsem)
        mine.start()
        sends = []
        for k, (fx, fy) in enumerate(XY_FLIPS):
            cp = _remote(src, dst.at[2 * x + y], ssem.at[k], rsem.at[k], (_flip(x, fx), _flip(y, fy), c))
            cp.start()
            sends.append(cp)
        for k, (fx, fy) in enumerate(XY_FLIPS):
            px, py = _flip(x, fx), _flip(y, fy)
            _remote(src, dst.at[2 * px + py], ssem.at[k], rsem.at[k], (px, py, c)).wait_recv()
        for cp in sends:
            cp.wait_send()
        mine.wait()

    return pl.pallas_call(
        body, name=name, out_shape=jax.ShapeDtypeStruct((N_SHARD,) + shard.shape, shard.dtype),
        in_specs=[ANY], out_specs=ANY,
        scratch_shapes=[pltpu.SemaphoreType.DMA((3,)), pltpu.SemaphoreType.DMA((3,)), pltpu.SemaphoreType.DMA(())],
    )(shard)


def all_gather_8(name, blk):
    def body(src, dst, ssem, rsem, lsem):
        x, y, c = _xyc()
        me = 4 * x + 2 * y + c
        mine = pltpu.make_async_copy(src, dst.at[me], lsem)
        mine.start()
        sends = []
        for k, (fx, fy, fc) in enumerate(ALL_FLIPS):
            cp = _remote(src, dst.at[me], ssem.at[k], rsem.at[k], (_flip(x, fx), _flip(y, fy), _flip(c, fc)))
            cp.start()
            sends.append(cp)
        for k, (fx, fy, fc) in enumerate(ALL_FLIPS):
            px, py, pc = _flip(x, fx), _flip(y, fy), _flip(c, fc)
            _remote(src, dst.at[4 * px + 2 * py + pc], ssem.at[k], rsem.at[k], (px, py, pc)).wait_recv()
        for cp in sends:
            cp.wait_send()
        mine.wait()

    return pl.pallas_call(
        body, name=name, out_shape=jax.ShapeDtypeStruct((8,) + blk.shape, blk.dtype),
        in_specs=[ANY], out_specs=ANY,
        scratch_shapes=[pltpu.SemaphoreType.DMA((7,)), pltpu.SemaphoreType.DMA((7,)), pltpu.SemaphoreType.DMA(())],
    )(blk)


class GatherTask:
    n_send, n_recv, n_local = 3, 3, 1
    alias = {}

    def __init__(self, shard):
        self.ins = [shard]
        self.outs = [jax.ShapeDtypeStruct((N_SHARD,) + shard.shape, shard.dtype)]

    def _copies(self, xyc, ins, outs, ssem, rsem):
        x, y, c = xyc
        for k, (fx, fy) in enumerate(XY_FLIPS):
            px, py = _flip(x, fx), _flip(y, fy)
            send = _remote(ins[0], outs[0].at[2 * x + y], ssem.at[k], rsem.at[k], (px, py, c))
            recv = _remote(ins[0], outs[0].at[2 * px + py], ssem.at[k], rsem.at[k], (px, py, c))
            yield send, recv

    def start(self, xyc, ins, outs, ssem, rsem, lsem):
        x, y, _ = xyc
        pltpu.make_async_copy(ins[0], outs[0].at[2 * x + y], lsem.at[0]).start()
        for send, _ in self._copies(xyc, ins, outs, ssem, rsem):
            send.start()

    def finish(self, xyc, ins, outs, ssem, rsem, lsem):
        x, y, _ = xyc
        for send, recv in self._copies(xyc, ins, outs, ssem, rsem):
            recv.wait_recv()
            send.wait_send()
        pltpu.make_async_copy(ins[0], outs[0].at[2 * x + y], lsem.at[0]).wait()


class GatherAllTask:
    n_send, n_recv, n_local = 7, 7, 1
    alias = {}

    def __init__(self, blk):
        self.ins = [blk]
        self.outs = [jax.ShapeDtypeStruct((8,) + blk.shape, blk.dtype)]

    def _copies(self, xyc, ins, outs, ssem, rsem):
        x, y, c = xyc
        for k, (fx, fy, fc) in enumerate(ALL_FLIPS):
            px, py, pc = _flip(x, fx), _flip(y, fy), _flip(c, fc)
            send = _remote(ins[0], outs[0].at[4 * x + 2 * y + c], ssem.at[k], rsem.at[k], (px, py, pc))
            recv = _remote(ins[0], outs[0].at[4 * px + 2 * py + pc], ssem.at[k], rsem.at[k], (px, py, pc))
            yield send, recv

    def _local(self, xyc, ins, outs, lsem):
        x, y, c = xyc
        return pltpu.make_async_copy(ins[0], outs[0].at[4 * x + 2 * y + c], lsem.at[0])

    def start(self, xyc, ins, outs, ssem, rsem, lsem):
        self._local(xyc, ins, outs, lsem).start()
        for send, _ in self._copies(xyc, ins, outs, ssem, rsem):
            send.start()

    def finish(self, xyc, ins, outs, ssem, rsem, lsem):
        for send, recv in self._copies(xyc, ins, outs, ssem, rsem):
            recv.wait_recv()
            send.wait_send()
        self._local(xyc, ins, outs, lsem).wait()


class ScatterTask:
    n_send, n_recv, n_local = 4, 4, 1
    alias = {1: 0, 2: 1}

    def __init__(self, part, land, land2, l):
        self.ins = [part, land, land2]
        self.outs = [jax.ShapeDtypeStruct(land.shape, land.dtype), jax.ShapeDtypeStruct(land2.shape, land2.dtype)]
        self.l = l

    def _copies(self, xyc, ins, outs, ssem, rsem):
        x, y, c = xyc
        me_s = 2 * x + y
        part, land, land2 = ins[0], outs[0], outs[1]
        sib = (x, y, 1 - c)
        own = _remote(part.at[me_s], land2.at[me_s, self.l], ssem.at[3], rsem.at[3], sib)
        yield own, own
        for k, (fx, fy) in enumerate(XY_FLIPS):
            px, py = _flip(x, fx), _flip(y, fy)
            ps = 2 * px + py
            send = _remote(part.at[ps], land.at[me_s, self.l], ssem.at[k], rsem.at[k], (px, py, c))
            recv = _remote(part.at[ps], land.at[ps, self.l], ssem.at[k], rsem.at[k], (px, py, c))
            yield send, recv

    def _local(self, xyc, ins, outs, lsem):
        me_s = 2 * xyc[0] + xyc[1]
        return pltpu.make_async_copy(ins[0].at[me_s], outs[0].at[me_s, self.l], lsem.at[0])

    def start(self, xyc, ins, outs, ssem, rsem, lsem):
        self._local(xyc, ins, outs, lsem).start()
        for send, _ in self._copies(xyc, ins, outs, ssem, rsem):
            send.start()

    def finish(self, xyc, ins, outs, ssem, rsem, lsem):
        for send, recv in self._copies(xyc, ins, outs, ssem, rsem):
            recv.wait_recv()
            send.wait_send()
        self._local(xyc, ins, outs, lsem).wait()


class ForwardTask:
    n_send, n_recv, n_local = 3, 3, 0
    alias = {0: 0, 1: 1}

    def __init__(self, land, land2, l):
        self.ins = [land, land2]
        self.outs = [jax.ShapeDtypeStruct(land.shape, land.dtype), jax.ShapeDtypeStruct(land2.shape, land2.dtype)]
        self.l = l

    def _copies(self, xyc, outs, ssem, rsem):
        x, y, c = xyc
        for k, (fx, fy) in enumerate(XY_FLIPS):
            ps = 2 * _flip(x, fx) + _flip(y, fy)
            yield _remote(outs[0].at[ps, self.l], outs[1].at[ps, self.l], ssem.at[k], rsem.at[k], (x, y, 1 - c))

    def start(self, xyc, ins, outs, ssem, rsem, lsem):
        for cp in self._copies(xyc, outs, ssem, rsem):
            cp.start()

    def finish(self, xyc, ins, outs, ssem, rsem, lsem):
        for cp in self._copies(xyc, outs, ssem, rsem):
            cp.wait_recv()
            cp.wait_send()


def host_call(name, body, grid, arrays, in_specs, out_shape, out_specs, scratch=(), tasks=()):
    out_shape, out_specs, scratch = tuple(out_shape), tuple(out_specs), list(scratch)
    n_in, n_out, n_scr, n_grid = len(arrays), len(out_shape), len(scratch), len(grid)
    t_arrays, t_outs, aliases, spans, sems = [], [], {}, [], []
    for t in tasks:
        i0, o0 = len(t_arrays), len(t_outs)
        t_arrays += t.ins
        t_outs += t.outs
        for a, b in t.alias.items():
            aliases[n_in + i0 + a] = n_out + o0 + b
        spans.append((i0, len(t_arrays), o0, len(t_outs)))
        sems += [pltpu.SemaphoreType.DMA((t.n_send,)), pltpu.SemaphoreType.DMA((t.n_recv,)),
                 pltpu.SemaphoreType.DMA((max(t.n_local, 1),))]
    n_tin, n_tout = len(t_arrays), len(t_outs)

    def full_body(*refs):
        ins = refs[:n_in]
        tin = refs[n_in:n_in + n_tin]
        outs = refs[n_in + n_tin:n_in + n_tin + n_out]
        tout = refs[n_in + n_tin + n_out:n_in + n_tin + n_out + n_tout]
        rest = refs[n_in + n_tin + n_out + n_tout:]
        scr, sm = rest[:n_scr], rest[n_scr:]
        if not tasks:
            body(*ins, *outs, *scr)
            return
        pids = [pl.program_id(d) for d in range(n_grid)]
        first = functools.reduce(jnp.logical_and, [p == 0 for p in pids])
        last = functools.reduce(jnp.logical_and, [p == n - 1 for p, n in zip(pids, grid)])
        xyc = _xyc()

        def each(method):
            for k, (t, (i0, i1, o0, o1)) in enumerate(zip(tasks, spans)):
                getattr(t, method)(xyc, tin[i0:i1], tout[o0:o1], sm[3 * k], sm[3 * k + 1], sm[3 * k + 2])

        @pl.when(first)
        def _():
            each("start")

        body(*ins, *outs, *scr)

        @pl.when(last)
        def _():
            each("finish")

    res = pl.pallas_call(
        full_body, name=name, grid=grid, in_specs=list(in_specs) + [ANY] * n_tin,
        out_specs=out_specs + (ANY,) * n_tout, out_shape=out_shape + tuple(t_outs),
        scratch_shapes=scratch + sems, input_output_aliases=aliases, compiler_params=_params(n_grid),
    )(*arrays, *t_arrays)
    return tuple(res[:n_out]), [tuple(res[n_out + o0:n_out + o1]) for (_, _, o0, o1) in spans]


def comm_only(name, tasks):
    def body(i_ref, o_ref):
        o_ref[...] = i_ref[...]

    spec = pl.BlockSpec((8, 128), lambda i: (0, 0))
    _, touts = host_call(name, body, (1,), [jnp.zeros((8, 128), F32)], [spec],
                         [jax.ShapeDtypeStruct((8, 128), F32)], [spec], tasks=tasks)
    return touts


def matmul(name, grid, order, pairs, pair_dims, acc_of_pair, acc_shapes, extras, outs, epilogue, tasks=()):
    ni, nj, nk = grid

    def wrap(m):
        if order == "ij":
            return lambda g0, g1, k: m(g0, g1, k)
        return lambda g0, g1, k: m(g1, g0, k)

    g = (ni, nj, nk) if order == "ij" else (nj, ni, nk)
    arrays, in_specs = [], []
    for a, b in pairs:
        for arr, blk, m in (a, b):
            arrays.append(arr)
            in_specs.append(pl.BlockSpec(blk, wrap(m)))
    for arr, blk, m in extras:
        arrays.append(arr)
        in_specs.append(pl.BlockSpec(blk, wrap(m)))
    n_in = len(arrays)
    out_shape = tuple(o[0] for o in outs)
    out_specs = tuple(pl.BlockSpec(o[1], wrap(o[2])) for o in outs)
    n_pairs, n_ex, n_out, n_acc = len(pairs), len(extras), len(outs), len(acc_shapes)

    def body(*refs):
        ins = refs[:n_in]
        out_refs = refs[n_in:n_in + n_out]
        accs = refs[n_in + n_out:]
        pid = (pl.program_id(0), pl.program_id(1)) if order == "ij" else (pl.program_id(1), pl.program_id(0))
        k = pl.program_id(2)

        def partial(p):
            return lax.dot_general(ins[2 * p][...], ins[2 * p + 1][...], pair_dims[p],
                                   preferred_element_type=F32)

        if nk == 1:
            vals = [None] * n_acc
            for p in range(n_pairs):
                d = partial(p)
                q = acc_of_pair[p]
                vals[q] = d if vals[q] is None else vals[q] + d
            epilogue(vals, ins[2 * n_pairs:2 * n_pairs + n_ex], out_refs, pid)
        else:
            @pl.when(k == 0)
            def _():
                for acc in accs:
                    acc[...] = jnp.zeros_like(acc)

            for p in range(n_pairs):
                accs[acc_of_pair[p]][...] += partial(p)

            @pl.when(k == nk - 1)
            def _():
                epilogue([acc[...] for acc in accs], ins[2 * n_pairs:2 * n_pairs + n_ex], out_refs, pid)

    scratch = [] if nk == 1 else [pltpu.VMEM(s, F32) for s in acc_shapes]
    return host_call(name, body, g, arrays, in_specs, out_shape, out_specs, scratch, tasks)


def _store_cast(vals, extras, out_refs, pid):
    out_refs[0][...] = vals[0].astype(out_refs[0].dtype)


def mm_cols(name, x, wg, out_dtype, tm, tasks=()):
    T, D = x.shape
    ns = wg.shape[2]
    tn = _tile(ns, 1536)
    nb = ns // tn
    outs, touts = matmul(
        name, (T // tm, N_SHARD * nb, 1), "ji",
        [((x, (tm, D), lambda i, jj, k: (i, 0)),
          (wg, (None, D, tn), lambda i, jj, k: (jj // nb, 0, jj % nb)))],
        [NN], [0], [(tm, tn)], [],
        [(jax.ShapeDtypeStruct((T, N_SHARD * ns), out_dtype), (tm, tn), lambda i, jj, k: (i, jj))],
        _store_cast, tasks)
    return outs[0], touts


def mm_ffn_up(name, x, w1g, w3g, tm, tasks=()):
    T, D = x.shape
    ns = w1g.shape[2]
    tn = _tile(ns, 1536)
    nb = ns // tn

    def epi(vals, extras, out_refs, pid):
        a, b = vals
        out_refs[0][...] = a.astype(BF16)
        out_refs[1][...] = b.astype(BF16)
        out_refs[2][...] = (a * _sigmoid(a) * b).astype(BF16)

    wmap = lambda i, jj, k: (jj // nb, 0, jj % nb)
    xa = (x, (tm, D), lambda i, jj, k: (i, 0))
    o = (jax.ShapeDtypeStruct((T, N_SHARD * ns), BF16), (tm, tn), lambda i, jj, k: (i, jj))
    return matmul(name, (T // tm, N_SHARD * nb, 1), "ji",
                  [(xa, (w1g, (None, D, tn), wmap)), (xa, (w3g, (None, D, tn), wmap))],
                  [NN, NN], [0, 1], [(tm, tn), (tm, tn)], [], [o, o, o], epi, tasks)


def mm_rows_residual(name, p, wg, h, mod, gate_row, seq, tm, tasks=()):
    T, kin = p.shape
    ks, D = wg.shape[1], wg.shape[2]
    tk = _tile(ks, 1536)
    kb = ks // tk

    def epi(vals, extras, out_refs, pid):
        m = vals[0]
        h_ref, mod_ref = extras
        rows = pid[0] * tm + lax.broadcasted_iota(jnp.int32, (tm, 1), 0)
        gate = jnp.where(rows >= seq, mod_ref[6 + gate_row:7 + gate_row, :], mod_ref[gate_row:gate_row + 1, :])
        out_refs[0][...] = m.astype(BF16)
        out_refs[1][...] = h_ref[...] + gate * m

    return matmul(
        name, (T // tm, 1, N_SHARD * kb), "ij",
        [((p, (tm, tk), lambda i, jj, k: (i, k)),
          (wg, (None, tk, D), lambda i, jj, k: (k // kb, k % kb, 0)))],
        [NN], [0], [(tm, D)],
        [(h, (tm, D), lambda i, jj, k: (i, 0)), (mod, (16, D), lambda i, jj, k: (0, 0))],
        [(jax.ShapeDtypeStruct((T, D), BF16), (tm, D), lambda i, jj, k: (i, 0)),
         (jax.ShapeDtypeStruct((T, D), F32), (tm, D), lambda i, jj, k: (i, 0))],
        epi, tasks)


def mm_rows_dgrad(name, dm, wg, out_dtype, tm, ffn_ab=None, tasks=()):
    T, D = dm.shape
    ks = wg.shape[1]
    tn = _tile(ks, 1536)
    nb = ks // tn
    omap = lambda i, jj, k: (i, jj)
    o = (jax.ShapeDtypeStruct((T, N_SHARD * ks), out_dtype), (tm, tn), omap)
    pairs = [((dm, (tm, D), lambda i, jj, k: (i, 0)),
              (wg, (None, tn, D), lambda i, jj, k: (jj // nb, jj % nb, 0)))]
    if ffn_ab is None:
        return matmul(name, (T // tm, N_SHARD * nb, 1), "ji", pairs, [NT], [0], [(tm, tn)], [], [o],
                      _store_cast, tasks)

    def epi(vals, extras, out_refs, pid):
        dact = vals[0]
        a = extras[0][...].astype(F32)
        b = extras[1][...].astype(F32)
        sig = _sigmoid(a)
        out_refs[0][...] = (dact * b * (sig * (1.0 + a * (1.0 - sig)))).astype(BF16)
        out_refs[1][...] = (dact * (a * sig)).astype(BF16)

    a, b = ffn_ab
    return matmul(name, (T // tm, N_SHARD * nb, 1), "ji", pairs, [NT], [0], [(tm, tn)],
                  [(a, (tm, tn), omap), (b, (tm, tn), omap)], [o, o], epi, tasks)


def mm_cols_dgrad(name, dys, wgs, tm, tasks=()):
    T = dys[0].shape[0]
    D, ns = wgs[0].shape[1], wgs[0].shape[2]
    tk = _tile(ns, 1536)
    kb = ns // tk
    pairs = [((dy, (tm, tk), lambda i, jj, k: (i, k)),
              (wg, (None, D, tk), lambda i, jj, k: (k // kb, 0, k % kb))) for dy, wg in zip(dys, wgs)]
    outs, touts = matmul(name, (T // tm, 1, N_SHARD * kb), "ij", pairs, [NT] * len(dys), [0] * len(dys),
                         [(tm, D)], [],
                         [(jax.ShapeDtypeStruct((T, D), F32), (tm, D), lambda i, jj, k: (i, 0))], _store_cast, tasks)
    return outs[0], touts


def mm_wgrad_rows(name, p, dm, tk, tasks=()):
    T, kin = p.shape
    D = dm.shape[1]
    ks = kin // N_SHARD
    tm = _tile(ks, 1536)
    mb = ks // tm
    tn = _tile(D, 1024)
    outs, touts = matmul(
        name, (N_SHARD * mb, D // tn, T // tk), "ij",
        [((p, (tk, tm), lambda i, jj, k: (k, i)), (dm, (tk, tn), lambda i, jj, k: (k, jj)))],
        [TN], [0], [(tm, tn)], [],
        [(jax.ShapeDtypeStruct((N_SHARD, ks, D), BF16), (None, tm, tn), lambda i, jj, k: (i // mb, i % mb, jj))],
        _store_cast, tasks)
    return outs[0], touts


def mm_wgrad_cols(name, x, dy, tk, tasks=()):
    T, D = x.shape
    ns = dy.shape[1] // N_SHARD
    tm = _tile(D, 1024)
    tn = _tile(ns, 1536)
    nb = ns // tn
    outs, touts = matmul(
        name, (D // tm, N_SHARD * nb, T // tk), "ij",
        [((x, (tk, tm), lambda i, jj, k: (k, i)), (dy, (tk, tn), lambda i, jj, k: (k, jj)))],
        [TN], [0], [(tm, tn)], [],
        [(jax.ShapeDtypeStruct((N_SHARD, D, ns), BF16), (None, tm, tn), lambda i, jj, k: (jj // nb, i, jj % nb))],
        _store_cast, tasks)
    return outs[0], touts


def _row_tile(T, seq):
    if T == seq:
        return _tile(T, 256, 8)
    return math.gcd(256, math.gcd(seq, T - seq))


def _mod_row(mod_ref, row, is_ctx):
    return jnp.where(is_ctx, mod_ref[6 + row:7 + row, :], mod_ref[row:row + 1, :])


def _rowspec(tr, D):
    return pl.BlockSpec((tr, D), lambda i: (i, 0))


def _fullspec(shape):
    nd = len(shape)
    return pl.BlockSpec(shape, lambda i: (0,) * nd)


def _colsum(v):
    return jnp.sum(v, axis=0, keepdims=True)


def _acc_rows(st_ref, first, rows):
    @pl.when(first)
    def _():
        st_ref[...] = jnp.zeros_like(st_ref)
    for r, val in rows:
        st_ref[r:r + 1, :] += val


def _split_stats(is_ctx, val):
    zero = jnp.zeros_like(val)
    return jnp.where(is_ctx, zero, val), jnp.where(is_ctx, val, zero)


def nm_fwd(name, h, g, mod, r_sh, r_sc, seq):
    T, D = h.shape
    tr = _row_tile(T, seq)
    nlat = seq // tr

    def body(h_ref, g_ref, mod_ref, o_ref):
        is_ctx = pl.program_id(0) >= nlat
        x = h_ref[...]
        r = lax.rsqrt(jnp.mean(x * x, axis=-1, keepdims=True) + EPS)
        y = x * r * g_ref[...]
        o_ref[...] = (y * (1.0 + _mod_row(mod_ref, r_sc, is_ctx)) + _mod_row(mod_ref, r_sh, is_ctx)).astype(BF16)

    return pl.pallas_call(
        body, name=name, grid=(T // tr,), in_specs=[_rowspec(tr, D), _fullspec((1, D)), _fullspec((16, D))],
        out_specs=_rowspec(tr, D), out_shape=jax.ShapeDtypeStruct((T, D), BF16), compiler_params=_params(1),
    )(h, g, mod)


def nm_bwd(name, h, dhm, dres, g, mod, r_sc, seq):
    T, D = h.shape
    tr = _row_tile(T, seq)
    nlat = seq // tr

    def body(h_ref, d_ref, r_ref, g_ref, mod_ref, o_ref, st_ref):
        i = pl.program_id(0)
        is_ctx = i >= nlat
        x = h_ref[...]
        r = lax.rsqrt(jnp.mean(x * x, axis=-1, keepdims=True) + EPS)
        n = x * r
        gg = g_ref[...]
        dout = d_ref[...]
        dsh = _colsum(dout)
        dsc = _colsum(dout * (n * gg))
        dy = dout * (1.0 + _mod_row(mod_ref, r_sc, is_ctx))
        dn = dy * gg
        o_ref[...] = r_ref[...] + r * (dn - n * jnp.mean(dn * n, axis=-1, keepdims=True))
        dsh_l, dsh_c = _split_stats(is_ctx, dsh)
        dsc_l, dsc_c = _split_stats(is_ctx, dsc)
        _acc_rows(st_ref, i == 0, [(0, _colsum(dy * n)), (1, dsh_l), (2, dsc_l), (3, dsh_c), (4, dsc_c),
                                   (5, dsh), (6, dsc)])

    return pl.pallas_call(
        body, name=name, grid=(T // tr,),
        in_specs=[_rowspec(tr, D), _rowspec(tr, D), _rowspec(tr, D), _fullspec((1, D)), _fullspec((16, D))],
        out_specs=(_rowspec(tr, D), _fullspec((8, D))),
        out_shape=(jax.ShapeDtypeStruct((T, D), F32), jax.ShapeDtypeStruct((8, D), F32)),
        compiler_params=_params(1),
    )(h, dhm, dres, g, mod)


def gate_bwd(name, dh, m, mod, r_gt, seq):
    T, D = dh.shape
    tr = _row_tile(T, seq)
    nlat = seq // tr

    def body(d_ref, m_ref, mod_ref, o_ref, st_ref):
        i = pl.program_id(0)
        is_ctx = i >= nlat
        d = d_ref[...]
        o_ref[...] = (d * _mod_row(mod_ref, r_gt, is_ctx)).astype(BF16)
        dgt = _colsum(d * m_ref[...].astype(F32))
        dgt_l, dgt_c = _split_stats(is_ctx, dgt)
        _acc_rows(st_ref, i == 0, [(0, dgt_l), (1, dgt_c), (2, dgt)])

    return pl.pallas_call(
        body, name=name, grid=(T // tr,),
        in_specs=[_rowspec(tr, D), _rowspec(tr, D), _fullspec((16, D))],
        out_specs=(_rowspec(tr, D), _fullspec((8, D))),
        out_shape=(jax.ShapeDtypeStruct((T, D), BF16), jax.ShapeDtypeStruct((8, D), F32)),
        compiler_params=_params(1),
    )(dh, m, mod)


def loss_head(name, h, target, g):
    T, D = h.shape
    tr = _tile(T, 256, 8)

    def body(h_ref, t_ref, g_ref, o_ref, st_ref):
        i = pl.program_id(0)
        x = h_ref[...]
        r = lax.rsqrt(jnp.mean(x * x, axis=-1, keepdims=True) + EPS)
        n = x * r
        gg = g_ref[...]
        err = n * gg - t_ref[...]
        dy = err * (1.0 / D)
        dn = dy * gg
        o_ref[...] = r * (dn - n * jnp.mean(dn * n, axis=-1, keepdims=True))
        _acc_rows(st_ref, i == 0, [(0, _colsum(dy * n)), (1, _colsum(err * err))])

    return pl.pallas_call(
        body, name=name, grid=(T // tr,),
        in_specs=[_rowspec(tr, D), _rowspec(tr, D), _fullspec((1, D))],
        out_specs=(_rowspec(tr, D), _fullspec((8, D))),
        out_shape=(jax.ShapeDtypeStruct((T, D), F32), jax.ShapeDtypeStruct((8, D), F32)),
        compiler_params=_params(1),
    )(h, target, g)


GELU_C = math.sqrt(2.0 / math.pi)


def _gelu(x):
    t = jnp.tanh(GELU_C * (x + 0.044715 * (x * x * x)))
    return 0.5 * x * (1.0 + t), t


def _gelu_grad(x, t):
    return 0.5 * (1.0 + t) + 0.5 * x * (1.0 - t * t) * (GELU_C * (1.0 + 3.0 * 0.044715 * (x * x)))


def _layernorm_fwd(v, g, b):
    mu = jnp.mean(v, axis=-1, keepdims=True)
    xc = v - mu
    rs = lax.rsqrt(jnp.mean(xc * xc, axis=-1, keepdims=True) + EPS)
    xhat = xc * rs
    return xhat * g + b, xhat, rs


def _layernorm_bwd(dout, xhat, rs, g):
    dxh = dout * g
    return rs * (dxh - jnp.mean(dxh, axis=-1, keepdims=True) - xhat * jnp.mean(dxh * xhat, axis=-1, keepdims=True))


def gmlp_fwd(name, zp, ln_g, ln_b, ws, bfull):
    T, E2 = zp.shape
    E = E2 // 2
    G = E // CHUNK
    tr = 2 * CHUNK if T % (2 * CHUNK) == 0 else CHUNK

    def body(zp_ref, g_ref, b_ref, ws_ref, bf_ref, p_ref):
        for ch in range(tr // CHUNK):
            rs_ = slice(ch * CHUNK, (ch + 1) * CHUNK)
            u, _ = _gelu(zp_ref[rs_, 0:E].astype(F32))
            v, _ = _gelu(zp_ref[rs_, E:E2].astype(F32))
            vn, _, _ = _layernorm_fwd(v, g_ref[...], b_ref[...])
            vnb = vn.astype(BF16)
            for gi in range(G):
                cs = slice(gi * CHUNK, (gi + 1) * CHUNK)
                vs = jnp.dot(ws_ref[gi], vnb[:, cs], preferred_element_type=F32) + bf_ref[:, cs]
                p_ref[rs_, cs] = (u[:, cs] * vs).astype(BF16)

    return pl.pallas_call(
        body, name=name, grid=(T // tr,),
        in_specs=[_rowspec(tr, E2), _fullspec((1, E)), _fullspec((1, E)), _fullspec((G, CHUNK, CHUNK)),
                  _fullspec((CHUNK, E))],
        out_specs=_rowspec(tr, E), out_shape=jax.ShapeDtypeStruct((T, E), BF16), compiler_params=_params(1),
    )(zp, ln_g, ln_b, ws, bfull)


def gmlp_bwd(name, zp, dp, ln_g, ln_b, ws, bfull):
    T, E2 = zp.shape
    E = E2 // 2
    G = E // CHUNK
    tr = 2 * CHUNK if T % (2 * CHUNK) == 0 else CHUNK
    n = T // tr

    def body(zp_ref, dp_ref, g_ref, b_ref, ws_ref, bf_ref, dz_ref, dws_ref, dbs_ref, st_ref, dvn_ref, dbf_ref):
        i = pl.program_id(0)

        @pl.when(i == 0)
        def _():
            dws_ref[...] = jnp.zeros_like(dws_ref)
            dbf_ref[...] = jnp.zeros_like(dbf_ref)

        dlg = jnp.zeros((1, E), F32)
        dlb = jnp.zeros((1, E), F32)
        for ch in range(tr // CHUNK):
            rs_ = slice(ch * CHUNK, (ch + 1) * CHUNK)
            zu = zp_ref[rs_, 0:E].astype(F32)
            zv = zp_ref[rs_, E:E2].astype(F32)
            u, tu = _gelu(zu)
            v, tv = _gelu(zv)
            vn, xhat, rs = _layernorm_fwd(v, g_ref[...], b_ref[...])
            vnb = vn.astype(BF16)
            dpf = dp_ref[rs_, :].astype(F32)
            for gi in range(G):
                cs = slice(gi * CHUNK, (gi + 1) * CHUNK)
                w = ws_ref[gi]
                vs = jnp.dot(w, vnb[:, cs], preferred_element_type=F32) + bf_ref[:, cs]
                dz_ref[rs_, cs] = (dpf[:, cs] * vs * _gelu_grad(zu[:, cs], tu[:, cs])).astype(BF16)
                dvs = dpf[:, cs] * u[:, cs]
                dvsb = dvs.astype(BF16)
                dws_ref[gi] += lax.dot_general(dvsb, vnb[:, cs], NT, preferred_element_type=F32)
                dbf_ref[:, cs] += dvs
                dvn_ref[:, cs] = lax.dot_general(w, dvsb, TN, preferred_element_type=F32)
            dvn = dvn_ref[...]
            dlg = dlg + _colsum(dvn * xhat)
            dlb = dlb + _colsum(dvn)
            dv = _layernorm_bwd(dvn, xhat, rs, g_ref[...])
            dz_ref[rs_, E:E2] = (dv * _gelu_grad(zv, tv)).astype(BF16)
        _acc_rows(st_ref, i == 0, [(0, dlg), (1, dlb)])

        @pl.when(i == n - 1)
        def _():
            for gi in range(G):
                dbs_ref[:, gi:gi + 1] = jnp.sum(dbf_ref[:, gi * CHUNK:(gi + 1) * CHUNK], axis=1, keepdims=True)

    return pl.pallas_call(
        body, name=name, grid=(n,),
        in_specs=[_rowspec(tr, E2), _rowspec(tr, E), _fullspec((1, E)), _fullspec((1, E)),
                  _fullspec((G, CHUNK, CHUNK)), _fullspec((CHUNK, E))],
        out_specs=(_rowspec(tr, E2), _fullspec((G, CHUNK, CHUNK)), _fullspec((CHUNK, G)), _fullspec((8, E))),
        out_shape=(jax.ShapeDtypeStruct((T, E2), BF16), jax.ShapeDtypeStruct((G, CHUNK, CHUNK), F32),
                   jax.ShapeDtypeStruct((CHUNK, G), F32), jax.ShapeDtypeStruct((8, E), F32)),
        scratch_shapes=[pltpu.VMEM((CHUNK, E), F32), pltpu.VMEM((CHUNK, E), F32)],
        compiler_params=_params(1),
    )(zp, dp, ln_g, ln_b, ws, bfull)


def _halo_specs(tr, T, width):
    per = tr // HALO
    last = T // HALO - 1
    prev = pl.BlockSpec((HALO, width), lambda i: (jnp.maximum(i * per - 1, 0), 0))
    nxt = pl.BlockSpec((HALO, width), lambda i: (jnp.minimum((i + 1) * per, last), 0))
    return prev, nxt


def _halo_valid(i, n, nlat):
    return jnp.logical_and(i > 0, i != nlat), jnp.logical_and(i + 1 < n, i + 1 != nlat)


def _glu(tt, D):
    a = tt[:, 0:D].astype(F32)
    g = tt[:, D:2 * D].astype(F32)
    return a * _sigmoid(g)


def conv_fwd(name, t, wdw, bdw, ln_g, ln_b, seq):
    T, D2 = t.shape
    D = D2 // 2
    tr = _row_tile(T, seq)
    n, nlat = T // tr, seq // tr
    prev, nxt = _halo_specs(tr, T, D2)

    def body(tp_ref, tc_ref, tn_ref, w_ref, b_ref, g_ref, bb_ref, y_ref, yc_ref, s_ref, buf):
        i = pl.program_id(0)
        pv, nv = _halo_valid(i, n, nlat)
        y = _glu(tc_ref[...], D)
        y_ref[...] = y
        buf[0:HALO, :] = jnp.where(pv, _glu(tp_ref[...], D), 0.0)
        buf[HALO:HALO + tr, :] = y
        buf[HALO + tr:2 * HALO + tr, :] = jnp.where(nv, _glu(tn_ref[...], D), 0.0)
        acc = jnp.zeros((tr, D), F32) + b_ref[...]
        for k in range(CONV_W):
            acc = acc + w_ref[k:k + 1, :] * buf[pl.ds(k + 1, tr), :]
        yc_ref[...] = acc
        yl, _, _ = _layernorm_fwd(acc, g_ref[...], bb_ref[...])
        s_ref[...] = (yl * _sigmoid(yl)).astype(BF16)

    return pl.pallas_call(
        body, name=name, grid=(n,),
        in_specs=[prev, _rowspec(tr, D2), nxt, _fullspec((32, D)), _fullspec((1, D)), _fullspec((1, D)),
                  _fullspec((1, D))],
        out_specs=(_rowspec(tr, D), _rowspec(tr, D), _rowspec(tr, D)),
        out_shape=(jax.ShapeDtypeStruct((T, D), F32), jax.ShapeDtypeStruct((T, D), F32),
                   jax.ShapeDtypeStruct((T, D), BF16)),
        scratch_shapes=[pltpu.VMEM((tr + 2 * HALO, D), F32)], compiler_params=_params(1),
    )(t, t, t, wdw, bdw, ln_g, ln_b)


def conv_bwd_norm(name, ds, yc, ln_g, ln_b):
    T, D = yc.shape
    tr = _tile(T, 256, 8)

    def body(ds_ref, yc_ref, g_ref, b_ref, o_ref, st_ref):
        i = pl.program_id(0)
        yl, xhat, rs = _layernorm_fwd(yc_ref[...], g_ref[...], b_ref[...])
        sig = _sigmoid(yl)
        dyl = ds_ref[...] * (sig * (1.0 + yl * (1.0 - sig)))
        dyc = _layernorm_bwd(dyl, xhat, rs, g_ref[...])
        o_ref[...] = dyc
        _acc_rows(st_ref, i == 0, [(0, _colsum(dyl * xhat)), (1, _colsum(dyl)), (2, _colsum(dyc))])

    return pl.pallas_call(
        body, name=name, grid=(T // tr,),
        in_specs=[_rowspec(tr, D), _rowspec(tr, D), _fullspec((1, D)), _fullspec((1, D))],
        out_specs=(_rowspec(tr, D), _fullspec((8, D))),
        out_shape=(jax.ShapeDtypeStruct((T, D), F32), jax.ShapeDtypeStruct((8, D), F32)),
        compiler_params=_params(1),
    )(ds, yc, ln_g, ln_b)


def conv_bwd_taps(name, dyc, y, t, wdw, seq):
    T, D = y.shape
    tr = _row_tile(T, seq)
    n, nlat = T // tr, seq // tr
    prev, nxt = _halo_specs(tr, T, D)

    def body(dp_ref, dc_ref, dn_ref, yp_ref, ycur_ref, yn_ref, t_ref, w_ref, dt_ref, dw_ref, dbuf, ybuf):
        i = pl.program_id(0)
        pv, nv = _halo_valid(i, n, nlat)
        dcur = dc_ref[...]
        dbuf[0:HALO, :] = jnp.where(pv, dp_ref[...], 0.0)
        dbuf[HALO:HALO + tr, :] = dcur
        dbuf[HALO + tr:2 * HALO + tr, :] = jnp.where(nv, dn_ref[...], 0.0)
        ybuf[0:HALO, :] = jnp.where(pv, yp_ref[...], 0.0)
        ybuf[HALO:HALO + tr, :] = ycur_ref[...]
        ybuf[HALO + tr:2 * HALO + tr, :] = jnp.where(nv, yn_ref[...], 0.0)

        @pl.when(i == 0)
        def _():
            dw_ref[...] = jnp.zeros_like(dw_ref)

        dy = jnp.zeros((tr, D), F32)
        for k in range(CONV_W):
            dw_ref[k:k + 1, :] += _colsum(dcur * ybuf[pl.ds(k + 1, tr), :])
            dy = dy + w_ref[k:k + 1, :] * dbuf[pl.ds(CONV_W - k, tr), :]
        a = t_ref[:, 0:D].astype(F32)
        sig = _sigmoid(t_ref[:, D:2 * D].astype(F32))
        dt_ref[:, 0:D] = (dy * sig).astype(BF16)
        dt_ref[:, D:2 * D] = (dy * a * sig * (1.0 - sig)).astype(BF16)

    return pl.pallas_call(
        body, name=name, grid=(n,),
        in_specs=[prev, _rowspec(tr, D), nxt, prev, _rowspec(tr, D), nxt, _rowspec(tr, 2 * D), _fullspec((32, D))],
        out_specs=(_rowspec(tr, 2 * D), _fullspec((32, D))),
        out_shape=(jax.ShapeDtypeStruct((T, 2 * D), BF16), jax.ShapeDtypeStruct((32, D), F32)),
        scratch_shapes=[pltpu.VMEM((tr + 2 * HALO, D), F32), pltpu.VMEM((tr + 2 * HALO, D), F32)],
        compiler_params=_params(1),
    )(dyc, dyc, dyc, y, y, y, t, wdw)


Q_ROWS = 8
K_ROWS = 16
QN = Q_ROWS * GRID_W
WIN = K_ROWS * GRID_W
NA_Q_OFFSET = (0, 4, 8)


def _na_window_start(cls, qi):
    return (max(qi - NA_ROWS // 2, 0), qi, min(qi + NA_ROWS // 2, NA_ROWS))[cls]


def _na_pairs(cls):
    for qi in range(Q_ROWS):
        lo = _na_window_start(cls, qi)
        for kj in range(lo, lo + NA_ROWS):
            yield qi, kj, kj - NA_Q_OFFSET[cls] - qi + NA_ROWS - 1


def na_bias_tables(toep):
    H = toep.shape[0]
    neg = jnp.full((H, GRID_W, GRID_W), NEG_INF, F32)
    tables = []
    for cls in range(3):
        dr_of = {(qi, kj): dr for qi, kj, dr in _na_pairs(cls)}
        rows = [jnp.concatenate([toep[:, dr_of[(qi, kj)]] if (qi, kj) in dr_of else neg for kj in range(K_ROWS)],
                                axis=2) for qi in range(Q_ROWS)]
        tables.append(jnp.concatenate(rows, axis=1))
    return jnp.stack(tables)


def _na_class(rb, nblk):
    return jnp.where(rb == 0, 0, jnp.where(rb == nblk - 1, 2, 1))


def _na_specs(seq, ctx_rows, H):
    cb = seq // ctx_rows
    nblk = seq // QN
    return [
        pl.BlockSpec((QN, HEAD), lambda h, r: (r, h)),
        pl.BlockSpec((seq, HEAD), lambda h, r: (0, H + h)),
        pl.BlockSpec((seq, HEAD), lambda h, r: (0, 2 * H + h)),
        pl.BlockSpec((ctx_rows, HEAD), lambda h, r: (cb, H + h)),
        pl.BlockSpec((ctx_rows, HEAD), lambda h, r: (cb, 2 * H + h)),
        pl.BlockSpec((None, None, QN, WIN), lambda h, r: (_na_class(r, nblk), h, 0, 0)),
    ]


def _na_scores(q_ref, k_ref, v_ref, kc_ref, vc_ref, b_ref, rows):
    rb = pl.program_id(1)
    k_start = jnp.clip(Q_ROWS * rb - NA_ROWS // 2, 0, rows - K_ROWS)
    start = pl.multiple_of(k_start * GRID_W, GRID_W)
    scale = HEAD ** -0.5
    q = q_ref[...]
    kw = k_ref[pl.ds(start, WIN), :]
    vw = v_ref[pl.ds(start, WIN), :]
    kc = kc_ref[...]
    vc = vc_ref[...]
    s = lax.dot_general(q, kw, NT, preferred_element_type=F32) * scale + b_ref[...]
    sc = lax.dot_general(q, kc, NT, preferred_element_type=F32) * scale
    m = jnp.maximum(jnp.max(s, axis=-1, keepdims=True), jnp.max(sc, axis=-1, keepdims=True))
    p = jnp.exp(s - m)
    pc = jnp.exp(sc - m)
    l = jnp.sum(p, axis=-1, keepdims=True) + jnp.sum(pc, axis=-1, keepdims=True)
    return q, kw, vw, kc, vc, p, pc, l, start, scale


def na_fwd(name, qkv, bias, seq, tasks=()):
    T, D3 = qkv.shape
    D = D3 // 3
    H = D // HEAD
    rows = seq // GRID_W

    def body(q_ref, k_ref, v_ref, kc_ref, vc_ref, b_ref, o_ref):
        q, kw, vw, kc, vc, p, pc, l, start, scale = _na_scores(q_ref, k_ref, v_ref, kc_ref, vc_ref, b_ref, rows)
        o = (jnp.dot(p.astype(BF16), vw, preferred_element_type=F32)
             + jnp.dot(pc.astype(BF16), vc, preferred_element_type=F32))
        o_ref[...] = (o / l).astype(BF16)

    outs, touts = host_call(name, body, (H, seq // QN), [qkv, qkv, qkv, qkv, qkv, bias],
                            _na_specs(seq, T - seq, H), [jax.ShapeDtypeStruct((seq, D), BF16)],
                            [pl.BlockSpec((QN, HEAD), lambda h, r: (r, h))], tasks=tasks)
    return outs[0], touts


def na_bwd(name, qkv, bias, do, seq):
    T, D3 = qkv.shape
    D = D3 // 3
    H = D // HEAD
    rows = seq // GRID_W
    nblk = seq // QN
    ctx_rows = T - seq

    def body(q_ref, k_ref, v_ref, kc_ref, vc_ref, b_ref, do_ref, dq_ref, dk_ref, dv_ref, dkc_ref, dvc_ref, db_ref):
        rb = pl.program_id(1)

        @pl.when(rb == 0)
        def _():
            for ref in (dk_ref, dv_ref, dkc_ref, dvc_ref):
                ref[...] = jnp.zeros_like(ref)

        @pl.when(jnp.logical_or(rb <= 1, rb == nblk - 1))
        def _():
            db_ref[...] = jnp.zeros_like(db_ref)

        q, kw, vw, kc, vc, p, pc, l, start, scale = _na_scores(q_ref, k_ref, v_ref, kc_ref, vc_ref, b_ref, rows)
        inv = 1.0 / l
        pn = p * inv
        pcn = pc * inv
        do_ = do_ref[...]
        dp = lax.dot_general(do_, vw, NT, preferred_element_type=F32)
        dpc = lax.dot_general(do_, vc, NT, preferred_element_type=F32)
        delta = jnp.sum(pn * dp, axis=-1, keepdims=True) + jnp.sum(pcn * dpc, axis=-1, keepdims=True)
        ds = pn * (dp - delta)
        dsc = pcn * (dpc - delta)
        db_ref[...] += ds
        dsb = (ds * scale).astype(BF16)
        dscb = (dsc * scale).astype(BF16)
        dq = jnp.dot(dsb, kw, preferred_element_type=F32) + jnp.dot(dscb, kc, preferred_element_type=F32)
        dq_ref[...] = dq.astype(BF16)
        dk_ref[pl.ds(start, WIN), :] += lax.dot_general(dsb, q, TN, preferred_element_type=F32)
        dv_ref[pl.ds(start, WIN), :] += lax.dot_general(pn.astype(BF16), do_, TN, preferred_element_type=F32)
        dkc_ref[...] += lax.dot_general(dscb, q, TN, preferred_element_type=F32)
        dvc_ref[...] += lax.dot_general(pcn.astype(BF16), do_, TN, preferred_element_type=F32)

    head_lat = pl.BlockSpec((seq, HEAD), lambda h, r: (0, h))
    head_ctx = pl.BlockSpec((ctx_rows, HEAD), lambda h, r: (0, h))
    return pl.pallas_call(
        body, name=name, grid=(H, nblk),
        in_specs=_na_specs(seq, ctx_rows, H) + [pl.BlockSpec((QN, HEAD), lambda h, r: (r, h))],
        out_specs=(pl.BlockSpec((QN, HEAD), lambda h, r: (r, h)), head_lat, head_lat, head_ctx, head_ctx,
                   pl.BlockSpec((None, None, QN, WIN), lambda h, r: (_na_class(r, nblk), h, 0, 0))),
        out_shape=(jax.ShapeDtypeStruct((seq, D), BF16), jax.ShapeDtypeStruct((seq, D), F32),
                   jax.ShapeDtypeStruct((seq, D), F32), jax.ShapeDtypeStruct((ctx_rows, D), F32),
                   jax.ShapeDtypeStruct((ctx_rows, D), F32), jax.ShapeDtypeStruct(bias.shape, F32)),
        compiler_params=_params(2),
    )(qkv, qkv, qkv, qkv, qkv, bias, do)


def _ctx_specs(seq, ctx_rows, H):
    cb = seq // ctx_rows
    return [pl.BlockSpec((ctx_rows, HEAD), lambda h: (cb, h)),
            pl.BlockSpec((ctx_rows, HEAD), lambda h: (cb, H + h)),
            pl.BlockSpec((ctx_rows, HEAD), lambda h: (cb, 2 * H + h))]


def _ctx_probs(q_ref, k_ref):
    s = lax.dot_general(q_ref[...], k_ref[...], NT, preferred_element_type=F32) * (HEAD ** -0.5)
    p = jnp.exp(s - jnp.max(s, axis=-1, keepdims=True))
    return p / jnp.sum(p, axis=-1, keepdims=True)


def ctx_attn_fwd(name, qkv, seq):
    T, D3 = qkv.shape
    D = D3 // 3
    H = D // HEAD
    ctx_rows = T - seq

    def body(q_ref, k_ref, v_ref, o_ref):
        p = _ctx_probs(q_ref, k_ref)
        o_ref[...] = jnp.dot(p.astype(BF16), v_ref[...], preferred_element_type=F32).astype(BF16)

    return pl.pallas_call(
        body, name=name, grid=(H,), in_specs=_ctx_specs(seq, ctx_rows, H),
        out_specs=pl.BlockSpec((ctx_rows, HEAD), lambda h: (0, h)),
        out_shape=jax.ShapeDtypeStruct((ctx_rows, D), BF16), compiler_params=_params(1),
    )(qkv, qkv, qkv)


def ctx_attn_bwd(name, qkv, do, dkc_lat, dvc_lat, seq):
    T, D3 = qkv.shape
    D = D3 // 3
    H = D // HEAD
    ctx_rows = T - seq
    cb = seq // ctx_rows
    scale = HEAD ** -0.5

    def body(q_ref, k_ref, v_ref, do_ref, dkl_ref, dvl_ref, dq_ref, dk_ref, dv_ref):
        p = _ctx_probs(q_ref, k_ref)
        do_ = do_ref[...]
        dp = lax.dot_general(do_, v_ref[...], NT, preferred_element_type=F32)
        ds = p * (dp - jnp.sum(p * dp, axis=-1, keepdims=True))
        dsb = (ds * scale).astype(BF16)
        dq_ref[...] = jnp.dot(dsb, k_ref[...], preferred_element_type=F32).astype(BF16)
        dk_ref[...] = (dkl_ref[...] + lax.dot_general(dsb, q_ref[...], TN, preferred_element_type=F32)).astype(BF16)
        dv_ref[...] = (dvl_ref[...]
                       + lax.dot_general(p.astype(BF16), do_, TN, preferred_element_type=F32)).astype(BF16)

    blk = pl.BlockSpec((ctx_rows, HEAD), lambda h: (0, h))
    shp = jax.ShapeDtypeStruct((ctx_rows, D), BF16)
    return pl.pallas_call(
        body, name=name, grid=(H,),
        in_specs=_ctx_specs(seq, ctx_rows, H) + [pl.BlockSpec((ctx_rows, HEAD), lambda h: (cb, h)), blk, blk],
        out_specs=(blk, blk, blk), out_shape=(shp, shp, shp), compiler_params=_params(1),
    )(qkv, qkv, qkv, do, dkc_lat, dvc_lat)


def _rpb_tables():
    qc = jnp.arange(GRID_W)[:, None]
    kc = jnp.arange(GRID_W)[None, :]
    rel = (kc - qc + NA_COLS - 1).reshape(1, GRID_W * GRID_W)
    onehot = (rel == jnp.arange(32)[:, None]).astype(F32)
    c_start = jnp.clip(qc - NA_COLS // 2, 0, GRID_W - NA_COLS)
    mask = jnp.logical_and(kc >= c_start, kc < c_start + NA_COLS).astype(F32).reshape(1, GRID_W * GRID_W)
    return onehot, mask


def rpb_expand(name, rpb2, onehot, mask):
    R = rpb2.shape[0]

    def body(r_ref, oh_ref, m_ref, o_ref):
        t = jnp.dot(r_ref[...], oh_ref[...], preferred_element_type=F32, precision=lax.Precision.HIGHEST)
        o_ref[...] = jnp.where(m_ref[...] > 0.5, t, NEG_INF)

    return pl.pallas_call(body, name=name, out_shape=jax.ShapeDtypeStruct((R, GRID_W * GRID_W), F32),
                          compiler_params=_params())(rpb2, onehot, mask)


def rpb_fold(name, x, classes):
    H = x.shape[0]
    n_dr = 2 * NA_ROWS - 1

    def body(x_ref, y_ref):
        acc = [None] * n_dr
        for cls in classes:
            for qi, kj, dr in _na_pairs(cls):
                acc[dr] = x_ref[cls, qi, kj] if acc[dr] is None else acc[dr] + x_ref[cls, qi, kj]
        for dr in range(n_dr):
            y_ref[dr] = acc[dr]

    return pl.pallas_call(
        body, name=name, grid=(H,),
        in_specs=[pl.BlockSpec((None, 3, Q_ROWS, K_ROWS, GRID_W, GRID_W), lambda h: (h, 0, 0, 0, 0, 0))],
        out_specs=pl.BlockSpec((None, n_dr, GRID_W, GRID_W), lambda h: (h, 0, 0, 0)),
        out_shape=jax.ShapeDtypeStruct((H, n_dr, GRID_W, GRID_W), F32), compiler_params=_params(1),
    )(x)


def rpb_reduce(name, y2, onehot_t):
    R = y2.shape[0]

    def body(y_ref, oh_ref, o_ref):
        o_ref[...] = jnp.dot(y_ref[...], oh_ref[...], preferred_element_type=F32, precision=lax.Precision.HIGHEST)

    return pl.pallas_call(body, name=name, out_shape=jax.ShapeDtypeStruct((R, 32), F32),
                          compiler_params=_params())(y2, onehot_t)


def _adam(g, w, m, v):
    m2 = ADAM_B1 * m + (1.0 - ADAM_B1) * g
    v2 = ADAM_B2 * v + (1.0 - ADAM_B2) * (g * g)
    m_hat = m2 / (1.0 - ADAM_B1 ** ADAM_STEP)
    v_hat = v2 / (1.0 - ADAM_B2 ** ADAM_STEP)
    delta = -ADAM_LR * (m_hat / (jnp.sqrt(v_hat) + ADAM_EPS) + ADAM_WD * w)
    return delta, m2, v2


def adam_parts(name, land, land2, w, m, v, tasks=()):
    shape = w.shape
    C = shape[-1]
    R = math.prod(shape[:-1])
    tr = _tile(R, max(16, (256 * 1024 // C) // 16 * 16), 16)
    l1, l2 = land.reshape(N_SHARD, R, C), land2.reshape(N_SHARD, R, C)

    def body(l1_ref, l2_ref, w_ref, m_ref, v_ref, g_ref, d_ref, m2_ref, v2_ref):
        a = l1_ref[0].astype(F32)
        b = l2_ref[0].astype(F32)
        for k in range(1, N_SHARD):
            a = a + l1_ref[k].astype(F32)
            b = b + l2_ref[k].astype(F32)
        g = a + b
        g_ref[...] = g
        d_ref[...], m2_ref[...], v2_ref[...] = _adam(g, w_ref[...], m_ref[...], v_ref[...])

    part = pl.BlockSpec((N_SHARD, tr, C), lambda i: (0, i, 0))
    row = _rowspec(tr, C)
    shp = jax.ShapeDtypeStruct((R, C), F32)
    outs, touts = host_call(name, body, (R // tr,), [l1, l2, w.reshape(R, C), m.reshape(R, C), v.reshape(R, C)],
                            [part, part, row, row, row], (shp,) * 4, (row,) * 4, tasks=tasks)
    return tuple(o.reshape(shape) for o in outs), touts


def adam_flat(name, g, w, m, v):
    R, C = g.shape
    tr = _tile(R, 256, 8)

    def body(g_ref, w_ref, m_ref, v_ref, d_ref, m2_ref, v2_ref):
        d_ref[...], m2_ref[...], v2_ref[...] = _adam(g_ref[...], w_ref[...], m_ref[...], v_ref[...])

    row = _rowspec(tr, C)
    shp = jax.ShapeDtypeStruct((R, C), F32)
    return pl.pallas_call(body, name=name, grid=(R // tr,), in_specs=[row] * 4, out_specs=(row,) * 3,
                          out_shape=(shp,) * 3, compiler_params=_params(1))(g, w, m, v)


def reduce_8(name, gathered):
    _, R, D = gathered.shape
    tr = _tile(R, 64, 8)

    def body(x_ref, o_ref):
        acc = x_ref[0]
        for k in range(1, 8):
            acc = acc + x_ref[k]
        o_ref[...] = acc

    return pl.pallas_call(
        body, name=name, grid=(R // tr,), in_specs=[pl.BlockSpec((8, tr, D), lambda i: (0, i, 0))],
        out_specs=_rowspec(tr, D), out_shape=jax.ShapeDtypeStruct((R, D), F32), compiler_params=_params(1),
    )(gathered)


def ada_fwd(name, craw16, ada_w, ada_b3):
    L, D, Cs = ada_w.shape
    tn = _tile(Cs, 512)

    def body(c_ref, w_ref, b_ref, o_ref):
        cc = c_ref[...]
        s = cc * _sigmoid(cc)
        o_ref[...] = jnp.dot(s, w_ref[...], preferred_element_type=F32,
                             precision=lax.Precision.HIGHEST) + b_ref[...]

    return pl.pallas_call(
        body, name=name, grid=(L, Cs // tn),
        in_specs=[pl.BlockSpec((16, D), lambda l, j: (0, 0)), pl.BlockSpec((None, D, tn), lambda l, j: (l, 0, j)),
                  pl.BlockSpec((None, 1, tn), lambda l, j: (l, 0, j))],
        out_specs=pl.BlockSpec((None, 16, tn), lambda l, j: (l, 0, j)),
        out_shape=jax.ShapeDtypeStruct((L, 16, Cs), F32), compiler_params=_params(2),
    )(craw16, ada_w, ada_b3)


def ada_bwd_adam(name, craw16_t, dm16, dmc8, w, m, v):
    L, D, Cs = w.shape
    tn = _tile(Cs, 256)

    def body(c_ref, dm_ref, dc_ref, w_ref, m_ref, v_ref, g_ref, d_ref, m2_ref, v2_ref, ds_ref):
        step = pl.program_id(0) * (Cs // tn) + pl.program_id(1)
        cc = c_ref[...]
        s_t = cc * _sigmoid(cc)
        g = jnp.dot(s_t, dm_ref[...], preferred_element_type=F32, precision=lax.Precision.HIGHEST)
        ww = w_ref[...]
        g_ref[...] = g
        d_ref[...], m2_ref[...], v2_ref[...] = _adam(g, ww, m_ref[...], v_ref[...])

        @pl.when(step == 0)
        def _():
            ds_ref[...] = jnp.zeros_like(ds_ref)

        ds_ref[...] += lax.dot_general(dc_ref[...], ww, NT, preferred_element_type=F32,
                                       precision=lax.Precision.HIGHEST)

    wspec = pl.BlockSpec((None, D, tn), lambda l, j: (l, 0, j))
    shp = jax.ShapeDtypeStruct((L, D, Cs), F32)
    return pl.pallas_call(
        body, name=name, grid=(L, Cs // tn),
        in_specs=[pl.BlockSpec((D, 16), lambda l, j: (0, 0)), pl.BlockSpec((None, 16, tn), lambda l, j: (l, 0, j)),
                  pl.BlockSpec((None, 8, tn), lambda l, j: (l, 0, j)), wspec, wspec, wspec],
        out_specs=(wspec, wspec, wspec, wspec, pl.BlockSpec((8, D), lambda l, j: (0, 0))),
        out_shape=(shp, shp, shp, shp, jax.ShapeDtypeStruct((8, D), F32)), compiler_params=_params(2),
    )(craw16_t, dm16, dmc8, w, m, v)


def cctx_adam(name, ds_all, c_ctx, m, v):
    D = c_ctx.shape[1]

    def body(ds_ref, c_ref, m_ref, v_ref, g_ref, d_ref, m2_ref, v2_ref):
        ds = ds_ref[0, 0:1, :]
        for slot in (2, 4, 6):
            ds = ds + ds_ref[slot, 0:1, :]
        cc = c_ref[...]
        sig = _sigmoid(cc)
        g = ds * (sig * (1.0 + cc * (1.0 - sig)))
        g_ref[...] = g
        d_ref[...], m2_ref[...], v2_ref[...] = _adam(g, cc, m_ref[...], v_ref[...])

    shp = jax.ShapeDtypeStruct((1, D), F32)
    return pl.pallas_call(body, name=name, out_shape=(shp,) * 4, compiler_params=_params())(ds_all, c_ctx, m, v)


WEIGHT_NAMES = ['c_ctx', 'ada_w', 'ada_b', 'g_mix', 'g_ffn', 'ffn_w1', 'ffn_w3', 'ffn_w2', 'a_w_in', 'a_ln_g',
                'a_ln_b', 'a_w_s', 'a_b_s', 'a_w_out', 'b_w_qkv', 'b_rpb', 'b_w_out', 'c_w_pw1', 'c_w_dw', 'c_b_dw',
                'c_ln_g', 'c_ln_b', 'c_w_pw2', 'g_final']
BIG_NAMES = ['ffn_w1', 'ffn_w3', 'ffn_w2', 'a_w_in', 'a_w_out', 'b_w_qkv', 'b_w_out', 'c_w_pw1', 'c_w_pw2']
SMALL_NAMES = ['ada_b', 'g_mix', 'g_ffn', 'a_ln_g', 'a_ln_b', 'a_w_s', 'a_b_s', 'b_rpb', 'c_w_dw', 'c_b_dw',
               'c_ln_g', 'c_ln_b', 'g_final']
SMALL_PACK_COLS = 512
MIXER_IN = ('a_w_in', 'b_w_qkv', 'c_w_pw1')
MIXER_OUT = ('a_w_out', 'b_w_out', 'c_w_pw2')

FWD_PLAN = {
    "pre": [("a_w_in", 0)],
    "in_0": [("a_w_out", 0), ("ffn_w1", 0)],
    "out_0": [("ffn_w3", 0)],
    "ffn_up_0": [("ffn_w2", 0), ("b_w_qkv", 0)],
    "ffn_down_0": [("b_w_out", 0), ("ffn_w1", 1)],
    "in_1": [("ffn_w3", 1)],
    "b_na_1": [("ffn_w2", 1), ("c_w_pw1", 0), ("c_w_pw2", 0)],
    "out_1": [("ffn_w1", 2)],
    "ffn_up_1": [("ffn_w3", 2), ("ffn_w2", 2)],
    "ffn_down_1": [("a_w_in", 1), ("a_w_out", 1)],
    "in_2": [("ffn_w1", 3)],
    "ffn_up_2": [("ffn_w3", 3), ("ffn_w2", 3)],
}


def _bwd_plan():
    plan = {}
    for i in range(N_LAYERS):
        w_in, w_out = (MIXER_IN[i % 3], i // 3), (MIXER_OUT[i % 3], i // 3)
        if i + 1 < N_LAYERS:
            plan[f"ffn_down_dx_{i}"] = [("forward", MIXER_IN[(i + 1) % 3], (i + 1) // 3)]
        plan[f"ffn_w1_dw_{i}"] = [("scatter", "ffn_w2", i)]
        plan[f"ffn_w3_dw_{i}"] = [("forward", "ffn_w2", i), ("scatter", "ffn_w1", i)]
        plan[f"ffn_up_dx_{i}"] = [("forward", "ffn_w1", i), ("scatter", "ffn_w3", i)]
        plan[f"out_dx_{i}"] = [("forward", "ffn_w3", i)]
        plan[f"in_dw_{i}"] = [("scatter",) + w_out]
        plan[f"in_dx_{i}"] = [("forward",) + w_out, ("scatter",) + w_in]
    plan["rs_post"] = [("forward", MIXER_IN[0], 0)]
    return plan


BWD_PLAN = _bwd_plan()


def _pad_rows(a, mult):
    r = (-a.shape[0]) % mult
    return a if r == 0 else jnp.concatenate([a, jnp.zeros((r,) + a.shape[1:], a.dtype)], axis=0)


def _rows_of(flat, D):
    n = flat.shape[0]
    r = -(-n // D)
    return jnp.concatenate([flat, jnp.zeros((r * D - n,), flat.dtype)]).reshape(r, D)


def _step(W, Mo, Vo, x, c, ctx, loss_target):
    seq, D = x.shape[1], x.shape[2]
    ctx_rows = ctx.shape[1]
    T = seq + ctx_rows
    L = N_LAYERS
    H = D // HEAD
    G = D // CHUNK
    xi, yi, ci = _xyc()
    e_idx = 4 * xi + 2 * yi + ci
    s_idx = 2 * xi + yi

    c_all = all_gather_8("ag_c", c)
    craw16 = jnp.concatenate([c_all.reshape(8, D), W['c_ctx'].reshape(1, D), jnp.zeros((7, D), F32)], axis=0)
    ada_w = W['ada_w']
    Cs = ada_w.shape[2]
    ada_b_s = lax.dynamic_slice_in_dim(W['ada_b'], s_idx * Cs, Cs, axis=1).reshape(L, 1, Cs)
    mod_s = ada_fwd("ada_fwd", craw16, ada_w, ada_b_s)
    mod_g = all_gather_xy("ag_mod", mod_s).transpose(1, 2, 0, 3).reshape(L, 16, N_SHARD * Cs)
    mod_lat = lax.dynamic_index_in_dim(mod_g, e_idx, axis=1, keepdims=False).reshape(L, 6, D)
    mod_all = jnp.concatenate([mod_lat, mod_g[:, 8].reshape(L, 6, D), jnp.zeros((L, 4, D), F32)], axis=1)

    Wg = {}
    land = {n: lax.empty((N_SHARD,) + W[n].shape, BF16) for n in BIG_NAMES}
    land2 = {n: lax.empty((N_SHARD,) + W[n].shape, BF16) for n in BIG_NAMES}
    dW = {}

    def gather_tasks(host):
        return [GatherTask(W[n][l].astype(BF16)) for n, l in FWD_PLAN.get(host, ())]

    def gathered(host, touts):
        for (n, l), out in zip(FWD_PLAN.get(host, ()), touts):
            Wg[(n, l)] = out[0]

    def scatter_tasks(host):
        tasks = []
        for kind, n, l in BWD_PLAN.get(host, ()):
            if kind == "scatter":
                tasks.append(ScatterTask(dW[(n, l)], land[n], land2[n], l))
            else:
                tasks.append(ForwardTask(land[n], land2[n], l))
        return tasks

    def scattered(host, touts):
        for (kind, n, l), out in zip(BWD_PLAN.get(host, ()), touts):
            land[n], land2[n] = out

    gathered("pre", comm_only("ag_pre", gather_tasks("pre")))

    n_a, n_c = W['a_ln_g'].shape[0], W['c_ln_g'].shape[0]
    sh_rows = jnp.concatenate([W['a_ln_g'], W['a_ln_b'], W['c_w_dw'].reshape(n_c * CONV_W, -1), W['c_b_dw'],
                               W['c_ln_g'], W['c_ln_b']], axis=0)
    n_sh = sh_rows.shape[0]
    sh_full = all_gather_xy("ag_small", _pad_rows(sh_rows, 8)).transpose(1, 0, 2).reshape(-1, D)[:n_sh]
    o = 0
    a_ln_g_f, o = sh_full[o:o + n_a], o + n_a
    a_ln_b_f, o = sh_full[o:o + n_a], o + n_a
    c_w_dw_f, o = sh_full[o:o + n_c * CONV_W].reshape(n_c, CONV_W, D), o + n_c * CONV_W
    c_b_dw_f, o = sh_full[o:o + n_c], o + n_c
    c_ln_g_f, o = sh_full[o:o + n_c], o + n_c
    c_ln_b_f, o = sh_full[o:o + n_c], o + n_c

    onehot, colmask = _rpb_tables()
    n_dr = 2 * NA_ROWS - 1
    rpb2 = jnp.pad(W['b_rpb'][0].reshape(H * n_dr, 2 * NA_COLS - 1), ((0, 0), (0, 1)))
    toep = rpb_expand("rpb_expand", rpb2, onehot, colmask).reshape(H, n_dr, GRID_W, GRID_W)
    bias = na_bias_tables(toep)
    na_classes = (0, 1, 2) if seq // QN > 2 else (0, 2)

    def mixer_params(i):
        mixer, j = i % 3, i // 3
        if mixer == 0:
            return dict(ln_g=a_ln_g_f[j:j + 1], ln_b=a_ln_b_f[j:j + 1], ws=W['a_w_s'][j].astype(BF16),
                        bfull=jnp.repeat(W['a_b_s'][j].T, CHUNK, axis=1))
        if mixer == 2:
            return dict(wdw=_pad_rows(c_w_dw_f[j], 32), bdw=c_b_dw_f[j:j + 1], ln_g=c_ln_g_f[j:j + 1],
                        ln_b=c_ln_b_f[j:j + 1])
        return {}

    def fwd(fn, host, *args):
        res, touts = fn(host, *args, tasks=gather_tasks(host))
        gathered(host, touts)
        return res

    def bwd(fn, host, *args, **kw):
        res, touts = fn(host, *args, tasks=scatter_tasks(host), **kw)
        scattered(host, touts)
        return res

    h = jnp.concatenate([x[0], ctx[0]], axis=0)
    saved = []
    for i in range(L):
        mixer, j = i % 3, i // 3
        n_in, n_out = MIXER_IN[mixer], MIXER_OUT[mixer]
        if i == L - 1:
            h = h[:seq]
        Ti = h.shape[0]
        tm = _tile(Ti, 768)
        tmh = _tile(Ti, 384)
        mod = mod_all[i]
        mp = mixer_params(i)
        s = dict(h0=h, mp=mp)
        hm = nm_fwd(f"nm1_{i}", h, W['g_mix'][i:i + 1], mod, 0, 1, seq)
        s['hm'] = hm
        u = fwd(mm_cols, f"in_{i}", hm, Wg[(n_in, j)], BF16, tm)
        if mixer == 0:
            p = gmlp_fwd(f"a_mid_{i}", u, mp['ln_g'], mp['ln_b'], mp['ws'], mp['bfull'])
        elif mixer == 1:
            p = jnp.concatenate([fwd(na_fwd, f"b_na_{i}", u, bias, seq), ctx_attn_fwd(f"b_ctx_{i}", u, seq)], axis=0)
        else:
            y, yc, p = conv_fwd(f"c_mid_{i}", u, mp['wdw'], mp['bdw'], mp['ln_g'], mp['ln_b'], seq)
            s.update(y=y, yc=yc)
        s.update(u=u, p=p)
        m1, h = fwd(mm_rows_residual, f"out_{i}", p, Wg[(n_out, j)], h, mod, 2, seq, tm)
        s.update(m1=m1, h1=h)
        hf = nm_fwd(f"nm2_{i}", h, W['g_ffn'][i:i + 1], mod, 3, 4, seq)
        a, b, act = fwd(mm_ffn_up, f"ffn_up_{i}", hf, Wg[('ffn_w1', i)], Wg[('ffn_w3', i)], tmh)
        m2, h = fwd(mm_rows_residual, f"ffn_down_{i}", act, Wg[('ffn_w2', i)], h, mod, 5, seq, tmh)
        s.update(hf=hf, a=a, b=b, act=act, m2=m2)
        saved.append(s)

    dh, st_loss = loss_head("loss_head", h, loss_target[0], W['g_final'].reshape(1, D))

    dmod_lat, dmod_ctx, dmod_tot = [None] * L, [None] * L, [None] * L
    dg_mix, dg_ffn = [None] * L, [None] * L
    small = {}
    for i in reversed(range(L)):
        mixer, j = i % 3, i // 3
        n_in, n_out = MIXER_IN[mixer], MIXER_OUT[mixer]
        s = saved[i]
        mp = s['mp']
        mod = mod_all[i]
        if i == L - 2:
            dh = jnp.concatenate([dh, jnp.zeros((ctx_rows, D), F32)], axis=0)
        Ti = dh.shape[0]
        tm = _tile(Ti, 768)
        tmh = _tile(Ti, 384)
        dm2, st_g2 = gate_bwd(f"gate2_bwd_{i}", dh, s['m2'], mod, 5, seq)
        da, db = bwd(mm_rows_dgrad, f"ffn_down_dx_{i}", dm2, Wg[('ffn_w2', i)], BF16, tm, ffn_ab=(s['a'], s['b']))
        dW[('ffn_w2', i)] = bwd(mm_wgrad_rows, f"ffn_w2_dw_{i}", s['act'], dm2, tm)
        dW[('ffn_w1', i)] = bwd(mm_wgrad_cols, f"ffn_w1_dw_{i}", s['hf'], da, tm)
        dW[('ffn_w3', i)] = bwd(mm_wgrad_cols, f"ffn_w3_dw_{i}", s['hf'], db, tm)
        dhf = bwd(mm_cols_dgrad, f"ffn_up_dx_{i}", [da, db], [Wg[('ffn_w1', i)], Wg[('ffn_w3', i)]], tmh)
        dh, st_n2 = nm_bwd(f"nm2_bwd_{i}", s['h1'], dhf, dh, W['g_ffn'][i:i + 1], mod, 4, seq)
        dm1, st_g1 = gate_bwd(f"gate1_bwd_{i}", dh, s['m1'], mod, 2, seq)
        dp = bwd(mm_rows_dgrad, f"out_dx_{i}", dm1, Wg[(n_out, j)], F32 if mixer == 2 else BF16, tm)[0]
        dW[(n_out, j)] = bwd(mm_wgrad_rows, f"out_dw_{i}", s['p'], dm1, tm)
        if mixer == 0:
            du, dws, dbs, st_a = gmlp_bwd(f"a_mid_bwd_{i}", s['u'], dp, mp['ln_g'], mp['ln_b'], mp['ws'], mp['bfull'])
            small[('a', j)] = (st_a[0:1], st_a[1:2], dws.reshape(-1, D), dbs.T.reshape(1, D))
        elif mixer == 1:
            dq, dk, dv, dkc, dvc, dbias = na_bwd(f"b_na_bwd_{i}", s['u'], bias, dp, seq)
            dqc, dkc, dvc = ctx_attn_bwd(f"b_ctx_bwd_{i}", s['u'], dp, dkc, dvc, seq)
            du = jnp.concatenate([jnp.concatenate([dq, dk.astype(BF16), dv.astype(BF16)], axis=1),
                                  jnp.concatenate([dqc, dkc, dvc], axis=1)], axis=0)
            blocks = dbias.reshape(3, H, Q_ROWS, GRID_W, K_ROWS, GRID_W).transpose(1, 0, 2, 4, 3, 5)
            folded = rpb_fold(f"rpb_fold_{i}", blocks, na_classes)
            drpb = rpb_reduce(f"rpb_reduce_{i}", folded.reshape(H * n_dr, GRID_W * GRID_W), onehot.T)
            small[('b', j)] = _rows_of(drpb[:, :2 * NA_COLS - 1].reshape(-1), D)
        else:
            dyc, st_c = conv_bwd_norm(f"c_norm_bwd_{i}", dp, s['yc'], mp['ln_g'], mp['ln_b'])
            du, dwdw = conv_bwd_taps(f"c_taps_bwd_{i}", dyc, s['y'], s['u'], mp['wdw'], seq)
            small[('c', j)] = (dwdw, st_c[2:3], st_c[0:1], st_c[1:2])
        dW[(n_in, j)] = bwd(mm_wgrad_cols, f"in_dw_{i}", s['hm'], du, tm)
        dhm = bwd(mm_cols_dgrad, f"in_dx_{i}", [du], [Wg[(n_in, j)]], tm)
        dh, st_n1 = nm_bwd(f"nm1_bwd_{i}", s['h0'], dhm, dh, W['g_mix'][i:i + 1], mod, 1, seq)
        dg_mix[i], dg_ffn[i] = st_n1[0:1], st_n2[0:1]
        for dst, r_n, r_g in ((dmod_lat, (1, 2), 0), (dmod_ctx, (3, 4), 1), (dmod_tot, (5, 6), 2)):
            dst[i] = jnp.concatenate([st_n1[r_n[0]:r_n[0] + 1], st_n1[r_n[1]:r_n[1] + 1], st_g1[r_g:r_g + 1],
                                      st_n2[r_n[0]:r_n[0] + 1], st_n2[r_n[1]:r_n[1] + 1], st_g2[r_g:r_g + 1]], axis=0)
    grad_x = dh[:seq].reshape(1, seq, D)

    a_parts = [small[('a', j)] for j in range(n_a)]
    c_parts = [small[('c', j)] for j in range(n_c)]
    entries = [
        ('dmod_lat', jnp.concatenate(dmod_lat, axis=0)), ('dmod_ctx', jnp.concatenate(dmod_ctx, axis=0)),
        ('ada_b', jnp.concatenate(dmod_tot, axis=0)),
        ('g_mix', jnp.concatenate(dg_mix, axis=0)), ('g_ffn', jnp.concatenate(dg_ffn, axis=0)),
        ('a_ln_g', jnp.concatenate([p[0] for p in a_parts], axis=0)),
        ('a_ln_b', jnp.concatenate([p[1] for p in a_parts], axis=0)),
        ('a_w_s', jnp.concatenate([p[2] for p in a_parts], axis=0)),
        ('a_b_s', jnp.concatenate([p[3] for p in a_parts], axis=0)),
        ('b_rpb', small[('b', 0)]),
        ('c_w_dw', jnp.concatenate([p[0] for p in c_parts], axis=0)),
        ('c_b_dw', jnp.concatenate([p[1] for p in c_parts], axis=0)),
        ('c_ln_g', jnp.concatenate([p[2] for p in c_parts], axis=0)),
        ('c_ln_b', jnp.concatenate([p[3] for p in c_parts], axis=0)),
        ('g_final', st_loss[0:1]), ('loss', st_loss[1:2]),
    ]
    offs, o = {}, 0
    for n, arr in entries:
        offs[n] = (o, arr.shape[0])
        o += arr.shape[0]
    pack = _pad_rows(jnp.concatenate([arr for _, arr in entries], axis=0), 64)
    out = {}
    first = BIG_NAMES[0]
    post = scatter_tasks("rs_post")
    out[first], touts = adam_parts("adam_" + first, land[first], land2[first], W[first], Mo[first], Vo[first],
                                   tasks=[GatherAllTask(pack)] + post)
    gathered_small = touts[0][0]
    scattered("rs_post", touts[1:])
    sums = reduce_8("reduce_small_grads", gathered_small)

    def summed(n):
        return sums[offs[n][0]:offs[n][0] + offs[n][1]]

    loss = (0.5 / D) * jnp.sum(summed('loss'))

    lo, ln_ = offs['dmod_lat']
    dm_lat = gathered_small[:, lo:lo + ln_].reshape(8, L, 6 * D).transpose(1, 0, 2)
    dm_ctx = summed('dmod_ctx').reshape(L, 1, 6 * D)
    dm16 = jnp.concatenate([dm_lat, dm_ctx, jnp.zeros((L, 7, 6 * D), F32)], axis=1)
    dm16 = lax.dynamic_slice_in_dim(dm16, s_idx * Cs, Cs, axis=2)
    dmc8 = jnp.concatenate([dm16[:, 8:9], jnp.zeros((L, 7, Cs), F32)], axis=1)
    g_ada, d_ada, m_ada, v_ada, ds_part = ada_bwd_adam("ada_bwd_adam", craw16.T, dm16, dmc8, ada_w,
                                                       Mo['ada_w'], Vo['ada_w'])
    ds_all = all_gather_8("ag_ds_ctx", ds_part)
    cc = cctx_adam("cctx_adam", ds_all, W['c_ctx'].reshape(1, D), Mo['c_ctx'].reshape(1, D),
                   Vo['c_ctx'].reshape(1, D))
    out.update({'c_ctx': tuple(t.reshape(D) for t in cc), 'ada_w': (g_ada, d_ada, m_ada, v_ada)})

    for n in BIG_NAMES[1:]:
        out[n], _ = adam_parts("adam_" + n, land[n], land2[n], W[n], Mo[n], Vo[n])

    def own_cols(full):
        w = full.shape[-1] // N_SHARD
        return lax.dynamic_slice_in_dim(full, s_idx * w, w, axis=full.ndim - 1)

    small_g = {
        'ada_b': summed('ada_b').reshape(L, 6 * D), 'g_mix': summed('g_mix'), 'g_ffn': summed('g_ffn'),
        'a_ln_g': own_cols(summed('a_ln_g')), 'a_ln_b': own_cols(summed('a_ln_b')),
        'a_w_s': summed('a_w_s').reshape(n_a, G, CHUNK, CHUNK), 'a_b_s': summed('a_b_s').reshape(n_a, G, CHUNK),
        'b_rpb': summed('b_rpb').reshape(-1)[:H * n_dr * (2 * NA_COLS - 1)].reshape(W['b_rpb'].shape),
        'c_w_dw': own_cols(summed('c_w_dw').reshape(n_c, 32, D)[:, :CONV_W]),
        'c_b_dw': own_cols(summed('c_b_dw')), 'c_ln_g': own_cols(summed('c_ln_g')),
        'c_ln_b': own_cols(summed('c_ln_b')), 'g_final': summed('g_final').reshape(D),
    }

    def packed(d):
        flat = jnp.concatenate([d[n].reshape(-1) for n in SMALL_NAMES])
        return _pad_rows(_rows_of(flat, SMALL_PACK_COLS), 8)

    res = adam_flat("adam_small", packed(small_g), packed(W), packed(Mo), packed(Vo))
    o = 0
    for n in SMALL_NAMES:
        size, shape = W[n].size, W[n].shape
        out[n] = (small_g[n],) + tuple(r.reshape(-1)[o:o + size].reshape(shape) for r in res)
        o += size

    return (loss, grad_x) + tuple(out[n][k] for k in range(4) for n in WEIGHT_NAMES)


def kernel(x, c, ctx, c_ctx, ada_w, ada_b, g_mix, g_ffn, ffn_w1, ffn_w3, ffn_w2, a_w_in, a_ln_g, a_ln_b, a_w_s, a_b_s, a_w_out, b_w_qkv, b_rpb, b_w_out, c_w_pw1, c_w_dw, c_b_dw, c_ln_g, c_ln_b, c_w_pw2, g_final, loss_target, m_c_ctx, m_ada_w, m_ada_b, m_g_mix, m_g_ffn, m_ffn_w1, m_ffn_w3, m_ffn_w2, m_a_w_in, m_a_ln_g, m_a_ln_b, m_a_w_s, m_a_b_s, m_a_w_out, m_b_w_qkv, m_b_rpb, m_b_w_out, m_c_w_pw1, m_c_w_dw, m_c_b_dw, m_c_ln_g, m_c_ln_b, m_c_w_pw2, m_g_final, v_c_ctx, v_ada_w, v_ada_b, v_g_mix, v_g_ffn, v_ffn_w1, v_ffn_w3, v_ffn_w2, v_a_w_in, v_a_ln_g, v_a_ln_b, v_a_w_s, v_a_b_s, v_a_w_out, v_b_w_qkv, v_b_rpb, v_b_w_out, v_c_w_pw1, v_c_w_dw, v_c_b_dw, v_c_ln_g, v_c_ln_b, v_c_w_pw2, v_g_final):
    W = dict(zip(WEIGHT_NAMES, (c_ctx, ada_w, ada_b, g_mix, g_ffn, ffn_w1, ffn_w3, ffn_w2, a_w_in, a_ln_g, a_ln_b, a_w_s, a_b_s, a_w_out, b_w_qkv, b_rpb, b_w_out, c_w_pw1, c_w_dw, c_b_dw, c_ln_g, c_ln_b, c_w_pw2, g_final)))
    Mo = dict(zip(WEIGHT_NAMES, (m_c_ctx, m_ada_w, m_ada_b, m_g_mix, m_g_ffn, m_ffn_w1, m_ffn_w3, m_ffn_w2, m_a_w_in, m_a_ln_g, m_a_ln_b, m_a_w_s, m_a_b_s, m_a_w_out, m_b_w_qkv, m_b_rpb, m_b_w_out, m_c_w_pw1, m_c_w_dw, m_c_b_dw, m_c_ln_g, m_c_ln_b, m_c_w_pw2, m_g_final)))
    Vo = dict(zip(WEIGHT_NAMES, (v_c_ctx, v_ada_w, v_ada_b, v_g_mix, v_g_ffn, v_ffn_w1, v_ffn_w3, v_ffn_w2, v_a_w_in, v_a_ln_g, v_a_ln_b, v_a_w_s, v_a_b_s, v_a_w_out, v_b_w_qkv, v_b_rpb, v_b_w_out, v_c_w_pw1, v_c_w_dw, v_c_b_dw, v_c_ln_g, v_c_ln_b, v_c_w_pw2, v_g_final)))
    return _step(W, Mo, Vo, x, c, ctx, loss_target)
```

```python
import functools
import math

import jax
import jax.numpy as jnp
from jax import lax
from jax.experimental import pallas as pl
from jax.experimental.pallas import tpu as pltpu

F32 = jnp.float32
BF16 = jnp.bfloat16
MESH = pl.DeviceIdType.MESH
ANY = pl.BlockSpec(memory_space=pl.ANY)

GRID_W = 64
CHUNK = 128
HEAD = 128
NA_ROWS = 8
NA_COLS = 16
CONV_W = 31
HALO = 16
EPS = 1e-6
NEG_INF = -1e30
N_LAYERS = 4
N_SHARD = 4
V7X_VMEM_BYTES = 64 * 1024 * 1024
VMEM_LIMIT = V7X_VMEM_BYTES - 6 * 1024 * 1024

ADAM_LR = 0.001
ADAM_B1 = 0.9
ADAM_B2 = 0.999
ADAM_EPS = 1e-08
ADAM_WD = 0.01
ADAM_STEP = 10

NN = (((1,), (0,)), ((), ()))
NT = (((1,), (1,)), ((), ()))
TN = (((0,), (0,)), ((), ()))


def _params(n_grid=0):
    sem = ("arbitrary",) * n_grid if n_grid else None
    return pltpu.CompilerParams(dimension_semantics=sem, vmem_limit_bytes=VMEM_LIMIT)


def _xyc():
    return lax.axis_index("x"), lax.axis_index("y"), lax.axis_index("c")


def _flip(v, f):
    return 1 - v if f else v


def _tile(n, pref, mult=128):
    if n <= pref:
        return n
    t = (pref // mult) * mult
    while t > mult and n % t:
        t -= mult
    assert n % t == 0, (n, pref, mult)
    return t


def _sigmoid(x):
    return 1.0 / (1.0 + jnp.exp(-x))


XY_FLIPS = ((1, 0), (0, 1), (1, 1))
ALL_FLIPS = tuple((fx, fy, fc) for fx in (0, 1) for fy in (0, 1) for fc in (0, 1) if fx or fy or fc)


def _remote(src, dst, ssem, rsem, dev):
    return pltpu.make_async_remote_copy(src_ref=src, dst_ref=dst, send_sem=ssem, recv_sem=rsem,
                                        device_id=dev, device_id_type=MESH)


def all_gather_xy(name, shard):
    def body(src, dst, ssem, rsem, lsem):
        x, y, c = _xyc()
        mine = pltpu.make_async_copy(src, dst.at[2 * x + y], lsem)
        mine.start()
        sends = []
        for k, (fx, fy) in enumerate(XY_FLIPS):
            cp = _remote(src, dst.at[2 * x + y], ssem.at[k], rsem.at[k], (_flip(x, fx), _flip(y, fy), c))
            cp.start()
            sends.append(cp)
        for k, (fx, fy) in enumerate(XY_FLIPS):
            px, py = _flip(x, fx), _flip(y, fy)
            _remote(src, dst.at[2 * px + py], ssem.at[k], rsem.at[k], (px, py, c)).wait_recv()
        for cp in sends:
            cp.wait_send()
        mine.wait()

    return pl.pallas_call(
        body, name=name, out_shape=jax.ShapeDtypeStruct((N_SHARD,) + shard.shape, shard.dtype),
        in_specs=[ANY], out_specs=ANY,
        scratch_shapes=[pltpu.SemaphoreType.DMA((3,)), pltpu.SemaphoreType.DMA((3,)), pltpu.SemaphoreType.DMA(())],
    )(shard)


def all_gather_8(name, blk):
    def body(src, dst, ssem, rsem, lsem):
        x, y, c = _xyc()
        me = 4 * x + 2 * y + c
        mine = pltpu.make_async_copy(src, dst.at[me], lsem)
        mine.start()
        sends = []
        for k, (fx, fy, fc) in enumerate(ALL_FLIPS):
            cp = _remote(src, dst.at[me], ssem.at[k], rsem.at[k], (_flip(x, fx), _flip(y, fy), _flip(c, fc)))
            cp.start()
            sends.append(cp)
        for k, (fx, fy, fc) in enumerate(ALL_FLIPS):
            px, py, pc = _flip(x, fx), _flip(y, fy), _flip(c, fc)
            _remote(src, dst.at[4 * px + 2 * py + pc], ssem.at[k], rsem.at[k], (px, py, pc)).wait_recv()
        for cp in sends:
            cp.wait_send()
        mine.wait()

    return pl.pallas_call(
        body, name=name, out_shape=jax.ShapeDtypeStruct((8,) + blk.shape, blk.dtype),
        in_specs=[ANY], out_specs=ANY,
        scratch_shapes=[pltpu.SemaphoreType.DMA((7,)), pltpu.SemaphoreType.DMA((7,)), pltpu.SemaphoreType.DMA(())],
    )(blk)


class GatherTask:
    n_send, n_recv, n_local = 3, 3, 1
    alias = {}

    def __init__(self, shard):
        self.ins = [shard]
        self.outs = [jax.ShapeDtypeStruct((N_SHARD,) + shard.shape, shard.dtype)]

    def _copies(self, xyc, ins, outs, ssem, rsem):
        x, y, c = xyc
        for k, (fx, fy) in enumerate(XY_FLIPS):
            px, py = _flip(x, fx), _flip(y, fy)
            send = _remote(ins[0], outs[0].at[2 * x + y], ssem.at[k], rsem.at[k], (px, py, c))
            recv = _remote(ins[0], outs[0].at[2 * px + py], ssem.at[k], rsem.at[k], (px, py, c))
            yield send, recv

    def start(self, xyc, ins, outs, ssem, rsem, lsem):
        x, y, _ = xyc
        pltpu.make_async_copy(ins[0], outs[0].at[2 * x + y], lsem.at[0]).start()
        for send, _ in self._copies(xyc, ins, outs, ssem, rsem):
            send.start()

    def finish(self, xyc, ins, outs, ssem, rsem, lsem):
        x, y, _ = xyc
        for send, recv in self._copies(xyc, ins, outs, ssem, rsem):
            recv.wait_recv()
            send.wait_send()
        pltpu.make_async_copy(ins[0], outs[0].at[2 * x + y], lsem.at[0]).wait()


class GatherAllTask:
    n_send, n_recv, n_local = 7, 7, 1
    alias = {}

    def __init__(self, blk):
        self.ins = [blk]
        self.outs = [jax.ShapeDtypeStruct((8,) + blk.shape, blk.dtype)]

    def _copies(self, xyc, ins, outs, ssem, rsem):
        x, y, c = xyc
        for k, (fx, fy, fc) in enumerate(ALL_FLIPS):
            px, py, pc = _flip(x, fx), _flip(y, fy), _flip(c, fc)
            send = _remote(ins[0], outs[0].at[4 * x + 2 * y + c], ssem.at[k], rsem.at[k], (px, py, pc))
            recv = _remote(ins[0], outs[0].at[4 * px + 2 * py + pc], ssem.at[k], rsem.at[k], (px, py, pc))
            yield send, recv

    def _local(self, xyc, ins, outs, lsem):
        x, y, c = xyc
        return pltpu.make_async_copy(ins[0], outs[0].at[4 * x + 2 * y + c], lsem.at[0])

    def start(self, xyc, ins, outs, ssem, rsem, lsem):
        self._local(xyc, ins, outs, lsem).start()
        for send, _ in self._copies(xyc, ins, outs, ssem, rsem):
            send.start()

    def finish(self, xyc, ins, outs, ssem, rsem, lsem):
        for send, recv in self._copies(xyc, ins, outs, ssem, rsem):
            recv.wait_recv()
            send.wait_send()
        self._local(xyc, ins, outs, lsem).wait()


class ScatterTask:
    n_send, n_recv, n_local = 4, 4, 1
    alias = {1: 0, 2: 1}

    def __init__(self, part, land, land2, l):
        self.ins = [part, land, land2]
        self.outs = [jax.ShapeDtypeStruct(land.shape, land.dtype), jax.ShapeDtypeStruct(land2.shape, land2.dtype)]
        self.l = l

    def _copies(self, xyc, ins, outs, ssem, rsem):
        x, y, c = xyc
        me_s = 2 * x + y
        part, land, land2 = ins[0], outs[0], outs[1]
        sib = (x, y, 1 - c)
        own = _remote(part.at[me_s], land2.at[me_s, self.l], ssem.at[3], rsem.at[3], sib)
        yield own, own
        for k, (fx, fy) in enumerate(XY_FLIPS):
            px, py = _flip(x, fx), _flip(y, fy)
            ps = 2 * px + py
            send = _remote(part.at[ps], land.at[me_s, self.l], ssem.at[k], rsem.at[k], (px, py, c))
            recv = _remote(part.at[ps], land.at[ps, self.l], ssem.at[k], rsem.at[k], (px, py, c))
            yield send, recv

    def _local(self, xyc, ins, outs, lsem):
        me_s = 2 * xyc[0] + xyc[1]
        return pltpu.make_async_copy(ins[0].at[me_s], outs[0].at[me_s, self.l], lsem.at[0])

    def start(self, xyc, ins, outs, ssem, rsem, lsem):
        self._local(xyc, ins, outs, lsem).start()
        for send, _ in self._copies(xyc, ins, outs, ssem, rsem):
            send.start()

    def finish(self, xyc, ins, outs, ssem, rsem, lsem):
        for send, recv in self._copies(xyc, ins, outs, ssem, rsem):
            recv.wait_recv()
            send.wait_send()
        self._local(xyc, ins, outs, lsem).wait()


class ForwardTask:
    n_send, n_recv, n_local = 3, 3, 0
    alias = {0: 0, 1: 1}

    def __init__(self, land, land2, l):
        self.ins = [land, land2]
        self.outs = [jax.ShapeDtypeStruct(land.shape, land.dtype), jax.ShapeDtypeStruct(land2.shape, land2.dtype)]
        self.l = l

    def _copies(self, xyc, outs, ssem, rsem):
        x, y, c = xyc
        for k, (fx, fy) in enumerate(XY_FLIPS):
            ps = 2 * _flip(x, fx) + _flip(y, fy)
            yield _remote(outs[0].at[ps, self.l], outs[1].at[ps, self.l], ssem.at[k], rsem.at[k], (x, y, 1 - c))

    def start(self, xyc, ins, outs, ssem, rsem, lsem):
        for cp in self._copies(xyc, outs, ssem, rsem):
            cp.start()

    def finish(self, xyc, ins, outs, ssem, rsem, lsem):
        for cp in self._copies(xyc, outs, ssem, rsem):
            cp.wait_recv()
            cp.wait_send()


def host_call(name, body, grid, arrays, in_specs, out_shape, out_specs, scratch=(), tasks=()):
    out_shape, out_specs, scratch = tuple(out_shape), tuple(out_specs), list(scratch)
    n_in, n_out, n_scr, n_grid = len(arrays), len(out_shape), len(scratch), len(grid)
    t_arrays, t_outs, aliases, spans, sems = [], [], {}, [], []
    for t in tasks:
        i0, o0 = len(t_arrays), len(t_outs)
        t_arrays += t.ins
        t_outs += t.outs
        for a, b in t.alias.items():
            aliases[n_in + i0 + a] = n_out + o0 + b
        spans.append((i0, len(t_arrays), o0, len(t_outs)))
        sems += [pltpu.SemaphoreType.DMA((t.n_send,)), pltpu.SemaphoreType.DMA((t.n_recv,)),
                 pltpu.SemaphoreType.DMA((max(t.n_local, 1),))]
    n_tin, n_tout = len(t_arrays), len(t_outs)

    def full_body(*refs):
        ins = refs[:n_in]
        tin = refs[n_in:n_in + n_tin]
        outs = refs[n_in + n_tin:n_in + n_tin + n_out]
        tout = refs[n_in + n_tin + n_out:n_in + n_tin + n_out + n_tout]
        rest = refs[n_in + n_tin + n_out + n_tout:]
        scr, sm = rest[:n_scr], rest[n_scr:]
        if not tasks:
            body(*ins, *outs, *scr)
            return
        pids = [pl.program_id(d) for d in range(n_grid)]
        first = functools.reduce(jnp.logical_and, [p == 0 for p in pids])
        last = functools.reduce(jnp.logical_and, [p == n - 1 for p, n in zip(pids, grid)])
        xyc = _xyc()

        def each(method):
            for k, (t, (i0, i1, o0, o1)) in enumerate(zip(tasks, spans)):
                getattr(t, method)(xyc, tin[i0:i1], tout[o0:o1], sm[3 * k], sm[3 * k + 1], sm[3 * k + 2])

        @pl.when(first)
        def _():
            each("start")

        body(*ins, *outs, *scr)

        @pl.when(last)
        def _():
            each("finish")

    res = pl.pallas_call(
        full_body, name=name, grid=grid, in_specs=list(in_specs) + [ANY] * n_tin,
        out_specs=out_specs + (ANY,) * n_tout, out_shape=out_shape + tuple(t_outs),
        scratch_shapes=scratch + sems, input_output_aliases=aliases, compiler_params=_params(n_grid),
    )(*arrays, *t_arrays)
    return tuple(res[:n_out]), [tuple(res[n_out + o0:n_out + o1]) for (_, _, o0, o1) in spans]


def comm_only(name, tasks):
    def body(i_ref, o_ref):
        o_ref[...] = i_ref[...]

    spec = pl.BlockSpec((8, 128), lambda i: (0, 0))
    _, touts = host_call(name, body, (1,), [jnp.zeros((8, 128), F32)], [spec],
                         [jax.ShapeDtypeStruct((8, 128), F32)], [spec], tasks=tasks)
    return touts


def matmul(name, grid, order, pairs, pair_dims, acc_of_pair, acc_shapes, extras, outs, epilogue, tasks=(),
           col_chunk=1 << 30):
    ni, nj, nk = grid

    def wrap(m):
        if order == "ij":
            return lambda g0, g1, k: m(g0, g1, k)
        return lambda g0, g1, k: m(g1, g0, k)

    g = (ni, nj, nk) if order == "ij" else (nj, ni, nk)
    arrays, in_specs = [], []
    for a, b in pairs:
        for arr, blk, m in (a, b):
            arrays.append(arr)
            in_specs.append(pl.BlockSpec(blk, wrap(m)))
    for arr, blk, m in extras:
        arrays.append(arr)
        in_specs.append(pl.BlockSpec(blk, wrap(m)))
    n_in = len(arrays)
    out_shape = tuple(o[0] for o in outs)
    out_specs = tuple(pl.BlockSpec(o[1], wrap(o[2])) for o in outs)
    n_pairs, n_ex, n_out, n_acc = len(pairs), len(extras), len(outs), len(acc_shapes)
    tile_n = acc_shapes[0][1]
    chunks = [slice(c0, min(c0 + col_chunk, tile_n)) for c0 in range(0, tile_n, col_chunk)]

    def body(*refs):
        ins = refs[:n_in]
        out_refs = refs[n_in:n_in + n_out]
        accs = refs[n_in + n_out:]
        pid = (pl.program_id(0), pl.program_id(1)) if order == "ij" else (pl.program_id(1), pl.program_id(0))
        k = pl.program_id(2)

        def partial(p, cs):
            b_ref = ins[2 * p + 1]
            b = b_ref[cs, :] if pair_dims[p] == NT else b_ref[:, cs]
            return lax.dot_general(ins[2 * p][...], b, pair_dims[p], preferred_element_type=F32)

        ex = ins[2 * n_pairs:2 * n_pairs + n_ex]
        if nk == 1:
            for cs in chunks:
                vals = [None] * n_acc
                for p in range(n_pairs):
                    d = partial(p, cs)
                    q = acc_of_pair[p]
                    vals[q] = d if vals[q] is None else vals[q] + d
                epilogue(vals, ex, out_refs, pid, cs)
        else:
            @pl.when(k == 0)
            def _():
                for acc in accs:
                    acc[...] = jnp.zeros_like(acc)

            for cs in chunks:
                for p in range(n_pairs):
                    accs[acc_of_pair[p]][:, cs] += partial(p, cs)

            @pl.when(k == nk - 1)
            def _():
                for cs in chunks:
                    epilogue([acc[:, cs] for acc in accs], ex, out_refs, pid, cs)

    scratch = [] if nk == 1 else [pltpu.VMEM(s, F32) for s in acc_shapes]
    return host_call(name, body, g, arrays, in_specs, out_shape, out_specs, scratch, tasks)


def _store_cast(vals, extras, out_refs, pid, cs):
    out_refs[0][:, cs] = vals[0].astype(out_refs[0].dtype)


def mm_cols(name, x, wg, out_dtype, tm, tasks=()):
    T, D = x.shape
    ns = wg.shape[2]
    tn = _tile(ns, 1536)
    nb = ns // tn
    outs, touts = matmul(
        name, (T // tm, N_SHARD * nb, 1), "ji",
        [((x, (tm, D), lambda i, jj, k: (i, 0)),
          (wg, (None, D, tn), lambda i, jj, k: (jj // nb, 0, jj % nb)))],
        [NN], [0], [(tm, tn)], [],
        [(jax.ShapeDtypeStruct((T, N_SHARD * ns), out_dtype), (tm, tn), lambda i, jj, k: (i, jj))],
        _store_cast, tasks)
    return outs[0], touts


def mm_ffn_up(name, x, w1g, w3g, tm, tasks=()):
    T, D = x.shape
    ns = w1g.shape[2]
    tn = _tile(ns, 1536)
    nb = ns // tn

    def epi(vals, extras, out_refs, pid, cs):
        a, b = vals
        out_refs[0][:, cs] = a.astype(BF16)
        out_refs[1][:, cs] = b.astype(BF16)
        out_refs[2][:, cs] = (a * _sigmoid(a) * b).astype(BF16)

    wmap = lambda i, jj, k: (jj // nb, 0, jj % nb)
    xa = (x, (tm, D), lambda i, jj, k: (i, 0))
    o = (jax.ShapeDtypeStruct((T, N_SHARD * ns), BF16), (tm, tn), lambda i, jj, k: (i, jj))
    return matmul(name, (T // tm, N_SHARD * nb, 1), "ji",
                  [(xa, (w1g, (None, D, tn), wmap)), (xa, (w3g, (None, D, tn), wmap))],
                  [NN, NN], [0, 1], [(tm, tn), (tm, tn)], [], [o, o, o], epi, tasks)


def mm_rows_residual(name, p, wg, h, mod, gate_row, seq, tm, tasks=()):
    T, kin = p.shape
    ks, D = wg.shape[1], wg.shape[2]
    tn = _tile(D, 512)

    def epi(vals, extras, out_refs, pid, cs):
        m = vals[0]
        h_ref, mod_ref = extras
        rows = pid[0] * tm + lax.broadcasted_iota(jnp.int32, (tm, 1), 0)
        gate = jnp.where(rows >= seq, mod_ref[6 + gate_row:7 + gate_row, cs], mod_ref[gate_row:gate_row + 1, cs])
        out_refs[0][:, cs] = m.astype(BF16)
        out_refs[1][:, cs] = h_ref[:, cs] + gate * m

    pairs = [((p, (tm, ks), lambda i, jj, k, s=s: (i, s)), (wg, (None, ks, tn), lambda i, jj, k, s=s: (s, 0, jj)))
             for s in range(N_SHARD)]
    omap = lambda i, jj, k: (i, jj)
    return matmul(
        name, (T // tm, D // tn, 1), "ji", pairs, [NN] * N_SHARD, [0] * N_SHARD, [(tm, tn)],
        [(h, (tm, tn), omap), (mod, (16, tn), lambda i, jj, k: (0, jj))],
        [(jax.ShapeDtypeStruct((T, D), BF16), (tm, tn), omap), (jax.ShapeDtypeStruct((T, D), F32), (tm, tn), omap)],
        epi, tasks)


def mm_rows_dgrad(name, dm, wg, out_dtype, tm, ffn_ab=None, tasks=()):
    T, D = dm.shape
    ks = wg.shape[1]
    tn = _tile(ks, 1536)
    nb = ks // tn
    omap = lambda i, jj, k: (i, jj)
    o = (jax.ShapeDtypeStruct((T, N_SHARD * ks), out_dtype), (tm, tn), omap)
    pairs = [((dm, (tm, D), lambda i, jj, k: (i, 0)),
              (wg, (None, tn, D), lambda i, jj, k: (jj // nb, jj % nb, 0)))]
    if ffn_ab is None:
        return matmul(name, (T // tm, N_SHARD * nb, 1), "ji", pairs, [NT], [0], [(tm, tn)], [], [o],
                      _store_cast, tasks)

    def epi(vals, extras, out_refs, pid, cs):
        dact = vals[0]
        a = extras[0][:, cs].astype(F32)
        b = extras[1][:, cs].astype(F32)
        sig = _sigmoid(a)
        out_refs[0][:, cs] = (dact * b * (sig * (1.0 + a * (1.0 - sig)))).astype(BF16)
        out_refs[1][:, cs] = (dact * (a * sig)).astype(BF16)

    a, b = ffn_ab
    return matmul(name, (T // tm, N_SHARD * nb, 1), "ji", pairs, [NT], [0], [(tm, tn)],
                  [(a, (tm, tn), omap), (b, (tm, tn), omap)], [o, o], epi, tasks)


def mm_cols_dgrad(name, dys, wgs, tm, tasks=()):
    T = dys[0].shape[0]
    D, ns = wgs[0].shape[1], wgs[0].shape[2]
    tk = _tile(ns, 1536)
    kb = ns // tk
    pairs = [((dy, (tm, tk), lambda i, jj, k: (i, k)),
              (wg, (None, D, tk), lambda i, jj, k: (k // kb, 0, k % kb))) for dy, wg in zip(dys, wgs)]
    outs, touts = matmul(name, (T // tm, 1, N_SHARD * kb), "ij", pairs, [NT] * len(dys), [0] * len(dys),
                         [(tm, D)], [],
                         [(jax.ShapeDtypeStruct((T, D), F32), (tm, D), lambda i, jj, k: (i, 0))], _store_cast, tasks)
    return outs[0], touts


def mm_wgrad_rows(name, p, dm, tk, tasks=()):
    T, kin = p.shape
    D = dm.shape[1]
    ks = kin // N_SHARD
    tm = _tile(ks, 1536)
    mb = ks // tm
    tn = _tile(D, 1024)
    outs, touts = matmul(
        name, (N_SHARD * mb, D // tn, T // tk), "ij",
        [((p, (tk, tm), lambda i, jj, k: (k, i)), (dm, (tk, tn), lambda i, jj, k: (k, jj)))],
        [TN], [0], [(tm, tn)], [],
        [(jax.ShapeDtypeStruct((N_SHARD, ks, D), BF16), (None, tm, tn), lambda i, jj, k: (i // mb, i % mb, jj))],
        _store_cast, tasks)
    return outs[0], touts


def mm_wgrad_cols(name, x, dy, tk, tasks=()):
    T, D = x.shape
    ns = dy.shape[1] // N_SHARD
    tm = _tile(D, 1024)
    tn = _tile(ns, 1536)
    nb = ns // tn
    outs, touts = matmul(
        name, (D // tm, N_SHARD * nb, T // tk), "ij",
        [((x, (tk, tm), lambda i, jj, k: (k, i)), (dy, (tk, tn), lambda i, jj, k: (k, jj)))],
        [TN], [0], [(tm, tn)], [],
        [(jax.ShapeDtypeStruct((N_SHARD, D, ns), BF16), (None, tm, tn), lambda i, jj, k: (jj // nb, i, jj % nb))],
        _store_cast, tasks)
    return outs[0], touts


def _row_tile(T, seq):
    if T == seq:
        return _tile(T, 256, 8)
    return math.gcd(256, math.gcd(seq, T - seq))


def _mod_row(mod_ref, row, is_ctx):
    return jnp.where(is_ctx, mod_ref[6 + row:7 + row, :], mod_ref[row:row + 1, :])


def _rowspec(tr, D):
    return pl.BlockSpec((tr, D), lambda i: (i, 0))


def _fullspec(shape):
    nd = len(shape)
    return pl.BlockSpec(shape, lambda i: (0,) * nd)


def _colsum(v):
    return jnp.sum(v, axis=0, keepdims=True)


def _acc_rows(st_ref, first, rows):
    @pl.when(first)
    def _():
        st_ref[...] = jnp.zeros_like(st_ref)
    for r, val in rows:
        st_ref[r:r + 1, :] += val


def _split_stats(is_ctx, val):
    zero = jnp.zeros_like(val)
    return jnp.where(is_ctx, zero, val), jnp.where(is_ctx, val, zero)


def nm_fwd(name, h, g, mod, r_sh, r_sc, seq):
    T, D = h.shape
    tr = _row_tile(T, seq)
    nlat = seq // tr

    def body(h_ref, g_ref, mod_ref, o_ref):
        is_ctx = pl.program_id(0) >= nlat
        x = h_ref[...]
        r = lax.rsqrt(jnp.mean(x * x, axis=-1, keepdims=True) + EPS)
        y = x * r * g_ref[...]
        o_ref[...] = (y * (1.0 + _mod_row(mod_ref, r_sc, is_ctx)) + _mod_row(mod_ref, r_sh, is_ctx)).astype(BF16)

    return pl.pallas_call(
        body, name=name, grid=(T // tr,), in_specs=[_rowspec(tr, D), _fullspec((1, D)), _fullspec((16, D))],
        out_specs=_rowspec(tr, D), out_shape=jax.ShapeDtypeStruct((T, D), BF16), compiler_params=_params(1),
    )(h, g, mod)


def nm_bwd(name, h, dhm, dres, g, mod, r_sc, seq):
    T, D = h.shape
    tr = _row_tile(T, seq)
    nlat = seq // tr

    def body(h_ref, d_ref, r_ref, g_ref, mod_ref, o_ref, st_ref):
        i = pl.program_id(0)
        is_ctx = i >= nlat
        x = h_ref[...]
        r = lax.rsqrt(jnp.mean(x * x, axis=-1, keepdims=True) + EPS)
        n = x * r
        gg = g_ref[...]
        dout = d_ref[...]
        dsh = _colsum(dout)
        dsc = _colsum(dout * (n * gg))
        dy = dout * (1.0 + _mod_row(mod_ref, r_sc, is_ctx))
        dn = dy * gg
        o_ref[...] = r_ref[...] + r * (dn - n * jnp.mean(dn * n, axis=-1, keepdims=True))
        dsh_l, dsh_c = _split_stats(is_ctx, dsh)
        dsc_l, dsc_c = _split_stats(is_ctx, dsc)
        _acc_rows(st_ref, i == 0, [(0, _colsum(dy * n)), (1, dsh_l), (2, dsc_l), (3, dsh_c), (4, dsc_c),
                                   (5, dsh), (6, dsc)])

    return pl.pallas_call(
        body, name=name, grid=(T // tr,),
        in_specs=[_rowspec(tr, D), _rowspec(tr, D), _rowspec(tr, D), _fullspec((1, D)), _fullspec((16, D))],
        out_specs=(_rowspec(tr, D), _fullspec((8, D))),
        out_shape=(jax.ShapeDtypeStruct((T, D), F32), jax.ShapeDtypeStruct((8, D), F32)),
        compiler_params=_params(1),
    )(h, dhm, dres, g, mod)


def gate_bwd(name, dh, m, mod, r_gt, seq):
    T, D = dh.shape
    tr = _row_tile(T, seq)
    nlat = seq // tr

    def body(d_ref, m_ref, mod_ref, o_ref, st_ref):
        i = pl.program_id(0)
        is_ctx = i >= nlat
        d = d_ref[...]
        o_ref[...] = (d * _mod_row(mod_ref, r_gt, is_ctx)).astype(BF16)
        dgt = _colsum(d * m_ref[...].astype(F32))
        dgt_l, dgt_c = _split_stats(is_ctx, dgt)
        _acc_rows(st_ref, i == 0, [(0, dgt_l), (1, dgt_c), (2, dgt)])

    return pl.pallas_call(
        body, name=name, grid=(T // tr,),
        in_specs=[_rowspec(tr, D), _rowspec(tr, D), _fullspec((16, D))],
        out_specs=(_rowspec(tr, D), _fullspec((8, D))),
        out_shape=(jax.ShapeDtypeStruct((T, D), BF16), jax.ShapeDtypeStruct((8, D), F32)),
        compiler_params=_params(1),
    )(dh, m, mod)


def loss_head(name, h, target, g):
    T, D = h.shape
    tr = _tile(T, 256, 8)

    def body(h_ref, t_ref, g_ref, o_ref, st_ref):
        i = pl.program_id(0)
        x = h_ref[...]
        r = lax.rsqrt(jnp.mean(x * x, axis=-1, keepdims=True) + EPS)
        n = x * r
        gg = g_ref[...]
        err = n * gg - t_ref[...]
        dy = err * (1.0 / D)
        dn = dy * gg
        o_ref[...] = r * (dn - n * jnp.mean(dn * n, axis=-1, keepdims=True))
        _acc_rows(st_ref, i == 0, [(0, _colsum(dy * n)), (1, _colsum(err * err))])

    return pl.pallas_call(
        body, name=name, grid=(T // tr,),
        in_specs=[_rowspec(tr, D), _rowspec(tr, D), _fullspec((1, D))],
        out_specs=(_rowspec(tr, D), _fullspec((8, D))),
        out_shape=(jax.ShapeDtypeStruct((T, D), F32), jax.ShapeDtypeStruct((8, D), F32)),
        compiler_params=_params(1),
    )(h, target, g)


GELU_C = math.sqrt(2.0 / math.pi)


def _gelu(x):
    t = jnp.tanh(GELU_C * (x + 0.044715 * (x * x * x)))
    return 0.5 * x * (1.0 + t), t


def _gelu_grad(x, t):
    return 0.5 * (1.0 + t) + 0.5 * x * (1.0 - t * t) * (GELU_C * (1.0 + 3.0 * 0.044715 * (x * x)))


def _layernorm_fwd(v, g, b):
    mu = jnp.mean(v, axis=-1, keepdims=True)
    xc = v - mu
    rs = lax.rsqrt(jnp.mean(xc * xc, axis=-1, keepdims=True) + EPS)
    xhat = xc * rs
    return xhat * g + b, xhat, rs


def _layernorm_bwd(dout, xhat, rs, g):
    dxh = dout * g
    return rs * (dxh - jnp.mean(dxh, axis=-1, keepdims=True) - xhat * jnp.mean(dxh * xhat, axis=-1, keepdims=True))


def gmlp_fwd(name, zp, ln_g, ln_b, ws, bfull):
    T, E2 = zp.shape
    E = E2 // 2
    G = E // CHUNK
    tr = 2 * CHUNK if T % (2 * CHUNK) == 0 else CHUNK

    def body(zp_ref, g_ref, b_ref, ws_ref, bf_ref, p_ref):
        for ch in range(tr // CHUNK):
            rs_ = slice(ch * CHUNK, (ch + 1) * CHUNK)
            u, _ = _gelu(zp_ref[rs_, 0:E].astype(F32))
            v, _ = _gelu(zp_ref[rs_, E:E2].astype(F32))
            vn, _, _ = _layernorm_fwd(v, g_ref[...], b_ref[...])
            vnb = vn.astype(BF16)
            for gi in range(G):
                cs = slice(gi * CHUNK, (gi + 1) * CHUNK)
                vs = jnp.dot(ws_ref[gi], vnb[:, cs], preferred_element_type=F32) + bf_ref[:, cs]
                p_ref[rs_, cs] = (u[:, cs] * vs).astype(BF16)

    return pl.pallas_call(
        body, name=name, grid=(T // tr,),
        in_specs=[_rowspec(tr, E2), _fullspec((1, E)), _fullspec((1, E)), _fullspec((G, CHUNK, CHUNK)),
                  _fullspec((CHUNK, E))],
        out_specs=_rowspec(tr, E), out_shape=jax.ShapeDtypeStruct((T, E), BF16), compiler_params=_params(1),
    )(zp, ln_g, ln_b, ws, bfull)


def gmlp_bwd(name, zp, dp, ln_g, ln_b, ws, bfull):
    T, E2 = zp.shape
    E = E2 // 2
    G = E // CHUNK
    tr = 2 * CHUNK if T % (2 * CHUNK) == 0 else CHUNK
    n = T // tr

    def body(zp_ref, dp_ref, g_ref, b_ref, ws_ref, bf_ref, dz_ref, dws_ref, dbs_ref, st_ref, dvn_ref, dbf_ref):
        i = pl.program_id(0)

        @pl.when(i == 0)
        def _():
            dws_ref[...] = jnp.zeros_like(dws_ref)
            dbf_ref[...] = jnp.zeros_like(dbf_ref)

        dlg = jnp.zeros((1, E), F32)
        dlb = jnp.zeros((1, E), F32)
        for ch in range(tr // CHUNK):
            rs_ = slice(ch * CHUNK, (ch + 1) * CHUNK)
            zu = zp_ref[rs_, 0:E].astype(F32)
            zv = zp_ref[rs_, E:E2].astype(F32)
            u, tu = _gelu(zu)
            v, tv = _gelu(zv)
            vn, xhat, rs = _layernorm_fwd(v, g_ref[...], b_ref[...])
            vnb = vn.astype(BF16)
            dpf = dp_ref[rs_, :].astype(F32)
            for gi in range(G):
                cs = slice(gi * CHUNK, (gi + 1) * CHUNK)
                w = ws_ref[gi]
                vs = jnp.dot(w, vnb[:, cs], preferred_element_type=F32) + bf_ref[:, cs]
                dz_ref[rs_, cs] = (dpf[:, cs] * vs * _gelu_grad(zu[:, cs], tu[:, cs])).astype(BF16)
                dvs = dpf[:, cs] * u[:, cs]
                dvsb = dvs.astype(BF16)
                dws_ref[gi] += lax.dot_general(dvsb, vnb[:, cs], NT, preferred_element_type=F32)
                dbf_ref[:, cs] += dvs
                dvn_ref[:, cs] = lax.dot_general(w, dvsb, TN, preferred_element_type=F32)
            dvn = dvn_ref[...]
            dlg = dlg + _colsum(dvn * xhat)
            dlb = dlb + _colsum(dvn)
            dv = _layernorm_bwd(dvn, xhat, rs, g_ref[...])
            dz_ref[rs_, E:E2] = (dv * _gelu_grad(zv, tv)).astype(BF16)
        _acc_rows(st_ref, i == 0, [(0, dlg), (1, dlb)])

        @pl.when(i == n - 1)
        def _():
            for gi in range(G):
                dbs_ref[:, gi:gi + 1] = jnp.sum(dbf_ref[:, gi * CHUNK:(gi + 1) * CHUNK], axis=1, keepdims=True)

    return pl.pallas_call(
        body, name=name, grid=(n,),
        in_specs=[_rowspec(tr, E2), _rowspec(tr, E), _fullspec((1, E)), _fullspec((1, E)),
                  _fullspec((G, CHUNK, CHUNK)), _fullspec((CHUNK, E))],
        out_specs=(_rowspec(tr, E2), _fullspec((G, CHUNK, CHUNK)), _fullspec((CHUNK, G)), _fullspec((8, E))),
        out_shape=(jax.ShapeDtypeStruct((T, E2), BF16), jax.ShapeDtypeStruct((G, CHUNK, CHUNK), F32),
                   jax.ShapeDtypeStruct((CHUNK, G), F32), jax.ShapeDtypeStruct((8, E), F32)),
        scratch_shapes=[pltpu.VMEM((CHUNK, E), F32), pltpu.VMEM((CHUNK, E), F32)],
        compiler_params=_params(1),
    )(zp, dp, ln_g, ln_b, ws, bfull)


def _halo_specs(tr, T, width):
    per = tr // HALO
    last = T // HALO - 1
    prev = pl.BlockSpec((HALO, width), lambda i: (jnp.maximum(i * per - 1, 0), 0))
    nxt = pl.BlockSpec((HALO, width), lambda i: (jnp.minimum((i + 1) * per, last), 0))
    return prev, nxt


def _halo_valid(i, n, nlat):
    return jnp.logical_and(i > 0, i != nlat), jnp.logical_and(i + 1 < n, i + 1 != nlat)


def _glu(tt, D):
    a = tt[:, 0:D].astype(F32)
    g = tt[:, D:2 * D].astype(F32)
    return a * _sigmoid(g)


def conv_fwd(name, t, wdw, bdw, ln_g, ln_b, seq):
    T, D2 = t.shape
    D = D2 // 2
    tr = _row_tile(T, seq)
    n, nlat = T // tr, seq // tr
    prev, nxt = _halo_specs(tr, T, D2)

    def body(tp_ref, tc_ref, tn_ref, w_ref, b_ref, g_ref, bb_ref, y_ref, yc_ref, s_ref, buf):
        i = pl.program_id(0)
        pv, nv = _halo_valid(i, n, nlat)
        y = _glu(tc_ref[...], D)
        y_ref[...] = y
        buf[0:HALO, :] = jnp.where(pv, _glu(tp_ref[...], D), 0.0)
        buf[HALO:HALO + tr, :] = y
        buf[HALO + tr:2 * HALO + tr, :] = jnp.where(nv, _glu(tn_ref[...], D), 0.0)
        acc = jnp.zeros((tr, D), F32) + b_ref[...]
        for k in range(CONV_W):
            acc = acc + w_ref[k:k + 1, :] * buf[pl.ds(k + 1, tr), :]
        yc_ref[...] = acc
        yl, _, _ = _layernorm_fwd(acc, g_ref[...], bb_ref[...])
        s_ref[...] = (yl * _sigmoid(yl)).astype(BF16)

    return pl.pallas_call(
        body, name=name, grid=(n,),
        in_specs=[prev, _rowspec(tr, D2), nxt, _fullspec((32, D)), _fullspec((1, D)), _fullspec((1, D)),
                  _fullspec((1, D))],
        out_specs=(_rowspec(tr, D), _rowspec(tr, D), _rowspec(tr, D)),
        out_shape=(jax.ShapeDtypeStruct((T, D), F32), jax.ShapeDtypeStruct((T, D), F32),
                   jax.ShapeDtypeStruct((T, D), BF16)),
        scratch_shapes=[pltpu.VMEM((tr + 2 * HALO, D), F32)], compiler_params=_params(1),
    )(t, t, t, wdw, bdw, ln_g, ln_b)


def conv_bwd_norm(name, ds, yc, ln_g, ln_b):
    T, D = yc.shape
    tr = _tile(T, 256, 8)

    def body(ds_ref, yc_ref, g_ref, b_ref, o_ref, st_ref):
        i = pl.program_id(0)
        yl, xhat, rs = _layernorm_fwd(yc_ref[...], g_ref[...], b_ref[...])
        sig = _sigmoid(yl)
        dyl = ds_ref[...] * (sig * (1.0 + yl * (1.0 - sig)))
        dyc = _layernorm_bwd(dyl, xhat, rs, g_ref[...])
        o_ref[...] = dyc
        _acc_rows(st_ref, i == 0, [(0, _colsum(dyl * xhat)), (1, _colsum(dyl)), (2, _colsum(dyc))])

    return pl.pallas_call(
        body, name=name, grid=(T // tr,),
        in_specs=[_rowspec(tr, D), _rowspec(tr, D), _fullspec((1, D)), _fullspec((1, D))],
        out_specs=(_rowspec(tr, D), _fullspec((8, D))),
        out_shape=(jax.ShapeDtypeStruct((T, D), F32), jax.ShapeDtypeStruct((8, D), F32)),
        compiler_params=_params(1),
    )(ds, yc, ln_g, ln_b)


def conv_bwd_taps(name, dyc, y, t, wdw, seq):
    T, D = y.shape
    tr = _row_tile(T, seq)
    n, nlat = T // tr, seq // tr
    prev, nxt = _halo_specs(tr, T, D)

    def body(dp_ref, dc_ref, dn_ref, yp_ref, ycur_ref, yn_ref, t_ref, w_ref, dt_ref, dw_ref, dbuf, ybuf):
        i = pl.program_id(0)
        pv, nv = _halo_valid(i, n, nlat)
        dcur = dc_ref[...]
        dbuf[0:HALO, :] = jnp.where(pv, dp_ref[...], 0.0)
        dbuf[HALO:HALO + tr, :] = dcur
        dbuf[HALO + tr:2 * HALO + tr, :] = jnp.where(nv, dn_ref[...], 0.0)
        ybuf[0:HALO, :] = jnp.where(pv, yp_ref[...], 0.0)
        ybuf[HALO:HALO + tr, :] = ycur_ref[...]
        ybuf[HALO + tr:2 * HALO + tr, :] = jnp.where(nv, yn_ref[...], 0.0)

        @pl.when(i == 0)
        def _():
            dw_ref[...] = jnp.zeros_like(dw_ref)

        dy = jnp.zeros((tr, D), F32)
        for k in range(CONV_W):
            dw_ref[k:k + 1, :] += _colsum(dcur * ybuf[pl.ds(k + 1, tr), :])
            dy = dy + w_ref[k:k + 1, :] * dbuf[pl.ds(CONV_W - k, tr), :]
        a = t_ref[:, 0:D].astype(F32)
        sig = _sigmoid(t_ref[:, D:2 * D].astype(F32))
        dt_ref[:, 0:D] = (dy * sig).astype(BF16)
        dt_ref[:, D:2 * D] = (dy * a * sig * (1.0 - sig)).astype(BF16)

    return pl.pallas_call(
        body, name=name, grid=(n,),
        in_specs=[prev, _rowspec(tr, D), nxt, prev, _rowspec(tr, D), nxt, _rowspec(tr, 2 * D), _fullspec((32, D))],
        out_specs=(_rowspec(tr, 2 * D), _fullspec((32, D))),
        out_shape=(jax.ShapeDtypeStruct((T, 2 * D), BF16), jax.ShapeDtypeStruct((32, D), F32)),
        scratch_shapes=[pltpu.VMEM((tr + 2 * HALO, D), F32), pltpu.VMEM((tr + 2 * HALO, D), F32)],
        compiler_params=_params(1),
    )(dyc, dyc, dyc, y, y, y, t, wdw)


Q_ROWS = 8
K_ROWS = 16
QN = Q_ROWS * GRID_W
WIN = K_ROWS * GRID_W
NA_Q_OFFSET = (0, 4, 8)


def _na_window_start(cls, qi):
    return (max(qi - NA_ROWS // 2, 0), qi, min(qi + NA_ROWS // 2, NA_ROWS))[cls]


def _na_pairs(cls):
    for qi in range(Q_ROWS):
        lo = _na_window_start(cls, qi)
        for kj in range(lo, lo + NA_ROWS):
            yield qi, kj, kj - NA_Q_OFFSET[cls] - qi + NA_ROWS - 1


def na_bias_tables(toep):
    H = toep.shape[0]
    neg = jnp.full((H, GRID_W, GRID_W), NEG_INF, F32)
    tables = []
    for cls in range(3):
        dr_of = {(qi, kj): dr for qi, kj, dr in _na_pairs(cls)}
        rows = [jnp.concatenate([toep[:, dr_of[(qi, kj)]] if (qi, kj) in dr_of else neg for kj in range(K_ROWS)],
                                axis=2) for qi in range(Q_ROWS)]
        tables.append(jnp.concatenate(rows, axis=1))
    return jnp.stack(tables)


def _na_class(rb, nblk):
    return jnp.where(rb == 0, 0, jnp.where(rb == nblk - 1, 2, 1))


def _na_specs(seq, ctx_rows, H):
    cb = seq // ctx_rows
    nblk = seq // QN
    return [
        pl.BlockSpec((QN, HEAD), lambda h, r: (r, h)),
        pl.BlockSpec((seq, HEAD), lambda h, r: (0, H + h)),
        pl.BlockSpec((seq, HEAD), lambda h, r: (0, 2 * H + h)),
        pl.BlockSpec((ctx_rows, HEAD), lambda h, r: (cb, H + h)),
        pl.BlockSpec((ctx_rows, HEAD), lambda h, r: (cb, 2 * H + h)),
        pl.BlockSpec((None, None, QN, WIN), lambda h, r: (_na_class(r, nblk), h, 0, 0)),
    ]


def _na_scores(q_ref, k_ref, v_ref, kc_ref, vc_ref, b_ref, rows):
    rb = pl.program_id(1)
    k_start = jnp.clip(Q_ROWS * rb - NA_ROWS // 2, 0, rows - K_ROWS)
    start = pl.multiple_of(k_start * GRID_W, GRID_W)
    scale = HEAD ** -0.5
    q = q_ref[...]
    kw = k_ref[pl.ds(start, WIN), :]
    vw = v_ref[pl.ds(start, WIN), :]
    kc = kc_ref[...]
    vc = vc_ref[...]
    s = lax.dot_general(q, kw, NT, preferred_element_type=F32) * scale + b_ref[...]
    sc = lax.dot_general(q, kc, NT, preferred_element_type=F32) * scale
    m = jnp.maximum(jnp.max(s, axis=-1, keepdims=True), jnp.max(sc, axis=-1, keepdims=True))
    p = jnp.exp(s - m)
    pc = jnp.exp(sc - m)
    l = jnp.sum(p, axis=-1, keepdims=True) + jnp.sum(pc, axis=-1, keepdims=True)
    return q, kw, vw, kc, vc, p, pc, l, start, scale


def na_fwd(name, qkv, bias, seq, tasks=()):
    T, D3 = qkv.shape
    D = D3 // 3
    H = D // HEAD
    rows = seq // GRID_W

    def body(q_ref, k_ref, v_ref, kc_ref, vc_ref, b_ref, o_ref):
        q, kw, vw, kc, vc, p, pc, l, start, scale = _na_scores(q_ref, k_ref, v_ref, kc_ref, vc_ref, b_ref, rows)
        o = (jnp.dot(p.astype(BF16), vw, preferred_element_type=F32)
             + jnp.dot(pc.astype(BF16), vc, preferred_element_type=F32))
        o_ref[...] = (o / l).astype(BF16)

    outs, touts = host_call(name, body, (H, seq // QN), [qkv, qkv, qkv, qkv, qkv, bias],
                            _na_specs(seq, T - seq, H), [jax.ShapeDtypeStruct((seq, D), BF16)],
                            [pl.BlockSpec((QN, HEAD), lambda h, r: (r, h))], tasks=tasks)
    return outs[0], touts


def na_bwd(name, qkv, bias, do, seq):
    T, D3 = qkv.shape
    D = D3 // 3
    H = D // HEAD
    rows = seq // GRID_W
    nblk = seq // QN
    ctx_rows = T - seq

    def body(q_ref, k_ref, v_ref, kc_ref, vc_ref, b_ref, do_ref, dq_ref, dk_ref, dv_ref, dkc_ref, dvc_ref, db_ref):
        rb = pl.program_id(1)

        @pl.when(rb == 0)
        def _():
            for ref in (dk_ref, dv_ref, dkc_ref, dvc_ref):
                ref[...] = jnp.zeros_like(ref)

        @pl.when(jnp.logical_or(rb <= 1, rb == nblk - 1))
        def _():
            db_ref[...] = jnp.zeros_like(db_ref)

        q, kw, vw, kc, vc, p, pc, l, start, scale = _na_scores(q_ref, k_ref, v_ref, kc_ref, vc_ref, b_ref, rows)
        inv = 1.0 / l
        pn = p * inv
        pcn = pc * inv
        do_ = do_ref[...]
        dp = lax.dot_general(do_, vw, NT, preferred_element_type=F32)
        dpc = lax.dot_general(do_, vc, NT, preferred_element_type=F32)
        delta = jnp.sum(pn * dp, axis=-1, keepdims=True) + jnp.sum(pcn * dpc, axis=-1, keepdims=True)
        ds = pn * (dp - delta)
        dsc = pcn * (dpc - delta)
        db_ref[...] += ds
        dsb = (ds * scale).astype(BF16)
        dscb = (dsc * scale).astype(BF16)
        dq = jnp.dot(dsb, kw, preferred_element_type=F32) + jnp.dot(dscb, kc, preferred_element_type=F32)
        dq_ref[...] = dq.astype(BF16)
        dk_ref[pl.ds(start, WIN), :] += lax.dot_general(dsb, q, TN, preferred_element_type=F32)
        dv_ref[pl.ds(start, WIN), :] += lax.dot_general(pn.astype(BF16), do_, TN, preferred_element_type=F32)
        dkc_ref[...] += lax.dot_general(dscb, q, TN, preferred_element_type=F32)
        dvc_ref[...] += lax.dot_general(pcn.astype(BF16), do_, TN, preferred_element_type=F32)

    head_lat = pl.BlockSpec((seq, HEAD), lambda h, r: (0, h))
    head_ctx = pl.BlockSpec((ctx_rows, HEAD), lambda h, r: (0, h))
    return pl.pallas_call(
        body, name=name, grid=(H, nblk),
        in_specs=_na_specs(seq, ctx_rows, H) + [pl.BlockSpec((QN, HEAD), lambda h, r: (r, h))],
        out_specs=(pl.BlockSpec((QN, HEAD), lambda h, r: (r, h)), head_lat, head_lat, head_ctx, head_ctx,
                   pl.BlockSpec((None, None, QN, WIN), lambda h, r: (_na_class(r, nblk), h, 0, 0))),
        out_shape=(jax.ShapeDtypeStruct((seq, D), BF16), jax.ShapeDtypeStruct((seq, D), F32),
                   jax.ShapeDtypeStruct((seq, D), F32), jax.ShapeDtypeStruct((ctx_rows, D), F32),
                   jax.ShapeDtypeStruct((ctx_rows, D), F32), jax.ShapeDtypeStruct(bias.shape, F32)),
        compiler_params=_params(2),
    )(qkv, qkv, qkv, qkv, qkv, bias, do)


def _ctx_specs(seq, ctx_rows, H):
    cb = seq // ctx_rows
    return [pl.BlockSpec((ctx_rows, HEAD), lambda h: (cb, h)),
            pl.BlockSpec((ctx_rows, HEAD), lambda h: (cb, H + h)),
            pl.BlockSpec((ctx_rows, HEAD), lambda h: (cb, 2 * H + h))]


def _ctx_probs(q_ref, k_ref):
    s = lax.dot_general(q_ref[...], k_ref[...], NT, preferred_element_type=F32) * (HEAD ** -0.5)
    p = jnp.exp(s - jnp.max(s, axis=-1, keepdims=True))
    return p / jnp.sum(p, axis=-1, keepdims=True)


def ctx_attn_fwd(name, qkv, seq):
    T, D3 = qkv.shape
    D = D3 // 3
    H = D // HEAD
    ctx_rows = T - seq

    def body(q_ref, k_ref, v_ref, o_ref):
        p = _ctx_probs(q_ref, k_ref)
        o_ref[...] = jnp.dot(p.astype(BF16), v_ref[...], preferred_element_type=F32).astype(BF16)

    return pl.pallas_call(
        body, name=name, grid=(H,), in_specs=_ctx_specs(seq, ctx_rows, H),
        out_specs=pl.BlockSpec((ctx_rows, HEAD), lambda h: (0, h)),
        out_shape=jax.ShapeDtypeStruct((ctx_rows, D), BF16), compiler_params=_params(1),
    )(qkv, qkv, qkv)


def ctx_attn_bwd(name, qkv, do, dkc_lat, dvc_lat, seq):
    T, D3 = qkv.shape
    D = D3 // 3
    H = D // HEAD
    ctx_rows = T - seq
    cb = seq // ctx_rows
    scale = HEAD ** -0.5

    def body(q_ref, k_ref, v_ref, do_ref, dkl_ref, dvl_ref, dq_ref, dk_ref, dv_ref):
        p = _ctx_probs(q_ref, k_ref)
        do_ = do_ref[...]
        dp = lax.dot_general(do_, v_ref[...], NT, preferred_element_type=F32)
        ds = p * (dp - jnp.sum(p * dp, axis=-1, keepdims=True))
        dsb = (ds * scale).astype(BF16)
        dq_ref[...] = jnp.dot(dsb, k_ref[...], preferred_element_type=F32).astype(BF16)
        dk_ref[...] = (dkl_ref[...] + lax.dot_general(dsb, q_ref[...], TN, preferred_element_type=F32)).astype(BF16)
        dv_ref[...] = (dvl_ref[...]
                       + lax.dot_general(p.astype(BF16), do_, TN, preferred_element_type=F32)).astype(BF16)

    blk = pl.BlockSpec((ctx_rows, HEAD), lambda h: (0, h))
    shp = jax.ShapeDtypeStruct((ctx_rows, D), BF16)
    return pl.pallas_call(
        body, name=name, grid=(H,),
        in_specs=_ctx_specs(seq, ctx_rows, H) + [pl.BlockSpec((ctx_rows, HEAD), lambda h: (cb, h)), blk, blk],
        out_specs=(blk, blk, blk), out_shape=(shp, shp, shp), compiler_params=_params(1),
    )(qkv, qkv, qkv, do, dkc_lat, dvc_lat)


def _rpb_tables():
    qc = jnp.arange(GRID_W)[:, None]
    kc = jnp.arange(GRID_W)[None, :]
    rel = (kc - qc + NA_COLS - 1).reshape(1, GRID_W * GRID_W)
    onehot = (rel == jnp.arange(32)[:, None]).astype(F32)
    c_start = jnp.clip(qc - NA_COLS // 2, 0, GRID_W - NA_COLS)
    mask = jnp.logical_and(kc >= c_start, kc < c_start + NA_COLS).astype(F32).reshape(1, GRID_W * GRID_W)
    return onehot, mask


def rpb_expand(name, rpb2, onehot, mask):
    R = rpb2.shape[0]

    def body(r_ref, oh_ref, m_ref, o_ref):
        t = jnp.dot(r_ref[...], oh_ref[...], preferred_element_type=F32, precision=lax.Precision.HIGHEST)
        o_ref[...] = jnp.where(m_ref[...] > 0.5, t, NEG_INF)

    return pl.pallas_call(body, name=name, out_shape=jax.ShapeDtypeStruct((R, GRID_W * GRID_W), F32),
                          compiler_params=_params())(rpb2, onehot, mask)


def rpb_fold(name, x, classes):
    H = x.shape[0]
    n_dr = 2 * NA_ROWS - 1

    def body(x_ref, y_ref):
        acc = [None] * n_dr
        for cls in classes:
            for qi, kj, dr in _na_pairs(cls):
                acc[dr] = x_ref[cls, qi, kj] if acc[dr] is None else acc[dr] + x_ref[cls, qi, kj]
        for dr in range(n_dr):
            y_ref[dr] = acc[dr]

    return pl.pallas_call(
        body, name=name, grid=(H,),
        in_specs=[pl.BlockSpec((None, 3, Q_ROWS, K_ROWS, GRID_W, GRID_W), lambda h: (h, 0, 0, 0, 0, 0))],
        out_specs=pl.BlockSpec((None, n_dr, GRID_W, GRID_W), lambda h: (h, 0, 0, 0)),
        out_shape=jax.ShapeDtypeStruct((H, n_dr, GRID_W, GRID_W), F32), compiler_params=_params(1),
    )(x)


def rpb_reduce(name, y2, onehot_t):
    R = y2.shape[0]

    def body(y_ref, oh_ref, o_ref):
        o_ref[...] = jnp.dot(y_ref[...], oh_ref[...], preferred_element_type=F32, precision=lax.Precision.HIGHEST)

    return pl.pallas_call(body, name=name, out_shape=jax.ShapeDtypeStruct((R, 32), F32),
                          compiler_params=_params())(y2, onehot_t)


def _adam(g, w, m, v):
    m2 = ADAM_B1 * m + (1.0 - ADAM_B1) * g
    v2 = ADAM_B2 * v + (1.0 - ADAM_B2) * (g * g)
    m_hat = m2 / (1.0 - ADAM_B1 ** ADAM_STEP)
    v_hat = v2 / (1.0 - ADAM_B2 ** ADAM_STEP)
    delta = -ADAM_LR * (m_hat / (jnp.sqrt(v_hat) + ADAM_EPS) + ADAM_WD * w)
    return delta, m2, v2


def adam_parts(name, land, land2, w, m, v, tasks=()):
    shape = w.shape
    C = shape[-1]
    R = math.prod(shape[:-1])
    tr = _tile(R, max(16, (256 * 1024 // C) // 16 * 16), 16)
    l1, l2 = land.reshape(N_SHARD, R, C), land2.reshape(N_SHARD, R, C)

    def body(l1_ref, l2_ref, w_ref, m_ref, v_ref, g_ref, d_ref, m2_ref, v2_ref):
        a = l1_ref[0].astype(F32)
        b = l2_ref[0].astype(F32)
        for k in range(1, N_SHARD):
            a = a + l1_ref[k].astype(F32)
            b = b + l2_ref[k].astype(F32)
        g = a + b
        g_ref[...] = g
        d_ref[...], m2_ref[...], v2_ref[...] = _adam(g, w_ref[...], m_ref[...], v_ref[...])

    part = pl.BlockSpec((N_SHARD, tr, C), lambda i: (0, i, 0))
    row = _rowspec(tr, C)
    shp = jax.ShapeDtypeStruct((R, C), F32)
    outs, touts = host_call(name, body, (R // tr,), [l1, l2, w.reshape(R, C), m.reshape(R, C), v.reshape(R, C)],
                            [part, part, row, row, row], (shp,) * 4, (row,) * 4, tasks=tasks)
    return tuple(o.reshape(shape) for o in outs), touts


def adam_flat(name, g, w, m, v):
    R, C = g.shape
    tr = _tile(R, 256, 8)

    def body(g_ref, w_ref, m_ref, v_ref, d_ref, m2_ref, v2_ref):
        d_ref[...], m2_ref[...], v2_ref[...] = _adam(g_ref[...], w_ref[...], m_ref[...], v_ref[...])

    row = _rowspec(tr, C)
    shp = jax.ShapeDtypeStruct((R, C), F32)
    return pl.pallas_call(body, name=name, grid=(R // tr,), in_specs=[row] * 4, out_specs=(row,) * 3,
                          out_shape=(shp,) * 3, compiler_params=_params(1))(g, w, m, v)


def reduce_8(name, gathered):
    _, R, D = gathered.shape
    tr = _tile(R, 64, 8)

    def body(x_ref, o_ref):
        acc = x_ref[0]
        for k in range(1, 8):
            acc = acc + x_ref[k]
        o_ref[...] = acc

    return pl.pallas_call(
        body, name=name, grid=(R // tr,), in_specs=[pl.BlockSpec((8, tr, D), lambda i: (0, i, 0))],
        out_specs=_rowspec(tr, D), out_shape=jax.ShapeDtypeStruct((R, D), F32), compiler_params=_params(1),
    )(gathered)


def ada_fwd(name, craw16, ada_w, ada_b3):
    L, D, Cs = ada_w.shape
    tn = _tile(Cs, 512)

    def body(c_ref, w_ref, b_ref, o_ref):
        cc = c_ref[...]
        s = cc * _sigmoid(cc)
        o_ref[...] = jnp.dot(s, w_ref[...], preferred_element_type=F32,
                             precision=lax.Precision.HIGHEST) + b_ref[...]

    return pl.pallas_call(
        body, name=name, grid=(L, Cs // tn),
        in_specs=[pl.BlockSpec((16, D), lambda l, j: (0, 0)), pl.BlockSpec((None, D, tn), lambda l, j: (l, 0, j)),
                  pl.BlockSpec((None, 1, tn), lambda l, j: (l, 0, j))],
        out_specs=pl.BlockSpec((None, 16, tn), lambda l, j: (l, 0, j)),
        out_shape=jax.ShapeDtypeStruct((L, 16, Cs), F32), compiler_params=_params(2),
    )(craw16, ada_w, ada_b3)


def ada_bwd_adam(name, craw16_t, dm16, dmc8, w, m, v):
    L, D, Cs = w.shape
    tn = _tile(Cs, 256)

    def body(c_ref, dm_ref, dc_ref, w_ref, m_ref, v_ref, g_ref, d_ref, m2_ref, v2_ref, ds_ref):
        step = pl.program_id(0) * (Cs // tn) + pl.program_id(1)
        cc = c_ref[...]
        s_t = cc * _sigmoid(cc)
        g = jnp.dot(s_t, dm_ref[...], preferred_element_type=F32, precision=lax.Precision.HIGHEST)
        ww = w_ref[...]
        g_ref[...] = g
        d_ref[...], m2_ref[...], v2_ref[...] = _adam(g, ww, m_ref[...], v_ref[...])

        @pl.when(step == 0)
        def _():
            ds_ref[...] = jnp.zeros_like(ds_ref)

        ds_ref[...] += lax.dot_general(dc_ref[...].astype(BF16), ww.astype(BF16), NT, preferred_element_type=F32)

    wspec = pl.BlockSpec((None, D, tn), lambda l, j: (l, 0, j))
    shp = jax.ShapeDtypeStruct((L, D, Cs), F32)
    return pl.pallas_call(
        body, name=name, grid=(L, Cs // tn),
        in_specs=[pl.BlockSpec((D, 16), lambda l, j: (0, 0)), pl.BlockSpec((None, 16, tn), lambda l, j: (l, 0, j)),
                  pl.BlockSpec((None, 8, tn), lambda l, j: (l, 0, j)), wspec, wspec, wspec],
        out_specs=(wspec, wspec, wspec, wspec, pl.BlockSpec((8, D), lambda l, j: (0, 0))),
        out_shape=(shp, shp, shp, shp, jax.ShapeDtypeStruct((8, D), F32)), compiler_params=_params(2),
    )(craw16_t, dm16, dmc8, w, m, v)


def cctx_adam(name, ds_all, c_ctx, m, v):
    D = c_ctx.shape[1]

    def body(ds_ref, c_ref, m_ref, v_ref, g_ref, d_ref, m2_ref, v2_ref):
        ds = ds_ref[0, 0:1, :]
        for slot in (2, 4, 6):
            ds = ds + ds_ref[slot, 0:1, :]
        cc = c_ref[...]
        sig = _sigmoid(cc)
        g = ds * (sig * (1.0 + cc * (1.0 - sig)))
        g_ref[...] = g
        d_ref[...], m2_ref[...], v2_ref[...] = _adam(g, cc, m_ref[...], v_ref[...])

    shp = jax.ShapeDtypeStruct((1, D), F32)
    return pl.pallas_call(body, name=name, out_shape=(shp,) * 4, compiler_params=_params())(ds_all, c_ctx, m, v)


WEIGHT_NAMES = ['c_ctx', 'ada_w', 'ada_b', 'g_mix', 'g_ffn', 'ffn_w1', 'ffn_w3', 'ffn_w2', 'a_w_in', 'a_ln_g',
                'a_ln_b', 'a_w_s', 'a_b_s', 'a_w_out', 'b_w_qkv', 'b_rpb', 'b_w_out', 'c_w_pw1', 'c_w_dw', 'c_b_dw',
                'c_ln_g', 'c_ln_b', 'c_w_pw2', 'g_final']
BIG_NAMES = ['ffn_w1', 'ffn_w3', 'ffn_w2', 'a_w_in', 'a_w_out', 'b_w_qkv', 'b_w_out', 'c_w_pw1', 'c_w_pw2']
SMALL_NAMES = ['ada_b', 'g_mix', 'g_ffn', 'a_ln_g', 'a_ln_b', 'a_w_s', 'a_b_s', 'b_rpb', 'c_w_dw', 'c_b_dw',
               'c_ln_g', 'c_ln_b', 'g_final']
SMALL_PACK_COLS = 512
MIXER_IN = ('a_w_in', 'b_w_qkv', 'c_w_pw1')
MIXER_OUT = ('a_w_out', 'b_w_out', 'c_w_pw2')

FWD_PLAN = {
    "pre": [("a_w_in", 0)],
    "in_0": [("a_w_out", 0), ("ffn_w1", 0)],
    "out_0": [("ffn_w3", 0)],
    "ffn_up_0": [("ffn_w2", 0), ("b_w_qkv", 0)],
    "ffn_down_0": [("b_w_out", 0), ("ffn_w1", 1)],
    "in_1": [("ffn_w3", 1)],
    "b_na_1": [("ffn_w2", 1), ("c_w_pw1", 0), ("c_w_pw2", 0)],
    "out_1": [("ffn_w1", 2)],
    "ffn_up_1": [("ffn_w3", 2), ("ffn_w2", 2)],
    "ffn_down_1": [("a_w_in", 1), ("a_w_out", 1)],
    "in_2": [("ffn_w1", 3)],
    "ffn_up_2": [("ffn_w3", 3), ("ffn_w2", 3)],
}


def _bwd_plan():
    plan = {}
    for i in range(N_LAYERS):
        w_in, w_out = (MIXER_IN[i % 3], i // 3), (MIXER_OUT[i % 3], i // 3)
        if i + 1 < N_LAYERS:
            plan[f"ffn_down_dx_{i}"] = [("forward", MIXER_IN[(i + 1) % 3], (i + 1) // 3)]
        plan[f"ffn_w1_dw_{i}"] = [("scatter", "ffn_w2", i)]
        plan[f"ffn_w3_dw_{i}"] = [("forward", "ffn_w2", i), ("scatter", "ffn_w1", i)]
        plan[f"ffn_up_dx_{i}"] = [("forward", "ffn_w1", i), ("scatter", "ffn_w3", i)]
        plan[f"out_dx_{i}"] = [("forward", "ffn_w3", i)]
        plan[f"in_dw_{i}"] = [("scatter",) + w_out]
        plan[f"in_dx_{i}"] = [("forward",) + w_out, ("scatter",) + w_in]
    plan["rs_post"] = [("forward", MIXER_IN[0], 0)]
    return plan


BWD_PLAN = _bwd_plan()


def _pad_rows(a, mult):
    r = (-a.shape[0]) % mult
    return a if r == 0 else jnp.concatenate([a, jnp.zeros((r,) + a.shape[1:], a.dtype)], axis=0)


def _rows_of(flat, D):
    n = flat.shape[0]
    r = -(-n // D)
    return jnp.concatenate([flat, jnp.zeros((r * D - n,), flat.dtype)]).reshape(r, D)


def _step(W, Mo, Vo, x, c, ctx, loss_target):
    seq, D = x.shape[1], x.shape[2]
    ctx_rows = ctx.shape[1]
    T = seq + ctx_rows
    L = N_LAYERS
    H = D // HEAD
    G = D // CHUNK
    xi, yi, ci = _xyc()
    e_idx = 4 * xi + 2 * yi + ci
    s_idx = 2 * xi + yi

    c_all = all_gather_8("ag_c", c)
    craw16 = jnp.concatenate([c_all.reshape(8, D), W['c_ctx'].reshape(1, D), jnp.zeros((7, D), F32)], axis=0)
    ada_w = W['ada_w']
    Cs = ada_w.shape[2]
    ada_b_s = lax.dynamic_slice_in_dim(W['ada_b'], s_idx * Cs, Cs, axis=1).reshape(L, 1, Cs)
    mod_s = ada_fwd("ada_fwd", craw16, ada_w, ada_b_s)
    mod_g = all_gather_xy("ag_mod", mod_s).transpose(1, 2, 0, 3).reshape(L, 16, N_SHARD * Cs)
    mod_lat = lax.dynamic_index_in_dim(mod_g, e_idx, axis=1, keepdims=False).reshape(L, 6, D)
    mod_all = jnp.concatenate([mod_lat, mod_g[:, 8].reshape(L, 6, D), jnp.zeros((L, 4, D), F32)], axis=1)

    Wg = {}
    land = {n: lax.empty((N_SHARD,) + W[n].shape, BF16) for n in BIG_NAMES}
    land2 = {n: lax.empty((N_SHARD,) + W[n].shape, BF16) for n in BIG_NAMES}
    dW = {}

    def gather_tasks(host):
        return [GatherTask(W[n][l].astype(BF16)) for n, l in FWD_PLAN.get(host, ())]

    def gathered(host, touts):
        for (n, l), out in zip(FWD_PLAN.get(host, ()), touts):
            Wg[(n, l)] = out[0]

    def scatter_tasks(host):
        tasks = []
        for kind, n, l in BWD_PLAN.get(host, ()):
            if kind == "scatter":
                tasks.append(ScatterTask(dW[(n, l)], land[n], land2[n], l))
            else:
                tasks.append(ForwardTask(land[n], land2[n], l))
        return tasks

    def scattered(host, touts):
        for (kind, n, l), out in zip(BWD_PLAN.get(host, ()), touts):
            land[n], land2[n] = out

    gathered("pre", comm_only("ag_pre", gather_tasks("pre")))

    n_a, n_c = W['a_ln_g'].shape[0], W['c_ln_g'].shape[0]
    sh_rows = jnp.concatenate([W['a_ln_g'], W['a_ln_b'], W['c_w_dw'].reshape(n_c * CONV_W, -1), W['c_b_dw'],
                               W['c_ln_g'], W['c_ln_b']], axis=0)
    n_sh = sh_rows.shape[0]
    sh_full = all_gather_xy("ag_small", _pad_rows(sh_rows, 8)).transpose(1, 0, 2).reshape(-1, D)[:n_sh]
    o = 0
    a_ln_g_f, o = sh_full[o:o + n_a], o + n_a
    a_ln_b_f, o = sh_full[o:o + n_a], o + n_a
    c_w_dw_f, o = sh_full[o:o + n_c * CONV_W].reshape(n_c, CONV_W, D), o + n_c * CONV_W
    c_b_dw_f, o = sh_full[o:o + n_c], o + n_c
    c_ln_g_f, o = sh_full[o:o + n_c], o + n_c
    c_ln_b_f, o = sh_full[o:o + n_c], o + n_c

    onehot, colmask = _rpb_tables()
    n_dr = 2 * NA_ROWS - 1
    rpb2 = jnp.pad(W['b_rpb'][0].reshape(H * n_dr, 2 * NA_COLS - 1), ((0, 0), (0, 1)))
    toep = rpb_expand("rpb_expand", rpb2, onehot, colmask).reshape(H, n_dr, GRID_W, GRID_W)
    bias = na_bias_tables(toep)
    na_classes = (0, 1, 2) if seq // QN > 2 else (0, 2)

    def mixer_params(i):
        mixer, j = i % 3, i // 3
        if mixer == 0:
            return dict(ln_g=a_ln_g_f[j:j + 1], ln_b=a_ln_b_f[j:j + 1], ws=W['a_w_s'][j].astype(BF16),
                        bfull=jnp.repeat(W['a_b_s'][j].T, CHUNK, axis=1))
        if mixer == 2:
            return dict(wdw=_pad_rows(c_w_dw_f[j], 32), bdw=c_b_dw_f[j:j + 1], ln_g=c_ln_g_f[j:j + 1],
                        ln_b=c_ln_b_f[j:j + 1])
        return {}

    def fwd(fn, host, *args):
        res, touts = fn(host, *args, tasks=gather_tasks(host))
        gathered(host, touts)
        return res

    def bwd(fn, host, *args, extra=(), **kw):
        tasks = scatter_tasks(host)
        res, touts = fn(host, *args, tasks=tasks + list(extra), **kw)
        scattered(host, touts[:len(tasks)])
        return (res, touts[len(tasks):]) if extra else res

    h = jnp.concatenate([x[0], ctx[0]], axis=0)
    saved = []
    for i in range(L):
        mixer, j = i % 3, i // 3
        n_in, n_out = MIXER_IN[mixer], MIXER_OUT[mixer]
        if i == L - 1:
            h = h[:seq]
        Ti = h.shape[0]
        tm = _tile(Ti, 768)
        tmh = _tile(Ti, 384)
        mod = mod_all[i]
        mp = mixer_params(i)
        s = dict(h0=h, mp=mp)
        hm = nm_fwd(f"nm1_{i}", h, W['g_mix'][i:i + 1], mod, 0, 1, seq)
        s['hm'] = hm
        u = fwd(mm_cols, f"in_{i}", hm, Wg[(n_in, j)], BF16, tm)
        if mixer == 0:
            p = gmlp_fwd(f"a_mid_{i}", u, mp['ln_g'], mp['ln_b'], mp['ws'], mp['bfull'])
        elif mixer == 1:
            p = jnp.concatenate([fwd(na_fwd, f"b_na_{i}", u, bias, seq), ctx_attn_fwd(f"b_ctx_{i}", u, seq)], axis=0)
        else:
            y, yc, p = conv_fwd(f"c_mid_{i}", u, mp['wdw'], mp['bdw'], mp['ln_g'], mp['ln_b'], seq)
            s.update(y=y, yc=yc)
        s.update(u=u, p=p)
        m1, h = fwd(mm_rows_residual, f"out_{i}", p, Wg[(n_out, j)], h, mod, 2, seq, tm)
        s.update(m1=m1, h1=h)
        hf = nm_fwd(f"nm2_{i}", h, W['g_ffn'][i:i + 1], mod, 3, 4, seq)
        a, b, act = fwd(mm_ffn_up, f"ffn_up_{i}", hf, Wg[('ffn_w1', i)], Wg[('ffn_w3', i)], tmh)
        m2, h = fwd(mm_rows_residual, f"ffn_down_{i}", act, Wg[('ffn_w2', i)], h, mod, 5, seq, tm)
        s.update(hf=hf, a=a, b=b, act=act, m2=m2)
        saved.append(s)

    dh, st_loss = loss_head("loss_head", h, loss_target[0], W['g_final'].reshape(1, D))

    dmod_lat, dmod_ctx, dmod_tot = [None] * L, [None] * L, [None] * L
    dg_mix, dg_ffn = [None] * L, [None] * L
    small = {}

    def set_dmod(i, st_n1, st_g1, st_n2, st_g2):
        for dst, r_n, r_g in ((dmod_lat, (1, 2), 0), (dmod_ctx, (3, 4), 1), (dmod_tot, (5, 6), 2)):
            dst[i] = jnp.concatenate([st_n1[r_n[0]:r_n[0] + 1], st_n1[r_n[1]:r_n[1] + 1], st_g1[r_g:r_g + 1],
                                      st_n2[r_n[0]:r_n[0] + 1], st_n2[r_n[1]:r_n[1] + 1], st_g2[r_g:r_g + 1]], axis=0)

    def small_pack():
        a_parts = [small[('a', j)] for j in range(n_a)]
        c_parts = [small[('c', j)] for j in range(n_c)]
        entries = [
            ('dmod_lat', jnp.concatenate(dmod_lat, axis=0)), ('dmod_ctx', jnp.concatenate(dmod_ctx, axis=0)),
            ('ada_b', jnp.concatenate(dmod_tot, axis=0)),
            ('g_mix', jnp.concatenate(dg_mix, axis=0)), ('g_ffn', jnp.concatenate(dg_ffn, axis=0)),
            ('a_ln_g', jnp.concatenate([p[0] for p in a_parts], axis=0)),
            ('a_ln_b', jnp.concatenate([p[1] for p in a_parts], axis=0)),
            ('a_w_s', jnp.concatenate([p[2] for p in a_parts], axis=0)),
            ('a_b_s', jnp.concatenate([p[3] for p in a_parts], axis=0)),
            ('b_rpb', small[('b', 0)]),
            ('c_w_dw', jnp.concatenate([p[0] for p in c_parts], axis=0)),
            ('c_b_dw', jnp.concatenate([p[1] for p in c_parts], axis=0)),
            ('c_ln_g', jnp.concatenate([p[2] for p in c_parts], axis=0)),
            ('c_ln_b', jnp.concatenate([p[3] for p in c_parts], axis=0)),
            ('g_final', st_loss[0:1]), ('loss', st_loss[1:2]),
        ]
        offsets, o = {}, 0
        for n, arr in entries:
            offsets[n] = (o, arr.shape[0])
            o += arr.shape[0]
        return _pad_rows(jnp.concatenate([arr for _, arr in entries], axis=0), 64), offsets

    for i in reversed(range(L)):
        mixer, j = i % 3, i // 3
        n_in, n_out = MIXER_IN[mixer], MIXER_OUT[mixer]
        s = saved[i]
        mp = s['mp']
        mod = mod_all[i]
        if i == L - 2:
            dh = jnp.concatenate([dh, jnp.zeros((ctx_rows, D), F32)], axis=0)
        Ti = dh.shape[0]
        tm = _tile(Ti, 768)
        tmh = _tile(Ti, 384)
        dm2, st_g2 = gate_bwd(f"gate2_bwd_{i}", dh, s['m2'], mod, 5, seq)
        da, db = bwd(mm_rows_dgrad, f"ffn_down_dx_{i}", dm2, Wg[('ffn_w2', i)], BF16, tm, ffn_ab=(s['a'], s['b']))
        dW[('ffn_w2', i)] = bwd(mm_wgrad_rows, f"ffn_w2_dw_{i}", s['act'], dm2, tm)
        dW[('ffn_w1', i)] = bwd(mm_wgrad_cols, f"ffn_w1_dw_{i}", s['hf'], da, tm)
        dW[('ffn_w3', i)] = bwd(mm_wgrad_cols, f"ffn_w3_dw_{i}", s['hf'], db, tm)
        dhf = bwd(mm_cols_dgrad, f"ffn_up_dx_{i}", [da, db], [Wg[('ffn_w1', i)], Wg[('ffn_w3', i)]], tmh)
        dh, st_n2 = nm_bwd(f"nm2_bwd_{i}", s['h1'], dhf, dh, W['g_ffn'][i:i + 1], mod, 4, seq)
        dm1, st_g1 = gate_bwd(f"gate1_bwd_{i}", dh, s['m1'], mod, 2, seq)
        dp = bwd(mm_rows_dgrad, f"out_dx_{i}", dm1, Wg[(n_out, j)], F32 if mixer == 2 else BF16, tm)[0]
        dW[(n_out, j)] = bwd(mm_wgrad_rows, f"out_dw_{i}", s['p'], dm1, tm)
        if mixer == 0:
            du, dws, dbs, st_a = gmlp_bwd(f"a_mid_bwd_{i}", s['u'], dp, mp['ln_g'], mp['ln_b'], mp['ws'], mp['bfull'])
            small[('a', j)] = (st_a[0:1], st_a[1:2], dws.reshape(-1, D), dbs.T.reshape(1, D))
        elif mixer == 1:
            dq, dk, dv, dkc, dvc, dbias = na_bwd(f"b_na_bwd_{i}", s['u'], bias, dp, seq)
            dqc, dkc, dvc = ctx_attn_bwd(f"b_ctx_bwd_{i}", s['u'], dp, dkc, dvc, seq)
            du = jnp.concatenate([jnp.concatenate([dq, dk.astype(BF16), dv.astype(BF16)], axis=1),
                                  jnp.concatenate([dqc, dkc, dvc], axis=1)], axis=0)
            blocks = dbias.reshape(3, H, Q_ROWS, GRID_W, K_ROWS, GRID_W).transpose(1, 0, 2, 4, 3, 5)
            folded = rpb_fold(f"rpb_fold_{i}", blocks, na_classes)
            drpb = rpb_reduce(f"rpb_reduce_{i}", folded.reshape(H * n_dr, GRID_W * GRID_W), onehot.T)
            small[('b', j)] = _rows_of(drpb[:, :2 * NA_COLS - 1].reshape(-1), D)
        else:
            dyc, st_c = conv_bwd_norm(f"c_norm_bwd_{i}", dp, s['yc'], mp['ln_g'], mp['ln_b'])
            du, dwdw = conv_bwd_taps(f"c_taps_bwd_{i}", dyc, s['y'], s['u'], mp['wdw'], seq)
            small[('c', j)] = (dwdw, st_c[2:3], st_c[0:1], st_c[1:2])
        dW[(n_in, j)] = bwd(mm_wgrad_cols, f"in_dw_{i}", s['hm'], du, tm)
        if i == 0:
            dg_mix[0], dg_ffn[0] = jnp.zeros((1, D), F32), st_n2[0:1]
            set_dmod(0, jnp.zeros((8, D), F32), st_g1, st_n2, st_g2)
            pack, offs = small_pack()
            dhm, touts = bwd(mm_cols_dgrad, f"in_dx_{i}", [du], [Wg[(n_in, j)]], tm, extra=[GatherAllTask(pack)])
            gathered_small = touts[0][0]
        else:
            dhm = bwd(mm_cols_dgrad, f"in_dx_{i}", [du], [Wg[(n_in, j)]], tm)
        dh, st_n1 = nm_bwd(f"nm1_bwd_{i}", s['h0'], dhm, dh, W['g_mix'][i:i + 1], mod, 1, seq)
        if i > 0:
            dg_mix[i], dg_ffn[i] = st_n1[0:1], st_n2[0:1]
            set_dmod(i, st_n1, st_g1, st_n2, st_g2)
    grad_x = dh[:seq].reshape(1, seq, D)

    late = _pad_rows(jnp.concatenate([st_n1[1:7], st_n1[0:1]], axis=0), 8)
    touts = comm_only("ag_small_late", [GatherAllTask(late)] + scatter_tasks("rs_post"))
    gathered_late = touts[0][0]
    scattered("rs_post", touts[1:])
    sums = reduce_8("reduce_small_grads", gathered_small)
    sums_late = reduce_8("reduce_small_late", gathered_late)
    late_rows = {'dmod_ctx': sums_late[2:4], 'ada_b': sums_late[4:6], 'g_mix': sums_late[6:7]}
    out = {}

    def summed(n):
        rows = sums[offs[n][0]:offs[n][0] + offs[n][1]]
        if n in late_rows:
            rows = jnp.concatenate([late_rows[n], rows[late_rows[n].shape[0]:]], axis=0)
        return rows

    loss = (0.5 / D) * jnp.sum(summed('loss'))

    lo, ln_ = offs['dmod_lat']
    dm_lat = jnp.concatenate([gathered_late[:, 0:2], gathered_small[:, lo + 2:lo + ln_]], axis=1)
    dm_lat = dm_lat.reshape(8, L, 6 * D).transpose(1, 0, 2)
    dm_ctx = summed('dmod_ctx').reshape(L, 1, 6 * D)
    dm16 = jnp.concatenate([dm_lat, dm_ctx, jnp.zeros((L, 7, 6 * D), F32)], axis=1)
    dm16 = lax.dynamic_slice_in_dim(dm16, s_idx * Cs, Cs, axis=2)
    dmc8 = jnp.concatenate([dm16[:, 8:9], jnp.zeros((L, 7, Cs), F32)], axis=1)
    g_ada, d_ada, m_ada, v_ada, ds_part = ada_bwd_adam("ada_bwd_adam", craw16.T, dm16, dmc8, ada_w,
                                                       Mo['ada_w'], Vo['ada_w'])
    ds_all = all_gather_8("ag_ds_ctx", ds_part)
    cc = cctx_adam("cctx_adam", ds_all, W['c_ctx'].reshape(1, D), Mo['c_ctx'].reshape(1, D),
                   Vo['c_ctx'].reshape(1, D))
    out.update({'c_ctx': tuple(t.reshape(D) for t in cc), 'ada_w': (g_ada, d_ada, m_ada, v_ada)})

    for n in BIG_NAMES:
        out[n], _ = adam_parts("adam_" + n, land[n], land2[n], W[n], Mo[n], Vo[n])

    def own_cols(full):
        w = full.shape[-1] // N_SHARD
        return lax.dynamic_slice_in_dim(full, s_idx * w, w, axis=full.ndim - 1)

    small_g = {
        'ada_b': summed('ada_b').reshape(L, 6 * D), 'g_mix': summed('g_mix'), 'g_ffn': summed('g_ffn'),
        'a_ln_g': own_cols(summed('a_ln_g')), 'a_ln_b': own_cols(summed('a_ln_b')),
        'a_w_s': summed('a_w_s').reshape(n_a, G, CHUNK, CHUNK), 'a_b_s': summed('a_b_s').reshape(n_a, G, CHUNK),
        'b_rpb': summed('b_rpb').reshape(-1)[:H * n_dr * (2 * NA_COLS - 1)].reshape(W['b_rpb'].shape),
        'c_w_dw': own_cols(summed('c_w_dw').reshape(n_c, 32, D)[:, :CONV_W]),
        'c_b_dw': own_cols(summed('c_b_dw')), 'c_ln_g': own_cols(summed('c_ln_g')),
        'c_ln_b': own_cols(summed('c_ln_b')), 'g_final': summed('g_final').reshape(D),
    }

    def packed(d):
        flat = jnp.concatenate([d[n].reshape(-1) for n in SMALL_NAMES])
        return _pad_rows(_rows_of(flat, SMALL_PACK_COLS), 8)

    res = adam_flat("adam_small", packed(small_g), packed(W), packed(Mo), packed(Vo))
    o = 0
    for n in SMALL_NAMES:
        size, shape = W[n].size, W[n].shape
        out[n] = (small_g[n],) + tuple(r.reshape(-1)[o:o + size].reshape(shape) for r in res)
        o += size

    return (loss, grad_x) + tuple(out[n][k] for k in range(4) for n in WEIGHT_NAMES)


def kernel(x, c, ctx, c_ctx, ada_w, ada_b, g_mix, g_ffn, ffn_w1, ffn_w3, ffn_w2, a_w_in, a_ln_g, a_ln_b, a_w_s, a_b_s, a_w_out, b_w_qkv, b_rpb, b_w_out, c_w_pw1, c_w_dw, c_b_dw, c_ln_g, c_ln_b, c_w_pw2, g_final, loss_target, m_c_ctx, m_ada_w, m_ada_b, m_g_mix, m_g_ffn, m_ffn_w1, m_ffn_w3, m_ffn_w2, m_a_w_in, m_a_ln_g, m_a_ln_b, m_a_w_s, m_a_b_s, m_a_w_out, m_b_w_qkv, m_b_rpb, m_b_w_out, m_c_w_pw1, m_c_w_dw, m_c_b_dw, m_c_ln_g, m_c_ln_b, m_c_w_pw2, m_g_final, v_c_ctx, v_ada_w, v_ada_b, v_g_mix, v_g_ffn, v_ffn_w1, v_ffn_w3, v_ffn_w2, v_a_w_in, v_a_ln_g, v_a_ln_b, v_a_w_s, v_a_b_s, v_a_w_out, v_b_w_qkv, v_b_rpb, v_b_w_out, v_c_w_pw1, v_c_w_dw, v_c_b_dw, v_c_ln_g, v_c_ln_b, v_c_w_pw2, v_g_final):
    W = dict(zip(WEIGHT_NAMES, (c_ctx, ada_w, ada_b, g_mix, g_ffn, ffn_w1, ffn_w3, ffn_w2, a_w_in, a_ln_g, a_ln_b, a_w_s, a_b_s, a_w_out, b_w_qkv, b_rpb, b_w_out, c_w_pw1, c_w_dw, c_b_dw, c_ln_g, c_ln_b, c_w_pw2, g_final)))
    Mo = dict(zip(WEIGHT_NAMES, (m_c_ctx, m_ada_w, m_ada_b, m_g_mix, m_g_ffn, m_ffn_w1, m_ffn_w3, m_ffn_w2, m_a_w_in, m_a_ln_g, m_a_ln_b, m_a_w_s, m_a_b_s, m_a_w_out, m_b_w_qkv, m_b_rpb, m_b_w_out, m_c_w_pw1, m_c_w_dw, m_c_b_dw, m_c_ln_g, m_c_ln_b, m_c_w_pw2, m_g_final)))
    Vo = dict(zip(WEIGHT_NAMES, (v_c_ctx, v_ada_w, v_ada_b, v_g_mix, v_g_ffn, v_ffn_w1, v_ffn_w3, v_ffn_w2, v_a_w_in, v_a_ln_g, v_a_ln_b, v_a_w_s, v_a_b_s, v_a_w_out, v_b_w_qkv, v_b_rpb, v_b_w_out, v_c_w_pw1, v_c_w_dw, v_c_b_dw, v_c_ln_g, v_c_ln_b, v_c_w_pw2, v_g_final)))
    return _step(W, Mo, Vo, x, c, ctx, loss_target)
```

```python
import functools
import math

import jax
import jax.numpy as jnp
from jax import lax
from jax.experimental import pallas as pl
from jax.experimental.pallas import tpu as pltpu

F32 = jnp.float32
BF16 = jnp.bfloat16
MESH = pl.DeviceIdType.MESH
ANY = pl.BlockSpec(memory_space=pl.ANY)

GRID_W = 64
CHUNK = 128
HEAD = 128
NA_ROWS = 8
NA_COLS = 16
CONV_W = 31
HALO = 16
EPS = 1e-6
NEG_INF = -1e30
N_LAYERS = 4
N_SHARD = 4
V7X_VMEM_BYTES = 64 * 1024 * 1024
VMEM_LIMIT = V7X_VMEM_BYTES - 6 * 1024 * 1024

ADAM_LR = 0.001
ADAM_B1 = 0.9
ADAM_B2 = 0.999
ADAM_EPS = 1e-08
ADAM_WD = 0.01
ADAM_STEP = 10

NN = (((1,), (0,)), ((), ()))
NT = (((1,), (1,)), ((), ()))
TN = (((0,), (0,)), ((), ()))


def _params(n_grid=0):
    sem = ("arbitrary",) * n_grid if n_grid else None
    return pltpu.CompilerParams(dimension_semantics=sem, vmem_limit_bytes=VMEM_LIMIT)


def _xyc():
    return lax.axis_index("x"), lax.axis_index("y"), lax.axis_index("c")


def _flip(v, f):
    return 1 - v if f else v


def _tile(n, pref, mult=128):
    if n <= pref:
        return n
    t = (pref // mult) * mult
    while t > mult and n % t:
        t -= mult
    assert n % t == 0, (n, pref, mult)
    return t


def _sigmoid(x):
    return 1.0 / (1.0 + jnp.exp(-x))


XY_FLIPS = ((1, 0), (0, 1), (1, 1))
ALL_FLIPS = tuple((fx, fy, fc) for fx in (0, 1) for fy in (0, 1) for fc in (0, 1) if fx or fy or fc)


def _remote(src, dst, ssem, rsem, dev):
    return pltpu.make_async_remote_copy(src_ref=src, dst_ref=dst, send_sem=ssem, recv_sem=rsem,
                                        device_id=dev, device_id_type=MESH)


def all_gather_xy(name, shard):
    def body(src, dst, ssem, rsem, lsem):
        x, y, c = _xyc()
        mine = pltpu.make_async_copy(src, dst.at[2 * x + y], lsem)
        mine.start()
        sends = []
        for k, (fx, fy) in enumerate(XY_FLIPS):
            cp = _remote(src, dst.at[2 * x + y], ssem.at[k], rsem.at[k], (_flip(x, fx), _flip(y, fy), c))
            cp.start()
            sends.append(cp)
        for k, (fx, fy) in enumerate(XY_FLIPS):
            px, py = _flip(x, fx), _flip(y, fy)
            _remote(src, dst.at[2 * px + py], ssem.at[k], rsem.at[k], (px, py, c)).wait_recv()
        for cp in sends:
            cp.wait_send()
        mine.wait()

    return pl.pallas_call(
        body, name=name, out_shape=jax.ShapeDtypeStruct((N_SHARD,) + shard.shape, shard.dtype),
        in_specs=[ANY], out_specs=ANY,
        scratch_shapes=[pltpu.SemaphoreType.DMA((3,)), pltpu.SemaphoreType.DMA((3,)), pltpu.SemaphoreType.DMA(())],
    )(shard)


def all_gather_8(name, blk):
    def body(src, dst, ssem, rsem, lsem):
        x, y, c = _xyc()
        me = 4 * x + 2 * y + c
        mine = pltpu.make_async_copy(src, dst.at[me], lsem)
        mine.start()
        sends = []
        for k, (fx, fy, fc) in enumerate(ALL_FLIPS):
            cp = _remote(src, dst.at[me], ssem.at[k], rsem.at[k], (_flip(x, fx), _flip(y, fy), _flip(c, fc)))
            cp.start()
            sends.append(cp)
        for k, (fx, fy, fc) in enumerate(ALL_FLIPS):
            px, py, pc = _flip(x, fx), _flip(y, fy), _flip(c, fc)
            _remote(src, dst.at[4 * px + 2 * py + pc], ssem.at[k], rsem.at[k], (px, py, pc)).wait_recv()
        for cp in sends:
            cp.wait_send()
        mine.wait()

    return pl.pallas_call(
        body, name=name, out_shape=jax.ShapeDtypeStruct((8,) + blk.shape, blk.dtype),
        in_specs=[ANY], out_specs=ANY,
        scratch_shapes=[pltpu.SemaphoreType.DMA((7,)), pltpu.SemaphoreType.DMA((7,)), pltpu.SemaphoreType.DMA(())],
    )(blk)


class GatherTask:
    n_send, n_recv, n_local = 3, 3, 1
    alias = {}

    def __init__(self, shard):
        self.ins = [shard]
        self.outs = [jax.ShapeDtypeStruct((N_SHARD,) + shard.shape, shard.dtype)]

    def _copies(self, xyc, ins, outs, ssem, rsem):
        x, y, c = xyc
        for k, (fx, fy) in enumerate(XY_FLIPS):
            px, py = _flip(x, fx), _flip(y, fy)
            send = _remote(ins[0], outs[0].at[2 * x + y], ssem.at[k], rsem.at[k], (px, py, c))
            recv = _remote(ins[0], outs[0].at[2 * px + py], ssem.at[k], rsem.at[k], (px, py, c))
            yield send, recv

    def start(self, xyc, ins, outs, ssem, rsem, lsem):
        x, y, _ = xyc
        pltpu.make_async_copy(ins[0], outs[0].at[2 * x + y], lsem.at[0]).start()
        for send, _ in self._copies(xyc, ins, outs, ssem, rsem):
            send.start()

    def finish(self, xyc, ins, outs, ssem, rsem, lsem):
        x, y, _ = xyc
        for send, recv in self._copies(xyc, ins, outs, ssem, rsem):
            recv.wait_recv()
            send.wait_send()
        pltpu.make_async_copy(ins[0], outs[0].at[2 * x + y], lsem.at[0]).wait()


class GatherAllTask:
    n_send, n_recv, n_local = 7, 7, 1
    alias = {}

    def __init__(self, blk):
        self.ins = [blk]
        self.outs = [jax.ShapeDtypeStruct((8,) + blk.shape, blk.dtype)]

    def _copies(self, xyc, ins, outs, ssem, rsem):
        x, y, c = xyc
        for k, (fx, fy, fc) in enumerate(ALL_FLIPS):
            px, py, pc = _flip(x, fx), _flip(y, fy), _flip(c, fc)
            send = _remote(ins[0], outs[0].at[4 * x + 2 * y + c], ssem.at[k], rsem.at[k], (px, py, pc))
            recv = _remote(ins[0], outs[0].at[4 * px + 2 * py + pc], ssem.at[k], rsem.at[k], (px, py, pc))
            yield send, recv

    def _local(self, xyc, ins, outs, lsem):
        x, y, c = xyc
        return pltpu.make_async_copy(ins[0], outs[0].at[4 * x + 2 * y + c], lsem.at[0])

    def start(self, xyc, ins, outs, ssem, rsem, lsem):
        self._local(xyc, ins, outs, lsem).start()
        for send, _ in self._copies(xyc, ins, outs, ssem, rsem):
            send.start()

    def finish(self, xyc, ins, outs, ssem, rsem, lsem):
        for send, recv in self._copies(xyc, ins, outs, ssem, rsem):
            recv.wait_recv()
            send.wait_send()
        self._local(xyc, ins, outs, lsem).wait()


class ScatterTask:
    n_send, n_recv, n_local = 4, 4, 1
    alias = {1: 0, 2: 1}

    def __init__(self, part, land, land2, l):
        self.ins = [part, land, land2]
        self.outs = [jax.ShapeDtypeStruct(land.shape, land.dtype), jax.ShapeDtypeStruct(land2.shape, land2.dtype)]
        self.l = l

    def _copies(self, xyc, ins, outs, ssem, rsem):
        x, y, c = xyc
        me_s = 2 * x + y
        part, land, land2 = ins[0], outs[0], outs[1]
        sib = (x, y, 1 - c)
        own = _remote(part.at[me_s], land2.at[me_s, self.l], ssem.at[3], rsem.at[3], sib)
        yield own, own
        for k, (fx, fy) in enumerate(XY_FLIPS):
            px, py = _flip(x, fx), _flip(y, fy)
            ps = 2 * px + py
            send = _remote(part.at[ps], land.at[me_s, self.l], ssem.at[k], rsem.at[k], (px, py, c))
            recv = _remote(part.at[ps], land.at[ps, self.l], ssem.at[k], rsem.at[k], (px, py, c))
            yield send, recv

    def _local(self, xyc, ins, outs, lsem):
        me_s = 2 * xyc[0] + xyc[1]
        return pltpu.make_async_copy(ins[0].at[me_s], outs[0].at[me_s, self.l], lsem.at[0])

    def start(self, xyc, ins, outs, ssem, rsem, lsem):
        self._local(xyc, ins, outs, lsem).start()
        for send, _ in self._copies(xyc, ins, outs, ssem, rsem):
            send.start()

    def finish(self, xyc, ins, outs, ssem, rsem, lsem):
        for send, recv in self._copies(xyc, ins, outs, ssem, rsem):
            recv.wait_recv()
            send.wait_send()
        self._local(xyc, ins, outs, lsem).wait()


class ForwardTask:
    n_send, n_recv, n_local = 3, 3, 0
    alias = {0: 0, 1: 1}

    def __init__(self, land, land2, l):
        self.ins = [land, land2]
        self.outs = [jax.ShapeDtypeStruct(land.shape, land.dtype), jax.ShapeDtypeStruct(land2.shape, land2.dtype)]
        self.l = l

    def _copies(self, xyc, outs, ssem, rsem):
        x, y, c = xyc
        for k, (fx, fy) in enumerate(XY_FLIPS):
            ps = 2 * _flip(x, fx) + _flip(y, fy)
            yield _remote(outs[0].at[ps, self.l], outs[1].at[ps, self.l], ssem.at[k], rsem.at[k], (x, y, 1 - c))

    def start(self, xyc, ins, outs, ssem, rsem, lsem):
        for cp in self._copies(xyc, outs, ssem, rsem):
            cp.start()

    def finish(self, xyc, ins, outs, ssem, rsem, lsem):
        for cp in self._copies(xyc, outs, ssem, rsem):
            cp.wait_recv()
            cp.wait_send()


def host_call(name, body, grid, arrays, in_specs, out_shape, out_specs, scratch=(), tasks=()):
    out_shape, out_specs, scratch = tuple(out_shape), tuple(out_specs), list(scratch)
    n_in, n_out, n_scr, n_grid = len(arrays), len(out_shape), len(scratch), len(grid)
    t_arrays, t_outs, aliases, spans, sems = [], [], {}, [], []
    for t in tasks:
        i0, o0 = len(t_arrays), len(t_outs)
        t_arrays += t.ins
        t_outs += t.outs
        for a, b in t.alias.items():
            aliases[n_in + i0 + a] = n_out + o0 + b
        spans.append((i0, len(t_arrays), o0, len(t_outs)))
        sems += [pltpu.SemaphoreType.DMA((t.n_send,)), pltpu.SemaphoreType.DMA((t.n_recv,)),
                 pltpu.SemaphoreType.DMA((max(t.n_local, 1),))]
    n_tin, n_tout = len(t_arrays), len(t_outs)

    def full_body(*refs):
        ins = refs[:n_in]
        tin = refs[n_in:n_in + n_tin]
        outs = refs[n_in + n_tin:n_in + n_tin + n_out]
        tout = refs[n_in + n_tin + n_out:n_in + n_tin + n_out + n_tout]
        rest = refs[n_in + n_tin + n_out + n_tout:]
        scr, sm = rest[:n_scr], rest[n_scr:]
        if not tasks:
            body(*ins, *outs, *scr)
            return
        pids = [pl.program_id(d) for d in range(n_grid)]
        first = functools.reduce(jnp.logical_and, [p == 0 for p in pids])
        last = functools.reduce(jnp.logical_and, [p == n - 1 for p, n in zip(pids, grid)])
        xyc = _xyc()

        def each(method):
            for k, (t, (i0, i1, o0, o1)) in enumerate(zip(tasks, spans)):
                getattr(t, method)(xyc, tin[i0:i1], tout[o0:o1], sm[3 * k], sm[3 * k + 1], sm[3 * k + 2])

        @pl.when(first)
        def _():
            each("start")

        body(*ins, *outs, *scr)

        @pl.when(last)
        def _():
            each("finish")

    res = pl.pallas_call(
        full_body, name=name, grid=grid, in_specs=list(in_specs) + [ANY] * n_tin,
        out_specs=out_specs + (ANY,) * n_tout, out_shape=out_shape + tuple(t_outs),
        scratch_shapes=scratch + sems, input_output_aliases=aliases, compiler_params=_params(n_grid),
    )(*arrays, *t_arrays)
    return tuple(res[:n_out]), [tuple(res[n_out + o0:n_out + o1]) for (_, _, o0, o1) in spans]


def comm_only(name, tasks):
    def body(i_ref, o_ref):
        o_ref[...] = i_ref[...]

    spec = pl.BlockSpec((8, 128), lambda i: (0, 0))
    _, touts = host_call(name, body, (1,), [jnp.zeros((8, 128), F32)], [spec],
                         [jax.ShapeDtypeStruct((8, 128), F32)], [spec], tasks=tasks)
    return touts


def matmul(name, grid, order, pairs, pair_dims, acc_of_pair, acc_shapes, extras, outs, epilogue, tasks=(),
           col_chunk=1 << 30):
    ni, nj, nk = grid

    def wrap(m):
        if order == "ij":
            return lambda g0, g1, k: m(g0, g1, k)
        return lambda g0, g1, k: m(g1, g0, k)

    g = (ni, nj, nk) if order == "ij" else (nj, ni, nk)
    arrays, in_specs = [], []
    for a, b in pairs:
        for arr, blk, m in (a, b):
            arrays.append(arr)
            in_specs.append(pl.BlockSpec(blk, wrap(m)))
    for arr, blk, m in extras:
        arrays.append(arr)
        in_specs.append(pl.BlockSpec(blk, wrap(m)))
    n_in = len(arrays)
    out_shape = tuple(o[0] for o in outs)
    out_specs = tuple(pl.BlockSpec(o[1], wrap(o[2])) for o in outs)
    n_pairs, n_ex, n_out, n_acc = len(pairs), len(extras), len(outs), len(acc_shapes)
    tile_n = acc_shapes[0][1]
    chunks = [slice(c0, min(c0 + col_chunk, tile_n)) for c0 in range(0, tile_n, col_chunk)]

    def body(*refs):
        ins = refs[:n_in]
        out_refs = refs[n_in:n_in + n_out]
        accs = refs[n_in + n_out:]
        pid = (pl.program_id(0), pl.program_id(1)) if order == "ij" else (pl.program_id(1), pl.program_id(0))
        k = pl.program_id(2)

        def partial(p, cs):
            b_ref = ins[2 * p + 1]
            b = b_ref[cs, :] if pair_dims[p] == NT else b_ref[:, cs]
            return lax.dot_general(ins[2 * p][...], b, pair_dims[p], preferred_element_type=F32)

        ex = ins[2 * n_pairs:2 * n_pairs + n_ex]
        if nk == 1:
            for cs in chunks:
                vals = [None] * n_acc
                for p in range(n_pairs):
                    d = partial(p, cs)
                    q = acc_of_pair[p]
                    vals[q] = d if vals[q] is None else vals[q] + d
                epilogue(vals, ex, out_refs, pid, cs)
        else:
            @pl.when(k == 0)
            def _():
                for acc in accs:
                    acc[...] = jnp.zeros_like(acc)

            for cs in chunks:
                for p in range(n_pairs):
                    accs[acc_of_pair[p]][:, cs] += partial(p, cs)

            @pl.when(k == nk - 1)
            def _():
                for cs in chunks:
                    epilogue([acc[:, cs] for acc in accs], ex, out_refs, pid, cs)

    scratch = [] if nk == 1 else [pltpu.VMEM(s, F32) for s in acc_shapes]
    return host_call(name, body, g, arrays, in_specs, out_shape, out_specs, scratch, tasks)


def _store_cast(vals, extras, out_refs, pid, cs):
    out_refs[0][:, cs] = vals[0].astype(out_refs[0].dtype)


def mm_cols(name, x, wg, out_dtype, tm, tasks=()):
    T, D = x.shape
    ns = wg.shape[2]
    tn = _tile(ns, 1536)
    nb = ns // tn
    outs, touts = matmul(
        name, (T // tm, N_SHARD * nb, 1), "ji",
        [((x, (tm, D), lambda i, jj, k: (i, 0)),
          (wg, (None, D, tn), lambda i, jj, k: (jj // nb, 0, jj % nb)))],
        [NN], [0], [(tm, tn)], [],
        [(jax.ShapeDtypeStruct((T, N_SHARD * ns), out_dtype), (tm, tn), lambda i, jj, k: (i, jj))],
        _store_cast, tasks)
    return outs[0], touts


def mm_ffn_up(name, x, w1g, w3g, tm, tasks=()):
    T, D = x.shape
    ns = w1g.shape[2]
    tn = _tile(ns, 1536)
    nb = ns // tn

    def epi(vals, extras, out_refs, pid, cs):
        a, b = vals
        out_refs[0][:, cs] = a.astype(BF16)
        out_refs[1][:, cs] = b.astype(BF16)
        out_refs[2][:, cs] = (a * _sigmoid(a) * b).astype(BF16)

    wmap = lambda i, jj, k: (jj // nb, 0, jj % nb)
    xa = (x, (tm, D), lambda i, jj, k: (i, 0))
    o = (jax.ShapeDtypeStruct((T, N_SHARD * ns), BF16), (tm, tn), lambda i, jj, k: (i, jj))
    return matmul(name, (T // tm, N_SHARD * nb, 1), "ji",
                  [(xa, (w1g, (None, D, tn), wmap)), (xa, (w3g, (None, D, tn), wmap))],
                  [NN, NN], [0, 1], [(tm, tn), (tm, tn)], [], [o, o, o], epi, tasks)


def mm_rows_residual(name, p, wg, h, mod, gate_row, seq, tm, tasks=()):
    T, kin = p.shape
    ks, D = wg.shape[1], wg.shape[2]
    tn = _tile(D, 512)

    def epi(vals, extras, out_refs, pid, cs):
        m = vals[0]
        h_ref, mod_ref = extras
        rows = pid[0] * tm + lax.broadcasted_iota(jnp.int32, (tm, 1), 0)
        gate = jnp.where(rows >= seq, mod_ref[6 + gate_row:7 + gate_row, cs], mod_ref[gate_row:gate_row + 1, cs])
        out_refs[0][:, cs] = m.astype(BF16)
        out_refs[1][:, cs] = h_ref[:, cs] + gate * m

    pairs = [((p, (tm, ks), lambda i, jj, k, s=s: (i, s)), (wg, (None, ks, tn), lambda i, jj, k, s=s: (s, 0, jj)))
             for s in range(N_SHARD)]
    omap = lambda i, jj, k: (i, jj)
    return matmul(
        name, (T // tm, D // tn, 1), "ji", pairs, [NN] * N_SHARD, [0] * N_SHARD, [(tm, tn)],
        [(h, (tm, tn), omap), (mod, (16, tn), lambda i, jj, k: (0, jj))],
        [(jax.ShapeDtypeStruct((T, D), BF16), (tm, tn), omap), (jax.ShapeDtypeStruct((T, D), F32), (tm, tn), omap)],
        epi, tasks)


def mm_rows_dgrad(name, dm, wg, out_dtype, tm, ffn_ab=None, tasks=()):
    T, D = dm.shape
    ks = wg.shape[1]
    tn = _tile(ks, 1536)
    nb = ks // tn
    omap = lambda i, jj, k: (i, jj)
    o = (jax.ShapeDtypeStruct((T, N_SHARD * ks), out_dtype), (tm, tn), omap)
    pairs = [((dm, (tm, D), lambda i, jj, k: (i, 0)),
              (wg, (None, tn, D), lambda i, jj, k: (jj // nb, jj % nb, 0)))]
    if ffn_ab is None:
        return matmul(name, (T // tm, N_SHARD * nb, 1), "ji", pairs, [NT], [0], [(tm, tn)], [], [o],
                      _store_cast, tasks)

    def epi(vals, extras, out_refs, pid, cs):
        dact = vals[0]
        a = extras[0][:, cs].astype(F32)
        b = extras[1][:, cs].astype(F32)
        sig = _sigmoid(a)
        out_refs[0][:, cs] = (dact * b * (sig * (1.0 + a * (1.0 - sig)))).astype(BF16)
        out_refs[1][:, cs] = (dact * (a * sig)).astype(BF16)

    a, b = ffn_ab
    return matmul(name, (T // tm, N_SHARD * nb, 1), "ji", pairs, [NT], [0], [(tm, tn)],
                  [(a, (tm, tn), omap), (b, (tm, tn), omap)], [o, o], epi, tasks)


def mm_cols_dgrad(name, dys, wgs, tm, tasks=()):
    T = dys[0].shape[0]
    D, ns = wgs[0].shape[1], wgs[0].shape[2]
    tk = _tile(ns, 1536)
    kb = ns // tk
    pairs = [((dy, (tm, tk), lambda i, jj, k: (i, k)),
              (wg, (None, D, tk), lambda i, jj, k: (k // kb, 0, k % kb))) for dy, wg in zip(dys, wgs)]
    outs, touts = matmul(name, (T // tm, 1, N_SHARD * kb), "ij", pairs, [NT] * len(dys), [0] * len(dys),
                         [(tm, D)], [],
                         [(jax.ShapeDtypeStruct((T, D), F32), (tm, D), lambda i, jj, k: (i, 0))], _store_cast, tasks)
    return outs[0], touts


def mm_wgrad_rows(name, p, dm, tk, tasks=()):
    T, kin = p.shape
    D = dm.shape[1]
    ks = kin // N_SHARD
    tm = _tile(ks, 1536)
    mb = ks // tm
    tn = _tile(D, 1024)
    outs, touts = matmul(
        name, (N_SHARD * mb, D // tn, T // tk), "ij",
        [((p, (tk, tm), lambda i, jj, k: (k, i)), (dm, (tk, tn), lambda i, jj, k: (k, jj)))],
        [TN], [0], [(tm, tn)], [],
        [(jax.ShapeDtypeStruct((N_SHARD, ks, D), BF16), (None, tm, tn), lambda i, jj, k: (i // mb, i % mb, jj))],
        _store_cast, tasks)
    return outs[0], touts


def mm_wgrad_cols(name, x, dy, tk, tasks=()):
    T, D = x.shape
    ns = dy.shape[1] // N_SHARD
    tm = _tile(D, 1024)
    tn = _tile(ns, 1536)
    nb = ns // tn
    outs, touts = matmul(
        name, (D // tm, N_SHARD * nb, T // tk), "ij",
        [((x, (tk, tm), lambda i, jj, k: (k, i)), (dy, (tk, tn), lambda i, jj, k: (k, jj)))],
        [TN], [0], [(tm, tn)], [],
        [(jax.ShapeDtypeStruct((N_SHARD, D, ns), BF16), (None, tm, tn), lambda i, jj, k: (jj // nb, i, jj % nb))],
        _store_cast, tasks)
    return outs[0], touts


def _row_tile(T, seq):
    if T == seq:
        return _tile(T, 256, 8)
    return math.gcd(256, math.gcd(seq, T - seq))


def _mod_row(mod_ref, row, is_ctx):
    return jnp.where(is_ctx, mod_ref[6 + row:7 + row, :], mod_ref[row:row + 1, :])


def _rowspec(tr, D):
    return pl.BlockSpec((tr, D), lambda i: (i, 0))


def _fullspec(shape):
    nd = len(shape)
    return pl.BlockSpec(shape, lambda i: (0,) * nd)


def _colsum(v):
    return jnp.sum(v, axis=0, keepdims=True)


def _acc_rows(st_ref, first, rows):
    @pl.when(first)
    def _():
        st_ref[...] = jnp.zeros_like(st_ref)
    for r, val in rows:
        st_ref[r:r + 1, :] += val


def _split_stats(is_ctx, val):
    zero = jnp.zeros_like(val)
    return jnp.where(is_ctx, zero, val), jnp.where(is_ctx, val, zero)


def nm_fwd(name, h, g, mod, r_sh, r_sc, seq):
    T, D = h.shape
    tr = _row_tile(T, seq)
    nlat = seq // tr

    def body(h_ref, g_ref, mod_ref, o_ref):
        is_ctx = pl.program_id(0) >= nlat
        x = h_ref[...]
        r = lax.rsqrt(jnp.mean(x * x, axis=-1, keepdims=True) + EPS)
        y = x * r * g_ref[...]
        o_ref[...] = (y * (1.0 + _mod_row(mod_ref, r_sc, is_ctx)) + _mod_row(mod_ref, r_sh, is_ctx)).astype(BF16)

    return pl.pallas_call(
        body, name=name, grid=(T // tr,), in_specs=[_rowspec(tr, D), _fullspec((1, D)), _fullspec((16, D))],
        out_specs=_rowspec(tr, D), out_shape=jax.ShapeDtypeStruct((T, D), BF16), compiler_params=_params(1),
    )(h, g, mod)


def nm_bwd(name, h, dhm, dres, g, mod, r_sc, seq):
    T, D = h.shape
    tr = _row_tile(T, seq)
    nlat = seq // tr

    def body(h_ref, d_ref, r_ref, g_ref, mod_ref, o_ref, st_ref):
        i = pl.program_id(0)
        is_ctx = i >= nlat
        x = h_ref[...]
        r = lax.rsqrt(jnp.mean(x * x, axis=-1, keepdims=True) + EPS)
        n = x * r
        gg = g_ref[...]
        dout = d_ref[...]
        dsh = _colsum(dout)
        dsc = _colsum(dout * (n * gg))
        dy = dout * (1.0 + _mod_row(mod_ref, r_sc, is_ctx))
        dn = dy * gg
        o_ref[...] = r_ref[...] + r * (dn - n * jnp.mean(dn * n, axis=-1, keepdims=True))
        dsh_l, dsh_c = _split_stats(is_ctx, dsh)
        dsc_l, dsc_c = _split_stats(is_ctx, dsc)
        _acc_rows(st_ref, i == 0, [(0, _colsum(dy * n)), (1, dsh_l), (2, dsc_l), (3, dsh_c), (4, dsc_c),
                                   (5, dsh), (6, dsc)])

    return pl.pallas_call(
        body, name=name, grid=(T // tr,),
        in_specs=[_rowspec(tr, D), _rowspec(tr, D), _rowspec(tr, D), _fullspec((1, D)), _fullspec((16, D))],
        out_specs=(_rowspec(tr, D), _fullspec((8, D))),
        out_shape=(jax.ShapeDtypeStruct((T, D), F32), jax.ShapeDtypeStruct((8, D), F32)),
        compiler_params=_params(1),
    )(h, dhm, dres, g, mod)


def gate_bwd(name, dh, m, mod, r_gt, seq):
    T, D = dh.shape
    tr = _row_tile(T, seq)
    nlat = seq // tr

    def body(d_ref, m_ref, mod_ref, o_ref, st_ref):
        i = pl.program_id(0)
        is_ctx = i >= nlat
        d = d_ref[...]
        o_ref[...] = (d * _mod_row(mod_ref, r_gt, is_ctx)).astype(BF16)
        dgt = _colsum(d * m_ref[...].astype(F32))
        dgt_l, dgt_c = _split_stats(is_ctx, dgt)
        _acc_rows(st_ref, i == 0, [(0, dgt_l), (1, dgt_c), (2, dgt)])

    return pl.pallas_call(
        body, name=name, grid=(T // tr,),
        in_specs=[_rowspec(tr, D), _rowspec(tr, D), _fullspec((16, D))],
        out_specs=(_rowspec(tr, D), _fullspec((8, D))),
        out_shape=(jax.ShapeDtypeStruct((T, D), BF16), jax.ShapeDtypeStruct((8, D), F32)),
        compiler_params=_params(1),
    )(dh, m, mod)


def loss_head(name, h, target, g):
    T, D = h.shape
    tr = _tile(T, 256, 8)

    def body(h_ref, t_ref, g_ref, o_ref, st_ref):
        i = pl.program_id(0)
        x = h_ref[...]
        r = lax.rsqrt(jnp.mean(x * x, axis=-1, keepdims=True) + EPS)
        n = x * r
        gg = g_ref[...]
        err = n * gg - t_ref[...]
        dy = err * (1.0 / D)
        dn = dy * gg
        o_ref[...] = r * (dn - n * jnp.mean(dn * n, axis=-1, keepdims=True))
        _acc_rows(st_ref, i == 0, [(0, _colsum(dy * n)), (1, _colsum(err * err))])

    return pl.pallas_call(
        body, name=name, grid=(T // tr,),
        in_specs=[_rowspec(tr, D), _rowspec(tr, D), _fullspec((1, D))],
        out_specs=(_rowspec(tr, D), _fullspec((8, D))),
        out_shape=(jax.ShapeDtypeStruct((T, D), F32), jax.ShapeDtypeStruct((8, D), F32)),
        compiler_params=_params(1),
    )(h, target, g)


GELU_C = math.sqrt(2.0 / math.pi)


def _gelu(x):
    t = jnp.tanh(GELU_C * (x + 0.044715 * (x * x * x)))
    return 0.5 * x * (1.0 + t), t


def _gelu_grad(x, t):
    return 0.5 * (1.0 + t) + 0.5 * x * (1.0 - t * t) * (GELU_C * (1.0 + 3.0 * 0.044715 * (x * x)))


def _layernorm_fwd(v, g, b):
    mu = jnp.mean(v, axis=-1, keepdims=True)
    xc = v - mu
    rs = lax.rsqrt(jnp.mean(xc * xc, axis=-1, keepdims=True) + EPS)
    xhat = xc * rs
    return xhat * g + b, xhat, rs


def _layernorm_bwd(dout, xhat, rs, g):
    dxh = dout * g
    return rs * (dxh - jnp.mean(dxh, axis=-1, keepdims=True) - xhat * jnp.mean(dxh * xhat, axis=-1, keepdims=True))


def gmlp_fwd(name, zp, ln_g, ln_b, ws, bfull):
    T, E2 = zp.shape
    E = E2 // 2
    G = E // CHUNK
    tr = 2 * CHUNK if T % (2 * CHUNK) == 0 else CHUNK

    def body(zp_ref, g_ref, b_ref, ws_ref, bf_ref, p_ref):
        for ch in range(tr // CHUNK):
            rs_ = slice(ch * CHUNK, (ch + 1) * CHUNK)
            u, _ = _gelu(zp_ref[rs_, 0:E].astype(F32))
            v, _ = _gelu(zp_ref[rs_, E:E2].astype(F32))
            vn, _, _ = _layernorm_fwd(v, g_ref[...], b_ref[...])
            vnb = vn.astype(BF16)
            for gi in range(G):
                cs = slice(gi * CHUNK, (gi + 1) * CHUNK)
                vs = jnp.dot(ws_ref[gi], vnb[:, cs], preferred_element_type=F32) + bf_ref[:, cs]
                p_ref[rs_, cs] = (u[:, cs] * vs).astype(BF16)

    return pl.pallas_call(
        body, name=name, grid=(T // tr,),
        in_specs=[_rowspec(tr, E2), _fullspec((1, E)), _fullspec((1, E)), _fullspec((G, CHUNK, CHUNK)),
                  _fullspec((CHUNK, E))],
        out_specs=_rowspec(tr, E), out_shape=jax.ShapeDtypeStruct((T, E), BF16), compiler_params=_params(1),
    )(zp, ln_g, ln_b, ws, bfull)


def gmlp_bwd(name, zp, dp, ln_g, ln_b, ws, bfull):
    T, E2 = zp.shape
    E = E2 // 2
    G = E // CHUNK
    tr = 2 * CHUNK if T % (2 * CHUNK) == 0 else CHUNK
    n = T // tr

    def body(zp_ref, dp_ref, g_ref, b_ref, ws_ref, bf_ref, dz_ref, dws_ref, dbs_ref, st_ref, dvn_ref, dbf_ref):
        i = pl.program_id(0)

        @pl.when(i == 0)
        def _():
            dws_ref[...] = jnp.zeros_like(dws_ref)
            dbf_ref[...] = jnp.zeros_like(dbf_ref)

        dlg = jnp.zeros((1, E), F32)
        dlb = jnp.zeros((1, E), F32)
        for ch in range(tr // CHUNK):
            rs_ = slice(ch * CHUNK, (ch + 1) * CHUNK)
            zu = zp_ref[rs_, 0:E].astype(F32)
            zv = zp_ref[rs_, E:E2].astype(F32)
            u, tu = _gelu(zu)
            v, tv = _gelu(zv)
            vn, xhat, rs = _layernorm_fwd(v, g_ref[...], b_ref[...])
            vnb = vn.astype(BF16)
            dpf = dp_ref[rs_, :].astype(F32)
            for gi in range(G):
                cs = slice(gi * CHUNK, (gi + 1) * CHUNK)
                w = ws_ref[gi]
                vs = jnp.dot(w, vnb[:, cs], preferred_element_type=F32) + bf_ref[:, cs]
                dz_ref[rs_, cs] = (dpf[:, cs] * vs * _gelu_grad(zu[:, cs], tu[:, cs])).astype(BF16)
                dvs = dpf[:, cs] * u[:, cs]
                dvsb = dvs.astype(BF16)
                dws_ref[gi] += lax.dot_general(dvsb, vnb[:, cs], NT, preferred_element_type=F32)
                dbf_ref[:, cs] += dvs
                dvn_ref[:, cs] = lax.dot_general(w, dvsb, TN, preferred_element_type=F32)
            dvn = dvn_ref[...]
            dlg = dlg + _colsum(dvn * xhat)
            dlb = dlb + _colsum(dvn)
            dv = _layernorm_bwd(dvn, xhat, rs, g_ref[...])
            dz_ref[rs_, E:E2] = (dv * _gelu_grad(zv, tv)).astype(BF16)
        _acc_rows(st_ref, i == 0, [(0, dlg), (1, dlb)])

        @pl.when(i == n - 1)
        def _():
            for gi in range(G):
                dbs_ref[:, gi:gi + 1] = jnp.sum(dbf_ref[:, gi * CHUNK:(gi + 1) * CHUNK], axis=1, keepdims=True)

    return pl.pallas_call(
        body, name=name, grid=(n,),
        in_specs=[_rowspec(tr, E2), _rowspec(tr, E), _fullspec((1, E)), _fullspec((1, E)),
                  _fullspec((G, CHUNK, CHUNK)), _fullspec((CHUNK, E))],
        out_specs=(_rowspec(tr, E2), _fullspec((G, CHUNK, CHUNK)), _fullspec((CHUNK, G)), _fullspec((8, E))),
        out_shape=(jax.ShapeDtypeStruct((T, E2), BF16), jax.ShapeDtypeStruct((G, CHUNK, CHUNK), F32),
                   jax.ShapeDtypeStruct((CHUNK, G), F32), jax.ShapeDtypeStruct((8, E), F32)),
        scratch_shapes=[pltpu.VMEM((CHUNK, E), F32), pltpu.VMEM((CHUNK, E), F32)],
        compiler_params=_params(1),
    )(zp, dp, ln_g, ln_b, ws, bfull)


def _halo_specs(tr, T, width):
    per = tr // HALO
    last = T // HALO - 1
    prev = pl.BlockSpec((HALO, width), lambda i: (jnp.maximum(i * per - 1, 0), 0))
    nxt = pl.BlockSpec((HALO, width), lambda i: (jnp.minimum((i + 1) * per, last), 0))
    return prev, nxt


def _halo_valid(i, n, nlat):
    return jnp.logical_and(i > 0, i != nlat), jnp.logical_and(i + 1 < n, i + 1 != nlat)


def _glu(tt, D):
    a = tt[:, 0:D].astype(F32)
    g = tt[:, D:2 * D].astype(F32)
    return a * _sigmoid(g)


CONV_SUB = 32
CONV_SPAN = 8 * ((CONV_W + 7) // 8) - 8


def _build_shifts(buf, sh, tr):
    for s in range(1, 8):
        sh[s - 1, 0:tr + CONV_SPAN, :] = buf[pl.ds(s, tr + CONV_SPAN), :]


def _tap(buf, sh, offset, r0, cs):
    s = offset % 8
    start = pl.multiple_of(r0 + (offset - s), 8)
    if s == 0:
        return buf[pl.ds(start, CONV_SUB), cs]
    return sh[s - 1, pl.ds(start, CONV_SUB), cs]


def conv_fwd(name, t, wdw, bdw, ln_g, ln_b, seq):
    T, D2 = t.shape
    D = D2 // 2
    tr = _row_tile(T, seq)
    n, nlat = T // tr, seq // tr
    prev, nxt = _halo_specs(tr, T, D2)

    tc = D // 2

    def body(tp_ref, tc_ref, tn_ref, w_ref, b_ref, g_ref, bb_ref, y_ref, yc_ref, s_ref, buf, sh):
        i = pl.program_id(0)
        pv, nv = _halo_valid(i, n, nlat)
        y = _glu(tc_ref[...], D)
        y_ref[...] = y
        buf[0:HALO, :] = jnp.where(pv, _glu(tp_ref[...], D), 0.0)
        buf[HALO:HALO + tr, :] = y
        buf[HALO + tr:2 * HALO + tr, :] = jnp.where(nv, _glu(tn_ref[...], D), 0.0)
        _build_shifts(buf, sh, tr)

        def rows_block(rb, carry):
            r0 = pl.multiple_of(rb * CONV_SUB, CONV_SUB)
            for c0 in range(0, D, tc):
                cs = slice(c0, c0 + tc)
                acc = jnp.zeros((CONV_SUB, tc), F32) + b_ref[:, cs]
                for k in range(CONV_W):
                    acc = acc + w_ref[k:k + 1, cs] * _tap(buf, sh, k + 1, r0, cs)
                yc_ref[pl.ds(r0, CONV_SUB), cs] = acc
            return carry

        lax.fori_loop(0, tr // CONV_SUB, rows_block, 0)
        yl, _, _ = _layernorm_fwd(yc_ref[...], g_ref[...], bb_ref[...])
        s_ref[...] = (yl * _sigmoid(yl)).astype(BF16)

    return pl.pallas_call(
        body, name=name, grid=(n,),
        in_specs=[prev, _rowspec(tr, D2), nxt, _fullspec((32, D)), _fullspec((1, D)), _fullspec((1, D)),
                  _fullspec((1, D))],
        out_specs=(_rowspec(tr, D), _rowspec(tr, D), _rowspec(tr, D)),
        out_shape=(jax.ShapeDtypeStruct((T, D), F32), jax.ShapeDtypeStruct((T, D), F32),
                   jax.ShapeDtypeStruct((T, D), BF16)),
        scratch_shapes=[pltpu.VMEM((tr + 2 * HALO, D), F32), pltpu.VMEM((7, tr + CONV_SPAN, D), F32)],
        compiler_params=_params(1),
    )(t, t, t, wdw, bdw, ln_g, ln_b)


def conv_bwd_norm(name, ds, yc, ln_g, ln_b):
    T, D = yc.shape
    tr = _tile(T, 256, 8)

    def body(ds_ref, yc_ref, g_ref, b_ref, o_ref, st_ref):
        i = pl.program_id(0)
        yl, xhat, rs = _layernorm_fwd(yc_ref[...], g_ref[...], b_ref[...])
        sig = _sigmoid(yl)
        dyl = ds_ref[...] * (sig * (1.0 + yl * (1.0 - sig)))
        dyc = _layernorm_bwd(dyl, xhat, rs, g_ref[...])
        o_ref[...] = dyc
        _acc_rows(st_ref, i == 0, [(0, _colsum(dyl * xhat)), (1, _colsum(dyl)), (2, _colsum(dyc))])

    return pl.pallas_call(
        body, name=name, grid=(T // tr,),
        in_specs=[_rowspec(tr, D), _rowspec(tr, D), _fullspec((1, D)), _fullspec((1, D))],
        out_specs=(_rowspec(tr, D), _fullspec((8, D))),
        out_shape=(jax.ShapeDtypeStruct((T, D), F32), jax.ShapeDtypeStruct((8, D), F32)),
        compiler_params=_params(1),
    )(ds, yc, ln_g, ln_b)


def conv_bwd_taps(name, dyc, y, t, wdw, seq):
    T, D = y.shape
    tr = _row_tile(T, seq)
    n, nlat = T // tr, seq // tr
    prev, nxt = _halo_specs(tr, T, D)

    tc = D // 2
    full = slice(0, tc)

    def body(dp_ref, dc_ref, dn_ref, yp_ref, ycur_ref, yn_ref, t_ref, w_ref, dt_ref, dw_ref,
             dbuf, ybuf, dsh, ysh, dwacc):
        i = pl.program_id(0)
        pv, nv = _halo_valid(i, n, nlat)

        @pl.when(i == 0)
        def _():
            dw_ref[...] = jnp.zeros_like(dw_ref)

        for c0 in range(0, D, tc):
            cs = slice(c0, c0 + tc)
            for buf, sh, p_ref, c_ref, n_ref in ((dbuf, dsh, dp_ref, dc_ref, dn_ref), (ybuf, ysh, yp_ref, ycur_ref, yn_ref)):
                buf[0:HALO, :] = jnp.where(pv, p_ref[:, cs], 0.0)
                buf[HALO:HALO + tr, :] = c_ref[:, cs]
                buf[HALO + tr:2 * HALO + tr, :] = jnp.where(nv, n_ref[:, cs], 0.0)
                _build_shifts(buf, sh, tr)
            dwacc[...] = jnp.zeros_like(dwacc)

            def rows_block(rb, carry):
                r0 = pl.multiple_of(rb * CONV_SUB, CONV_SUB)
                rows = pl.ds(r0, CONV_SUB)
                dcur = dc_ref[rows, cs]
                dy = jnp.zeros((CONV_SUB, tc), F32)
                for k in range(CONV_W):
                    prod = dcur * _tap(ybuf, ysh, k + 1, r0, full)
                    dwacc[k] += prod[0:8] + prod[8:16] + prod[16:24] + prod[24:32]
                    dy = dy + w_ref[k:k + 1, cs] * _tap(dbuf, dsh, CONV_W - k, r0, full)
                a = t_ref[rows, cs].astype(F32)
                sig = _sigmoid(t_ref[rows, slice(D + c0, D + c0 + tc)].astype(F32))
                dt_ref[rows, cs] = (dy * sig).astype(BF16)
                dt_ref[rows, slice(D + c0, D + c0 + tc)] = (dy * a * sig * (1.0 - sig)).astype(BF16)
                return carry

            lax.fori_loop(0, tr // CONV_SUB, rows_block, 0)
            for k in range(CONV_W):
                dw_ref[k:k + 1, cs] += jnp.sum(dwacc[k], axis=0, keepdims=True)

    return pl.pallas_call(
        body, name=name, grid=(n,),
        in_specs=[prev, _rowspec(tr, D), nxt, prev, _rowspec(tr, D), nxt, _rowspec(tr, 2 * D), _fullspec((32, D))],
        out_specs=(_rowspec(tr, 2 * D), _fullspec((32, D))),
        out_shape=(jax.ShapeDtypeStruct((T, 2 * D), BF16), jax.ShapeDtypeStruct((32, D), F32)),
        scratch_shapes=[pltpu.VMEM((tr + 2 * HALO, tc), F32), pltpu.VMEM((tr + 2 * HALO, tc), F32),
                        pltpu.VMEM((7, tr + CONV_SPAN, tc), F32), pltpu.VMEM((7, tr + CONV_SPAN, tc), F32),
                        pltpu.VMEM((32, 8, tc), F32)],
        compiler_params=_params(1),
    )(dyc, dyc, dyc, y, y, y, t, wdw)


Q_ROWS = 8
K_ROWS = 16
QN = Q_ROWS * GRID_W
WIN = K_ROWS * GRID_W
NA_Q_OFFSET = (0, 4, 8)


def _na_window_start(cls, qi):
    return (max(qi - NA_ROWS // 2, 0), qi, min(qi + NA_ROWS // 2, NA_ROWS))[cls]


def _na_pairs(cls):
    for qi in range(Q_ROWS):
        lo = _na_window_start(cls, qi)
        for kj in range(lo, lo + NA_ROWS):
            yield qi, kj, kj - NA_Q_OFFSET[cls] - qi + NA_ROWS - 1


def na_bias_tables(toep):
    H = toep.shape[0]
    neg = jnp.full((H, GRID_W, GRID_W), NEG_INF, F32)
    tables = []
    for cls in range(3):
        dr_of = {(qi, kj): dr for qi, kj, dr in _na_pairs(cls)}
        rows = [jnp.concatenate([toep[:, dr_of[(qi, kj)]] if (qi, kj) in dr_of else neg for kj in range(K_ROWS)],
                                axis=2) for qi in range(Q_ROWS)]
        tables.append(jnp.concatenate(rows, axis=1))
    return jnp.stack(tables)


def _na_class(rb, nblk):
    return jnp.where(rb == 0, 0, jnp.where(rb == nblk - 1, 2, 1))


def _na_specs(seq, ctx_rows, H):
    cb = seq // ctx_rows
    nblk = seq // QN
    return [
        pl.BlockSpec((QN, HEAD), lambda h, r: (r, h)),
        pl.BlockSpec((seq, HEAD), lambda h, r: (0, H + h)),
        pl.BlockSpec((seq, HEAD), lambda h, r: (0, 2 * H + h)),
        pl.BlockSpec((ctx_rows, HEAD), lambda h, r: (cb, H + h)),
        pl.BlockSpec((ctx_rows, HEAD), lambda h, r: (cb, 2 * H + h)),
        pl.BlockSpec((None, None, QN, WIN), lambda h, r: (_na_class(r, nblk), h, 0, 0)),
    ]


def _na_scores(q_ref, k_ref, v_ref, kc_ref, vc_ref, b_ref, rows):
    rb = pl.program_id(1)
    k_start = jnp.clip(Q_ROWS * rb - NA_ROWS // 2, 0, rows - K_ROWS)
    start = pl.multiple_of(k_start * GRID_W, GRID_W)
    scale = HEAD ** -0.5
    q = q_ref[...]
    kw = k_ref[pl.ds(start, WIN), :]
    vw = v_ref[pl.ds(start, WIN), :]
    kc = kc_ref[...]
    vc = vc_ref[...]
    s = lax.dot_general(q, kw, NT, preferred_element_type=F32) * scale + b_ref[...]
    sc = lax.dot_general(q, kc, NT, preferred_element_type=F32) * scale
    m = jnp.maximum(jnp.max(s, axis=-1, keepdims=True), jnp.max(sc, axis=-1, keepdims=True))
    p = jnp.exp(s - m)
    pc = jnp.exp(sc - m)
    l = jnp.sum(p, axis=-1, keepdims=True) + jnp.sum(pc, axis=-1, keepdims=True)
    return q, kw, vw, kc, vc, p, pc, l, start, scale


def na_fwd(name, qkv, bias, seq, tasks=()):
    T, D3 = qkv.shape
    D = D3 // 3
    H = D // HEAD
    rows = seq // GRID_W

    def body(q_ref, k_ref, v_ref, kc_ref, vc_ref, b_ref, o_ref):
        q, kw, vw, kc, vc, p, pc, l, start, scale = _na_scores(q_ref, k_ref, v_ref, kc_ref, vc_ref, b_ref, rows)
        o = (jnp.dot(p.astype(BF16), vw, preferred_element_type=F32)
             + jnp.dot(pc.astype(BF16), vc, preferred_element_type=F32))
        o_ref[...] = (o / l).astype(BF16)

    outs, touts = host_call(name, body, (H, seq // QN), [qkv, qkv, qkv, qkv, qkv, bias],
                            _na_specs(seq, T - seq, H), [jax.ShapeDtypeStruct((seq, D), BF16)],
                            [pl.BlockSpec((QN, HEAD), lambda h, r: (r, h))], tasks=tasks)
    return outs[0], touts


def na_bwd(name, qkv, bias, do, seq):
    T, D3 = qkv.shape
    D = D3 // 3
    H = D // HEAD
    rows = seq // GRID_W
    nblk = seq // QN
    ctx_rows = T - seq

    def body(q_ref, k_ref, v_ref, kc_ref, vc_ref, b_ref, do_ref, dq_ref, dk_ref, dv_ref, dkc_ref, dvc_ref, db_ref):
        rb = pl.program_id(1)

        @pl.when(rb == 0)
        def _():
            for ref in (dk_ref, dv_ref, dkc_ref, dvc_ref):
                ref[...] = jnp.zeros_like(ref)

        @pl.when(jnp.logical_or(rb <= 1, rb == nblk - 1))
        def _():
            db_ref[...] = jnp.zeros_like(db_ref)

        q, kw, vw, kc, vc, p, pc, l, start, scale = _na_scores(q_ref, k_ref, v_ref, kc_ref, vc_ref, b_ref, rows)
        inv = 1.0 / l
        pn = p * inv
        pcn = pc * inv
        do_ = do_ref[...]
        dp = lax.dot_general(do_, vw, NT, preferred_element_type=F32)
        dpc = lax.dot_general(do_, vc, NT, preferred_element_type=F32)
        delta = jnp.sum(pn * dp, axis=-1, keepdims=True) + jnp.sum(pcn * dpc, axis=-1, keepdims=True)
        ds = pn * (dp - delta)
        dsc = pcn * (dpc - delta)
        db_ref[...] += ds
        dsb = (ds * scale).astype(BF16)
        dscb = (dsc * scale).astype(BF16)
        dq = jnp.dot(dsb, kw, preferred_element_type=F32) + jnp.dot(dscb, kc, preferred_element_type=F32)
        dq_ref[...] = dq.astype(BF16)
        dk_ref[pl.ds(start, WIN), :] += lax.dot_general(dsb, q, TN, preferred_element_type=F32)
        dv_ref[pl.ds(start, WIN), :] += lax.dot_general(pn.astype(BF16), do_, TN, preferred_element_type=F32)
        dkc_ref[...] += lax.dot_general(dscb, q, TN, preferred_element_type=F32)
        dvc_ref[...] += lax.dot_general(pcn.astype(BF16), do_, TN, preferred_element_type=F32)

    head_lat = pl.BlockSpec((seq, HEAD), lambda h, r: (0, h))
    head_ctx = pl.BlockSpec((ctx_rows, HEAD), lambda h, r: (0, h))
    return pl.pallas_call(
        body, name=name, grid=(H, nblk),
        in_specs=_na_specs(seq, ctx_rows, H) + [pl.BlockSpec((QN, HEAD), lambda h, r: (r, h))],
        out_specs=(pl.BlockSpec((QN, HEAD), lambda h, r: (r, h)), head_lat, head_lat, head_ctx, head_ctx,
                   pl.BlockSpec((None, None, QN, WIN), lambda h, r: (_na_class(r, nblk), h, 0, 0))),
        out_shape=(jax.ShapeDtypeStruct((seq, D), BF16), jax.ShapeDtypeStruct((seq, D), F32),
                   jax.ShapeDtypeStruct((seq, D), F32), jax.ShapeDtypeStruct((ctx_rows, D), F32),
                   jax.ShapeDtypeStruct((ctx_rows, D), F32), jax.ShapeDtypeStruct(bias.shape, F32)),
        compiler_params=_params(2),
    )(qkv, qkv, qkv, qkv, qkv, bias, do)


def _ctx_specs(seq, ctx_rows, H):
    cb = seq // ctx_rows
    return [pl.BlockSpec((ctx_rows, HEAD), lambda h: (cb, h)),
            pl.BlockSpec((ctx_rows, HEAD), lambda h: (cb, H + h)),
            pl.BlockSpec((ctx_rows, HEAD), lambda h: (cb, 2 * H + h))]


def _ctx_probs(q_ref, k_ref):
    s = lax.dot_general(q_ref[...], k_ref[...], NT, preferred_element_type=F32) * (HEAD ** -0.5)
    p = jnp.exp(s - jnp.max(s, axis=-1, keepdims=True))
    return p / jnp.sum(p, axis=-1, keepdims=True)


def ctx_attn_fwd(name, qkv, seq):
    T, D3 = qkv.shape
    D = D3 // 3
    H = D // HEAD
    ctx_rows = T - seq

    def body(q_ref, k_ref, v_ref, o_ref):
        p = _ctx_probs(q_ref, k_ref)
        o_ref[...] = jnp.dot(p.astype(BF16), v_ref[...], preferred_element_type=F32).astype(BF16)

    return pl.pallas_call(
        body, name=name, grid=(H,), in_specs=_ctx_specs(seq, ctx_rows, H),
        out_specs=pl.BlockSpec((ctx_rows, HEAD), lambda h: (0, h)),
        out_shape=jax.ShapeDtypeStruct((ctx_rows, D), BF16), compiler_params=_params(1),
    )(qkv, qkv, qkv)


def ctx_attn_bwd(name, qkv, do, dkc_lat, dvc_lat, seq):
    T, D3 = qkv.shape
    D = D3 // 3
    H = D // HEAD
    ctx_rows = T - seq
    cb = seq // ctx_rows
    scale = HEAD ** -0.5

    def body(q_ref, k_ref, v_ref, do_ref, dkl_ref, dvl_ref, dq_ref, dk_ref, dv_ref):
        p = _ctx_probs(q_ref, k_ref)
        do_ = do_ref[...]
        dp = lax.dot_general(do_, v_ref[...], NT, preferred_element_type=F32)
        ds = p * (dp - jnp.sum(p * dp, axis=-1, keepdims=True))
        dsb = (ds * scale).astype(BF16)
        dq_ref[...] = jnp.dot(dsb, k_ref[...], preferred_element_type=F32).astype(BF16)
        dk_ref[...] = (dkl_ref[...] + lax.dot_general(dsb, q_ref[...], TN, preferred_element_type=F32)).astype(BF16)
        dv_ref[...] = (dvl_ref[...]
                       + lax.dot_general(p.astype(BF16), do_, TN, preferred_element_type=F32)).astype(BF16)

    blk = pl.BlockSpec((ctx_rows, HEAD), lambda h: (0, h))
    shp = jax.ShapeDtypeStruct((ctx_rows, D), BF16)
    return pl.pallas_call(
        body, name=name, grid=(H,),
        in_specs=_ctx_specs(seq, ctx_rows, H) + [pl.BlockSpec((ctx_rows, HEAD), lambda h: (cb, h)), blk, blk],
        out_specs=(blk, blk, blk), out_shape=(shp, shp, shp), compiler_params=_params(1),
    )(qkv, qkv, qkv, do, dkc_lat, dvc_lat)


def _rpb_tables():
    qc = jnp.arange(GRID_W)[:, None]
    kc = jnp.arange(GRID_W)[None, :]
    rel = (kc - qc + NA_COLS - 1).reshape(1, GRID_W * GRID_W)
    onehot = (rel == jnp.arange(32)[:, None]).astype(F32)
    c_start = jnp.clip(qc - NA_COLS // 2, 0, GRID_W - NA_COLS)
    mask = jnp.logical_and(kc >= c_start, kc < c_start + NA_COLS).astype(F32).reshape(1, GRID_W * GRID_W)
    return onehot, mask


def rpb_expand(name, rpb2, onehot, mask):
    R = rpb2.shape[0]

    def body(r_ref, oh_ref, m_ref, o_ref):
        t = jnp.dot(r_ref[...], oh_ref[...], preferred_element_type=F32, precision=lax.Precision.HIGHEST)
        o_ref[...] = jnp.where(m_ref[...] > 0.5, t, NEG_INF)

    return pl.pallas_call(body, name=name, out_shape=jax.ShapeDtypeStruct((R, GRID_W * GRID_W), F32),
                          compiler_params=_params())(rpb2, onehot, mask)


def rpb_fold(name, x, classes):
    H = x.shape[0]
    n_dr = 2 * NA_ROWS - 1

    def body(x_ref, y_ref):
        acc = [None] * n_dr
        for cls in classes:
            for qi, kj, dr in _na_pairs(cls):
                acc[dr] = x_ref[cls, qi, kj] if acc[dr] is None else acc[dr] + x_ref[cls, qi, kj]
        for dr in range(n_dr):
            y_ref[dr] = acc[dr]

    return pl.pallas_call(
        body, name=name, grid=(H,),
        in_specs=[pl.BlockSpec((None, 3, Q_ROWS, K_ROWS, GRID_W, GRID_W), lambda h: (h, 0, 0, 0, 0, 0))],
        out_specs=pl.BlockSpec((None, n_dr, GRID_W, GRID_W), lambda h: (h, 0, 0, 0)),
        out_shape=jax.ShapeDtypeStruct((H, n_dr, GRID_W, GRID_W), F32), compiler_params=_params(1),
    )(x)


def rpb_reduce(name, y2, onehot_t):
    R = y2.shape[0]

    def body(y_ref, oh_ref, o_ref):
        o_ref[...] = jnp.dot(y_ref[...], oh_ref[...], preferred_element_type=F32, precision=lax.Precision.HIGHEST)

    return pl.pallas_call(body, name=name, out_shape=jax.ShapeDtypeStruct((R, 32), F32),
                          compiler_params=_params())(y2, onehot_t)


def _adam(g, w, m, v):
    m2 = ADAM_B1 * m + (1.0 - ADAM_B1) * g
    v2 = ADAM_B2 * v + (1.0 - ADAM_B2) * (g * g)
    m_hat = m2 / (1.0 - ADAM_B1 ** ADAM_STEP)
    v_hat = v2 / (1.0 - ADAM_B2 ** ADAM_STEP)
    delta = -ADAM_LR * (m_hat / (jnp.sqrt(v_hat) + ADAM_EPS) + ADAM_WD * w)
    return delta, m2, v2


def adam_parts(name, land, land2, w, m, v, tasks=()):
    shape = w.shape
    C = shape[-1]
    R = math.prod(shape[:-1])
    tr = _tile(R, max(16, (256 * 1024 // C) // 16 * 16), 16)
    l1, l2 = land.reshape(N_SHARD, R, C), land2.reshape(N_SHARD, R, C)

    def body(l1_ref, l2_ref, w_ref, m_ref, v_ref, g_ref, d_ref, m2_ref, v2_ref):
        a = l1_ref[0].astype(F32)
        b = l2_ref[0].astype(F32)
        for k in range(1, N_SHARD):
            a = a + l1_ref[k].astype(F32)
            b = b + l2_ref[k].astype(F32)
        g = a + b
        g_ref[...] = g
        d_ref[...], m2_ref[...], v2_ref[...] = _adam(g, w_ref[...], m_ref[...], v_ref[...])

    part = pl.BlockSpec((N_SHARD, tr, C), lambda i: (0, i, 0))
    row = _rowspec(tr, C)
    shp = jax.ShapeDtypeStruct((R, C), F32)
    outs, touts = host_call(name, body, (R // tr,), [l1, l2, w.reshape(R, C), m.reshape(R, C), v.reshape(R, C)],
                            [part, part, row, row, row], (shp,) * 4, (row,) * 4, tasks=tasks)
    return tuple(o.reshape(shape) for o in outs), touts


def adam_flat(name, g, w, m, v):
    R, C = g.shape
    tr = _tile(R, 256, 8)

    def body(g_ref, w_ref, m_ref, v_ref, d_ref, m2_ref, v2_ref):
        d_ref[...], m2_ref[...], v2_ref[...] = _adam(g_ref[...], w_ref[...], m_ref[...], v_ref[...])

    row = _rowspec(tr, C)
    shp = jax.ShapeDtypeStruct((R, C), F32)
    return pl.pallas_call(body, name=name, grid=(R // tr,), in_specs=[row] * 4, out_specs=(row,) * 3,
                          out_shape=(shp,) * 3, compiler_params=_params(1))(g, w, m, v)


def reduce_8(name, gathered):
    _, R, D = gathered.shape
    tr = _tile(R, 64, 8)

    def body(x_ref, o_ref):
        acc = x_ref[0]
        for k in range(1, 8):
            acc = acc + x_ref[k]
        o_ref[...] = acc

    return pl.pallas_call(
        body, name=name, grid=(R // tr,), in_specs=[pl.BlockSpec((8, tr, D), lambda i: (0, i, 0))],
        out_specs=_rowspec(tr, D), out_shape=jax.ShapeDtypeStruct((R, D), F32), compiler_params=_params(1),
    )(gathered)


def ada_fwd(name, craw16, ada_w, ada_b3):
    L, D, Cs = ada_w.shape
    tn = _tile(Cs, 512)

    def body(c_ref, w_ref, b_ref, o_ref):
        cc = c_ref[...]
        s = cc * _sigmoid(cc)
        o_ref[...] = jnp.dot(s, w_ref[...], preferred_element_type=F32,
                             precision=lax.Precision.HIGHEST) + b_ref[...]

    return pl.pallas_call(
        body, name=name, grid=(L, Cs // tn),
        in_specs=[pl.BlockSpec((16, D), lambda l, j: (0, 0)), pl.BlockSpec((None, D, tn), lambda l, j: (l, 0, j)),
                  pl.BlockSpec((None, 1, tn), lambda l, j: (l, 0, j))],
        out_specs=pl.BlockSpec((None, 16, tn), lambda l, j: (l, 0, j)),
        out_shape=jax.ShapeDtypeStruct((L, 16, Cs), F32), compiler_params=_params(2),
    )(craw16, ada_w, ada_b3)


def ada_bwd_adam(name, craw16_t, dm16, dmc8, w, m, v):
    L, D, Cs = w.shape
    tn = _tile(Cs, 256)

    def body(c_ref, dm_ref, dc_ref, w_ref, m_ref, v_ref, g_ref, d_ref, m2_ref, v2_ref, ds_ref):
        step = pl.program_id(0) * (Cs // tn) + pl.program_id(1)
        cc = c_ref[...]
        s_t = cc * _sigmoid(cc)
        g = jnp.dot(s_t, dm_ref[...], preferred_element_type=F32, precision=lax.Precision.HIGHEST)
        ww = w_ref[...]
        g_ref[...] = g
        d_ref[...], m2_ref[...], v2_ref[...] = _adam(g, ww, m_ref[...], v_ref[...])

        @pl.when(step == 0)
        def _():
            ds_ref[...] = jnp.zeros_like(ds_ref)

        ds_ref[...] += lax.dot_general(dc_ref[...].astype(BF16), ww.astype(BF16), NT, preferred_element_type=F32)

    wspec = pl.BlockSpec((None, D, tn), lambda l, j: (l, 0, j))
    shp = jax.ShapeDtypeStruct((L, D, Cs), F32)
    return pl.pallas_call(
        body, name=name, grid=(L, Cs // tn),
        in_specs=[pl.BlockSpec((D, 16), lambda l, j: (0, 0)), pl.BlockSpec((None, 16, tn), lambda l, j: (l, 0, j)),
                  pl.BlockSpec((None, 8, tn), lambda l, j: (l, 0, j)), wspec, wspec, wspec],
        out_specs=(wspec, wspec, wspec, wspec, pl.BlockSpec((8, D), lambda l, j: (0, 0))),
        out_shape=(shp, shp, shp, shp, jax.ShapeDtypeStruct((8, D), F32)), compiler_params=_params(2),
    )(craw16_t, dm16, dmc8, w, m, v)


def cctx_adam(name, ds_all, c_ctx, m, v):
    D = c_ctx.shape[1]

    def body(ds_ref, c_ref, m_ref, v_ref, g_ref, d_ref, m2_ref, v2_ref):
        ds = ds_ref[0, 0:1, :]
        for slot in (2, 4, 6):
            ds = ds + ds_ref[slot, 0:1, :]
        cc = c_ref[...]
        sig = _sigmoid(cc)
        g = ds * (sig * (1.0 + cc * (1.0 - sig)))
        g_ref[...] = g
        d_ref[...], m2_ref[...], v2_ref[...] = _adam(g, cc, m_ref[...], v_ref[...])

    shp = jax.ShapeDtypeStruct((1, D), F32)
    return pl.pallas_call(body, name=name, out_shape=(shp,) * 4, compiler_params=_params())(ds_all, c_ctx, m, v)


WEIGHT_NAMES = ['c_ctx', 'ada_w', 'ada_b', 'g_mix', 'g_ffn', 'ffn_w1', 'ffn_w3', 'ffn_w2', 'a_w_in', 'a_ln_g',
                'a_ln_b', 'a_w_s', 'a_b_s', 'a_w_out', 'b_w_qkv', 'b_rpb', 'b_w_out', 'c_w_pw1', 'c_w_dw', 'c_b_dw',
                'c_ln_g', 'c_ln_b', 'c_w_pw2', 'g_final']
BIG_NAMES = ['ffn_w1', 'ffn_w3', 'ffn_w2', 'a_w_in', 'a_w_out', 'b_w_qkv', 'b_w_out', 'c_w_pw1', 'c_w_pw2']
SMALL_NAMES = ['ada_b', 'g_mix', 'g_ffn', 'a_ln_g', 'a_ln_b', 'a_w_s', 'a_b_s', 'b_rpb', 'c_w_dw', 'c_b_dw',
               'c_ln_g', 'c_ln_b', 'g_final']
SMALL_PACK_COLS = 512
MIXER_IN = ('a_w_in', 'b_w_qkv', 'c_w_pw1')
MIXER_OUT = ('a_w_out', 'b_w_out', 'c_w_pw2')

FWD_PLAN = {
    "pre": [("a_w_in", 0)],
    "in_0": [("a_w_out", 0), ("ffn_w1", 0)],
    "out_0": [("ffn_w3", 0)],
    "ffn_up_0": [("ffn_w2", 0), ("b_w_qkv", 0)],
    "ffn_down_0": [("b_w_out", 0), ("ffn_w1", 1)],
    "in_1": [("ffn_w3", 1)],
    "b_na_1": [("ffn_w2", 1), ("c_w_pw1", 0), ("c_w_pw2", 0)],
    "out_1": [("ffn_w1", 2)],
    "ffn_up_1": [("ffn_w3", 2), ("ffn_w2", 2)],
    "ffn_down_1": [("a_w_in", 1), ("a_w_out", 1)],
    "in_2": [("ffn_w1", 3)],
    "ffn_up_2": [("ffn_w3", 3), ("ffn_w2", 3)],
}


def _bwd_plan():
    plan = {}
    for i in range(N_LAYERS):
        w_in, w_out = (MIXER_IN[i % 3], i // 3), (MIXER_OUT[i % 3], i // 3)
        if i + 1 < N_LAYERS:
            plan[f"ffn_down_dx_{i}"] = [("forward", MIXER_IN[(i + 1) % 3], (i + 1) // 3)]
        plan[f"ffn_w1_dw_{i}"] = [("scatter", "ffn_w2", i)]
        plan[f"ffn_w3_dw_{i}"] = [("forward", "ffn_w2", i), ("scatter", "ffn_w1", i)]
        plan[f"ffn_up_dx_{i}"] = [("forward", "ffn_w1", i), ("scatter", "ffn_w3", i)]
        plan[f"out_dx_{i}"] = [("forward", "ffn_w3", i)]
        plan[f"in_dw_{i}"] = [("scatter",) + w_out]
        plan[f"in_dx_{i}"] = [("forward",) + w_out, ("scatter",) + w_in]
    plan["rs_post"] = [("forward", MIXER_IN[0], 0)]
    return plan


BWD_PLAN = _bwd_plan()


def _pad_rows(a, mult):
    r = (-a.shape[0]) % mult
    return a if r == 0 else jnp.concatenate([a, jnp.zeros((r,) + a.shape[1:], a.dtype)], axis=0)


def _rows_of(flat, D):
    n = flat.shape[0]
    r = -(-n // D)
    return jnp.concatenate([flat, jnp.zeros((r * D - n,), flat.dtype)]).reshape(r, D)


def _step(W, Mo, Vo, x, c, ctx, loss_target):
    seq, D = x.shape[1], x.shape[2]
    ctx_rows = ctx.shape[1]
    T = seq + ctx_rows
    L = N_LAYERS
    H = D // HEAD
    G = D // CHUNK
    xi, yi, ci = _xyc()
    e_idx = 4 * xi + 2 * yi + ci
    s_idx = 2 * xi + yi

    c_all = all_gather_8("ag_c", c)
    craw16 = jnp.concatenate([c_all.reshape(8, D), W['c_ctx'].reshape(1, D), jnp.zeros((7, D), F32)], axis=0)
    ada_w = W['ada_w']
    Cs = ada_w.shape[2]
    ada_b_s = lax.dynamic_slice_in_dim(W['ada_b'], s_idx * Cs, Cs, axis=1).reshape(L, 1, Cs)
    mod_s = ada_fwd("ada_fwd", craw16, ada_w, ada_b_s)
    mod_g = all_gather_xy("ag_mod", mod_s).transpose(1, 2, 0, 3).reshape(L, 16, N_SHARD * Cs)
    mod_lat = lax.dynamic_index_in_dim(mod_g, e_idx, axis=1, keepdims=False).reshape(L, 6, D)
    mod_all = jnp.concatenate([mod_lat, mod_g[:, 8].reshape(L, 6, D), jnp.zeros((L, 4, D), F32)], axis=1)

    Wg = {}
    land = {n: lax.empty((N_SHARD,) + W[n].shape, BF16) for n in BIG_NAMES}
    land2 = {n: lax.empty((N_SHARD,) + W[n].shape, BF16) for n in BIG_NAMES}
    dW = {}

    def gather_tasks(host):
        return [GatherTask(W[n][l].astype(BF16)) for n, l in FWD_PLAN.get(host, ())]

    def gathered(host, touts):
        for (n, l), out in zip(FWD_PLAN.get(host, ()), touts):
            Wg[(n, l)] = out[0]

    def scatter_tasks(host):
        tasks = []
        for kind, n, l in BWD_PLAN.get(host, ()):
            if kind == "scatter":
                tasks.append(ScatterTask(dW[(n, l)], land[n], land2[n], l))
            else:
                tasks.append(ForwardTask(land[n], land2[n], l))
        return tasks

    def scattered(host, touts):
        for (kind, n, l), out in zip(BWD_PLAN.get(host, ()), touts):
            land[n], land2[n] = out

    gathered("pre", comm_only("ag_pre", gather_tasks("pre")))

    n_a, n_c = W['a_ln_g'].shape[0], W['c_ln_g'].shape[0]
    sh_rows = jnp.concatenate([W['a_ln_g'], W['a_ln_b'], W['c_w_dw'].reshape(n_c * CONV_W, -1), W['c_b_dw'],
                               W['c_ln_g'], W['c_ln_b']], axis=0)
    n_sh = sh_rows.shape[0]
    sh_full = all_gather_xy("ag_small", _pad_rows(sh_rows, 8)).transpose(1, 0, 2).reshape(-1, D)[:n_sh]
    o = 0
    a_ln_g_f, o = sh_full[o:o + n_a], o + n_a
    a_ln_b_f, o = sh_full[o:o + n_a], o + n_a
    c_w_dw_f, o = sh_full[o:o + n_c * CONV_W].reshape(n_c, CONV_W, D), o + n_c * CONV_W
    c_b_dw_f, o = sh_full[o:o + n_c], o + n_c
    c_ln_g_f, o = sh_full[o:o + n_c], o + n_c
    c_ln_b_f, o = sh_full[o:o + n_c], o + n_c

    onehot, colmask = _rpb_tables()
    n_dr = 2 * NA_ROWS - 1
    rpb2 = jnp.pad(W['b_rpb'][0].reshape(H * n_dr, 2 * NA_COLS - 1), ((0, 0), (0, 1)))
    toep = rpb_expand("rpb_expand", rpb2, onehot, colmask).reshape(H, n_dr, GRID_W, GRID_W)
    bias = na_bias_tables(toep)
    na_classes = (0, 1, 2) if seq // QN > 2 else (0, 2)

    def mixer_params(i):
        mixer, j = i % 3, i // 3
        if mixer == 0:
            return dict(ln_g=a_ln_g_f[j:j + 1], ln_b=a_ln_b_f[j:j + 1], ws=W['a_w_s'][j].astype(BF16),
                        bfull=jnp.repeat(W['a_b_s'][j].T, CHUNK, axis=1))
        if mixer == 2:
            return dict(wdw=_pad_rows(c_w_dw_f[j], 32), bdw=c_b_dw_f[j:j + 1], ln_g=c_ln_g_f[j:j + 1],
                        ln_b=c_ln_b_f[j:j + 1])
        return {}

    def fwd(fn, host, *args):
        res, touts = fn(host, *args, tasks=gather_tasks(host))
        gathered(host, touts)
        return res

    def bwd(fn, host, *args, extra=(), **kw):
        tasks = scatter_tasks(host)
        res, touts = fn(host, *args, tasks=tasks + list(extra), **kw)
        scattered(host, touts[:len(tasks)])
        return (res, touts[len(tasks):]) if extra else res

    h = jnp.concatenate([x[0], ctx[0]], axis=0)
    saved = []
    for i in range(L):
        mixer, j = i % 3, i // 3
        n_in, n_out = MIXER_IN[mixer], MIXER_OUT[mixer]
        if i == L - 1:
            h = h[:seq]
        Ti = h.shape[0]
        tm = _tile(Ti, 768)
        tmh = _tile(Ti, 384)
        mod = mod_all[i]
        mp = mixer_params(i)
        s = dict(h0=h, mp=mp)
        hm = nm_fwd(f"nm1_{i}", h, W['g_mix'][i:i + 1], mod, 0, 1, seq)
        s['hm'] = hm
        u = fwd(mm_cols, f"in_{i}", hm, Wg[(n_in, j)], BF16, tm)
        if mixer == 0:
            p = gmlp_fwd(f"a_mid_{i}", u, mp['ln_g'], mp['ln_b'], mp['ws'], mp['bfull'])
        elif mixer == 1:
            p = jnp.concatenate([fwd(na_fwd, f"b_na_{i}", u, bias, seq), ctx_attn_fwd(f"b_ctx_{i}", u, seq)], axis=0)
        else:
            y, yc, p = conv_fwd(f"c_mid_{i}", u, mp['wdw'], mp['bdw'], mp['ln_g'], mp['ln_b'], seq)
            s.update(y=y, yc=yc)
        s.update(u=u, p=p)
        m1, h = fwd(mm_rows_residual, f"out_{i}", p, Wg[(n_out, j)], h, mod, 2, seq, tm)
        s.update(m1=m1, h1=h)
        hf = nm_fwd(f"nm2_{i}", h, W['g_ffn'][i:i + 1], mod, 3, 4, seq)
        a, b, act = fwd(mm_ffn_up, f"ffn_up_{i}", hf, Wg[('ffn_w1', i)], Wg[('ffn_w3', i)], tmh)
        m2, h = fwd(mm_rows_residual, f"ffn_down_{i}", act, Wg[('ffn_w2', i)], h, mod, 5, seq, tm)
        s.update(hf=hf, a=a, b=b, act=act, m2=m2)
        saved.append(s)

    dh, st_loss = loss_head("loss_head", h, loss_target[0], W['g_final'].reshape(1, D))

    dmod_lat, dmod_ctx, dmod_tot = [None] * L, [None] * L, [None] * L
    dg_mix, dg_ffn = [None] * L, [None] * L
    small = {}

    def set_dmod(i, st_n1, st_g1, st_n2, st_g2):
        for dst, r_n, r_g in ((dmod_lat, (1, 2), 0), (dmod_ctx, (3, 4), 1), (dmod_tot, (5, 6), 2)):
            dst[i] = jnp.concatenate([st_n1[r_n[0]:r_n[0] + 1], st_n1[r_n[1]:r_n[1] + 1], st_g1[r_g:r_g + 1],
                                      st_n2[r_n[0]:r_n[0] + 1], st_n2[r_n[1]:r_n[1] + 1], st_g2[r_g:r_g + 1]], axis=0)

    packs = {}

    def small_pack(part):
        layers = [0] if part == 'layer0' else list(range(1, L))
        a_parts = [small[('a', j)] for j in range(n_a) if (3 * j in layers)]
        cat = lambda xs: jnp.concatenate(xs, axis=0)
        entries = [('dmod_lat', cat([dmod_lat[i] for i in layers])), ('dmod_ctx', cat([dmod_ctx[i] for i in layers])),
                   ('ada_b', cat([dmod_tot[i] for i in layers])),
                   ('g_mix', cat([dg_mix[i] for i in layers])), ('g_ffn', cat([dg_ffn[i] for i in layers]))]
        if a_parts:
            entries += [('a_ln_g', cat([p[0] for p in a_parts])), ('a_ln_b', cat([p[1] for p in a_parts])),
                        ('a_w_s', cat([p[2] for p in a_parts])), ('a_b_s', cat([p[3] for p in a_parts]))]
        if part == 'rest':
            c_parts = [small[('c', j)] for j in range(n_c)]
            entries += [('b_rpb', small[('b', 0)]),
                        ('c_w_dw', cat([p[0] for p in c_parts])), ('c_b_dw', cat([p[1] for p in c_parts])),
                        ('c_ln_g', cat([p[2] for p in c_parts])), ('c_ln_b', cat([p[3] for p in c_parts])),
                        ('g_final', st_loss[0:1]), ('loss', st_loss[1:2])]
        offsets, o = {}, 0
        for n, arr in entries:
            offsets[n] = (o, arr.shape[0])
            o += arr.shape[0]
        return _pad_rows(cat([arr for _, arr in entries]), 64), offsets

    def bwd_gather(fn, host, part, *args):
        pack, offsets = small_pack(part)
        res, touts = bwd(fn, host, *args, extra=[GatherAllTask(pack)])
        packs[part] = (touts[0][0], offsets)
        return res

    for i in reversed(range(L)):
        mixer, j = i % 3, i // 3
        n_in, n_out = MIXER_IN[mixer], MIXER_OUT[mixer]
        s = saved[i]
        mp = s['mp']
        mod = mod_all[i]
        if i == L - 2:
            dh = jnp.concatenate([dh, jnp.zeros((ctx_rows, D), F32)], axis=0)
        Ti = dh.shape[0]
        tm = _tile(Ti, 768)
        tmh = _tile(Ti, 384)
        dm2, st_g2 = gate_bwd(f"gate2_bwd_{i}", dh, s['m2'], mod, 5, seq)
        da, db = bwd(mm_rows_dgrad, f"ffn_down_dx_{i}", dm2, Wg[('ffn_w2', i)], BF16, tm, ffn_ab=(s['a'], s['b']))
        dW[('ffn_w2', i)] = bwd(mm_wgrad_rows, f"ffn_w2_dw_{i}", s['act'], dm2, tm)
        dW[('ffn_w1', i)] = bwd(mm_wgrad_cols, f"ffn_w1_dw_{i}", s['hf'], da, tm)
        dW[('ffn_w3', i)] = bwd(mm_wgrad_cols, f"ffn_w3_dw_{i}", s['hf'], db, tm)
        ffn_dx_args = (f"ffn_up_dx_{i}", [da, db], [Wg[('ffn_w1', i)], Wg[('ffn_w3', i)]], tmh)
        dhf = bwd_gather(mm_cols_dgrad, ffn_dx_args[0], 'rest', *ffn_dx_args[1:]) if i == 0 else \
            bwd(mm_cols_dgrad, *ffn_dx_args)
        dh, st_n2 = nm_bwd(f"nm2_bwd_{i}", s['h1'], dhf, dh, W['g_ffn'][i:i + 1], mod, 4, seq)
        dm1, st_g1 = gate_bwd(f"gate1_bwd_{i}", dh, s['m1'], mod, 2, seq)
        dp = bwd(mm_rows_dgrad, f"out_dx_{i}", dm1, Wg[(n_out, j)], F32 if mixer == 2 else BF16, tm)[0]
        dW[(n_out, j)] = bwd(mm_wgrad_rows, f"out_dw_{i}", s['p'], dm1, tm)
        if mixer == 0:
            du, dws, dbs, st_a = gmlp_bwd(f"a_mid_bwd_{i}", s['u'], dp, mp['ln_g'], mp['ln_b'], mp['ws'], mp['bfull'])
            small[('a', j)] = (st_a[0:1], st_a[1:2], dws.reshape(-1, D), dbs.T.reshape(1, D))
        elif mixer == 1:
            dq, dk, dv, dkc, dvc, dbias = na_bwd(f"b_na_bwd_{i}", s['u'], bias, dp, seq)
            dqc, dkc, dvc = ctx_attn_bwd(f"b_ctx_bwd_{i}", s['u'], dp, dkc, dvc, seq)
            du = jnp.concatenate([jnp.concatenate([dq, dk.astype(BF16), dv.astype(BF16)], axis=1),
                                  jnp.concatenate([dqc, dkc, dvc], axis=1)], axis=0)
            blocks = dbias.reshape(3, H, Q_ROWS, GRID_W, K_ROWS, GRID_W).transpose(1, 0, 2, 4, 3, 5)
            folded = rpb_fold(f"rpb_fold_{i}", blocks, na_classes)
            drpb = rpb_reduce(f"rpb_reduce_{i}", folded.reshape(H * n_dr, GRID_W * GRID_W), onehot.T)
            small[('b', j)] = _rows_of(drpb[:, :2 * NA_COLS - 1].reshape(-1), D)
        else:
            dyc, st_c = conv_bwd_norm(f"c_norm_bwd_{i}", dp, s['yc'], mp['ln_g'], mp['ln_b'])
            du, dwdw = conv_bwd_taps(f"c_taps_bwd_{i}", dyc, s['y'], s['u'], mp['wdw'], seq)
            small[('c', j)] = (dwdw, st_c[2:3], st_c[0:1], st_c[1:2])
        if i == 0:
            dg_mix[0], dg_ffn[0] = jnp.zeros((1, D), F32), st_n2[0:1]
            set_dmod(0, jnp.zeros((8, D), F32), st_g1, st_n2, st_g2)
            dW[(n_in, j)] = bwd_gather(mm_wgrad_cols, f"in_dw_{i}", 'layer0', s['hm'], du, tm)
        else:
            dW[(n_in, j)] = bwd(mm_wgrad_cols, f"in_dw_{i}", s['hm'], du, tm)
        dhm = bwd(mm_cols_dgrad, f"in_dx_{i}", [du], [Wg[(n_in, j)]], tm)
        dh, st_n1 = nm_bwd(f"nm1_bwd_{i}", s['h0'], dhm, dh, W['g_mix'][i:i + 1], mod, 1, seq)
        if i > 0:
            dg_mix[i], dg_ffn[i] = st_n1[0:1], st_n2[0:1]
            set_dmod(i, st_n1, st_g1, st_n2, st_g2)
    grad_x = dh[:seq].reshape(1, seq, D)

    late = _pad_rows(jnp.concatenate([st_n1[1:7], st_n1[0:1]], axis=0), 8)
    touts = comm_only("ag_small_late", [GatherAllTask(late)] + scatter_tasks("rs_post"))
    gathered_late = touts[0][0]
    scattered("rs_post", touts[1:])
    sums = {part: reduce_8("reduce_small_" + part, packs[part][0]) for part in ('layer0', 'rest')}
    sums_late = reduce_8("reduce_small_late", gathered_late)
    late_rows = {'dmod_ctx': sums_late[2:4], 'ada_b': sums_late[4:6], 'g_mix': sums_late[6:7]}
    out = {}

    def rows_of(n, arrays, axis):
        found = []
        for part in ('layer0', 'rest'):
            if n in packs[part][1]:
                lo, cnt = packs[part][1][n]
                found.append(lax.slice_in_dim(arrays[part], lo, lo + cnt, axis=axis))
        return jnp.concatenate(found, axis=axis)

    def summed(n):
        rows = rows_of(n, sums, 0)
        if n in late_rows:
            rows = jnp.concatenate([late_rows[n], rows[late_rows[n].shape[0]:]], axis=0)
        return rows

    loss = (0.5 / D) * jnp.sum(summed('loss'))

    dm_lat = rows_of('dmod_lat', {part: packs[part][0] for part in packs}, 1)
    dm_lat = jnp.concatenate([gathered_late[:, 0:2], dm_lat[:, 2:]], axis=1)
    dm_lat = dm_lat.reshape(8, L, 6 * D).transpose(1, 0, 2)
    dm_ctx = summed('dmod_ctx').reshape(L, 1, 6 * D)
    dm16 = jnp.concatenate([dm_lat, dm_ctx, jnp.zeros((L, 7, 6 * D), F32)], axis=1)
    dm16 = lax.dynamic_slice_in_dim(dm16, s_idx * Cs, Cs, axis=2)
    dmc8 = jnp.concatenate([dm16[:, 8:9], jnp.zeros((L, 7, Cs), F32)], axis=1)
    g_ada, d_ada, m_ada, v_ada, ds_part = ada_bwd_adam("ada_bwd_adam", craw16.T, dm16, dmc8, ada_w,
                                                       Mo['ada_w'], Vo['ada_w'])
    ds_all = all_gather_8("ag_ds_ctx", ds_part)
    cc = cctx_adam("cctx_adam", ds_all, W['c_ctx'].reshape(1, D), Mo['c_ctx'].reshape(1, D),
                   Vo['c_ctx'].reshape(1, D))
    out.update({'c_ctx': tuple(t.reshape(D) for t in cc), 'ada_w': (g_ada, d_ada, m_ada, v_ada)})

    for n in BIG_NAMES:
        out[n], _ = adam_parts("adam_" + n, land[n], land2[n], W[n], Mo[n], Vo[n])

    def own_cols(full):
        w = full.shape[-1] // N_SHARD
        return lax.dynamic_slice_in_dim(full, s_idx * w, w, axis=full.ndim - 1)

    small_g = {
        'ada_b': summed('ada_b').reshape(L, 6 * D), 'g_mix': summed('g_mix'), 'g_ffn': summed('g_ffn'),
        'a_ln_g': own_cols(summed('a_ln_g')), 'a_ln_b': own_cols(summed('a_ln_b')),
        'a_w_s': summed('a_w_s').reshape(n_a, G, CHUNK, CHUNK), 'a_b_s': summed('a_b_s').reshape(n_a, G, CHUNK),
        'b_rpb': summed('b_rpb').reshape(-1)[:H * n_dr * (2 * NA_COLS - 1)].reshape(W['b_rpb'].shape),
        'c_w_dw': own_cols(summed('c_w_dw').reshape(n_c, 32, D)[:, :CONV_W]),
        'c_b_dw': own_cols(summed('c_b_dw')), 'c_ln_g': own_cols(summed('c_ln_g')),
        'c_ln_b': own_cols(summed('c_ln_b')), 'g_final': summed('g_final').reshape(D),
    }

    def packed(d):
        flat = jnp.concatenate([d[n].reshape(-1) for n in SMALL_NAMES])
        return _pad_rows(_rows_of(flat, SMALL_PACK_COLS), 8)

    res = adam_flat("adam_small", packed(small_g), packed(W), packed(Mo), packed(Vo))
    o = 0
    for n in SMALL_NAMES:
        size, shape = W[n].size, W[n].shape
        out[n] = (small_g[n],) + tuple(r.reshape(-1)[o:o + size].reshape(shape) for r in res)
        o += size

    return (loss, grad_x) + tuple(out[n][k] for k in range(4) for n in WEIGHT_NAMES)


def kernel(x, c, ctx, c_ctx, ada_w, ada_b, g_mix, g_ffn, ffn_w1, ffn_w3, ffn_w2, a_w_in, a_ln_g, a_ln_b, a_w_s, a_b_s, a_w_out, b_w_qkv, b_rpb, b_w_out, c_w_pw1, c_w_dw, c_b_dw, c_ln_g, c_ln_b, c_w_pw2, g_final, loss_target, m_c_ctx, m_ada_w, m_ada_b, m_g_mix, m_g_ffn, m_ffn_w1, m_ffn_w3, m_ffn_w2, m_a_w_in, m_a_ln_g, m_a_ln_b, m_a_w_s, m_a_b_s, m_a_w_out, m_b_w_qkv, m_b_rpb, m_b_w_out, m_c_w_pw1, m_c_w_dw, m_c_b_dw, m_c_ln_g, m_c_ln_b, m_c_w_pw2, m_g_final, v_c_ctx, v_ada_w, v_ada_b, v_g_mix, v_g_ffn, v_ffn_w1, v_ffn_w3, v_ffn_w2, v_a_w_in, v_a_ln_g, v_a_ln_b, v_a_w_s, v_a_b_s, v_a_w_out, v_b_w_qkv, v_b_rpb, v_b_w_out, v_c_w_pw1, v_c_w_dw, v_c_b_dw, v_c_ln_g, v_c_ln_b, v_c_w_pw2, v_g_final):
    W = dict(zip(WEIGHT_NAMES, (c_ctx, ada_w, ada_b, g_mix, g_ffn, ffn_w1, ffn_w3, ffn_w2, a_w_in, a_ln_g, a_ln_b, a_w_s, a_b_s, a_w_out, b_w_qkv, b_rpb, b_w_out, c_w_pw1, c_w_dw, c_b_dw, c_ln_g, c_ln_b, c_w_pw2, g_final)))
    Mo = dict(zip(WEIGHT_NAMES, (m_c_ctx, m_ada_w, m_ada_b, m_g_mix, m_g_ffn, m_ffn_w1, m_ffn_w3, m_ffn_w2, m_a_w_in, m_a_ln_g, m_a_ln_b, m_a_w_s, m_a_b_s, m_a_w_out, m_b_w_qkv, m_b_rpb, m_b_w_out, m_c_w_pw1, m_c_w_dw, m_c_b_dw, m_c_ln_g, m_c_ln_b, m_c_w_pw2, m_g_final)))
    Vo = dict(zip(WEIGHT_NAMES, (v_c_ctx, v_ada_w, v_ada_b, v_g_mix, v_g_ffn, v_ffn_w1, v_ffn_w3, v_ffn_w2, v_a_w_in, v_a_ln_g, v_a_ln_b, v_a_w_s, v_a_b_s, v_a_w_out, v_b_w_qkv, v_b_rpb, v_b_w_out, v_c_w_pw1, v_c_w_dw, v_c_b_dw, v_c_ln_g, v_c_ln_b, v_c_w_pw2, v_g_final)))
    return _step(W, Mo, Vo, x, c, ctx, loss_target)
```

```python
import functools
import math

import jax
import jax.numpy as jnp
from jax import lax
from jax.experimental import pallas as pl
from jax.experimental.pallas import tpu as pltpu

F32 = jnp.float32
BF16 = jnp.bfloat16
MESH = pl.DeviceIdType.MESH
ANY = pl.BlockSpec(memory_space=pl.ANY)

GRID_W = 64
CHUNK = 128
HEAD = 128
NA_ROWS = 8
NA_COLS = 16
CONV_W = 31
HALO = 16
EPS = 1e-6
NEG_INF = -1e30
N_LAYERS = 4
N_SHARD = 4
V7X_VMEM_BYTES = 64 * 1024 * 1024
VMEM_LIMIT = V7X_VMEM_BYTES - 6 * 1024 * 1024

ADAM_LR = 0.001
ADAM_B1 = 0.9
ADAM_B2 = 0.999
ADAM_EPS = 1e-08
ADAM_WD = 0.01
ADAM_STEP = 10

NN = (((1,), (0,)), ((), ()))
NT = (((1,), (1,)), ((), ()))
TN = (((0,), (0,)), ((), ()))


def _params(n_grid=0):
    sem = ("arbitrary",) * n_grid if n_grid else None
    return pltpu.CompilerParams(dimension_semantics=sem, vmem_limit_bytes=VMEM_LIMIT)


def _xyc():
    return lax.axis_index("x"), lax.axis_index("y"), lax.axis_index("c")


def _flip(v, f):
    return 1 - v if f else v


def _tile(n, pref, mult=128):
    if n <= pref:
        return n
    t = (pref // mult) * mult
    while t > mult and n % t:
        t -= mult
    assert n % t == 0, (n, pref, mult)
    return t


def _sigmoid(x):
    return 1.0 / (1.0 + jnp.exp(-x))


XY_FLIPS = ((1, 0), (0, 1), (1, 1))
ALL_FLIPS = tuple((fx, fy, fc) for fx in (0, 1) for fy in (0, 1) for fc in (0, 1) if fx or fy or fc)


def _remote(src, dst, ssem, rsem, dev):
    return pltpu.make_async_remote_copy(src_ref=src, dst_ref=dst, send_sem=ssem, recv_sem=rsem,
                                        device_id=dev, device_id_type=MESH)


def all_gather_xy(name, shard):
    def body(src, dst, ssem, rsem, lsem):
        x, y, c = _xyc()
        mine = pltpu.make_async_copy(src, dst.at[2 * x + y], lsem)
        mine.start()
        sends = []
        for k, (fx, fy) in enumerate(XY_FLIPS):
            cp = _remote(src, dst.at[2 * x + y], ssem.at[k], rsem.at[k], (_flip(x, fx), _flip(y, fy), c))
            cp.start()
            sends.append(cp)
        for k, (fx, fy) in enumerate(XY_FLIPS):
            px, py = _flip(x, fx), _flip(y, fy)
            _remote(src, dst.at[2 * px + py], ssem.at[k], rsem.at[k], (px, py, c)).wait_recv()
        for cp in sends:
            cp.wait_send()
        mine.wait()

    return pl.pallas_call(
        body, name=name, out_shape=jax.ShapeDtypeStruct((N_SHARD,) + shard.shape, shard.dtype),
        in_specs=[ANY], out_specs=ANY,
        scratch_shapes=[pltpu.SemaphoreType.DMA((3,)), pltpu.SemaphoreType.DMA((3,)), pltpu.SemaphoreType.DMA(())],
    )(shard)


def all_gather_8(name, blk):
    def body(src, dst, ssem, rsem, lsem):
        x, y, c = _xyc()
        me = 4 * x + 2 * y + c
        mine = pltpu.make_async_copy(src, dst.at[me], lsem)
        mine.start()
        sends = []
        for k, (fx, fy, fc) in enumerate(ALL_FLIPS):
            cp = _remote(src, dst.at[me], ssem.at[k], rsem.at[k], (_flip(x, fx), _flip(y, fy), _flip(c, fc)))
            cp.start()
            sends.append(cp)
        for k, (fx, fy, fc) in enumerate(ALL_FLIPS):
            px, py, pc = _flip(x, fx), _flip(y, fy), _flip(c, fc)
            _remote(src, dst.at[4 * px + 2 * py + pc], ssem.at[k], rsem.at[k], (px, py, pc)).wait_recv()
        for cp in sends:
            cp.wait_send()
        mine.wait()

    return pl.pallas_call(
        body, name=name, out_shape=jax.ShapeDtypeStruct((8,) + blk.shape, blk.dtype),
        in_specs=[ANY], out_specs=ANY,
        scratch_shapes=[pltpu.SemaphoreType.DMA((7,)), pltpu.SemaphoreType.DMA((7,)), pltpu.SemaphoreType.DMA(())],
    )(blk)


class GatherTask:
    n_send, n_recv, n_local = 3, 3, 1
    alias = {}

    def __init__(self, shard):
        self.ins = [shard]
        self.outs = [jax.ShapeDtypeStruct((N_SHARD,) + shard.shape, shard.dtype)]

    def _copies(self, xyc, ins, outs, ssem, rsem):
        x, y, c = xyc
        for k, (fx, fy) in enumerate(XY_FLIPS):
            px, py = _flip(x, fx), _flip(y, fy)
            send = _remote(ins[0], outs[0].at[2 * x + y], ssem.at[k], rsem.at[k], (px, py, c))
            recv = _remote(ins[0], outs[0].at[2 * px + py], ssem.at[k], rsem.at[k], (px, py, c))
            yield send, recv

    def start(self, xyc, ins, outs, ssem, rsem, lsem):
        x, y, _ = xyc
        pltpu.make_async_copy(ins[0], outs[0].at[2 * x + y], lsem.at[0]).start()
        for send, _ in self._copies(xyc, ins, outs, ssem, rsem):
            send.start()

    def finish(self, xyc, ins, outs, ssem, rsem, lsem):
        x, y, _ = xyc
        for send, recv in self._copies(xyc, ins, outs, ssem, rsem):
            recv.wait_recv()
            send.wait_send()
        pltpu.make_async_copy(ins[0], outs[0].at[2 * x + y], lsem.at[0]).wait()


class GatherAllTask:
    n_send, n_recv, n_local = 7, 7, 1
    alias = {}

    def __init__(self, blk):
        self.ins = [blk]
        self.outs = [jax.ShapeDtypeStruct((8,) + blk.shape, blk.dtype)]

    def _copies(self, xyc, ins, outs, ssem, rsem):
        x, y, c = xyc
        for k, (fx, fy, fc) in enumerate(ALL_FLIPS):
            px, py, pc = _flip(x, fx), _flip(y, fy), _flip(c, fc)
            send = _remote(ins[0], outs[0].at[4 * x + 2 * y + c], ssem.at[k], rsem.at[k], (px, py, pc))
            recv = _remote(ins[0], outs[0].at[4 * px + 2 * py + pc], ssem.at[k], rsem.at[k], (px, py, pc))
            yield send, recv

    def _local(self, xyc, ins, outs, lsem):
        x, y, c = xyc
        return pltpu.make_async_copy(ins[0], outs[0].at[4 * x + 2 * y + c], lsem.at[0])

    def start(self, xyc, ins, outs, ssem, rsem, lsem):
        self._local(xyc, ins, outs, lsem).start()
        for send, _ in self._copies(xyc, ins, outs, ssem, rsem):
            send.start()

    def finish(self, xyc, ins, outs, ssem, rsem, lsem):
        for send, recv in self._copies(xyc, ins, outs, ssem, rsem):
            recv.wait_recv()
            send.wait_send()
        self._local(xyc, ins, outs, lsem).wait()


class ScatterTask:
    n_send, n_recv, n_local = 4, 4, 1
    alias = {1: 0, 2: 1}

    def __init__(self, part, land, land2, l):
        self.ins = [part, land, land2]
        self.outs = [jax.ShapeDtypeStruct(land.shape, land.dtype), jax.ShapeDtypeStruct(land2.shape, land2.dtype)]
        self.l = l

    def _copies(self, xyc, ins, outs, ssem, rsem):
        x, y, c = xyc
        me_s = 2 * x + y
        part, land, land2 = ins[0], outs[0], outs[1]
        sib = (x, y, 1 - c)
        own = _remote(part.at[me_s], land2.at[me_s, self.l], ssem.at[3], rsem.at[3], sib)
        yield own, own
        for k, (fx, fy) in enumerate(XY_FLIPS):
            px, py = _flip(x, fx), _flip(y, fy)
            ps = 2 * px + py
            send = _remote(part.at[ps], land.at[me_s, self.l], ssem.at[k], rsem.at[k], (px, py, c))
            recv = _remote(part.at[ps], land.at[ps, self.l], ssem.at[k], rsem.at[k], (px, py, c))
            yield send, recv

    def _local(self, xyc, ins, outs, lsem):
        me_s = 2 * xyc[0] + xyc[1]
        return pltpu.make_async_copy(ins[0].at[me_s], outs[0].at[me_s, self.l], lsem.at[0])

    def start(self, xyc, ins, outs, ssem, rsem, lsem):
        self._local(xyc, ins, outs, lsem).start()
        for send, _ in self._copies(xyc, ins, outs, ssem, rsem):
            send.start()

    def finish(self, xyc, ins, outs, ssem, rsem, lsem):
        for send, recv in self._copies(xyc, ins, outs, ssem, rsem):
            recv.wait_recv()
            send.wait_send()
        self._local(xyc, ins, outs, lsem).wait()


class ForwardTask:
    n_send, n_recv, n_local = 3, 3, 0
    alias = {0: 0, 1: 1}

    def __init__(self, land, land2, l):
        self.ins = [land, land2]
        self.outs = [jax.ShapeDtypeStruct(land.shape, land.dtype), jax.ShapeDtypeStruct(land2.shape, land2.dtype)]
        self.l = l

    def _copies(self, xyc, outs, ssem, rsem):
        x, y, c = xyc
        for k, (fx, fy) in enumerate(XY_FLIPS):
            ps = 2 * _flip(x, fx) + _flip(y, fy)
            yield _remote(outs[0].at[ps, self.l], outs[1].at[ps, self.l], ssem.at[k], rsem.at[k], (x, y, 1 - c))

    def start(self, xyc, ins, outs, ssem, rsem, lsem):
        for cp in self._copies(xyc, outs, ssem, rsem):
            cp.start()

    def finish(self, xyc, ins, outs, ssem, rsem, lsem):
        for cp in self._copies(xyc, outs, ssem, rsem):
            cp.wait_recv()
            cp.wait_send()


def host_call(name, body, grid, arrays, in_specs, out_shape, out_specs, scratch=(), tasks=()):
    out_shape, out_specs, scratch = tuple(out_shape), tuple(out_specs), list(scratch)
    n_in, n_out, n_scr, n_grid = len(arrays), len(out_shape), len(scratch), len(grid)
    t_arrays, t_outs, aliases, spans, sems = [], [], {}, [], []
    for t in tasks:
        i0, o0 = len(t_arrays), len(t_outs)
        t_arrays += t.ins
        t_outs += t.outs
        for a, b in t.alias.items():
            aliases[n_in + i0 + a] = n_out + o0 + b
        spans.append((i0, len(t_arrays), o0, len(t_outs)))
        sems += [pltpu.SemaphoreType.DMA((t.n_send,)), pltpu.SemaphoreType.DMA((t.n_recv,)),
                 pltpu.SemaphoreType.DMA((max(t.n_local, 1),))]
    n_tin, n_tout = len(t_arrays), len(t_outs)

    def full_body(*refs):
        ins = refs[:n_in]
        tin = refs[n_in:n_in + n_tin]
        outs = refs[n_in + n_tin:n_in + n_tin + n_out]
        tout = refs[n_in + n_tin + n_out:n_in + n_tin + n_out + n_tout]
        rest = refs[n_in + n_tin + n_out + n_tout:]
        scr, sm = rest[:n_scr], rest[n_scr:]
        if not tasks:
            body(*ins, *outs, *scr)
            return
        pids = [pl.program_id(d) for d in range(n_grid)]
        first = functools.reduce(jnp.logical_and, [p == 0 for p in pids])
        last = functools.reduce(jnp.logical_and, [p == n - 1 for p, n in zip(pids, grid)])
        xyc = _xyc()

        def each(method):
            for k, (t, (i0, i1, o0, o1)) in enumerate(zip(tasks, spans)):
                getattr(t, method)(xyc, tin[i0:i1], tout[o0:o1], sm[3 * k], sm[3 * k + 1], sm[3 * k + 2])

        @pl.when(first)
        def _():
            each("start")

        body(*ins, *outs, *scr)

        @pl.when(last)
        def _():
            each("finish")

    res = pl.pallas_call(
        full_body, name=name, grid=grid, in_specs=list(in_specs) + [ANY] * n_tin,
        out_specs=out_specs + (ANY,) * n_tout, out_shape=out_shape + tuple(t_outs),
        scratch_shapes=scratch + sems, input_output_aliases=aliases, compiler_params=_params(n_grid),
    )(*arrays, *t_arrays)
    return tuple(res[:n_out]), [tuple(res[n_out + o0:n_out + o1]) for (_, _, o0, o1) in spans]


def comm_only(name, tasks):
    def body(i_ref, o_ref):
        o_ref[...] = i_ref[...]

    spec = pl.BlockSpec((8, 128), lambda i: (0, 0))
    _, touts = host_call(name, body, (1,), [jnp.zeros((8, 128), F32)], [spec],
                         [jax.ShapeDtypeStruct((8, 128), F32)], [spec], tasks=tasks)
    return touts


def matmul(name, grid, order, pairs, pair_dims, acc_of_pair, acc_shapes, extras, outs, epilogue, tasks=(),
           col_chunk=1 << 30):
    ni, nj, nk = grid

    def wrap(m):
        if order == "ij":
            return lambda g0, g1, k: m(g0, g1, k)
        return lambda g0, g1, k: m(g1, g0, k)

    g = (ni, nj, nk) if order == "ij" else (nj, ni, nk)
    arrays, in_specs = [], []
    for a, b in pairs:
        for arr, blk, m in (a, b):
            arrays.append(arr)
            in_specs.append(pl.BlockSpec(blk, wrap(m)))
    for arr, blk, m in extras:
        arrays.append(arr)
        in_specs.append(pl.BlockSpec(blk, wrap(m)))
    n_in = len(arrays)
    out_shape = tuple(o[0] for o in outs)
    out_specs = tuple(pl.BlockSpec(o[1], wrap(o[2])) for o in outs)
    n_pairs, n_ex, n_out, n_acc = len(pairs), len(extras), len(outs), len(acc_shapes)
    tile_n = acc_shapes[0][1]
    chunks = [slice(c0, min(c0 + col_chunk, tile_n)) for c0 in range(0, tile_n, col_chunk)]

    def body(*refs):
        ins = refs[:n_in]
        out_refs = refs[n_in:n_in + n_out]
        accs = refs[n_in + n_out:]
        pid = (pl.program_id(0), pl.program_id(1)) if order == "ij" else (pl.program_id(1), pl.program_id(0))
        k = pl.program_id(2)

        def partial(p, cs):
            b_ref = ins[2 * p + 1]
            b = b_ref[cs, :] if pair_dims[p] == NT else b_ref[:, cs]
            return lax.dot_general(ins[2 * p][...], b, pair_dims[p], preferred_element_type=F32)

        ex = ins[2 * n_pairs:2 * n_pairs + n_ex]
        if nk == 1:
            for cs in chunks:
                vals = [None] * n_acc
                for p in range(n_pairs):
                    d = partial(p, cs)
                    q = acc_of_pair[p]
                    vals[q] = d if vals[q] is None else vals[q] + d
                epilogue(vals, ex, out_refs, pid, cs)
        else:
            @pl.when(k == 0)
            def _():
                for acc in accs:
                    acc[...] = jnp.zeros_like(acc)

            for cs in chunks:
                for p in range(n_pairs):
                    accs[acc_of_pair[p]][:, cs] += partial(p, cs)

            @pl.when(k == nk - 1)
            def _():
                for cs in chunks:
                    epilogue([acc[:, cs] for acc in accs], ex, out_refs, pid, cs)

    scratch = [] if nk == 1 else [pltpu.VMEM(s, F32) for s in acc_shapes]
    return host_call(name, body, g, arrays, in_specs, out_shape, out_specs, scratch, tasks)


def _store_cast(vals, extras, out_refs, pid, cs):
    out_refs[0][:, cs] = vals[0].astype(out_refs[0].dtype)


def mm_cols(name, x, wg, out_dtype, tm, tasks=()):
    T, D = x.shape
    ns = wg.shape[2]
    tn = _tile(ns, 1536)
    nb = ns // tn
    outs, touts = matmul(
        name, (T // tm, N_SHARD * nb, 1), "ji",
        [((x, (tm, D), lambda i, jj, k: (i, 0)),
          (wg, (None, D, tn), lambda i, jj, k: (jj // nb, 0, jj % nb)))],
        [NN], [0], [(tm, tn)], [],
        [(jax.ShapeDtypeStruct((T, N_SHARD * ns), out_dtype), (tm, tn), lambda i, jj, k: (i, jj))],
        _store_cast, tasks)
    return outs[0], touts


def mm_ffn_up(name, x, w1g, w3g, tm, tasks=()):
    T, D = x.shape
    ns = w1g.shape[2]
    tn = _tile(ns, 1536)
    nb = ns // tn

    def epi(vals, extras, out_refs, pid, cs):
        a, b = vals
        out_refs[0][:, cs] = a.astype(BF16)
        out_refs[1][:, cs] = b.astype(BF16)
        out_refs[2][:, cs] = (a * _sigmoid(a) * b).astype(BF16)

    wmap = lambda i, jj, k: (jj // nb, 0, jj % nb)
    xa = (x, (tm, D), lambda i, jj, k: (i, 0))
    o = (jax.ShapeDtypeStruct((T, N_SHARD * ns), BF16), (tm, tn), lambda i, jj, k: (i, jj))
    return matmul(name, (T // tm, N_SHARD * nb, 1), "ji",
                  [(xa, (w1g, (None, D, tn), wmap)), (xa, (w3g, (None, D, tn), wmap))],
                  [NN, NN], [0, 1], [(tm, tn), (tm, tn)], [], [o, o, o], epi, tasks)


def mm_rows_residual(name, p, wg, h, mod, gate_row, seq, tm, tasks=()):
    T, kin = p.shape
    ks, D = wg.shape[1], wg.shape[2]
    tn = _tile(D, 512)

    def epi(vals, extras, out_refs, pid, cs):
        m = vals[0]
        h_ref, mod_ref = extras
        rows = pid[0] * tm + lax.broadcasted_iota(jnp.int32, (tm, 1), 0)
        gate = jnp.where(rows >= seq, mod_ref[6 + gate_row:7 + gate_row, cs], mod_ref[gate_row:gate_row + 1, cs])
        out_refs[0][:, cs] = m.astype(BF16)
        out_refs[1][:, cs] = h_ref[:, cs] + gate * m

    pairs = [((p, (tm, ks), lambda i, jj, k, s=s: (i, s)), (wg, (None, ks, tn), lambda i, jj, k, s=s: (s, 0, jj)))
             for s in range(N_SHARD)]
    omap = lambda i, jj, k: (i, jj)
    return matmul(
        name, (T // tm, D // tn, 1), "ji", pairs, [NN] * N_SHARD, [0] * N_SHARD, [(tm, tn)],
        [(h, (tm, tn), omap), (mod, (16, tn), lambda i, jj, k: (0, jj))],
        [(jax.ShapeDtypeStruct((T, D), BF16), (tm, tn), omap), (jax.ShapeDtypeStruct((T, D), F32), (tm, tn), omap)],
        epi, tasks)


def mm_rows_dgrad(name, dm, wg, out_dtype, tm, ffn_ab=None, tasks=()):
    T, D = dm.shape
    ks = wg.shape[1]
    tn = _tile(ks, 1536)
    nb = ks // tn
    omap = lambda i, jj, k: (i, jj)
    o = (jax.ShapeDtypeStruct((T, N_SHARD * ks), out_dtype), (tm, tn), omap)
    pairs = [((dm, (tm, D), lambda i, jj, k: (i, 0)),
              (wg, (None, tn, D), lambda i, jj, k: (jj // nb, jj % nb, 0)))]
    if ffn_ab is None:
        return matmul(name, (T // tm, N_SHARD * nb, 1), "ji", pairs, [NT], [0], [(tm, tn)], [], [o],
                      _store_cast, tasks)

    def epi(vals, extras, out_refs, pid, cs):
        dact = vals[0]
        a = extras[0][:, cs].astype(F32)
        b = extras[1][:, cs].astype(F32)
        sig = _sigmoid(a)
        out_refs[0][:, cs] = (dact * b * (sig * (1.0 + a * (1.0 - sig)))).astype(BF16)
        out_refs[1][:, cs] = (dact * (a * sig)).astype(BF16)

    a, b = ffn_ab
    return matmul(name, (T // tm, N_SHARD * nb, 1), "ji", pairs, [NT], [0], [(tm, tn)],
                  [(a, (tm, tn), omap), (b, (tm, tn), omap)], [o, o], epi, tasks)


def mm_cols_dgrad(name, dys, wgs, tm, tasks=()):
    T = dys[0].shape[0]
    D, ns = wgs[0].shape[1], wgs[0].shape[2]
    tk = _tile(ns, 1536)
    kb = ns // tk
    pairs = [((dy, (tm, tk), lambda i, jj, k: (i, k)),
              (wg, (None, D, tk), lambda i, jj, k: (k // kb, 0, k % kb))) for dy, wg in zip(dys, wgs)]
    outs, touts = matmul(name, (T // tm, 1, N_SHARD * kb), "ij", pairs, [NT] * len(dys), [0] * len(dys),
                         [(tm, D)], [],
                         [(jax.ShapeDtypeStruct((T, D), F32), (tm, D), lambda i, jj, k: (i, 0))], _store_cast, tasks)
    return outs[0], touts


def mm_wgrad_rows(name, p, dm, tk, tasks=()):
    T, kin = p.shape
    D = dm.shape[1]
    ks = kin // N_SHARD
    tm = _tile(ks, 1536)
    mb = ks // tm
    tn = _tile(D, 1024)
    outs, touts = matmul(
        name, (N_SHARD * mb, D // tn, T // tk), "ij",
        [((p, (tk, tm), lambda i, jj, k: (k, i)), (dm, (tk, tn), lambda i, jj, k: (k, jj)))],
        [TN], [0], [(tm, tn)], [],
        [(jax.ShapeDtypeStruct((N_SHARD, ks, D), BF16), (None, tm, tn), lambda i, jj, k: (i // mb, i % mb, jj))],
        _store_cast, tasks)
    return outs[0], touts


def mm_wgrad_cols(name, x, dy, tk, tasks=()):
    T, D = x.shape
    ns = dy.shape[1] // N_SHARD
    tm = _tile(D, 1024)
    tn = _tile(ns, 1536)
    nb = ns // tn
    outs, touts = matmul(
        name, (D // tm, N_SHARD * nb, T // tk), "ij",
        [((x, (tk, tm), lambda i, jj, k: (k, i)), (dy, (tk, tn), lambda i, jj, k: (k, jj)))],
        [TN], [0], [(tm, tn)], [],
        [(jax.ShapeDtypeStruct((N_SHARD, D, ns), BF16), (None, tm, tn), lambda i, jj, k: (jj // nb, i, jj % nb))],
        _store_cast, tasks)
    return outs[0], touts


def _row_tile(T, seq):
    if T == seq:
        return _tile(T, 256, 8)
    return math.gcd(256, math.gcd(seq, T - seq))


def _mod_row(mod_ref, row, is_ctx):
    return jnp.where(is_ctx, mod_ref[6 + row:7 + row, :], mod_ref[row:row + 1, :])


def _rowspec(tr, D):
    return pl.BlockSpec((tr, D), lambda i: (i, 0))


def _fullspec(shape):
    nd = len(shape)
    return pl.BlockSpec(shape, lambda i: (0,) * nd)


def _colsum(v):
    return jnp.sum(v, axis=0, keepdims=True)


def _acc_rows(st_ref, first, rows):
    @pl.when(first)
    def _():
        st_ref[...] = jnp.zeros_like(st_ref)
    for r, val in rows:
        st_ref[r:r + 1, :] += val


def _split_stats(is_ctx, val):
    zero = jnp.zeros_like(val)
    return jnp.where(is_ctx, zero, val), jnp.where(is_ctx, val, zero)


def nm_fwd(name, h, g, mod, r_sh, r_sc, seq):
    T, D = h.shape
    tr = _row_tile(T, seq)
    nlat = seq // tr

    def body(h_ref, g_ref, mod_ref, o_ref):
        is_ctx = pl.program_id(0) >= nlat
        x = h_ref[...]
        r = lax.rsqrt(jnp.mean(x * x, axis=-1, keepdims=True) + EPS)
        y = x * r * g_ref[...]
        o_ref[...] = (y * (1.0 + _mod_row(mod_ref, r_sc, is_ctx)) + _mod_row(mod_ref, r_sh, is_ctx)).astype(BF16)

    return pl.pallas_call(
        body, name=name, grid=(T // tr,), in_specs=[_rowspec(tr, D), _fullspec((1, D)), _fullspec((16, D))],
        out_specs=_rowspec(tr, D), out_shape=jax.ShapeDtypeStruct((T, D), BF16), compiler_params=_params(1),
    )(h, g, mod)


def nm_bwd(name, h, dhm, dres, g, mod, r_sc, seq, gate):
    T, D = h.shape
    tr = _row_tile(T, seq)
    nlat = seq // tr
    m_arr, mod_gate, r_gt = gate

    def body(h_ref, d_ref, r_ref, g_ref, mod_ref, m_ref, modg_ref, o_ref, st_ref, dm_ref, gst_ref):
        i = pl.program_id(0)
        is_ctx = i >= nlat
        x = h_ref[...]
        r = lax.rsqrt(jnp.mean(x * x, axis=-1, keepdims=True) + EPS)
        n = x * r
        gg = g_ref[...]
        dout = d_ref[...]
        dsh = _colsum(dout)
        dsc = _colsum(dout * (n * gg))
        dy = dout * (1.0 + _mod_row(mod_ref, r_sc, is_ctx))
        dn = dy * gg
        dh_out = r_ref[...] + r * (dn - n * jnp.mean(dn * n, axis=-1, keepdims=True))
        o_ref[...] = dh_out
        dsh_l, dsh_c = _split_stats(is_ctx, dsh)
        dsc_l, dsc_c = _split_stats(is_ctx, dsc)
        _acc_rows(st_ref, i == 0, [(0, _colsum(dy * n)), (1, dsh_l), (2, dsc_l), (3, dsh_c), (4, dsc_c),
                                   (5, dsh), (6, dsc)])
        dm_ref[...] = (dh_out * _mod_row(modg_ref, r_gt, is_ctx)).astype(BF16)
        dgt = _colsum(dh_out * m_ref[...].astype(F32))
        dgt_l, dgt_c = _split_stats(is_ctx, dgt)
        _acc_rows(gst_ref, i == 0, [(0, dgt_l), (1, dgt_c), (2, dgt)])

    return pl.pallas_call(
        body, name=name, grid=(T // tr,),
        in_specs=[_rowspec(tr, D), _rowspec(tr, D), _rowspec(tr, D), _fullspec((1, D)), _fullspec((16, D)),
                  _rowspec(tr, D), _fullspec((16, D))],
        out_specs=(_rowspec(tr, D), _fullspec((8, D)), _rowspec(tr, D), _fullspec((8, D))),
        out_shape=(jax.ShapeDtypeStruct((T, D), F32), jax.ShapeDtypeStruct((8, D), F32),
                   jax.ShapeDtypeStruct((T, D), BF16), jax.ShapeDtypeStruct((8, D), F32)),
        compiler_params=_params(1),
    )(h, dhm, dres, g, mod, m_arr, mod_gate)


def gate_bwd(name, dh, m, mod, r_gt, seq):
    T, D = dh.shape
    tr = _row_tile(T, seq)
    nlat = seq // tr

    def body(d_ref, m_ref, mod_ref, o_ref, st_ref):
        i = pl.program_id(0)
        is_ctx = i >= nlat
        d = d_ref[...]
        o_ref[...] = (d * _mod_row(mod_ref, r_gt, is_ctx)).astype(BF16)
        dgt = _colsum(d * m_ref[...].astype(F32))
        dgt_l, dgt_c = _split_stats(is_ctx, dgt)
        _acc_rows(st_ref, i == 0, [(0, dgt_l), (1, dgt_c), (2, dgt)])

    return pl.pallas_call(
        body, name=name, grid=(T // tr,),
        in_specs=[_rowspec(tr, D), _rowspec(tr, D), _fullspec((16, D))],
        out_specs=(_rowspec(tr, D), _fullspec((8, D))),
        out_shape=(jax.ShapeDtypeStruct((T, D), BF16), jax.ShapeDtypeStruct((8, D), F32)),
        compiler_params=_params(1),
    )(dh, m, mod)


def loss_head(name, h, target, g):
    T, D = h.shape
    tr = _tile(T, 256, 8)

    def body(h_ref, t_ref, g_ref, o_ref, st_ref):
        i = pl.program_id(0)
        x = h_ref[...]
        r = lax.rsqrt(jnp.mean(x * x, axis=-1, keepdims=True) + EPS)
        n = x * r
        gg = g_ref[...]
        err = n * gg - t_ref[...]
        dy = err * (1.0 / D)
        dn = dy * gg
        o_ref[...] = r * (dn - n * jnp.mean(dn * n, axis=-1, keepdims=True))
        _acc_rows(st_ref, i == 0, [(0, _colsum(dy * n)), (1, _colsum(err * err))])

    return pl.pallas_call(
        body, name=name, grid=(T // tr,),
        in_specs=[_rowspec(tr, D), _rowspec(tr, D), _fullspec((1, D))],
        out_specs=(_rowspec(tr, D), _fullspec((8, D))),
        out_shape=(jax.ShapeDtypeStruct((T, D), F32), jax.ShapeDtypeStruct((8, D), F32)),
        compiler_params=_params(1),
    )(h, target, g)


GELU_C = math.sqrt(2.0 / math.pi)


def _gelu(x):
    t = jnp.tanh(GELU_C * (x + 0.044715 * (x * x * x)))
    return 0.5 * x * (1.0 + t), t


def _gelu_grad(x, t):
    return 0.5 * (1.0 + t) + 0.5 * x * (1.0 - t * t) * (GELU_C * (1.0 + 3.0 * 0.044715 * (x * x)))


def _layernorm_fwd(v, g, b):
    mu = jnp.mean(v, axis=-1, keepdims=True)
    xc = v - mu
    rs = lax.rsqrt(jnp.mean(xc * xc, axis=-1, keepdims=True) + EPS)
    xhat = xc * rs
    return xhat * g + b, xhat, rs


def _layernorm_bwd(dout, xhat, rs, g):
    dxh = dout * g
    return rs * (dxh - jnp.mean(dxh, axis=-1, keepdims=True) - xhat * jnp.mean(dxh * xhat, axis=-1, keepdims=True))


def gmlp_fwd(name, zp, ln_g, ln_b, ws, bfull):
    T, E2 = zp.shape
    E = E2 // 2
    G = E // CHUNK
    tr = 2 * CHUNK if T % (2 * CHUNK) == 0 else CHUNK

    def body(zp_ref, g_ref, b_ref, ws_ref, bf_ref, p_ref):
        for ch in range(tr // CHUNK):
            rs_ = slice(ch * CHUNK, (ch + 1) * CHUNK)
            u, _ = _gelu(zp_ref[rs_, 0:E].astype(F32))
            v, _ = _gelu(zp_ref[rs_, E:E2].astype(F32))
            vn, _, _ = _layernorm_fwd(v, g_ref[...], b_ref[...])
            vnb = vn.astype(BF16)
            for gi in range(G):
                cs = slice(gi * CHUNK, (gi + 1) * CHUNK)
                vs = jnp.dot(ws_ref[gi], vnb[:, cs], preferred_element_type=F32) + bf_ref[:, cs]
                p_ref[rs_, cs] = (u[:, cs] * vs).astype(BF16)

    return pl.pallas_call(
        body, name=name, grid=(T // tr,),
        in_specs=[_rowspec(tr, E2), _fullspec((1, E)), _fullspec((1, E)), _fullspec((G, CHUNK, CHUNK)),
                  _fullspec((CHUNK, E))],
        out_specs=_rowspec(tr, E), out_shape=jax.ShapeDtypeStruct((T, E), BF16), compiler_params=_params(1),
    )(zp, ln_g, ln_b, ws, bfull)


def gmlp_bwd(name, zp, dp, ln_g, ln_b, ws, bfull):
    T, E2 = zp.shape
    E = E2 // 2
    G = E // CHUNK
    tr = 2 * CHUNK if T % (2 * CHUNK) == 0 else CHUNK
    n = T // tr

    def body(zp_ref, dp_ref, g_ref, b_ref, ws_ref, bf_ref, dz_ref, dws_ref, dbs_ref, st_ref, dvn_ref, dbf_ref):
        i = pl.program_id(0)

        @pl.when(i == 0)
        def _():
            dws_ref[...] = jnp.zeros_like(dws_ref)
            dbf_ref[...] = jnp.zeros_like(dbf_ref)

        dlg = jnp.zeros((1, E), F32)
        dlb = jnp.zeros((1, E), F32)
        for ch in range(tr // CHUNK):
            rs_ = slice(ch * CHUNK, (ch + 1) * CHUNK)
            zu = zp_ref[rs_, 0:E].astype(F32)
            zv = zp_ref[rs_, E:E2].astype(F32)
            u, tu = _gelu(zu)
            v, tv = _gelu(zv)
            vn, xhat, rs = _layernorm_fwd(v, g_ref[...], b_ref[...])
            vnb = vn.astype(BF16)
            dpf = dp_ref[rs_, :].astype(F32)
            for gi in range(G):
                cs = slice(gi * CHUNK, (gi + 1) * CHUNK)
                w = ws_ref[gi]
                vs = jnp.dot(w, vnb[:, cs], preferred_element_type=F32) + bf_ref[:, cs]
                dz_ref[rs_, cs] = (dpf[:, cs] * vs * _gelu_grad(zu[:, cs], tu[:, cs])).astype(BF16)
                dvs = dpf[:, cs] * u[:, cs]
                dvsb = dvs.astype(BF16)
                dws_ref[gi] += lax.dot_general(dvsb, vnb[:, cs], NT, preferred_element_type=F32)
                dbf_ref[:, cs] += dvs
                dvn_ref[:, cs] = lax.dot_general(w, dvsb, TN, preferred_element_type=F32)
            dvn = dvn_ref[...]
            dlg = dlg + _colsum(dvn * xhat)
            dlb = dlb + _colsum(dvn)
            dv = _layernorm_bwd(dvn, xhat, rs, g_ref[...])
            dz_ref[rs_, E:E2] = (dv * _gelu_grad(zv, tv)).astype(BF16)
        _acc_rows(st_ref, i == 0, [(0, dlg), (1, dlb)])

        @pl.when(i == n - 1)
        def _():
            for gi in range(G):
                dbs_ref[:, gi:gi + 1] = jnp.sum(dbf_ref[:, gi * CHUNK:(gi + 1) * CHUNK], axis=1, keepdims=True)

    return pl.pallas_call(
        body, name=name, grid=(n,),
        in_specs=[_rowspec(tr, E2), _rowspec(tr, E), _fullspec((1, E)), _fullspec((1, E)),
                  _fullspec((G, CHUNK, CHUNK)), _fullspec((CHUNK, E))],
        out_specs=(_rowspec(tr, E2), _fullspec((G, CHUNK, CHUNK)), _fullspec((CHUNK, G)), _fullspec((8, E))),
        out_shape=(jax.ShapeDtypeStruct((T, E2), BF16), jax.ShapeDtypeStruct((G, CHUNK, CHUNK), F32),
                   jax.ShapeDtypeStruct((CHUNK, G), F32), jax.ShapeDtypeStruct((8, E), F32)),
        scratch_shapes=[pltpu.VMEM((CHUNK, E), F32), pltpu.VMEM((CHUNK, E), F32)],
        compiler_params=_params(1),
    )(zp, dp, ln_g, ln_b, ws, bfull)


def _halo_specs(tr, T, width):
    per = tr // HALO
    last = T // HALO - 1
    prev = pl.BlockSpec((HALO, width), lambda i: (jnp.maximum(i * per - 1, 0), 0))
    nxt = pl.BlockSpec((HALO, width), lambda i: (jnp.minimum((i + 1) * per, last), 0))
    return prev, nxt


def _halo_valid(i, n, nlat):
    return jnp.logical_and(i > 0, i != nlat), jnp.logical_and(i + 1 < n, i + 1 != nlat)


def _glu(tt, D):
    a = tt[:, 0:D].astype(F32)
    g = tt[:, D:2 * D].astype(F32)
    return a * _sigmoid(g)


CONV_SUB = 32
CONV_SPAN = 8 * ((CONV_W + 7) // 8) - 8


def _build_shifts(buf, sh, tr):
    for s in range(1, 8):
        sh[s - 1, 0:tr + CONV_SPAN, :] = buf[pl.ds(s, tr + CONV_SPAN), :]


def _tap(buf, sh, offset, r0, cs):
    s = offset % 8
    start = pl.multiple_of(r0 + (offset - s), 8)
    if s == 0:
        return buf[pl.ds(start, CONV_SUB), cs]
    return sh[s - 1, pl.ds(start, CONV_SUB), cs]


def conv_fwd(name, t, wdw, bdw, ln_g, ln_b, seq):
    T, D2 = t.shape
    D = D2 // 2
    tr = _row_tile(T, seq)
    n, nlat = T // tr, seq // tr
    prev, nxt = _halo_specs(tr, T, D2)

    tc = D // 2

    def body(tp_ref, tc_ref, tn_ref, w_ref, b_ref, g_ref, bb_ref, y_ref, yc_ref, s_ref, buf, sh):
        i = pl.program_id(0)
        pv, nv = _halo_valid(i, n, nlat)
        y = _glu(tc_ref[...], D)
        y_ref[...] = y
        buf[0:HALO, :] = jnp.where(pv, _glu(tp_ref[...], D), 0.0)
        buf[HALO:HALO + tr, :] = y
        buf[HALO + tr:2 * HALO + tr, :] = jnp.where(nv, _glu(tn_ref[...], D), 0.0)
        _build_shifts(buf, sh, tr)

        def rows_block(rb, carry):
            r0 = pl.multiple_of(rb * CONV_SUB, CONV_SUB)
            for c0 in range(0, D, tc):
                cs = slice(c0, c0 + tc)
                acc = jnp.zeros((CONV_SUB, tc), F32) + b_ref[:, cs]
                for k in range(CONV_W):
                    acc = acc + w_ref[k:k + 1, cs] * _tap(buf, sh, k + 1, r0, cs)
                yc_ref[pl.ds(r0, CONV_SUB), cs] = acc
            return carry

        lax.fori_loop(0, tr // CONV_SUB, rows_block, 0)
        yl, _, _ = _layernorm_fwd(yc_ref[...], g_ref[...], bb_ref[...])
        s_ref[...] = (yl * _sigmoid(yl)).astype(BF16)

    return pl.pallas_call(
        body, name=name, grid=(n,),
        in_specs=[prev, _rowspec(tr, D2), nxt, _fullspec((32, D)), _fullspec((1, D)), _fullspec((1, D)),
                  _fullspec((1, D))],
        out_specs=(_rowspec(tr, D), _rowspec(tr, D), _rowspec(tr, D)),
        out_shape=(jax.ShapeDtypeStruct((T, D), F32), jax.ShapeDtypeStruct((T, D), F32),
                   jax.ShapeDtypeStruct((T, D), BF16)),
        scratch_shapes=[pltpu.VMEM((tr + 2 * HALO, D), F32), pltpu.VMEM((7, tr + CONV_SPAN, D), F32)],
        compiler_params=_params(1),
    )(t, t, t, wdw, bdw, ln_g, ln_b)


def conv_bwd_norm(name, ds, yc, ln_g, ln_b):
    T, D = yc.shape
    tr = _tile(T, 256, 8)

    def body(ds_ref, yc_ref, g_ref, b_ref, o_ref, st_ref):
        i = pl.program_id(0)
        yl, xhat, rs = _layernorm_fwd(yc_ref[...], g_ref[...], b_ref[...])
        sig = _sigmoid(yl)
        dyl = ds_ref[...] * (sig * (1.0 + yl * (1.0 - sig)))
        dyc = _layernorm_bwd(dyl, xhat, rs, g_ref[...])
        o_ref[...] = dyc
        _acc_rows(st_ref, i == 0, [(0, _colsum(dyl * xhat)), (1, _colsum(dyl)), (2, _colsum(dyc))])

    return pl.pallas_call(
        body, name=name, grid=(T // tr,),
        in_specs=[_rowspec(tr, D), _rowspec(tr, D), _fullspec((1, D)), _fullspec((1, D))],
        out_specs=(_rowspec(tr, D), _fullspec((8, D))),
        out_shape=(jax.ShapeDtypeStruct((T, D), F32), jax.ShapeDtypeStruct((8, D), F32)),
        compiler_params=_params(1),
    )(ds, yc, ln_g, ln_b)


def conv_bwd_taps(name, dyc, y, t, wdw, seq):
    T, D = y.shape
    tr = _row_tile(T, seq)
    n, nlat = T // tr, seq // tr
    prev, nxt = _halo_specs(tr, T, D)

    tc = D // 2
    full = slice(0, tc)

    def body(dp_ref, dc_ref, dn_ref, yp_ref, ycur_ref, yn_ref, t_ref, w_ref, dt_ref, dw_ref,
             dbuf, ybuf, dsh, ysh, dwacc):
        i = pl.program_id(0)
        pv, nv = _halo_valid(i, n, nlat)

        @pl.when(i == 0)
        def _():
            dw_ref[...] = jnp.zeros_like(dw_ref)

        for c0 in range(0, D, tc):
            cs = slice(c0, c0 + tc)
            for buf, sh, p_ref, c_ref, n_ref in ((dbuf, dsh, dp_ref, dc_ref, dn_ref), (ybuf, ysh, yp_ref, ycur_ref, yn_ref)):
                buf[0:HALO, :] = jnp.where(pv, p_ref[:, cs], 0.0)
                buf[HALO:HALO + tr, :] = c_ref[:, cs]
                buf[HALO + tr:2 * HALO + tr, :] = jnp.where(nv, n_ref[:, cs], 0.0)
                _build_shifts(buf, sh, tr)
            dwacc[...] = jnp.zeros_like(dwacc)

            def rows_block(rb, carry):
                r0 = pl.multiple_of(rb * CONV_SUB, CONV_SUB)
                rows = pl.ds(r0, CONV_SUB)
                dcur = dc_ref[rows, cs]
                dy = jnp.zeros((CONV_SUB, tc), F32)
                for k in range(CONV_W):
                    prod = dcur * _tap(ybuf, ysh, k + 1, r0, full)
                    dwacc[k] += prod[0:8] + prod[8:16] + prod[16:24] + prod[24:32]
                    dy = dy + w_ref[k:k + 1, cs] * _tap(dbuf, dsh, CONV_W - k, r0, full)
                a = t_ref[rows, cs].astype(F32)
                sig = _sigmoid(t_ref[rows, slice(D + c0, D + c0 + tc)].astype(F32))
                dt_ref[rows, cs] = (dy * sig).astype(BF16)
                dt_ref[rows, slice(D + c0, D + c0 + tc)] = (dy * a * sig * (1.0 - sig)).astype(BF16)
                return carry

            lax.fori_loop(0, tr // CONV_SUB, rows_block, 0)
            for k in range(CONV_W):
                dw_ref[k:k + 1, cs] += jnp.sum(dwacc[k], axis=0, keepdims=True)

    return pl.pallas_call(
        body, name=name, grid=(n,),
        in_specs=[prev, _rowspec(tr, D), nxt, prev, _rowspec(tr, D), nxt, _rowspec(tr, 2 * D), _fullspec((32, D))],
        out_specs=(_rowspec(tr, 2 * D), _fullspec((32, D))),
        out_shape=(jax.ShapeDtypeStruct((T, 2 * D), BF16), jax.ShapeDtypeStruct((32, D), F32)),
        scratch_shapes=[pltpu.VMEM((tr + 2 * HALO, tc), F32), pltpu.VMEM((tr + 2 * HALO, tc), F32),
                        pltpu.VMEM((7, tr + CONV_SPAN, tc), F32), pltpu.VMEM((7, tr + CONV_SPAN, tc), F32),
                        pltpu.VMEM((32, 8, tc), F32)],
        compiler_params=_params(1),
    )(dyc, dyc, dyc, y, y, y, t, wdw)


Q_ROWS = 8
K_ROWS = 16
QN = Q_ROWS * GRID_W
WIN = K_ROWS * GRID_W
NA_Q_OFFSET = (0, 4, 8)


def _na_window_start(cls, qi):
    return (max(qi - NA_ROWS // 2, 0), qi, min(qi + NA_ROWS // 2, NA_ROWS))[cls]


def _na_pairs(cls):
    for qi in range(Q_ROWS):
        lo = _na_window_start(cls, qi)
        for kj in range(lo, lo + NA_ROWS):
            yield qi, kj, kj - NA_Q_OFFSET[cls] - qi + NA_ROWS - 1


def na_bias_tables(toep):
    H = toep.shape[0]
    neg = jnp.full((H, GRID_W, GRID_W), NEG_INF, F32)
    tables = []
    for cls in range(3):
        dr_of = {(qi, kj): dr for qi, kj, dr in _na_pairs(cls)}
        rows = [jnp.concatenate([toep[:, dr_of[(qi, kj)]] if (qi, kj) in dr_of else neg for kj in range(K_ROWS)],
                                axis=2) for qi in range(Q_ROWS)]
        tables.append(jnp.concatenate(rows, axis=1))
    return jnp.stack(tables)


def _na_class(rb, nblk):
    return jnp.where(rb == 0, 0, jnp.where(rb == nblk - 1, 2, 1))


def _na_specs(seq, ctx_rows, H):
    cb = seq // ctx_rows
    nblk = seq // QN
    return [
        pl.BlockSpec((QN, HEAD), lambda h, r: (r, h)),
        pl.BlockSpec((seq, HEAD), lambda h, r: (0, H + h)),
        pl.BlockSpec((seq, HEAD), lambda h, r: (0, 2 * H + h)),
        pl.BlockSpec((ctx_rows, HEAD), lambda h, r: (cb, H + h)),
        pl.BlockSpec((ctx_rows, HEAD), lambda h, r: (cb, 2 * H + h)),
        pl.BlockSpec((None, None, QN, WIN), lambda h, r: (_na_class(r, nblk), h, 0, 0)),
    ]


def _na_scores(q_ref, k_ref, v_ref, kc_ref, vc_ref, b_ref, rows):
    rb = pl.program_id(1)
    k_start = jnp.clip(Q_ROWS * rb - NA_ROWS // 2, 0, rows - K_ROWS)
    start = pl.multiple_of(k_start * GRID_W, GRID_W)
    scale = HEAD ** -0.5
    q = q_ref[...]
    kw = k_ref[pl.ds(start, WIN), :]
    vw = v_ref[pl.ds(start, WIN), :]
    kc = kc_ref[...]
    vc = vc_ref[...]
    s = lax.dot_general(q, kw, NT, preferred_element_type=F32) * scale + b_ref[...]
    sc = lax.dot_general(q, kc, NT, preferred_element_type=F32) * scale
    m = jnp.maximum(jnp.max(s, axis=-1, keepdims=True), jnp.max(sc, axis=-1, keepdims=True))
    p = jnp.exp(s - m)
    pc = jnp.exp(sc - m)
    l = jnp.sum(p, axis=-1, keepdims=True) + jnp.sum(pc, axis=-1, keepdims=True)
    return q, kw, vw, kc, vc, p, pc, l, start, scale


def na_fwd(name, qkv, bias, seq, tasks=()):
    T, D3 = qkv.shape
    D = D3 // 3
    H = D // HEAD
    rows = seq // GRID_W

    def body(q_ref, k_ref, v_ref, kc_ref, vc_ref, b_ref, o_ref):
        q, kw, vw, kc, vc, p, pc, l, start, scale = _na_scores(q_ref, k_ref, v_ref, kc_ref, vc_ref, b_ref, rows)
        o = (jnp.dot(p.astype(BF16), vw, preferred_element_type=F32)
             + jnp.dot(pc.astype(BF16), vc, preferred_element_type=F32))
        o_ref[...] = (o / l).astype(BF16)

    outs, touts = host_call(name, body, (H, seq // QN), [qkv, qkv, qkv, qkv, qkv, bias],
                            _na_specs(seq, T - seq, H), [jax.ShapeDtypeStruct((seq, D), BF16)],
                            [pl.BlockSpec((QN, HEAD), lambda h, r: (r, h))], tasks=tasks)
    return outs[0], touts


def na_bwd(name, qkv, bias, do, seq):
    T, D3 = qkv.shape
    D = D3 // 3
    H = D // HEAD
    rows = seq // GRID_W
    nblk = seq // QN
    ctx_rows = T - seq

    def body(q_ref, k_ref, v_ref, kc_ref, vc_ref, b_ref, do_ref, dq_ref, dk_ref, dv_ref, dkc_ref, dvc_ref, db_ref):
        rb = pl.program_id(1)

        @pl.when(rb == 0)
        def _():
            for ref in (dk_ref, dv_ref, dkc_ref, dvc_ref):
                ref[...] = jnp.zeros_like(ref)

        @pl.when(jnp.logical_or(rb <= 1, rb == nblk - 1))
        def _():
            db_ref[...] = jnp.zeros_like(db_ref)

        q, kw, vw, kc, vc, p, pc, l, start, scale = _na_scores(q_ref, k_ref, v_ref, kc_ref, vc_ref, b_ref, rows)
        inv = 1.0 / l
        pn = p * inv
        pcn = pc * inv
        do_ = do_ref[...]
        dp = lax.dot_general(do_, vw, NT, preferred_element_type=F32)
        dpc = lax.dot_general(do_, vc, NT, preferred_element_type=F32)
        delta = jnp.sum(pn * dp, axis=-1, keepdims=True) + jnp.sum(pcn * dpc, axis=-1, keepdims=True)
        ds = pn * (dp - delta)
        dsc = pcn * (dpc - delta)
        db_ref[...] += ds
        dsb = (ds * scale).astype(BF16)
        dscb = (dsc * scale).astype(BF16)
        dq = jnp.dot(dsb, kw, preferred_element_type=F32) + jnp.dot(dscb, kc, preferred_element_type=F32)
        dq_ref[...] = dq.astype(BF16)
        dk_ref[pl.ds(start, WIN), :] += lax.dot_general(dsb, q, TN, preferred_element_type=F32)
        dv_ref[pl.ds(start, WIN), :] += lax.dot_general(pn.astype(BF16), do_, TN, preferred_element_type=F32)
        dkc_ref[...] += lax.dot_general(dscb, q, TN, preferred_element_type=F32)
        dvc_ref[...] += lax.dot_general(pcn.astype(BF16), do_, TN, preferred_element_type=F32)

    head_lat = pl.BlockSpec((seq, HEAD), lambda h, r: (0, h))
    head_ctx = pl.BlockSpec((ctx_rows, HEAD), lambda h, r: (0, h))
    return pl.pallas_call(
        body, name=name, grid=(H, nblk),
        in_specs=_na_specs(seq, ctx_rows, H) + [pl.BlockSpec((QN, HEAD), lambda h, r: (r, h))],
        out_specs=(pl.BlockSpec((QN, HEAD), lambda h, r: (r, h)), head_lat, head_lat, head_ctx, head_ctx,
                   pl.BlockSpec((None, None, QN, WIN), lambda h, r: (_na_class(r, nblk), h, 0, 0))),
        out_shape=(jax.ShapeDtypeStruct((seq, D), BF16), jax.ShapeDtypeStruct((seq, D), F32),
                   jax.ShapeDtypeStruct((seq, D), F32), jax.ShapeDtypeStruct((ctx_rows, D), F32),
                   jax.ShapeDtypeStruct((ctx_rows, D), F32), jax.ShapeDtypeStruct(bias.shape, F32)),
        compiler_params=_params(2),
    )(qkv, qkv, qkv, qkv, qkv, bias, do)


def _ctx_specs(seq, ctx_rows, H):
    cb = seq // ctx_rows
    return [pl.BlockSpec((ctx_rows, HEAD), lambda h: (cb, h)),
            pl.BlockSpec((ctx_rows, HEAD), lambda h: (cb, H + h)),
            pl.BlockSpec((ctx_rows, HEAD), lambda h: (cb, 2 * H + h))]


def _ctx_probs(q_ref, k_ref):
    s = lax.dot_general(q_ref[...], k_ref[...], NT, preferred_element_type=F32) * (HEAD ** -0.5)
    p = jnp.exp(s - jnp.max(s, axis=-1, keepdims=True))
    return p / jnp.sum(p, axis=-1, keepdims=True)


def ctx_attn_fwd(name, qkv, seq):
    T, D3 = qkv.shape
    D = D3 // 3
    H = D // HEAD
    ctx_rows = T - seq

    def body(q_ref, k_ref, v_ref, o_ref):
        p = _ctx_probs(q_ref, k_ref)
        o_ref[...] = jnp.dot(p.astype(BF16), v_ref[...], preferred_element_type=F32).astype(BF16)

    return pl.pallas_call(
        body, name=name, grid=(H,), in_specs=_ctx_specs(seq, ctx_rows, H),
        out_specs=pl.BlockSpec((ctx_rows, HEAD), lambda h: (0, h)),
        out_shape=jax.ShapeDtypeStruct((ctx_rows, D), BF16), compiler_params=_params(1),
    )(qkv, qkv, qkv)


def ctx_attn_bwd(name, qkv, do, dkc_lat, dvc_lat, seq):
    T, D3 = qkv.shape
    D = D3 // 3
    H = D // HEAD
    ctx_rows = T - seq
    cb = seq // ctx_rows
    scale = HEAD ** -0.5

    def body(q_ref, k_ref, v_ref, do_ref, dkl_ref, dvl_ref, dq_ref, dk_ref, dv_ref):
        p = _ctx_probs(q_ref, k_ref)
        do_ = do_ref[...]
        dp = lax.dot_general(do_, v_ref[...], NT, preferred_element_type=F32)
        ds = p * (dp - jnp.sum(p * dp, axis=-1, keepdims=True))
        dsb = (ds * scale).astype(BF16)
        dq_ref[...] = jnp.dot(dsb, k_ref[...], preferred_element_type=F32).astype(BF16)
        dk_ref[...] = (dkl_ref[...] + lax.dot_general(dsb, q_ref[...], TN, preferred_element_type=F32)).astype(BF16)
        dv_ref[...] = (dvl_ref[...]
                       + lax.dot_general(p.astype(BF16), do_, TN, preferred_element_type=F32)).astype(BF16)

    blk = pl.BlockSpec((ctx_rows, HEAD), lambda h: (0, h))
    shp = jax.ShapeDtypeStruct((ctx_rows, D), BF16)
    return pl.pallas_call(
        body, name=name, grid=(H,),
        in_specs=_ctx_specs(seq, ctx_rows, H) + [pl.BlockSpec((ctx_rows, HEAD), lambda h: (cb, h)), blk, blk],
        out_specs=(blk, blk, blk), out_shape=(shp, shp, shp), compiler_params=_params(1),
    )(qkv, qkv, qkv, do, dkc_lat, dvc_lat)


def _rpb_tables():
    qc = jnp.arange(GRID_W)[:, None]
    kc = jnp.arange(GRID_W)[None, :]
    rel = (kc - qc + NA_COLS - 1).reshape(1, GRID_W * GRID_W)
    onehot = (rel == jnp.arange(32)[:, None]).astype(F32)
    c_start = jnp.clip(qc - NA_COLS // 2, 0, GRID_W - NA_COLS)
    mask = jnp.logical_and(kc >= c_start, kc < c_start + NA_COLS).astype(F32).reshape(1, GRID_W * GRID_W)
    return onehot, mask


def rpb_expand(name, rpb2, onehot, mask):
    R = rpb2.shape[0]

    def body(r_ref, oh_ref, m_ref, o_ref):
        t = jnp.dot(r_ref[...], oh_ref[...], preferred_element_type=F32, precision=lax.Precision.HIGHEST)
        o_ref[...] = jnp.where(m_ref[...] > 0.5, t, NEG_INF)

    return pl.pallas_call(body, name=name, out_shape=jax.ShapeDtypeStruct((R, GRID_W * GRID_W), F32),
                          compiler_params=_params())(rpb2, onehot, mask)


def rpb_fold(name, x, classes):
    H = x.shape[0]
    n_dr = 2 * NA_ROWS - 1

    def body(x_ref, y_ref):
        acc = [None] * n_dr
        for cls in classes:
            for qi, kj, dr in _na_pairs(cls):
                acc[dr] = x_ref[cls, qi, kj] if acc[dr] is None else acc[dr] + x_ref[cls, qi, kj]
        for dr in range(n_dr):
            y_ref[dr] = acc[dr]

    return pl.pallas_call(
        body, name=name, grid=(H,),
        in_specs=[pl.BlockSpec((None, 3, Q_ROWS, K_ROWS, GRID_W, GRID_W), lambda h: (h, 0, 0, 0, 0, 0))],
        out_specs=pl.BlockSpec((None, n_dr, GRID_W, GRID_W), lambda h: (h, 0, 0, 0)),
        out_shape=jax.ShapeDtypeStruct((H, n_dr, GRID_W, GRID_W), F32), compiler_params=_params(1),
    )(x)


def rpb_reduce(name, y2, onehot_t):
    R = y2.shape[0]

    def body(y_ref, oh_ref, o_ref):
        o_ref[...] = jnp.dot(y_ref[...], oh_ref[...], preferred_element_type=F32, precision=lax.Precision.HIGHEST)

    return pl.pallas_call(body, name=name, out_shape=jax.ShapeDtypeStruct((R, 32), F32),
                          compiler_params=_params())(y2, onehot_t)


def _adam(g, w, m, v):
    m2 = ADAM_B1 * m + (1.0 - ADAM_B1) * g
    v2 = ADAM_B2 * v + (1.0 - ADAM_B2) * (g * g)
    m_hat = m2 / (1.0 - ADAM_B1 ** ADAM_STEP)
    v_hat = v2 / (1.0 - ADAM_B2 ** ADAM_STEP)
    delta = -ADAM_LR * (m_hat / (jnp.sqrt(v_hat) + ADAM_EPS) + ADAM_WD * w)
    return delta, m2, v2


def adam_parts(name, land, land2, w, m, v, tasks=()):
    shape = w.shape
    C = shape[-1]
    R = math.prod(shape[:-1])
    tr = _tile(R, max(16, (256 * 1024 // C) // 16 * 16), 16)
    l1, l2 = land.reshape(N_SHARD, R, C), land2.reshape(N_SHARD, R, C)

    def body(l1_ref, l2_ref, w_ref, m_ref, v_ref, g_ref, d_ref, m2_ref, v2_ref):
        a = l1_ref[0].astype(F32)
        b = l2_ref[0].astype(F32)
        for k in range(1, N_SHARD):
            a = a + l1_ref[k].astype(F32)
            b = b + l2_ref[k].astype(F32)
        g = a + b
        g_ref[...] = g
        d_ref[...], m2_ref[...], v2_ref[...] = _adam(g, w_ref[...], m_ref[...], v_ref[...])

    part = pl.BlockSpec((N_SHARD, tr, C), lambda i: (0, i, 0))
    row = _rowspec(tr, C)
    shp = jax.ShapeDtypeStruct((R, C), F32)
    outs, touts = host_call(name, body, (R // tr,), [l1, l2, w.reshape(R, C), m.reshape(R, C), v.reshape(R, C)],
                            [part, part, row, row, row], (shp,) * 4, (row,) * 4, tasks=tasks)
    return tuple(o.reshape(shape) for o in outs), touts


def adam_flat(name, g, w, m, v):
    R, C = g.shape
    tr = _tile(R, 256, 8)

    def body(g_ref, w_ref, m_ref, v_ref, d_ref, m2_ref, v2_ref):
        d_ref[...], m2_ref[...], v2_ref[...] = _adam(g_ref[...], w_ref[...], m_ref[...], v_ref[...])

    row = _rowspec(tr, C)
    shp = jax.ShapeDtypeStruct((R, C), F32)
    return pl.pallas_call(body, name=name, grid=(R // tr,), in_specs=[row] * 4, out_specs=(row,) * 3,
                          out_shape=(shp,) * 3, compiler_params=_params(1))(g, w, m, v)


def reduce_8(name, gathered):
    _, R, D = gathered.shape
    tr = _tile(R, 64, 8)

    def body(x_ref, o_ref):
        acc = x_ref[0]
        for k in range(1, 8):
            acc = acc + x_ref[k]
        o_ref[...] = acc

    return pl.pallas_call(
        body, name=name, grid=(R // tr,), in_specs=[pl.BlockSpec((8, tr, D), lambda i: (0, i, 0))],
        out_specs=_rowspec(tr, D), out_shape=jax.ShapeDtypeStruct((R, D), F32), compiler_params=_params(1),
    )(gathered)


def ada_fwd(name, craw16, ada_w, ada_b3):
    L, D, Cs = ada_w.shape
    tn = _tile(Cs, 512)

    def body(c_ref, w_ref, b_ref, o_ref):
        cc = c_ref[...]
        s = cc * _sigmoid(cc)
        o_ref[...] = jnp.dot(s, w_ref[...], preferred_element_type=F32,
                             precision=lax.Precision.HIGHEST) + b_ref[...]

    return pl.pallas_call(
        body, name=name, grid=(L, Cs // tn),
        in_specs=[pl.BlockSpec((16, D), lambda l, j: (0, 0)), pl.BlockSpec((None, D, tn), lambda l, j: (l, 0, j)),
                  pl.BlockSpec((None, 1, tn), lambda l, j: (l, 0, j))],
        out_specs=pl.BlockSpec((None, 16, tn), lambda l, j: (l, 0, j)),
        out_shape=jax.ShapeDtypeStruct((L, 16, Cs), F32), compiler_params=_params(2),
    )(craw16, ada_w, ada_b3)


def ada_bwd_adam(name, craw16_t, dm16, dmc8, w, m, v):
    L, D, Cs = w.shape
    tn = _tile(Cs, 256)

    def body(c_ref, dm_ref, dc_ref, w_ref, m_ref, v_ref, g_ref, d_ref, m2_ref, v2_ref, ds_ref):
        step = pl.program_id(0) * (Cs // tn) + pl.program_id(1)
        cc = c_ref[...]
        s_t = cc * _sigmoid(cc)
        g = jnp.dot(s_t, dm_ref[...], preferred_element_type=F32, precision=lax.Precision.HIGHEST)
        ww = w_ref[...]
        g_ref[...] = g
        d_ref[...], m2_ref[...], v2_ref[...] = _adam(g, ww, m_ref[...], v_ref[...])

        @pl.when(step == 0)
        def _():
            ds_ref[...] = jnp.zeros_like(ds_ref)

        ds_ref[...] += lax.dot_general(dc_ref[...].astype(BF16), ww.astype(BF16), NT, preferred_element_type=F32)

    wspec = pl.BlockSpec((None, D, tn), lambda l, j: (l, 0, j))
    shp = jax.ShapeDtypeStruct((L, D, Cs), F32)
    return pl.pallas_call(
        body, name=name, grid=(L, Cs // tn),
        in_specs=[pl.BlockSpec((D, 16), lambda l, j: (0, 0)), pl.BlockSpec((None, 16, tn), lambda l, j: (l, 0, j)),
                  pl.BlockSpec((None, 8, tn), lambda l, j: (l, 0, j)), wspec, wspec, wspec],
        out_specs=(wspec, wspec, wspec, wspec, pl.BlockSpec((8, D), lambda l, j: (0, 0))),
        out_shape=(shp, shp, shp, shp, jax.ShapeDtypeStruct((8, D), F32)), compiler_params=_params(2),
    )(craw16_t, dm16, dmc8, w, m, v)


def cctx_adam(name, ds_all, c_ctx, m, v):
    D = c_ctx.shape[1]

    def body(ds_ref, c_ref, m_ref, v_ref, g_ref, d_ref, m2_ref, v2_ref):
        ds = ds_ref[0, 0:1, :]
        for slot in (2, 4, 6):
            ds = ds + ds_ref[slot, 0:1, :]
        cc = c_ref[...]
        sig = _sigmoid(cc)
        g = ds * (sig * (1.0 + cc * (1.0 - sig)))
        g_ref[...] = g
        d_ref[...], m2_ref[...], v2_ref[...] = _adam(g, cc, m_ref[...], v_ref[...])

    shp = jax.ShapeDtypeStruct((1, D), F32)
    return pl.pallas_call(body, name=name, out_shape=(shp,) * 4, compiler_params=_params())(ds_all, c_ctx, m, v)


WEIGHT_NAMES = ['c_ctx', 'ada_w', 'ada_b', 'g_mix', 'g_ffn', 'ffn_w1', 'ffn_w3', 'ffn_w2', 'a_w_in', 'a_ln_g',
                'a_ln_b', 'a_w_s', 'a_b_s', 'a_w_out', 'b_w_qkv', 'b_rpb', 'b_w_out', 'c_w_pw1', 'c_w_dw', 'c_b_dw',
                'c_ln_g', 'c_ln_b', 'c_w_pw2', 'g_final']
BIG_NAMES = ['ffn_w1', 'ffn_w3', 'ffn_w2', 'a_w_in', 'a_w_out', 'b_w_qkv', 'b_w_out', 'c_w_pw1', 'c_w_pw2']
SMALL_NAMES = ['ada_b', 'g_mix', 'g_ffn', 'a_ln_g', 'a_ln_b', 'a_w_s', 'a_b_s', 'b_rpb', 'c_w_dw', 'c_b_dw',
               'c_ln_g', 'c_ln_b', 'g_final']
SMALL_PACK_COLS = 512
MIXER_IN = ('a_w_in', 'b_w_qkv', 'c_w_pw1')
MIXER_OUT = ('a_w_out', 'b_w_out', 'c_w_pw2')

FWD_PLAN = {
    "pre": [("a_w_in", 0)],
    "in_0": [("a_w_out", 0), ("ffn_w1", 0)],
    "out_0": [("ffn_w3", 0)],
    "ffn_up_0": [("ffn_w2", 0), ("b_w_qkv", 0)],
    "ffn_down_0": [("b_w_out", 0), ("ffn_w1", 1)],
    "in_1": [("ffn_w3", 1)],
    "b_na_1": [("ffn_w2", 1), ("c_w_pw1", 0), ("c_w_pw2", 0)],
    "out_1": [("ffn_w1", 2)],
    "ffn_up_1": [("ffn_w3", 2), ("ffn_w2", 2)],
    "ffn_down_1": [("a_w_in", 1), ("a_w_out", 1)],
    "in_2": [("ffn_w1", 3)],
    "ffn_up_2": [("ffn_w3", 3), ("ffn_w2", 3)],
}


def _bwd_plan():
    plan = {}
    for i in range(N_LAYERS):
        w_in, w_out = (MIXER_IN[i % 3], i // 3), (MIXER_OUT[i % 3], i // 3)
        if i + 1 < N_LAYERS:
            plan[f"ffn_down_dx_{i}"] = [("forward", MIXER_IN[(i + 1) % 3], (i + 1) // 3)]
        plan[f"ffn_w1_dw_{i}"] = [("scatter", "ffn_w2", i)]
        plan[f"ffn_w3_dw_{i}"] = [("forward", "ffn_w2", i), ("scatter", "ffn_w1", i)]
        plan[f"ffn_up_dx_{i}"] = [("forward", "ffn_w1", i), ("scatter", "ffn_w3", i)]
        plan[f"out_dx_{i}"] = [("forward", "ffn_w3", i)]
        plan[f"in_dw_{i}"] = [("scatter",) + w_out]
        plan[f"in_dx_{i}"] = [("forward",) + w_out, ("scatter",) + w_in]
    plan["rs_post"] = [("forward", MIXER_IN[0], 0)]
    return plan


BWD_PLAN = _bwd_plan()


def _pad_rows(a, mult):
    r = (-a.shape[0]) % mult
    return a if r == 0 else jnp.concatenate([a, jnp.zeros((r,) + a.shape[1:], a.dtype)], axis=0)


def _rows_of(flat, D):
    n = flat.shape[0]
    r = -(-n // D)
    return jnp.concatenate([flat, jnp.zeros((r * D - n,), flat.dtype)]).reshape(r, D)


def _step(W, Mo, Vo, x, c, ctx, loss_target):
    seq, D = x.shape[1], x.shape[2]
    ctx_rows = ctx.shape[1]
    T = seq + ctx_rows
    L = N_LAYERS
    H = D // HEAD
    G = D // CHUNK
    xi, yi, ci = _xyc()
    e_idx = 4 * xi + 2 * yi + ci
    s_idx = 2 * xi + yi

    c_all = all_gather_8("ag_c", c)
    craw16 = jnp.concatenate([c_all.reshape(8, D), W['c_ctx'].reshape(1, D), jnp.zeros((7, D), F32)], axis=0)
    ada_w = W['ada_w']
    Cs = ada_w.shape[2]
    ada_b_s = lax.dynamic_slice_in_dim(W['ada_b'], s_idx * Cs, Cs, axis=1).reshape(L, 1, Cs)
    mod_s = ada_fwd("ada_fwd", craw16, ada_w, ada_b_s)
    mod_g = all_gather_xy("ag_mod", mod_s).transpose(1, 2, 0, 3).reshape(L, 16, N_SHARD * Cs)
    mod_lat = lax.dynamic_index_in_dim(mod_g, e_idx, axis=1, keepdims=False).reshape(L, 6, D)
    mod_all = jnp.concatenate([mod_lat, mod_g[:, 8].reshape(L, 6, D), jnp.zeros((L, 4, D), F32)], axis=1)

    Wg = {}
    land = {n: lax.empty((N_SHARD,) + W[n].shape, BF16) for n in BIG_NAMES}
    land2 = {n: lax.empty((N_SHARD,) + W[n].shape, BF16) for n in BIG_NAMES}
    dW = {}

    def gather_tasks(host):
        return [GatherTask(W[n][l].astype(BF16)) for n, l in FWD_PLAN.get(host, ())]

    def gathered(host, touts):
        for (n, l), out in zip(FWD_PLAN.get(host, ()), touts):
            Wg[(n, l)] = out[0]

    def scatter_tasks(host):
        tasks = []
        for kind, n, l in BWD_PLAN.get(host, ()):
            if kind == "scatter":
                tasks.append(ScatterTask(dW[(n, l)], land[n], land2[n], l))
            else:
                tasks.append(ForwardTask(land[n], land2[n], l))
        return tasks

    def scattered(host, touts):
        for (kind, n, l), out in zip(BWD_PLAN.get(host, ()), touts):
            land[n], land2[n] = out

    gathered("pre", comm_only("ag_pre", gather_tasks("pre")))

    n_a, n_c = W['a_ln_g'].shape[0], W['c_ln_g'].shape[0]
    sh_rows = jnp.concatenate([W['a_ln_g'], W['a_ln_b'], W['c_w_dw'].reshape(n_c * CONV_W, -1), W['c_b_dw'],
                               W['c_ln_g'], W['c_ln_b']], axis=0)
    n_sh = sh_rows.shape[0]
    sh_full = all_gather_xy("ag_small", _pad_rows(sh_rows, 8)).transpose(1, 0, 2).reshape(-1, D)[:n_sh]
    o = 0
    a_ln_g_f, o = sh_full[o:o + n_a], o + n_a
    a_ln_b_f, o = sh_full[o:o + n_a], o + n_a
    c_w_dw_f, o = sh_full[o:o + n_c * CONV_W].reshape(n_c, CONV_W, D), o + n_c * CONV_W
    c_b_dw_f, o = sh_full[o:o + n_c], o + n_c
    c_ln_g_f, o = sh_full[o:o + n_c], o + n_c
    c_ln_b_f, o = sh_full[o:o + n_c], o + n_c

    onehot, colmask = _rpb_tables()
    n_dr = 2 * NA_ROWS - 1
    rpb2 = jnp.pad(W['b_rpb'][0].reshape(H * n_dr, 2 * NA_COLS - 1), ((0, 0), (0, 1)))
    toep = rpb_expand("rpb_expand", rpb2, onehot, colmask).reshape(H, n_dr, GRID_W, GRID_W)
    bias = na_bias_tables(toep)
    na_classes = (0, 1, 2) if seq // QN > 2 else (0, 2)

    def mixer_params(i):
        mixer, j = i % 3, i // 3
        if mixer == 0:
            return dict(ln_g=a_ln_g_f[j:j + 1], ln_b=a_ln_b_f[j:j + 1], ws=W['a_w_s'][j].astype(BF16),
                        bfull=jnp.repeat(W['a_b_s'][j].T, CHUNK, axis=1))
        if mixer == 2:
            return dict(wdw=_pad_rows(c_w_dw_f[j], 32), bdw=c_b_dw_f[j:j + 1], ln_g=c_ln_g_f[j:j + 1],
                        ln_b=c_ln_b_f[j:j + 1])
        return {}

    def fwd(fn, host, *args):
        res, touts = fn(host, *args, tasks=gather_tasks(host))
        gathered(host, touts)
        return res

    def bwd(fn, host, *args, extra=(), **kw):
        tasks = scatter_tasks(host)
        res, touts = fn(host, *args, tasks=tasks + list(extra), **kw)
        scattered(host, touts[:len(tasks)])
        return (res, touts[len(tasks):]) if extra else res

    h = jnp.concatenate([x[0], ctx[0]], axis=0)
    saved = []
    for i in range(L):
        mixer, j = i % 3, i // 3
        n_in, n_out = MIXER_IN[mixer], MIXER_OUT[mixer]
        if i == L - 1:
            h = h[:seq]
        Ti = h.shape[0]
        tm = _tile(Ti, 768)
        tmh = _tile(Ti, 384)
        mod = mod_all[i]
        mp = mixer_params(i)
        s = dict(h0=h, mp=mp)
        hm = nm_fwd(f"nm1_{i}", h, W['g_mix'][i:i + 1], mod, 0, 1, seq)
        s['hm'] = hm
        u = fwd(mm_cols, f"in_{i}", hm, Wg[(n_in, j)], BF16, tm)
        if mixer == 0:
            p = gmlp_fwd(f"a_mid_{i}", u, mp['ln_g'], mp['ln_b'], mp['ws'], mp['bfull'])
        elif mixer == 1:
            p = jnp.concatenate([fwd(na_fwd, f"b_na_{i}", u, bias, seq), ctx_attn_fwd(f"b_ctx_{i}", u, seq)], axis=0)
        else:
            y, yc, p = conv_fwd(f"c_mid_{i}", u, mp['wdw'], mp['bdw'], mp['ln_g'], mp['ln_b'], seq)
            s.update(y=y, yc=yc)
        s.update(u=u, p=p)
        m1, h = fwd(mm_rows_residual, f"out_{i}", p, Wg[(n_out, j)], h, mod, 2, seq, tm)
        s.update(m1=m1, h1=h)
        hf = nm_fwd(f"nm2_{i}", h, W['g_ffn'][i:i + 1], mod, 3, 4, seq)
        a, b, act = fwd(mm_ffn_up, f"ffn_up_{i}", hf, Wg[('ffn_w1', i)], Wg[('ffn_w3', i)], tmh)
        m2, h = fwd(mm_rows_residual, f"ffn_down_{i}", act, Wg[('ffn_w2', i)], h, mod, 5, seq, tm)
        s.update(hf=hf, a=a, b=b, act=act, m2=m2)
        saved.append(s)

    dh, st_loss = loss_head("loss_head", h, loss_target[0], W['g_final'].reshape(1, D))

    dmod_lat, dmod_ctx, dmod_tot = [None] * L, [None] * L, [None] * L
    dg_mix, dg_ffn = [None] * L, [None] * L
    small = {}
    gate2_done = None

    def set_dmod(i, st_n1, st_g1, st_n2, st_g2):
        for dst, r_n, r_g in ((dmod_lat, (1, 2), 0), (dmod_ctx, (3, 4), 1), (dmod_tot, (5, 6), 2)):
            dst[i] = jnp.concatenate([st_n1[r_n[0]:r_n[0] + 1], st_n1[r_n[1]:r_n[1] + 1], st_g1[r_g:r_g + 1],
                                      st_n2[r_n[0]:r_n[0] + 1], st_n2[r_n[1]:r_n[1] + 1], st_g2[r_g:r_g + 1]], axis=0)

    packs = {}

    def small_pack(part):
        layers = [0] if part == 'layer0' else list(range(1, L))
        a_parts = [small[('a', j)] for j in range(n_a) if (3 * j in layers)]
        cat = lambda xs: jnp.concatenate(xs, axis=0)
        entries = [('dmod_lat', cat([dmod_lat[i] for i in layers])), ('dmod_ctx', cat([dmod_ctx[i] for i in layers])),
                   ('ada_b', cat([dmod_tot[i] for i in layers])),
                   ('g_mix', cat([dg_mix[i] for i in layers])), ('g_ffn', cat([dg_ffn[i] for i in layers]))]
        if a_parts:
            entries += [('a_ln_g', cat([p[0] for p in a_parts])), ('a_ln_b', cat([p[1] for p in a_parts])),
                        ('a_w_s', cat([p[2] for p in a_parts])), ('a_b_s', cat([p[3] for p in a_parts]))]
        if part == 'rest':
            c_parts = [small[('c', j)] for j in range(n_c)]
            entries += [('b_rpb', small[('b', 0)]),
                        ('c_w_dw', cat([p[0] for p in c_parts])), ('c_b_dw', cat([p[1] for p in c_parts])),
                        ('c_ln_g', cat([p[2] for p in c_parts])), ('c_ln_b', cat([p[3] for p in c_parts])),
                        ('g_final', st_loss[0:1]), ('loss', st_loss[1:2])]
        offsets, o = {}, 0
        for n, arr in entries:
            offsets[n] = (o, arr.shape[0])
            o += arr.shape[0]
        return _pad_rows(cat([arr for _, arr in entries]), 64), offsets

    def bwd_gather(fn, host, part, *args):
        pack, offsets = small_pack(part)
        res, touts = bwd(fn, host, *args, extra=[GatherAllTask(pack)])
        packs[part] = (touts[0][0], offsets)
        return res

    for i in reversed(range(L)):
        mixer, j = i % 3, i // 3
        n_in, n_out = MIXER_IN[mixer], MIXER_OUT[mixer]
        s = saved[i]
        mp = s['mp']
        mod = mod_all[i]
        if i == L - 2:
            dh = jnp.concatenate([dh, jnp.zeros((ctx_rows, D), F32)], axis=0)
        Ti = dh.shape[0]
        tm = _tile(Ti, 768)
        tmh = _tile(Ti, 384)
        dm2, st_g2 = gate_bwd(f"gate2_bwd_{i}", dh, s['m2'], mod, 5, seq) if gate2_done is None else gate2_done
        da, db = bwd(mm_rows_dgrad, f"ffn_down_dx_{i}", dm2, Wg[('ffn_w2', i)], BF16, tm, ffn_ab=(s['a'], s['b']))
        dW[('ffn_w2', i)] = bwd(mm_wgrad_rows, f"ffn_w2_dw_{i}", s['act'], dm2, tm)
        dW[('ffn_w1', i)] = bwd(mm_wgrad_cols, f"ffn_w1_dw_{i}", s['hf'], da, tm)
        dW[('ffn_w3', i)] = bwd(mm_wgrad_cols, f"ffn_w3_dw_{i}", s['hf'], db, tm)
        ffn_dx_args = (f"ffn_up_dx_{i}", [da, db], [Wg[('ffn_w1', i)], Wg[('ffn_w3', i)]], tmh)
        dhf = bwd_gather(mm_cols_dgrad, ffn_dx_args[0], 'rest', *ffn_dx_args[1:]) if i == 0 else \
            bwd(mm_cols_dgrad, *ffn_dx_args)
        dh, st_n2, dm1, st_g1 = nm_bwd(f"nm2_bwd_{i}", s['h1'], dhf, dh, W['g_ffn'][i:i + 1], mod, 4, seq,
                                       (s['m1'], mod, 2))
        dp = bwd(mm_rows_dgrad, f"out_dx_{i}", dm1, Wg[(n_out, j)], F32 if mixer == 2 else BF16, tm)[0]
        dW[(n_out, j)] = bwd(mm_wgrad_rows, f"out_dw_{i}", s['p'], dm1, tm)
        if mixer == 0:
            du, dws, dbs, st_a = gmlp_bwd(f"a_mid_bwd_{i}", s['u'], dp, mp['ln_g'], mp['ln_b'], mp['ws'], mp['bfull'])
            small[('a', j)] = (st_a[0:1], st_a[1:2], dws.reshape(-1, D), dbs.T.reshape(1, D))
        elif mixer == 1:
            dq, dk, dv, dkc, dvc, dbias = na_bwd(f"b_na_bwd_{i}", s['u'], bias, dp, seq)
            dqc, dkc, dvc = ctx_attn_bwd(f"b_ctx_bwd_{i}", s['u'], dp, dkc, dvc, seq)
            du = jnp.concatenate([jnp.concatenate([dq, dk.astype(BF16), dv.astype(BF16)], axis=1),
                                  jnp.concatenate([dqc, dkc, dvc], axis=1)], axis=0)
            blocks = dbias.reshape(3, H, Q_ROWS, GRID_W, K_ROWS, GRID_W).transpose(1, 0, 2, 4, 3, 5)
            folded = rpb_fold(f"rpb_fold_{i}", blocks, na_classes)
            drpb = rpb_reduce(f"rpb_reduce_{i}", folded.reshape(H * n_dr, GRID_W * GRID_W), onehot.T)
            small[('b', j)] = _rows_of(drpb[:, :2 * NA_COLS - 1].reshape(-1), D)
        else:
            dyc, st_c = conv_bwd_norm(f"c_norm_bwd_{i}", dp, s['yc'], mp['ln_g'], mp['ln_b'])
            du, dwdw = conv_bwd_taps(f"c_taps_bwd_{i}", dyc, s['y'], s['u'], mp['wdw'], seq)
            small[('c', j)] = (dwdw, st_c[2:3], st_c[0:1], st_c[1:2])
        if i == 0:
            dg_mix[0], dg_ffn[0] = jnp.zeros((1, D), F32), st_n2[0:1]
            set_dmod(0, jnp.zeros((8, D), F32), st_g1, st_n2, st_g2)
            dW[(n_in, j)] = bwd_gather(mm_wgrad_cols, f"in_dw_{i}", 'layer0', s['hm'], du, tm)
        else:
            dW[(n_in, j)] = bwd(mm_wgrad_cols, f"in_dw_{i}", s['hm'], du, tm)
        dhm = bwd(mm_cols_dgrad, f"in_dx_{i}", [du], [Wg[(n_in, j)]], tm)
        below = 1 <= i <= L - 2
        gate_below = (saved[i - 1]['m2'], mod_all[i - 1], 5) if below else (s['m1'], mod, 2)
        dh, st_n1, dm_below, st_below = nm_bwd(f"nm1_bwd_{i}", s['h0'], dhm, dh, W['g_mix'][i:i + 1], mod, 1, seq,
                                               gate_below)
        gate2_done = (dm_below, st_below) if below else None
        if i > 0:
            dg_mix[i], dg_ffn[i] = st_n1[0:1], st_n2[0:1]
            set_dmod(i, st_n1, st_g1, st_n2, st_g2)
    grad_x = dh[:seq].reshape(1, seq, D)

    late = _pad_rows(jnp.concatenate([st_n1[1:7], st_n1[0:1]], axis=0), 8)
    touts = comm_only("ag_small_late", [GatherAllTask(late)] + scatter_tasks("rs_post"))
    gathered_late = touts[0][0]
    scattered("rs_post", touts[1:])
    sums = {part: reduce_8("reduce_small_" + part, packs[part][0]) for part in ('layer0', 'rest')}
    sums_late = reduce_8("reduce_small_late", gathered_late)
    late_rows = {'dmod_ctx': sums_late[2:4], 'ada_b': sums_late[4:6], 'g_mix': sums_late[6:7]}
    out = {}

    def rows_of(n, arrays, axis):
        found = []
        for part in ('layer0', 'rest'):
            if n in packs[part][1]:
                lo, cnt = packs[part][1][n]
                found.append(lax.slice_in_dim(arrays[part], lo, lo + cnt, axis=axis))
        return jnp.concatenate(found, axis=axis)

    def summed(n):
        rows = rows_of(n, sums, 0)
        if n in late_rows:
            rows = jnp.concatenate([late_rows[n], rows[late_rows[n].shape[0]:]], axis=0)
        return rows

    loss = (0.5 / D) * jnp.sum(summed('loss'))

    dm_lat = rows_of('dmod_lat', {part: packs[part][0] for part in packs}, 1)
    dm_lat = jnp.concatenate([gathered_late[:, 0:2], dm_lat[:, 2:]], axis=1)
    dm_lat = dm_lat.reshape(8, L, 6 * D).transpose(1, 0, 2)
    dm_ctx = summed('dmod_ctx').reshape(L, 1, 6 * D)
    dm16 = jnp.concatenate([dm_lat, dm_ctx, jnp.zeros((L, 7, 6 * D), F32)], axis=1)
    dm16 = lax.dynamic_slice_in_dim(dm16, s_idx * Cs, Cs, axis=2)
    dmc8 = jnp.concatenate([dm16[:, 8:9], jnp.zeros((L, 7, Cs), F32)], axis=1)
    g_ada, d_ada, m_ada, v_ada, ds_part = ada_bwd_adam("ada_bwd_adam", craw16.T, dm16, dmc8, ada_w,
                                                       Mo['ada_w'], Vo['ada_w'])
    ds_all = all_gather_8("ag_ds_ctx", ds_part)
    cc = cctx_adam("cctx_adam", ds_all, W['c_ctx'].reshape(1, D), Mo['c_ctx'].reshape(1, D),
                   Vo['c_ctx'].reshape(1, D))
    out.update({'c_ctx': tuple(t.reshape(D) for t in cc), 'ada_w': (g_ada, d_ada, m_ada, v_ada)})

    for n in BIG_NAMES:
        out[n], _ = adam_parts("adam_" + n, land[n], land2[n], W[n], Mo[n], Vo[n])

    def own_cols(full):
        w = full.shape[-1] // N_SHARD
        return lax.dynamic_slice_in_dim(full, s_idx * w, w, axis=full.ndim - 1)

    small_g = {
        'ada_b': summed('ada_b').reshape(L, 6 * D), 'g_mix': summed('g_mix'), 'g_ffn': summed('g_ffn'),
        'a_ln_g': own_cols(summed('a_ln_g')), 'a_ln_b': own_cols(summed('a_ln_b')),
        'a_w_s': summed('a_w_s').reshape(n_a, G, CHUNK, CHUNK), 'a_b_s': summed('a_b_s').reshape(n_a, G, CHUNK),
        'b_rpb': summed('b_rpb').reshape(-1)[:H * n_dr * (2 * NA_COLS - 1)].reshape(W['b_rpb'].shape),
        'c_w_dw': own_cols(summed('c_w_dw').reshape(n_c, 32, D)[:, :CONV_W]),
        'c_b_dw': own_cols(summed('c_b_dw')), 'c_ln_g': own_cols(summed('c_ln_g')),
        'c_ln_b': own_cols(summed('c_ln_b')), 'g_final': summed('g_final').reshape(D),
    }

    def packed(d):
        flat = jnp.concatenate([d[n].reshape(-1) for n in SMALL_NAMES])
        return _pad_rows(_rows_of(flat, SMALL_PACK_COLS), 8)

    res = adam_flat("adam_small", packed(small_g), packed(W), packed(Mo), packed(Vo))
    o = 0
    for n in SMALL_NAMES:
        size, shape = W[n].size, W[n].shape
        out[n] = (small_g[n],) + tuple(r.reshape(-1)[o:o + size].reshape(shape) for r in res)
        o += size

    return (loss, grad_x) + tuple(out[n][k] for k in range(4) for n in WEIGHT_NAMES)


def kernel(x, c, ctx, c_ctx, ada_w, ada_b, g_mix, g_ffn, ffn_w1, ffn_w3, ffn_w2, a_w_in, a_ln_g, a_ln_b, a_w_s, a_b_s, a_w_out, b_w_qkv, b_rpb, b_w_out, c_w_pw1, c_w_dw, c_b_dw, c_ln_g, c_ln_b, c_w_pw2, g_final, loss_target, m_c_ctx, m_ada_w, m_ada_b, m_g_mix, m_g_ffn, m_ffn_w1, m_ffn_w3, m_ffn_w2, m_a_w_in, m_a_ln_g, m_a_ln_b, m_a_w_s, m_a_b_s, m_a_w_out, m_b_w_qkv, m_b_rpb, m_b_w_out, m_c_w_pw1, m_c_w_dw, m_c_b_dw, m_c_ln_g, m_c_ln_b, m_c_w_pw2, m_g_final, v_c_ctx, v_ada_w, v_ada_b, v_g_mix, v_g_ffn, v_ffn_w1, v_ffn_w3, v_ffn_w2, v_a_w_in, v_a_ln_g, v_a_ln_b, v_a_w_s, v_a_b_s, v_a_w_out, v_b_w_qkv, v_b_rpb, v_b_w_out, v_c_w_pw1, v_c_w_dw, v_c_b_dw, v_c_ln_g, v_c_ln_b, v_c_w_pw2, v_g_final):
    W = dict(zip(WEIGHT_NAMES, (c_ctx, ada_w, ada_b, g_mix, g_ffn, ffn_w1, ffn_w3, ffn_w2, a_w_in, a_ln_g, a_ln_b, a_w_s, a_b_s, a_w_out, b_w_qkv, b_rpb, b_w_out, c_w_pw1, c_w_dw, c_b_dw, c_ln_g, c_ln_b, c_w_pw2, g_final)))
    Mo = dict(zip(WEIGHT_NAMES, (m_c_ctx, m_ada_w, m_ada_b, m_g_mix, m_g_ffn, m_ffn_w1, m_ffn_w3, m_ffn_w2, m_a_w_in, m_a_ln_g, m_a_ln_b, m_a_w_s, m_a_b_s, m_a_w_out, m_b_w_qkv, m_b_rpb, m_b_w_out, m_c_w_pw1, m_c_w_dw, m_c_b_dw, m_c_ln_g, m_c_ln_b, m_c_w_pw2, m_g_final)))
    Vo = dict(zip(WEIGHT_NAMES, (v_c_ctx, v_ada_w, v_ada_b, v_g_mix, v_g_ffn, v_ffn_w1, v_ffn_w3, v_ffn_w2, v_a_w_in, v_a_ln_g, v_a_ln_b, v_a_w_s, v_a_b_s, v_a_w_out, v_b_w_qkv, v_b_rpb, v_b_w_out, v_c_w_pw1, v_c_w_dw, v_c_b_dw, v_c_ln_g, v_c_ln_b, v_c_w_pw2, v_g_final)))
    return _step(W, Mo, Vo, x, c, ctx, loss_target)
```

```python
import functools
import math

import jax
import jax.numpy as jnp
from jax import lax
from jax.experimental import pallas as pl
from jax.experimental.pallas import tpu as pltpu

F32 = jnp.float32
BF16 = jnp.bfloat16
MESH = pl.DeviceIdType.MESH
ANY = pl.BlockSpec(memory_space=pl.ANY)

GRID_W = 64
CHUNK = 128
HEAD = 128
NA_ROWS = 8
NA_COLS = 16
CONV_W = 31
HALO = 16
EPS = 1e-6
NEG_INF = -1e30
N_LAYERS = 4
N_SHARD = 4
V7X_VMEM_BYTES = 64 * 1024 * 1024
VMEM_LIMIT = V7X_VMEM_BYTES - 6 * 1024 * 1024

ADAM_LR = 0.001
ADAM_B1 = 0.9
ADAM_B2 = 0.999
ADAM_EPS = 1e-08
ADAM_WD = 0.01
ADAM_STEP = 10

NN = (((1,), (0,)), ((), ()))
NT = (((1,), (1,)), ((), ()))
TN = (((0,), (0,)), ((), ()))


def _params(n_grid=0):
    sem = ("arbitrary",) * n_grid if n_grid else None
    return pltpu.CompilerParams(dimension_semantics=sem, vmem_limit_bytes=VMEM_LIMIT)


def _xyc():
    return lax.axis_index("x"), lax.axis_index("y"), lax.axis_index("c")


def _flip(v, f):
    return 1 - v if f else v


def _tile(n, pref, mult=128):
    if n <= pref:
        return n
    t = (pref // mult) * mult
    while t > mult and n % t:
        t -= mult
    assert n % t == 0, (n, pref, mult)
    return t


def _sigmoid(x):
    return 1.0 / (1.0 + jnp.exp(-x))


XY_FLIPS = ((1, 0), (0, 1), (1, 1))
ALL_FLIPS = tuple((fx, fy, fc) for fx in (0, 1) for fy in (0, 1) for fc in (0, 1) if fx or fy or fc)


def _remote(src, dst, ssem, rsem, dev):
    return pltpu.make_async_remote_copy(src_ref=src, dst_ref=dst, send_sem=ssem, recv_sem=rsem,
                                        device_id=dev, device_id_type=MESH)


def all_gather_xy(name, shard):
    def body(src, dst, ssem, rsem, lsem):
        x, y, c = _xyc()
        mine = pltpu.make_async_copy(src, dst.at[2 * x + y], lsem)
        mine.start()
        sends = []
        for k, (fx, fy) in enumerate(XY_FLIPS):
            cp = _remote(src, dst.at[2 * x + y], ssem.at[k], rsem.at[k], (_flip(x, fx), _flip(y, fy), c))
            cp.start()
            sends.append(cp)
        for k, (fx, fy) in enumerate(XY_FLIPS):
            px, py = _flip(x, fx), _flip(y, fy)
            _remote(src, dst.at[2 * px + py], ssem.at[k], rsem.at[k], (px, py, c)).wait_recv()
        for cp in sends:
            cp.wait_send()
        mine.wait()

    return pl.pallas_call(
        body, name=name, out_shape=jax.ShapeDtypeStruct((N_SHARD,) + shard.shape, shard.dtype),
        in_specs=[ANY], out_specs=ANY,
        scratch_shapes=[pltpu.SemaphoreType.DMA((3,)), pltpu.SemaphoreType.DMA((3,)), pltpu.SemaphoreType.DMA(())],
    )(shard)


def all_gather_8(name, blk):
    def body(src, dst, ssem, rsem, lsem):
        x, y, c = _xyc()
        me = 4 * x + 2 * y + c
        mine = pltpu.make_async_copy(src, dst.at[me], lsem)
        mine.start()
        sends = []
        for k, (fx, fy, fc) in enumerate(ALL_FLIPS):
            cp = _remote(src, dst.at[me], ssem.at[k], rsem.at[k], (_flip(x, fx), _flip(y, fy), _flip(c, fc)))
            cp.start()
            sends.append(cp)
        for k, (fx, fy, fc) in enumerate(ALL_FLIPS):
            px, py, pc = _flip(x, fx), _flip(y, fy), _flip(c, fc)
            _remote(src, dst.at[4 * px + 2 * py + pc], ssem.at[k], rsem.at[k], (px, py, pc)).wait_recv()
        for cp in sends:
            cp.wait_send()
        mine.wait()

    return pl.pallas_call(
        body, name=name, out_shape=jax.ShapeDtypeStruct((8,) + blk.shape, blk.dtype),
        in_specs=[ANY], out_specs=ANY,
        scratch_shapes=[pltpu.SemaphoreType.DMA((7,)), pltpu.SemaphoreType.DMA((7,)), pltpu.SemaphoreType.DMA(())],
    )(blk)


class GatherTask:
    n_send, n_recv, n_local = 3, 3, 1
    alias = {}

    def __init__(self, shard):
        self.ins = [shard]
        self.outs = [jax.ShapeDtypeStruct((N_SHARD,) + shard.shape, shard.dtype)]

    def _copies(self, xyc, ins, outs, ssem, rsem):
        x, y, c = xyc
        for k, (fx, fy) in enumerate(XY_FLIPS):
            px, py = _flip(x, fx), _flip(y, fy)
            send = _remote(ins[0], outs[0].at[2 * x + y], ssem.at[k], rsem.at[k], (px, py, c))
            recv = _remote(ins[0], outs[0].at[2 * px + py], ssem.at[k], rsem.at[k], (px, py, c))
            yield send, recv

    def start(self, xyc, ins, outs, ssem, rsem, lsem):
        x, y, _ = xyc
        pltpu.make_async_copy(ins[0], outs[0].at[2 * x + y], lsem.at[0]).start()
        for send, _ in self._copies(xyc, ins, outs, ssem, rsem):
            send.start()

    def finish(self, xyc, ins, outs, ssem, rsem, lsem):
        x, y, _ = xyc
        for send, recv in self._copies(xyc, ins, outs, ssem, rsem):
            recv.wait_recv()
            send.wait_send()
        pltpu.make_async_copy(ins[0], outs[0].at[2 * x + y], lsem.at[0]).wait()


class GatherAllTask:
    n_send, n_recv, n_local = 7, 7, 1
    alias = {}

    def __init__(self, blk):
        self.ins = [blk]
        self.outs = [jax.ShapeDtypeStruct((8,) + blk.shape, blk.dtype)]

    def _copies(self, xyc, ins, outs, ssem, rsem):
        x, y, c = xyc
        for k, (fx, fy, fc) in enumerate(ALL_FLIPS):
            px, py, pc = _flip(x, fx), _flip(y, fy), _flip(c, fc)
            send = _remote(ins[0], outs[0].at[4 * x + 2 * y + c], ssem.at[k], rsem.at[k], (px, py, pc))
            recv = _remote(ins[0], outs[0].at[4 * px + 2 * py + pc], ssem.at[k], rsem.at[k], (px, py, pc))
            yield send, recv

    def _local(self, xyc, ins, outs, lsem):
        x, y, c = xyc
        return pltpu.make_async_copy(ins[0], outs[0].at[4 * x + 2 * y + c], lsem.at[0])

    def start(self, xyc, ins, outs, ssem, rsem, lsem):
        self._local(xyc, ins, outs, lsem).start()
        for send, _ in self._copies(xyc, ins, outs, ssem, rsem):
            send.start()

    def finish(self, xyc, ins, outs, ssem, rsem, lsem):
        for send, recv in self._copies(xyc, ins, outs, ssem, rsem):
            recv.wait_recv()
            send.wait_send()
        self._local(xyc, ins, outs, lsem).wait()


class ScatterTask:
    n_send, n_recv, n_local = 4, 4, 1
    alias = {1: 0, 2: 1}

    def __init__(self, part, land, land2, l):
        self.ins = [part, land, land2]
        self.outs = [jax.ShapeDtypeStruct(land.shape, land.dtype), jax.ShapeDtypeStruct(land2.shape, land2.dtype)]
        self.l = l

    def _copies(self, xyc, ins, outs, ssem, rsem):
        x, y, c = xyc
        me_s = 2 * x + y
        part, land, land2 = ins[0], outs[0], outs[1]
        sib = (x, y, 1 - c)
        own = _remote(part.at[me_s], land2.at[me_s, self.l], ssem.at[3], rsem.at[3], sib)
        yield own, own
        for k, (fx, fy) in enumerate(XY_FLIPS):
            px, py = _flip(x, fx), _flip(y, fy)
            ps = 2 * px + py
            send = _remote(part.at[ps], land.at[me_s, self.l], ssem.at[k], rsem.at[k], (px, py, c))
            recv = _remote(part.at[ps], land.at[ps, self.l], ssem.at[k], rsem.at[k], (px, py, c))
            yield send, recv

    def _local(self, xyc, ins, outs, lsem):
        me_s = 2 * xyc[0] + xyc[1]
        return pltpu.make_async_copy(ins[0].at[me_s], outs[0].at[me_s, self.l], lsem.at[0])

    def start(self, xyc, ins, outs, ssem, rsem, lsem):
        self._local(xyc, ins, outs, lsem).start()
        for send, _ in self._copies(xyc, ins, outs, ssem, rsem):
            send.start()

    def finish(self, xyc, ins, outs, ssem, rsem, lsem):
        for send, recv in self._copies(xyc, ins, outs, ssem, rsem):
            recv.wait_recv()
            send.wait_send()
        self._local(xyc, ins, outs, lsem).wait()


class ForwardTask:
    n_send, n_recv, n_local = 3, 3, 0
    alias = {0: 0, 1: 1}

    def __init__(self, land, land2, l):
        self.ins = [land, land2]
        self.outs = [jax.ShapeDtypeStruct(land.shape, land.dtype), jax.ShapeDtypeStruct(land2.shape, land2.dtype)]
        self.l = l

    def _copies(self, xyc, outs, ssem, rsem):
        x, y, c = xyc
        for k, (fx, fy) in enumerate(XY_FLIPS):
            ps = 2 * _flip(x, fx) + _flip(y, fy)
            yield _remote(outs[0].at[ps, self.l], outs[1].at[ps, self.l], ssem.at[k], rsem.at[k], (x, y, 1 - c))

    def start(self, xyc, ins, outs, ssem, rsem, lsem):
        for cp in self._copies(xyc, outs, ssem, rsem):
            cp.start()

    def finish(self, xyc, ins, outs, ssem, rsem, lsem):
        for cp in self._copies(xyc, outs, ssem, rsem):
            cp.wait_recv()
            cp.wait_send()


def host_call(name, body, grid, arrays, in_specs, out_shape, out_specs, scratch=(), tasks=()):
    out_shape, out_specs, scratch = tuple(out_shape), tuple(out_specs), list(scratch)
    n_in, n_out, n_scr, n_grid = len(arrays), len(out_shape), len(scratch), len(grid)
    t_arrays, t_outs, aliases, spans, sems = [], [], {}, [], []
    for t in tasks:
        i0, o0 = len(t_arrays), len(t_outs)
        t_arrays += t.ins
        t_outs += t.outs
        for a, b in t.alias.items():
            aliases[n_in + i0 + a] = n_out + o0 + b
        spans.append((i0, len(t_arrays), o0, len(t_outs)))
        sems += [pltpu.SemaphoreType.DMA((t.n_send,)), pltpu.SemaphoreType.DMA((t.n_recv,)),
                 pltpu.SemaphoreType.DMA((max(t.n_local, 1),))]
    n_tin, n_tout = len(t_arrays), len(t_outs)

    def full_body(*refs):
        ins = refs[:n_in]
        tin = refs[n_in:n_in + n_tin]
        outs = refs[n_in + n_tin:n_in + n_tin + n_out]
        tout = refs[n_in + n_tin + n_out:n_in + n_tin + n_out + n_tout]
        rest = refs[n_in + n_tin + n_out + n_tout:]
        scr, sm = rest[:n_scr], rest[n_scr:]
        if not tasks:
            body(*ins, *outs, *scr)
            return
        pids = [pl.program_id(d) for d in range(n_grid)]
        first = functools.reduce(jnp.logical_and, [p == 0 for p in pids])
        last = functools.reduce(jnp.logical_and, [p == n - 1 for p, n in zip(pids, grid)])
        xyc = _xyc()

        def each(method):
            for k, (t, (i0, i1, o0, o1)) in enumerate(zip(tasks, spans)):
                getattr(t, method)(xyc, tin[i0:i1], tout[o0:o1], sm[3 * k], sm[3 * k + 1], sm[3 * k + 2])

        @pl.when(first)
        def _():
            each("start")

        body(*ins, *outs, *scr)

        @pl.when(last)
        def _():
            each("finish")

    res = pl.pallas_call(
        full_body, name=name, grid=grid, in_specs=list(in_specs) + [ANY] * n_tin,
        out_specs=out_specs + (ANY,) * n_tout, out_shape=out_shape + tuple(t_outs),
        scratch_shapes=scratch + sems, input_output_aliases=aliases, compiler_params=_params(n_grid),
    )(*arrays, *t_arrays)
    return tuple(res[:n_out]), [tuple(res[n_out + o0:n_out + o1]) for (_, _, o0, o1) in spans]


def comm_only(name, tasks):
    def body(i_ref, o_ref):
        o_ref[...] = i_ref[...]

    spec = pl.BlockSpec((8, 128), lambda i: (0, 0))
    _, touts = host_call(name, body, (1,), [jnp.zeros((8, 128), F32)], [spec],
                         [jax.ShapeDtypeStruct((8, 128), F32)], [spec], tasks=tasks)
    return touts


def matmul(name, grid, order, pairs, pair_dims, acc_of_pair, acc_shapes, extras, outs, epilogue, tasks=(),
           col_chunk=1 << 30):
    ni, nj, nk = grid

    def wrap(m):
        if order == "ij":
            return lambda g0, g1, k: m(g0, g1, k)
        return lambda g0, g1, k: m(g1, g0, k)

    g = (ni, nj, nk) if order == "ij" else (nj, ni, nk)
    arrays, in_specs = [], []
    for a, b in pairs:
        for arr, blk, m in (a, b):
            arrays.append(arr)
            in_specs.append(pl.BlockSpec(blk, wrap(m)))
    for arr, blk, m in extras:
        arrays.append(arr)
        in_specs.append(pl.BlockSpec(blk, wrap(m)))
    n_in = len(arrays)
    out_shape = tuple(o[0] for o in outs)
    out_specs = tuple(pl.BlockSpec(o[1], wrap(o[2])) for o in outs)
    n_pairs, n_ex, n_out, n_acc = len(pairs), len(extras), len(outs), len(acc_shapes)
    tile_n = acc_shapes[0][1]
    chunks = [slice(c0, min(c0 + col_chunk, tile_n)) for c0 in range(0, tile_n, col_chunk)]

    def body(*refs):
        ins = refs[:n_in]
        out_refs = refs[n_in:n_in + n_out]
        accs = refs[n_in + n_out:]
        pid = (pl.program_id(0), pl.program_id(1)) if order == "ij" else (pl.program_id(1), pl.program_id(0))
        k = pl.program_id(2)

        def partial(p, cs):
            b_ref = ins[2 * p + 1]
            b = b_ref[cs, :] if pair_dims[p] == NT else b_ref[:, cs]
            return lax.dot_general(ins[2 * p][...], b, pair_dims[p], preferred_element_type=F32)

        ex = ins[2 * n_pairs:2 * n_pairs + n_ex]
        if nk == 1:
            for cs in chunks:
                vals = [None] * n_acc
                for p in range(n_pairs):
                    d = partial(p, cs)
                    q = acc_of_pair[p]
                    vals[q] = d if vals[q] is None else vals[q] + d
                epilogue(vals, ex, out_refs, pid, cs)
        else:
            @pl.when(k == 0)
            def _():
                for acc in accs:
                    acc[...] = jnp.zeros_like(acc)

            for cs in chunks:
                for p in range(n_pairs):
                    accs[acc_of_pair[p]][:, cs] += partial(p, cs)

            @pl.when(k == nk - 1)
            def _():
                for cs in chunks:
                    epilogue([acc[:, cs] for acc in accs], ex, out_refs, pid, cs)

    scratch = [] if nk == 1 else [pltpu.VMEM(s, F32) for s in acc_shapes]
    return host_call(name, body, g, arrays, in_specs, out_shape, out_specs, scratch, tasks)


def _store_cast(vals, extras, out_refs, pid, cs):
    out_refs[0][:, cs] = vals[0].astype(out_refs[0].dtype)


def mm_cols(name, x, wg, out_dtype, tm, tasks=()):
    T, D = x.shape
    ns = wg.shape[2]
    tn = _tile(ns, 1536)
    nb = ns // tn
    outs, touts = matmul(
        name, (T // tm, N_SHARD * nb, 1), "ji",
        [((x, (tm, D), lambda i, jj, k: (i, 0)),
          (wg, (None, D, tn), lambda i, jj, k: (jj // nb, 0, jj % nb)))],
        [NN], [0], [(tm, tn)], [],
        [(jax.ShapeDtypeStruct((T, N_SHARD * ns), out_dtype), (tm, tn), lambda i, jj, k: (i, jj))],
        _store_cast, tasks)
    return outs[0], touts


def mm_ffn_up(name, x, w1g, w3g, tm, tasks=()):
    T, D = x.shape
    ns = w1g.shape[2]
    tn = _tile(ns, 1536)
    nb = ns // tn

    def epi(vals, extras, out_refs, pid, cs):
        a, b = vals
        out_refs[0][:, cs] = a.astype(BF16)
        out_refs[1][:, cs] = b.astype(BF16)
        out_refs[2][:, cs] = (a * _sigmoid(a) * b).astype(BF16)

    wmap = lambda i, jj, k: (jj // nb, 0, jj % nb)
    xa = (x, (tm, D), lambda i, jj, k: (i, 0))
    o = (jax.ShapeDtypeStruct((T, N_SHARD * ns), BF16), (tm, tn), lambda i, jj, k: (i, jj))
    return matmul(name, (T // tm, N_SHARD * nb, 1), "ji",
                  [(xa, (w1g, (None, D, tn), wmap)), (xa, (w3g, (None, D, tn), wmap))],
                  [NN, NN], [0, 1], [(tm, tn), (tm, tn)], [], [o, o, o], epi, tasks)


def mm_rows_residual(name, p, wg, h, mod, gate_row, seq, tm, tasks=()):
    T, kin = p.shape
    ks, D = wg.shape[1], wg.shape[2]
    tn = _tile(D, 512)

    def epi(vals, extras, out_refs, pid, cs):
        m = vals[0]
        h_ref, mod_ref = extras
        rows = pid[0] * tm + lax.broadcasted_iota(jnp.int32, (tm, 1), 0)
        gate = jnp.where(rows >= seq, mod_ref[6 + gate_row:7 + gate_row, cs], mod_ref[gate_row:gate_row + 1, cs])
        out_refs[0][:, cs] = m.astype(BF16)
        out_refs[1][:, cs] = h_ref[:, cs] + gate * m

    pairs = [((p, (tm, ks), lambda i, jj, k, s=s: (i, s)), (wg, (None, ks, tn), lambda i, jj, k, s=s: (s, 0, jj)))
             for s in range(N_SHARD)]
    omap = lambda i, jj, k: (i, jj)
    return matmul(
        name, (T // tm, D // tn, 1), "ji", pairs, [NN] * N_SHARD, [0] * N_SHARD, [(tm, tn)],
        [(h, (tm, tn), omap), (mod, (16, tn), lambda i, jj, k: (0, jj))],
        [(jax.ShapeDtypeStruct((T, D), BF16), (tm, tn), omap), (jax.ShapeDtypeStruct((T, D), F32), (tm, tn), omap)],
        epi, tasks)


def mm_rows_dgrad(name, dm, wg, out_dtype, tm, ffn_ab=None, tasks=()):
    T, D = dm.shape
    ks = wg.shape[1]
    tn = _tile(ks, 1536)
    nb = ks // tn
    omap = lambda i, jj, k: (i, jj)
    o = (jax.ShapeDtypeStruct((T, N_SHARD * ks), out_dtype), (tm, tn), omap)
    pairs = [((dm, (tm, D), lambda i, jj, k: (i, 0)),
              (wg, (None, tn, D), lambda i, jj, k: (jj // nb, jj % nb, 0)))]
    if ffn_ab is None:
        return matmul(name, (T // tm, N_SHARD * nb, 1), "ji", pairs, [NT], [0], [(tm, tn)], [], [o],
                      _store_cast, tasks)

    def epi(vals, extras, out_refs, pid, cs):
        dact = vals[0]
        a = extras[0][:, cs].astype(F32)
        b = extras[1][:, cs].astype(F32)
        sig = _sigmoid(a)
        out_refs[0][:, cs] = (dact * b * (sig * (1.0 + a * (1.0 - sig)))).astype(BF16)
        out_refs[1][:, cs] = (dact * (a * sig)).astype(BF16)

    a, b = ffn_ab
    return matmul(name, (T // tm, N_SHARD * nb, 1), "ji", pairs, [NT], [0], [(tm, tn)],
                  [(a, (tm, tn), omap), (b, (tm, tn), omap)], [o, o], epi, tasks)


def mm_cols_dgrad(name, dys, wgs, tm, tasks=()):
    T = dys[0].shape[0]
    D, ns = wgs[0].shape[1], wgs[0].shape[2]
    tk = _tile(ns, 1536)
    kb = ns // tk
    pairs = [((dy, (tm, tk), lambda i, jj, k: (i, k)),
              (wg, (None, D, tk), lambda i, jj, k: (k // kb, 0, k % kb))) for dy, wg in zip(dys, wgs)]
    outs, touts = matmul(name, (T // tm, 1, N_SHARD * kb), "ij", pairs, [NT] * len(dys), [0] * len(dys),
                         [(tm, D)], [],
                         [(jax.ShapeDtypeStruct((T, D), F32), (tm, D), lambda i, jj, k: (i, 0))], _store_cast, tasks)
    return outs[0], touts


def mm_wgrad_rows(name, p, dm, tk, tasks=()):
    T, kin = p.shape
    D = dm.shape[1]
    ks = kin // N_SHARD
    tm = _tile(ks, 1536)
    mb = ks // tm
    tn = _tile(D, 1024)
    outs, touts = matmul(
        name, (N_SHARD * mb, D // tn, T // tk), "ij",
        [((p, (tk, tm), lambda i, jj, k: (k, i)), (dm, (tk, tn), lambda i, jj, k: (k, jj)))],
        [TN], [0], [(tm, tn)], [],
        [(jax.ShapeDtypeStruct((N_SHARD, ks, D), BF16), (None, tm, tn), lambda i, jj, k: (i // mb, i % mb, jj))],
        _store_cast, tasks)
    return outs[0], touts


def mm_wgrad_cols(name, x, dy, tk, tasks=()):
    T, D = x.shape
    ns = dy.shape[1] // N_SHARD
    tm = _tile(D, 1024)
    tn = _tile(ns, 1536)
    nb = ns // tn
    outs, touts = matmul(
        name, (D // tm, N_SHARD * nb, T // tk), "ij",
        [((x, (tk, tm), lambda i, jj, k: (k, i)), (dy, (tk, tn), lambda i, jj, k: (k, jj)))],
        [TN], [0], [(tm, tn)], [],
        [(jax.ShapeDtypeStruct((N_SHARD, D, ns), BF16), (None, tm, tn), lambda i, jj, k: (jj // nb, i, jj % nb))],
        _store_cast, tasks)
    return outs[0], touts


def _row_tile(T, seq):
    if T == seq:
        return _tile(T, 256, 8)
    return math.gcd(256, math.gcd(seq, T - seq))


def _mod_row(mod_ref, row, is_ctx):
    return jnp.where(is_ctx, mod_ref[6 + row:7 + row, :], mod_ref[row:row + 1, :])


def _rowspec(tr, D):
    return pl.BlockSpec((tr, D), lambda i: (i, 0))


def _fullspec(shape):
    nd = len(shape)
    return pl.BlockSpec(shape, lambda i: (0,) * nd)


def _colsum(v):
    return jnp.sum(v, axis=0, keepdims=True)


def _acc_rows(st_ref, first, rows):
    @pl.when(first)
    def _():
        st_ref[...] = jnp.zeros_like(st_ref)
    for r, val in rows:
        st_ref[r:r + 1, :] += val


def _split_stats(is_ctx, val):
    zero = jnp.zeros_like(val)
    return jnp.where(is_ctx, zero, val), jnp.where(is_ctx, val, zero)


def nm_fwd(name, h, g, mod, r_sh, r_sc, seq):
    T, D = h.shape
    tr = _row_tile(T, seq)
    nlat = seq // tr

    def body(h_ref, g_ref, mod_ref, o_ref):
        is_ctx = pl.program_id(0) >= nlat
        x = h_ref[...]
        r = lax.rsqrt(jnp.mean(x * x, axis=-1, keepdims=True) + EPS)
        y = x * r * g_ref[...]
        o_ref[...] = (y * (1.0 + _mod_row(mod_ref, r_sc, is_ctx)) + _mod_row(mod_ref, r_sh, is_ctx)).astype(BF16)

    return pl.pallas_call(
        body, name=name, grid=(T // tr,), in_specs=[_rowspec(tr, D), _fullspec((1, D)), _fullspec((16, D))],
        out_specs=_rowspec(tr, D), out_shape=jax.ShapeDtypeStruct((T, D), BF16), compiler_params=_params(1),
    )(h, g, mod)


def nm_bwd(name, h, dhm, dres, g, mod, r_sc, seq, gate):
    T, D = h.shape
    tr = _row_tile(T, seq)
    nlat = seq // tr
    m_arr, mod_gate, r_gt = gate

    def body(h_ref, d_ref, r_ref, g_ref, mod_ref, m_ref, modg_ref, o_ref, st_ref, dm_ref, gst_ref):
        i = pl.program_id(0)
        is_ctx = i >= nlat
        x = h_ref[...]
        r = lax.rsqrt(jnp.mean(x * x, axis=-1, keepdims=True) + EPS)
        n = x * r
        gg = g_ref[...]
        dout = d_ref[...]
        dsh = _colsum(dout)
        dsc = _colsum(dout * (n * gg))
        dy = dout * (1.0 + _mod_row(mod_ref, r_sc, is_ctx))
        dn = dy * gg
        dh_out = r_ref[...] + r * (dn - n * jnp.mean(dn * n, axis=-1, keepdims=True))
        o_ref[...] = dh_out
        dsh_l, dsh_c = _split_stats(is_ctx, dsh)
        dsc_l, dsc_c = _split_stats(is_ctx, dsc)
        _acc_rows(st_ref, i == 0, [(0, _colsum(dy * n)), (1, dsh_l), (2, dsc_l), (3, dsh_c), (4, dsc_c),
                                   (5, dsh), (6, dsc)])
        dm_ref[...] = (dh_out * _mod_row(modg_ref, r_gt, is_ctx)).astype(BF16)
        dgt = _colsum(dh_out * m_ref[...].astype(F32))
        dgt_l, dgt_c = _split_stats(is_ctx, dgt)
        _acc_rows(gst_ref, i == 0, [(0, dgt_l), (1, dgt_c), (2, dgt)])

    return pl.pallas_call(
        body, name=name, grid=(T // tr,),
        in_specs=[_rowspec(tr, D), _rowspec(tr, D), _rowspec(tr, D), _fullspec((1, D)), _fullspec((16, D)),
                  _rowspec(tr, D), _fullspec((16, D))],
        out_specs=(_rowspec(tr, D), _fullspec((8, D)), _rowspec(tr, D), _fullspec((8, D))),
        out_shape=(jax.ShapeDtypeStruct((T, D), F32), jax.ShapeDtypeStruct((8, D), F32),
                   jax.ShapeDtypeStruct((T, D), BF16), jax.ShapeDtypeStruct((8, D), F32)),
        compiler_params=_params(1),
    )(h, dhm, dres, g, mod, m_arr, mod_gate)


def gate_bwd(name, dh, m, mod, r_gt, seq):
    T, D = dh.shape
    tr = _row_tile(T, seq)
    nlat = seq // tr

    def body(d_ref, m_ref, mod_ref, o_ref, st_ref):
        i = pl.program_id(0)
        is_ctx = i >= nlat
        d = d_ref[...]
        o_ref[...] = (d * _mod_row(mod_ref, r_gt, is_ctx)).astype(BF16)
        dgt = _colsum(d * m_ref[...].astype(F32))
        dgt_l, dgt_c = _split_stats(is_ctx, dgt)
        _acc_rows(st_ref, i == 0, [(0, dgt_l), (1, dgt_c), (2, dgt)])

    return pl.pallas_call(
        body, name=name, grid=(T // tr,),
        in_specs=[_rowspec(tr, D), _rowspec(tr, D), _fullspec((16, D))],
        out_specs=(_rowspec(tr, D), _fullspec((8, D))),
        out_shape=(jax.ShapeDtypeStruct((T, D), BF16), jax.ShapeDtypeStruct((8, D), F32)),
        compiler_params=_params(1),
    )(dh, m, mod)


def loss_head(name, h, target, g):
    T, D = h.shape
    tr = _tile(T, 256, 8)

    def body(h_ref, t_ref, g_ref, o_ref, st_ref):
        i = pl.program_id(0)
        x = h_ref[...]
        r = lax.rsqrt(jnp.mean(x * x, axis=-1, keepdims=True) + EPS)
        n = x * r
        gg = g_ref[...]
        err = n * gg - t_ref[...]
        dy = err * (1.0 / D)
        dn = dy * gg
        o_ref[...] = r * (dn - n * jnp.mean(dn * n, axis=-1, keepdims=True))
        _acc_rows(st_ref, i == 0, [(0, _colsum(dy * n)), (1, _colsum(err * err))])

    return pl.pallas_call(
        body, name=name, grid=(T // tr,),
        in_specs=[_rowspec(tr, D), _rowspec(tr, D), _fullspec((1, D))],
        out_specs=(_rowspec(tr, D), _fullspec((8, D))),
        out_shape=(jax.ShapeDtypeStruct((T, D), F32), jax.ShapeDtypeStruct((8, D), F32)),
        compiler_params=_params(1),
    )(h, target, g)


GELU_C = math.sqrt(2.0 / math.pi)


def _gelu(x):
    t = jnp.tanh(GELU_C * (x + 0.044715 * (x * x * x)))
    return 0.5 * x * (1.0 + t), t


def _gelu_grad(x, t):
    return 0.5 * (1.0 + t) + 0.5 * x * (1.0 - t * t) * (GELU_C * (1.0 + 3.0 * 0.044715 * (x * x)))


def _layernorm_fwd(v, g, b):
    mu = jnp.mean(v, axis=-1, keepdims=True)
    xc = v - mu
    rs = lax.rsqrt(jnp.mean(xc * xc, axis=-1, keepdims=True) + EPS)
    xhat = xc * rs
    return xhat * g + b, xhat, rs


def _layernorm_bwd(dout, xhat, rs, g):
    dxh = dout * g
    return rs * (dxh - jnp.mean(dxh, axis=-1, keepdims=True) - xhat * jnp.mean(dxh * xhat, axis=-1, keepdims=True))


def gmlp_fwd(name, zp, ln_g, ln_b, ws, bfull):
    T, E2 = zp.shape
    E = E2 // 2
    G = E // CHUNK
    tr = 2 * CHUNK if T % (2 * CHUNK) == 0 else CHUNK

    def body(zp_ref, g_ref, b_ref, ws_ref, bf_ref, p_ref):
        for ch in range(tr // CHUNK):
            rs_ = slice(ch * CHUNK, (ch + 1) * CHUNK)
            u, _ = _gelu(zp_ref[rs_, 0:E].astype(F32))
            v, _ = _gelu(zp_ref[rs_, E:E2].astype(F32))
            vn, _, _ = _layernorm_fwd(v, g_ref[...], b_ref[...])
            vnb = vn.astype(BF16)
            for gi in range(G):
                cs = slice(gi * CHUNK, (gi + 1) * CHUNK)
                vs = jnp.dot(ws_ref[gi], vnb[:, cs], preferred_element_type=F32) + bf_ref[:, cs]
                p_ref[rs_, cs] = (u[:, cs] * vs).astype(BF16)

    return pl.pallas_call(
        body, name=name, grid=(T // tr,),
        in_specs=[_rowspec(tr, E2), _fullspec((1, E)), _fullspec((1, E)), _fullspec((G, CHUNK, CHUNK)),
                  _fullspec((CHUNK, E))],
        out_specs=_rowspec(tr, E), out_shape=jax.ShapeDtypeStruct((T, E), BF16), compiler_params=_params(1),
    )(zp, ln_g, ln_b, ws, bfull)


def gmlp_bwd(name, zp, dp, ln_g, ln_b, ws, bfull):
    T, E2 = zp.shape
    E = E2 // 2
    G = E // CHUNK
    tr = 2 * CHUNK if T % (2 * CHUNK) == 0 else CHUNK
    n = T // tr

    def body(zp_ref, dp_ref, g_ref, b_ref, ws_ref, bf_ref, dz_ref, dws_ref, dbs_ref, st_ref, dvn_ref, dbf_ref):
        i = pl.program_id(0)

        @pl.when(i == 0)
        def _():
            dws_ref[...] = jnp.zeros_like(dws_ref)
            dbf_ref[...] = jnp.zeros_like(dbf_ref)

        dlg = jnp.zeros((1, E), F32)
        dlb = jnp.zeros((1, E), F32)
        for ch in range(tr // CHUNK):
            rs_ = slice(ch * CHUNK, (ch + 1) * CHUNK)
            zu = zp_ref[rs_, 0:E].astype(F32)
            zv = zp_ref[rs_, E:E2].astype(F32)
            u, tu = _gelu(zu)
            v, tv = _gelu(zv)
            vn, xhat, rs = _layernorm_fwd(v, g_ref[...], b_ref[...])
            vnb = vn.astype(BF16)
            dpf = dp_ref[rs_, :].astype(F32)
            for gi in range(G):
                cs = slice(gi * CHUNK, (gi + 1) * CHUNK)
                w = ws_ref[gi]
                vs = jnp.dot(w, vnb[:, cs], preferred_element_type=F32) + bf_ref[:, cs]
                dz_ref[rs_, cs] = (dpf[:, cs] * vs * _gelu_grad(zu[:, cs], tu[:, cs])).astype(BF16)
                dvs = dpf[:, cs] * u[:, cs]
                dvsb = dvs.astype(BF16)
                dws_ref[gi] += lax.dot_general(dvsb, vnb[:, cs], NT, preferred_element_type=F32)
                dbf_ref[:, cs] += dvs
                dvn_ref[:, cs] = lax.dot_general(w, dvsb, TN, preferred_element_type=F32)
            dvn = dvn_ref[...]
            dlg = dlg + _colsum(dvn * xhat)
            dlb = dlb + _colsum(dvn)
            dv = _layernorm_bwd(dvn, xhat, rs, g_ref[...])
            dz_ref[rs_, E:E2] = (dv * _gelu_grad(zv, tv)).astype(BF16)
        _acc_rows(st_ref, i == 0, [(0, dlg), (1, dlb)])

        @pl.when(i == n - 1)
        def _():
            for gi in range(G):
                dbs_ref[:, gi:gi + 1] = jnp.sum(dbf_ref[:, gi * CHUNK:(gi + 1) * CHUNK], axis=1, keepdims=True)

    return pl.pallas_call(
        body, name=name, grid=(n,),
        in_specs=[_rowspec(tr, E2), _rowspec(tr, E), _fullspec((1, E)), _fullspec((1, E)),
                  _fullspec((G, CHUNK, CHUNK)), _fullspec((CHUNK, E))],
        out_specs=(_rowspec(tr, E2), _fullspec((G, CHUNK, CHUNK)), _fullspec((CHUNK, G)), _fullspec((8, E))),
        out_shape=(jax.ShapeDtypeStruct((T, E2), BF16), jax.ShapeDtypeStruct((G, CHUNK, CHUNK), F32),
                   jax.ShapeDtypeStruct((CHUNK, G), F32), jax.ShapeDtypeStruct((8, E), F32)),
        scratch_shapes=[pltpu.VMEM((CHUNK, E), F32), pltpu.VMEM((CHUNK, E), F32)],
        compiler_params=_params(1),
    )(zp, dp, ln_g, ln_b, ws, bfull)


def _halo_specs(tr, T, width):
    per = tr // HALO
    last = T // HALO - 1
    prev = pl.BlockSpec((HALO, width), lambda i: (jnp.maximum(i * per - 1, 0), 0))
    nxt = pl.BlockSpec((HALO, width), lambda i: (jnp.minimum((i + 1) * per, last), 0))
    return prev, nxt


def _halo_valid(i, n, nlat):
    return jnp.logical_and(i > 0, i != nlat), jnp.logical_and(i + 1 < n, i + 1 != nlat)


def _glu(tt, D):
    a = tt[:, 0:D].astype(F32)
    g = tt[:, D:2 * D].astype(F32)
    return a * _sigmoid(g)


CONV_SUB = 32
CONV_SPAN = 8 * ((CONV_W + 7) // 8) - 8


def _build_shifts(buf, sh, tr):
    for s in range(1, 8):
        sh[s - 1, 0:tr + CONV_SPAN, :] = buf[pl.ds(s, tr + CONV_SPAN), :]


def _tap(buf, sh, offset, r0, cs):
    s = offset % 8
    start = pl.multiple_of(r0 + (offset - s), 8)
    if s == 0:
        return buf[pl.ds(start, CONV_SUB), cs]
    return sh[s - 1, pl.ds(start, CONV_SUB), cs]


def conv_fwd(name, t, wdw, bdw, ln_g, ln_b, seq):
    T, D2 = t.shape
    D = D2 // 2
    tr = _row_tile(T, seq)
    n, nlat = T // tr, seq // tr
    prev, nxt = _halo_specs(tr, T, D2)

    tc = D // 2

    def body(tp_ref, tc_ref, tn_ref, w_ref, b_ref, g_ref, bb_ref, y_ref, yc_ref, s_ref, buf, sh):
        i = pl.program_id(0)
        pv, nv = _halo_valid(i, n, nlat)
        y = _glu(tc_ref[...], D)
        y_ref[...] = y
        buf[0:HALO, :] = jnp.where(pv, _glu(tp_ref[...], D), 0.0)
        buf[HALO:HALO + tr, :] = y
        buf[HALO + tr:2 * HALO + tr, :] = jnp.where(nv, _glu(tn_ref[...], D), 0.0)
        _build_shifts(buf, sh, tr)

        def rows_block(rb, carry):
            r0 = pl.multiple_of(rb * CONV_SUB, CONV_SUB)
            for c0 in range(0, D, tc):
                cs = slice(c0, c0 + tc)
                acc = jnp.zeros((CONV_SUB, tc), F32) + b_ref[:, cs]
                for k in range(CONV_W):
                    acc = acc + w_ref[k:k + 1, cs] * _tap(buf, sh, k + 1, r0, cs)
                yc_ref[pl.ds(r0, CONV_SUB), cs] = acc
            return carry

        lax.fori_loop(0, tr // CONV_SUB, rows_block, 0)
        yl, _, _ = _layernorm_fwd(yc_ref[...], g_ref[...], bb_ref[...])
        s_ref[...] = (yl * _sigmoid(yl)).astype(BF16)

    return pl.pallas_call(
        body, name=name, grid=(n,),
        in_specs=[prev, _rowspec(tr, D2), nxt, _fullspec((32, D)), _fullspec((1, D)), _fullspec((1, D)),
                  _fullspec((1, D))],
        out_specs=(_rowspec(tr, D), _rowspec(tr, D), _rowspec(tr, D)),
        out_shape=(jax.ShapeDtypeStruct((T, D), F32), jax.ShapeDtypeStruct((T, D), F32),
                   jax.ShapeDtypeStruct((T, D), BF16)),
        scratch_shapes=[pltpu.VMEM((tr + 2 * HALO, D), F32), pltpu.VMEM((7, tr + CONV_SPAN, D), F32)],
        compiler_params=_params(1),
    )(t, t, t, wdw, bdw, ln_g, ln_b)


def conv_bwd_norm(name, ds, yc, ln_g, ln_b):
    T, D = yc.shape
    tr = _tile(T, 256, 8)

    def body(ds_ref, yc_ref, g_ref, b_ref, o_ref, st_ref):
        i = pl.program_id(0)
        yl, xhat, rs = _layernorm_fwd(yc_ref[...], g_ref[...], b_ref[...])
        sig = _sigmoid(yl)
        dyl = ds_ref[...] * (sig * (1.0 + yl * (1.0 - sig)))
        dyc = _layernorm_bwd(dyl, xhat, rs, g_ref[...])
        o_ref[...] = dyc
        _acc_rows(st_ref, i == 0, [(0, _colsum(dyl * xhat)), (1, _colsum(dyl)), (2, _colsum(dyc))])

    return pl.pallas_call(
        body, name=name, grid=(T // tr,),
        in_specs=[_rowspec(tr, D), _rowspec(tr, D), _fullspec((1, D)), _fullspec((1, D))],
        out_specs=(_rowspec(tr, D), _fullspec((8, D))),
        out_shape=(jax.ShapeDtypeStruct((T, D), F32), jax.ShapeDtypeStruct((8, D), F32)),
        compiler_params=_params(1),
    )(ds, yc, ln_g, ln_b)


def conv_bwd_taps(name, dyc, y, t, wdw, seq):
    T, D = y.shape
    tr = _row_tile(T, seq)
    n, nlat = T // tr, seq // tr
    prev, nxt = _halo_specs(tr, T, D)

    tc = D // 2
    full = slice(0, tc)

    def body(dp_ref, dc_ref, dn_ref, yp_ref, ycur_ref, yn_ref, t_ref, w_ref, dt_ref, dw_ref,
             dbuf, ybuf, dsh, ysh, dwacc):
        i = pl.program_id(0)
        pv, nv = _halo_valid(i, n, nlat)

        @pl.when(i == 0)
        def _():
            dw_ref[...] = jnp.zeros_like(dw_ref)

        for c0 in range(0, D, tc):
            cs = slice(c0, c0 + tc)
            for buf, sh, p_ref, c_ref, n_ref in ((dbuf, dsh, dp_ref, dc_ref, dn_ref), (ybuf, ysh, yp_ref, ycur_ref, yn_ref)):
                buf[0:HALO, :] = jnp.where(pv, p_ref[:, cs], 0.0)
                buf[HALO:HALO + tr, :] = c_ref[:, cs]
                buf[HALO + tr:2 * HALO + tr, :] = jnp.where(nv, n_ref[:, cs], 0.0)
                _build_shifts(buf, sh, tr)
            dwacc[...] = jnp.zeros_like(dwacc)

            def rows_block(rb, carry):
                r0 = pl.multiple_of(rb * CONV_SUB, CONV_SUB)
                rows = pl.ds(r0, CONV_SUB)
                dcur = dc_ref[rows, cs]
                dy = jnp.zeros((CONV_SUB, tc), F32)
                for k in range(CONV_W):
                    prod = dcur * _tap(ybuf, ysh, k + 1, r0, full)
                    dwacc[k] += prod[0:8] + prod[8:16] + prod[16:24] + prod[24:32]
                    dy = dy + w_ref[k:k + 1, cs] * _tap(dbuf, dsh, CONV_W - k, r0, full)
                a = t_ref[rows, cs].astype(F32)
                sig = _sigmoid(t_ref[rows, slice(D + c0, D + c0 + tc)].astype(F32))
                dt_ref[rows, cs] = (dy * sig).astype(BF16)
                dt_ref[rows, slice(D + c0, D + c0 + tc)] = (dy * a * sig * (1.0 - sig)).astype(BF16)
                return carry

            lax.fori_loop(0, tr // CONV_SUB, rows_block, 0)
            for k in range(CONV_W):
                dw_ref[k:k + 1, cs] += jnp.sum(dwacc[k], axis=0, keepdims=True)

    return pl.pallas_call(
        body, name=name, grid=(n,),
        in_specs=[prev, _rowspec(tr, D), nxt, prev, _rowspec(tr, D), nxt, _rowspec(tr, 2 * D), _fullspec((32, D))],
        out_specs=(_rowspec(tr, 2 * D), _fullspec((32, D))),
        out_shape=(jax.ShapeDtypeStruct((T, 2 * D), BF16), jax.ShapeDtypeStruct((32, D), F32)),
        scratch_shapes=[pltpu.VMEM((tr + 2 * HALO, tc), F32), pltpu.VMEM((tr + 2 * HALO, tc), F32),
                        pltpu.VMEM((7, tr + CONV_SPAN, tc), F32), pltpu.VMEM((7, tr + CONV_SPAN, tc), F32),
                        pltpu.VMEM((32, 8, tc), F32)],
        compiler_params=_params(1),
    )(dyc, dyc, dyc, y, y, y, t, wdw)


Q_ROWS = 8
K_ROWS = 16
QN = Q_ROWS * GRID_W
WIN = K_ROWS * GRID_W
NA_Q_OFFSET = (0, 4, 8)


def _na_window_start(cls, qi):
    return (max(qi - NA_ROWS // 2, 0), qi, min(qi + NA_ROWS // 2, NA_ROWS))[cls]


def _na_pairs(cls):
    for qi in range(Q_ROWS):
        lo = _na_window_start(cls, qi)
        for kj in range(lo, lo + NA_ROWS):
            yield qi, kj, kj - NA_Q_OFFSET[cls] - qi + NA_ROWS - 1


def na_bias_tables(toep):
    H = toep.shape[0]
    neg = jnp.full((H, GRID_W, GRID_W), NEG_INF, F32)
    tables = []
    for cls in range(3):
        dr_of = {(qi, kj): dr for qi, kj, dr in _na_pairs(cls)}
        rows = [jnp.concatenate([toep[:, dr_of[(qi, kj)]] if (qi, kj) in dr_of else neg for kj in range(K_ROWS)],
                                axis=2) for qi in range(Q_ROWS)]
        tables.append(jnp.concatenate(rows, axis=1))
    return jnp.stack(tables)


def _na_class(rb, nblk):
    return jnp.where(rb == 0, 0, jnp.where(rb == nblk - 1, 2, 1))


def _na_specs(seq, ctx_rows, H):
    cb = seq // ctx_rows
    nblk = seq // QN
    return [
        pl.BlockSpec((QN, HEAD), lambda h, r: (r, h)),
        pl.BlockSpec((seq, HEAD), lambda h, r: (0, H + h)),
        pl.BlockSpec((seq, HEAD), lambda h, r: (0, 2 * H + h)),
        pl.BlockSpec((ctx_rows, HEAD), lambda h, r: (cb, H + h)),
        pl.BlockSpec((ctx_rows, HEAD), lambda h, r: (cb, 2 * H + h)),
        pl.BlockSpec((None, None, QN, WIN), lambda h, r: (_na_class(r, nblk), h, 0, 0)),
    ]


def _na_scores(q_ref, k_ref, v_ref, kc_ref, vc_ref, b_ref, rows):
    rb = pl.program_id(1)
    k_start = jnp.clip(Q_ROWS * rb - NA_ROWS // 2, 0, rows - K_ROWS)
    start = pl.multiple_of(k_start * GRID_W, GRID_W)
    scale = HEAD ** -0.5
    q = q_ref[...]
    kw = k_ref[pl.ds(start, WIN), :]
    vw = v_ref[pl.ds(start, WIN), :]
    kc = kc_ref[...]
    vc = vc_ref[...]
    s = lax.dot_general(q, kw, NT, preferred_element_type=F32) * scale + b_ref[...]
    sc = lax.dot_general(q, kc, NT, preferred_element_type=F32) * scale
    m = jnp.maximum(jnp.max(s, axis=-1, keepdims=True), jnp.max(sc, axis=-1, keepdims=True))
    p = jnp.exp(s - m)
    pc = jnp.exp(sc - m)
    l = jnp.sum(p, axis=-1, keepdims=True) + jnp.sum(pc, axis=-1, keepdims=True)
    return q, kw, vw, kc, vc, p, pc, l, start, scale


def na_fwd(name, qkv, bias, seq, tasks=()):
    T, D3 = qkv.shape
    D = D3 // 3
    H = D // HEAD
    rows = seq // GRID_W

    def body(q_ref, k_ref, v_ref, kc_ref, vc_ref, b_ref, o_ref):
        q, kw, vw, kc, vc, p, pc, l, start, scale = _na_scores(q_ref, k_ref, v_ref, kc_ref, vc_ref, b_ref, rows)
        o = (jnp.dot(p.astype(BF16), vw, preferred_element_type=F32)
             + jnp.dot(pc.astype(BF16), vc, preferred_element_type=F32))
        o_ref[...] = (o / l).astype(BF16)

    outs, touts = host_call(name, body, (H, seq // QN), [qkv, qkv, qkv, qkv, qkv, bias],
                            _na_specs(seq, T - seq, H), [jax.ShapeDtypeStruct((seq, D), BF16)],
                            [pl.BlockSpec((QN, HEAD), lambda h, r: (r, h))], tasks=tasks)
    return outs[0], touts


def na_bwd(name, qkv, bias, do, seq):
    T, D3 = qkv.shape
    D = D3 // 3
    H = D // HEAD
    rows = seq // GRID_W
    nblk = seq // QN
    ctx_rows = T - seq

    def body(q_ref, k_ref, v_ref, kc_ref, vc_ref, b_ref, do_ref, dq_ref, dk_ref, dv_ref, dkc_ref, dvc_ref, db_ref):
        rb = pl.program_id(1)

        @pl.when(rb == 0)
        def _():
            for ref in (dk_ref, dv_ref, dkc_ref, dvc_ref):
                ref[...] = jnp.zeros_like(ref)

        @pl.when(jnp.logical_or(rb <= 1, rb == nblk - 1))
        def _():
            db_ref[...] = jnp.zeros_like(db_ref)

        q, kw, vw, kc, vc, p, pc, l, start, scale = _na_scores(q_ref, k_ref, v_ref, kc_ref, vc_ref, b_ref, rows)
        inv = 1.0 / l
        pn = p * inv
        pcn = pc * inv
        do_ = do_ref[...]
        dp = lax.dot_general(do_, vw, NT, preferred_element_type=F32)
        dpc = lax.dot_general(do_, vc, NT, preferred_element_type=F32)
        delta = jnp.sum(pn * dp, axis=-1, keepdims=True) + jnp.sum(pcn * dpc, axis=-1, keepdims=True)
        ds = pn * (dp - delta)
        dsc = pcn * (dpc - delta)
        db_ref[...] += ds
        dsb = (ds * scale).astype(BF16)
        dscb = (dsc * scale).astype(BF16)
        dq = jnp.dot(dsb, kw, preferred_element_type=F32) + jnp.dot(dscb, kc, preferred_element_type=F32)
        dq_ref[...] = dq.astype(BF16)
        dk_ref[pl.ds(start, WIN), :] += lax.dot_general(dsb, q, TN, preferred_element_type=F32)
        dv_ref[pl.ds(start, WIN), :] += lax.dot_general(pn.astype(BF16), do_, TN, preferred_element_type=F32)
        dkc_ref[...] += lax.dot_general(dscb, q, TN, preferred_element_type=F32)
        dvc_ref[...] += lax.dot_general(pcn.astype(BF16), do_, TN, preferred_element_type=F32)

    head_lat = pl.BlockSpec((seq, HEAD), lambda h, r: (0, h))
    head_ctx = pl.BlockSpec((ctx_rows, HEAD), lambda h, r: (0, h))
    return pl.pallas_call(
        body, name=name, grid=(H, nblk),
        in_specs=_na_specs(seq, ctx_rows, H) + [pl.BlockSpec((QN, HEAD), lambda h, r: (r, h))],
        out_specs=(pl.BlockSpec((QN, HEAD), lambda h, r: (r, h)), head_lat, head_lat, head_ctx, head_ctx,
                   pl.BlockSpec((None, None, QN, WIN), lambda h, r: (_na_class(r, nblk), h, 0, 0))),
        out_shape=(jax.ShapeDtypeStruct((seq, D), BF16), jax.ShapeDtypeStruct((seq, D), F32),
                   jax.ShapeDtypeStruct((seq, D), F32), jax.ShapeDtypeStruct((ctx_rows, D), F32),
                   jax.ShapeDtypeStruct((ctx_rows, D), F32), jax.ShapeDtypeStruct(bias.shape, F32)),
        compiler_params=_params(2),
    )(qkv, qkv, qkv, qkv, qkv, bias, do)


def _ctx_specs(seq, ctx_rows, H):
    cb = seq // ctx_rows
    return [pl.BlockSpec((ctx_rows, HEAD), lambda h: (cb, h)),
            pl.BlockSpec((ctx_rows, HEAD), lambda h: (cb, H + h)),
            pl.BlockSpec((ctx_rows, HEAD), lambda h: (cb, 2 * H + h))]


def _ctx_probs(q_ref, k_ref):
    s = lax.dot_general(q_ref[...], k_ref[...], NT, preferred_element_type=F32) * (HEAD ** -0.5)
    p = jnp.exp(s - jnp.max(s, axis=-1, keepdims=True))
    return p / jnp.sum(p, axis=-1, keepdims=True)


def ctx_attn_fwd(name, qkv, seq):
    T, D3 = qkv.shape
    D = D3 // 3
    H = D // HEAD
    ctx_rows = T - seq

    def body(q_ref, k_ref, v_ref, o_ref):
        p = _ctx_probs(q_ref, k_ref)
        o_ref[...] = jnp.dot(p.astype(BF16), v_ref[...], preferred_element_type=F32).astype(BF16)

    return pl.pallas_call(
        body, name=name, grid=(H,), in_specs=_ctx_specs(seq, ctx_rows, H),
        out_specs=pl.BlockSpec((ctx_rows, HEAD), lambda h: (0, h)),
        out_shape=jax.ShapeDtypeStruct((ctx_rows, D), BF16), compiler_params=_params(1),
    )(qkv, qkv, qkv)


def ctx_attn_bwd(name, qkv, do, dkc_lat, dvc_lat, seq):
    T, D3 = qkv.shape
    D = D3 // 3
    H = D // HEAD
    ctx_rows = T - seq
    cb = seq // ctx_rows
    scale = HEAD ** -0.5

    def body(q_ref, k_ref, v_ref, do_ref, dkl_ref, dvl_ref, dq_ref, dk_ref, dv_ref):
        p = _ctx_probs(q_ref, k_ref)
        do_ = do_ref[...]
        dp = lax.dot_general(do_, v_ref[...], NT, preferred_element_type=F32)
        ds = p * (dp - jnp.sum(p * dp, axis=-1, keepdims=True))
        dsb = (ds * scale).astype(BF16)
        dq_ref[...] = jnp.dot(dsb, k_ref[...], preferred_element_type=F32).astype(BF16)
        dk_ref[...] = (dkl_ref[...] + lax.dot_general(dsb, q_ref[...], TN, preferred_element_type=F32)).astype(BF16)
        dv_ref[...] = (dvl_ref[...]
                       + lax.dot_general(p.astype(BF16), do_, TN, preferred_element_type=F32)).astype(BF16)

    blk = pl.BlockSpec((ctx_rows, HEAD), lambda h: (0, h))
    shp = jax.ShapeDtypeStruct((ctx_rows, D), BF16)
    return pl.pallas_call(
        body, name=name, grid=(H,),
        in_specs=_ctx_specs(seq, ctx_rows, H) + [pl.BlockSpec((ctx_rows, HEAD), lambda h: (cb, h)), blk, blk],
        out_specs=(blk, blk, blk), out_shape=(shp, shp, shp), compiler_params=_params(1),
    )(qkv, qkv, qkv, do, dkc_lat, dvc_lat)


def _rpb_tables():
    qc = jnp.arange(GRID_W)[:, None]
    kc = jnp.arange(GRID_W)[None, :]
    rel = (kc - qc + NA_COLS - 1).reshape(1, GRID_W * GRID_W)
    onehot = (rel == jnp.arange(32)[:, None]).astype(F32)
    c_start = jnp.clip(qc - NA_COLS // 2, 0, GRID_W - NA_COLS)
    mask = jnp.logical_and(kc >= c_start, kc < c_start + NA_COLS).astype(F32).reshape(1, GRID_W * GRID_W)
    return onehot, mask


def rpb_expand(name, rpb2, onehot, mask):
    R = rpb2.shape[0]

    def body(r_ref, oh_ref, m_ref, o_ref):
        t = jnp.dot(r_ref[...], oh_ref[...], preferred_element_type=F32, precision=lax.Precision.HIGHEST)
        o_ref[...] = jnp.where(m_ref[...] > 0.5, t, NEG_INF)

    return pl.pallas_call(body, name=name, out_shape=jax.ShapeDtypeStruct((R, GRID_W * GRID_W), F32),
                          compiler_params=_params())(rpb2, onehot, mask)


def rpb_fold(name, x, classes):
    H = x.shape[0]
    n_dr = 2 * NA_ROWS - 1

    def body(x_ref, y_ref):
        acc = [None] * n_dr
        for cls in classes:
            for qi, kj, dr in _na_pairs(cls):
                acc[dr] = x_ref[cls, qi, kj] if acc[dr] is None else acc[dr] + x_ref[cls, qi, kj]
        for dr in range(n_dr):
            y_ref[dr] = acc[dr]

    return pl.pallas_call(
        body, name=name, grid=(H,),
        in_specs=[pl.BlockSpec((None, 3, Q_ROWS, K_ROWS, GRID_W, GRID_W), lambda h: (h, 0, 0, 0, 0, 0))],
        out_specs=pl.BlockSpec((None, n_dr, GRID_W, GRID_W), lambda h: (h, 0, 0, 0)),
        out_shape=jax.ShapeDtypeStruct((H, n_dr, GRID_W, GRID_W), F32), compiler_params=_params(1),
    )(x)


def rpb_reduce(name, y2, onehot_t):
    R = y2.shape[0]

    def body(y_ref, oh_ref, o_ref):
        o_ref[...] = jnp.dot(y_ref[...], oh_ref[...], preferred_element_type=F32, precision=lax.Precision.HIGHEST)

    return pl.pallas_call(body, name=name, out_shape=jax.ShapeDtypeStruct((R, 32), F32),
                          compiler_params=_params())(y2, onehot_t)


def _adam(g, w, m, v):
    m2 = ADAM_B1 * m + (1.0 - ADAM_B1) * g
    v2 = ADAM_B2 * v + (1.0 - ADAM_B2) * (g * g)
    m_hat = m2 / (1.0 - ADAM_B1 ** ADAM_STEP)
    v_hat = v2 / (1.0 - ADAM_B2 ** ADAM_STEP)
    delta = -ADAM_LR * (m_hat / (jnp.sqrt(v_hat) + ADAM_EPS) + ADAM_WD * w)
    return delta, m2, v2


def adam_parts(name, land, land2, w, m, v, tasks=()):
    shape = w.shape
    C = shape[-1]
    R = math.prod(shape[:-1])
    tr = _tile(R, max(16, (256 * 1024 // C) // 16 * 16), 16)
    l1, l2 = land.reshape(N_SHARD, R, C), land2.reshape(N_SHARD, R, C)

    def body(l1_ref, l2_ref, w_ref, m_ref, v_ref, g_ref, d_ref, m2_ref, v2_ref):
        a = l1_ref[0].astype(F32)
        b = l2_ref[0].astype(F32)
        for k in range(1, N_SHARD):
            a = a + l1_ref[k].astype(F32)
            b = b + l2_ref[k].astype(F32)
        g = a + b
        g_ref[...] = g
        d_ref[...], m2_ref[...], v2_ref[...] = _adam(g, w_ref[...], m_ref[...], v_ref[...])

    part = pl.BlockSpec((N_SHARD, tr, C), lambda i: (0, i, 0))
    row = _rowspec(tr, C)
    shp = jax.ShapeDtypeStruct((R, C), F32)
    outs, touts = host_call(name, body, (R // tr,), [l1, l2, w.reshape(R, C), m.reshape(R, C), v.reshape(R, C)],
                            [part, part, row, row, row], (shp,) * 4, (row,) * 4, tasks=tasks)
    return tuple(o.reshape(shape) for o in outs), touts


def adam_flat(name, g, w, m, v):
    R, C = g.shape
    tr = _tile(R, 256, 8)

    def body(g_ref, w_ref, m_ref, v_ref, d_ref, m2_ref, v2_ref):
        d_ref[...], m2_ref[...], v2_ref[...] = _adam(g_ref[...], w_ref[...], m_ref[...], v_ref[...])

    row = _rowspec(tr, C)
    shp = jax.ShapeDtypeStruct((R, C), F32)
    return pl.pallas_call(body, name=name, grid=(R // tr,), in_specs=[row] * 4, out_specs=(row,) * 3,
                          out_shape=(shp,) * 3, compiler_params=_params(1))(g, w, m, v)


def reduce_8(name, gathered):
    _, R, D = gathered.shape
    tr = _tile(R, 64, 8)

    def body(x_ref, o_ref):
        acc = x_ref[0]
        for k in range(1, 8):
            acc = acc + x_ref[k]
        o_ref[...] = acc

    return pl.pallas_call(
        body, name=name, grid=(R // tr,), in_specs=[pl.BlockSpec((8, tr, D), lambda i: (0, i, 0))],
        out_specs=_rowspec(tr, D), out_shape=jax.ShapeDtypeStruct((R, D), F32), compiler_params=_params(1),
    )(gathered)


def ada_fwd(name, craw16, ada_w, ada_b3):
    L, D, Cs = ada_w.shape
    tn = _tile(Cs, 512)

    def body(c_ref, w_ref, b_ref, o_ref):
        cc = c_ref[...]
        s = cc * _sigmoid(cc)
        o_ref[...] = jnp.dot(s, w_ref[...], preferred_element_type=F32,
                             precision=lax.Precision.HIGHEST) + b_ref[...]

    return pl.pallas_call(
        body, name=name, grid=(L, Cs // tn),
        in_specs=[pl.BlockSpec((16, D), lambda l, j: (0, 0)), pl.BlockSpec((None, D, tn), lambda l, j: (l, 0, j)),
                  pl.BlockSpec((None, 1, tn), lambda l, j: (l, 0, j))],
        out_specs=pl.BlockSpec((None, 16, tn), lambda l, j: (l, 0, j)),
        out_shape=jax.ShapeDtypeStruct((L, 16, Cs), F32), compiler_params=_params(2),
    )(craw16, ada_w, ada_b3)


def ada_bwd_adam(name, craw16_t, dm16, dmc8, w, m, v):
    L, D, Cs = w.shape
    tn = _tile(Cs, 256)

    def body(c_ref, dm_ref, dc_ref, w_ref, m_ref, v_ref, g_ref, d_ref, m2_ref, v2_ref, ds_ref):
        step = pl.program_id(0) * (Cs // tn) + pl.program_id(1)
        cc = c_ref[...]
        s_t = cc * _sigmoid(cc)
        g = jnp.dot(s_t, dm_ref[...], preferred_element_type=F32, precision=lax.Precision.HIGHEST)
        ww = w_ref[...]
        g_ref[...] = g
        d_ref[...], m2_ref[...], v2_ref[...] = _adam(g, ww, m_ref[...], v_ref[...])

        @pl.when(step == 0)
        def _():
            ds_ref[...] = jnp.zeros_like(ds_ref)

        ds_ref[...] += lax.dot_general(dc_ref[...].astype(BF16), ww.astype(BF16), NT, preferred_element_type=F32)

    wspec = pl.BlockSpec((None, D, tn), lambda l, j: (l, 0, j))
    shp = jax.ShapeDtypeStruct((L, D, Cs), F32)
    return pl.pallas_call(
        body, name=name, grid=(L, Cs // tn),
        in_specs=[pl.BlockSpec((D, 16), lambda l, j: (0, 0)), pl.BlockSpec((None, 16, tn), lambda l, j: (l, 0, j)),
                  pl.BlockSpec((None, 8, tn), lambda l, j: (l, 0, j)), wspec, wspec, wspec],
        out_specs=(wspec, wspec, wspec, wspec, pl.BlockSpec((8, D), lambda l, j: (0, 0))),
        out_shape=(shp, shp, shp, shp, jax.ShapeDtypeStruct((8, D), F32)), compiler_params=_params(2),
    )(craw16_t, dm16, dmc8, w, m, v)


def cctx_adam(name, ds_all, c_ctx, m, v):
    D = c_ctx.shape[1]

    def body(ds_ref, c_ref, m_ref, v_ref, g_ref, d_ref, m2_ref, v2_ref):
        ds = ds_ref[0, 0:1, :]
        for slot in (2, 4, 6):
            ds = ds + ds_ref[slot, 0:1, :]
        cc = c_ref[...]
        sig = _sigmoid(cc)
        g = ds * (sig * (1.0 + cc * (1.0 - sig)))
        g_ref[...] = g
        d_ref[...], m2_ref[...], v2_ref[...] = _adam(g, cc, m_ref[...], v_ref[...])

    shp = jax.ShapeDtypeStruct((1, D), F32)
    return pl.pallas_call(body, name=name, out_shape=(shp,) * 4, compiler_params=_params())(ds_all, c_ctx, m, v)


WEIGHT_NAMES = ['c_ctx', 'ada_w', 'ada_b', 'g_mix', 'g_ffn', 'ffn_w1', 'ffn_w3', 'ffn_w2', 'a_w_in', 'a_ln_g',
                'a_ln_b', 'a_w_s', 'a_b_s', 'a_w_out', 'b_w_qkv', 'b_rpb', 'b_w_out', 'c_w_pw1', 'c_w_dw', 'c_b_dw',
                'c_ln_g', 'c_ln_b', 'c_w_pw2', 'g_final']
BIG_NAMES = ['ffn_w1', 'ffn_w3', 'ffn_w2', 'a_w_in', 'a_w_out', 'b_w_qkv', 'b_w_out', 'c_w_pw1', 'c_w_pw2']
SMALL_NAMES = ['ada_b', 'g_mix', 'g_ffn', 'a_ln_g', 'a_ln_b', 'a_w_s', 'a_b_s', 'b_rpb', 'c_w_dw', 'c_b_dw',
               'c_ln_g', 'c_ln_b', 'g_final']
SMALL_PACK_COLS = 512
MIXER_IN = ('a_w_in', 'b_w_qkv', 'c_w_pw1')
MIXER_OUT = ('a_w_out', 'b_w_out', 'c_w_pw2')

FWD_PLAN = {
    "pre": [("a_w_in", 0)],
    "in_0": [("a_w_out", 0), ("ffn_w1", 0)],
    "out_0": [("ffn_w3", 0)],
    "ffn_up_0": [("ffn_w2", 0), ("b_w_qkv", 0)],
    "ffn_down_0": [("b_w_out", 0), ("ffn_w1", 1)],
    "in_1": [("ffn_w3", 1)],
    "b_na_1": [("ffn_w2", 1), ("c_w_pw1", 0), ("c_w_pw2", 0)],
    "out_1": [("ffn_w1", 2)],
    "ffn_up_1": [("ffn_w3", 2), ("ffn_w2", 2)],
    "ffn_down_1": [("a_w_in", 1), ("a_w_out", 1)],
    "in_2": [("ffn_w1", 3)],
    "ffn_up_2": [("ffn_w3", 3), ("ffn_w2", 3)],
}


def _bwd_plan():
    plan = {}
    for i in range(N_LAYERS):
        w_in, w_out = (MIXER_IN[i % 3], i // 3), (MIXER_OUT[i % 3], i // 3)
        if i + 1 < N_LAYERS:
            plan[f"ffn_down_dx_{i}"] = [("forward", MIXER_IN[(i + 1) % 3], (i + 1) // 3)]
        plan[f"ffn_w1_dw_{i}"] = [("scatter", "ffn_w2", i)]
        plan[f"ffn_w3_dw_{i}"] = [("forward", "ffn_w2", i), ("scatter", "ffn_w1", i)]
        plan[f"ffn_up_dx_{i}"] = [("forward", "ffn_w1", i), ("scatter", "ffn_w3", i)]
        plan[f"out_dx_{i}"] = [("forward", "ffn_w3", i)]
        plan[f"in_dw_{i}"] = [("scatter",) + w_out]
        plan[f"in_dx_{i}"] = [("forward",) + w_out, ("scatter",) + w_in]
    plan["rs_post"] = [("forward", MIXER_IN[0], 0)]
    return plan


BWD_PLAN = _bwd_plan()


def _pad_rows(a, mult):
    r = (-a.shape[0]) % mult
    return a if r == 0 else jnp.concatenate([a, jnp.zeros((r,) + a.shape[1:], a.dtype)], axis=0)


def _rows_of(flat, D):
    n = flat.shape[0]
    r = -(-n // D)
    return jnp.concatenate([flat, jnp.zeros((r * D - n,), flat.dtype)]).reshape(r, D)


def _step(W, Mo, Vo, x, c, ctx, loss_target):
    seq, D = x.shape[1], x.shape[2]
    ctx_rows = ctx.shape[1]
    T = seq + ctx_rows
    L = N_LAYERS
    H = D // HEAD
    G = D // CHUNK
    xi, yi, ci = _xyc()
    e_idx = 4 * xi + 2 * yi + ci
    s_idx = 2 * xi + yi

    c_all = all_gather_8("ag_c", c)
    craw16 = jnp.concatenate([c_all.reshape(8, D), W['c_ctx'].reshape(1, D), jnp.zeros((7, D), F32)], axis=0)
    ada_w = W['ada_w']
    Cs = ada_w.shape[2]
    ada_b_s = lax.dynamic_slice_in_dim(W['ada_b'], s_idx * Cs, Cs, axis=1).reshape(L, 1, Cs)
    mod_s = ada_fwd("ada_fwd", craw16, ada_w, ada_b_s)
    mod_g = all_gather_xy("ag_mod", mod_s).transpose(1, 2, 0, 3).reshape(L, 16, N_SHARD * Cs)
    mod_lat = lax.dynamic_index_in_dim(mod_g, e_idx, axis=1, keepdims=False).reshape(L, 6, D)
    mod_all = jnp.concatenate([mod_lat, mod_g[:, 8].reshape(L, 6, D), jnp.zeros((L, 4, D), F32)], axis=1)

    Wg = {}
    land = {n: lax.empty((N_SHARD,) + W[n].shape, BF16) for n in BIG_NAMES}
    land2 = {n: lax.empty((N_SHARD,) + W[n].shape, BF16) for n in BIG_NAMES}
    dW = {}

    def gather_tasks(host):
        return [GatherTask(W[n][l].astype(BF16)) for n, l in FWD_PLAN.get(host, ())]

    def gathered(host, touts):
        for (n, l), out in zip(FWD_PLAN.get(host, ()), touts):
            Wg[(n, l)] = out[0]

    def scatter_tasks(host):
        tasks = []
        for kind, n, l in BWD_PLAN.get(host, ()):
            if kind == "scatter":
                tasks.append(ScatterTask(dW[(n, l)], land[n], land2[n], l))
            else:
                tasks.append(ForwardTask(land[n], land2[n], l))
        return tasks

    def scattered(host, touts):
        for (kind, n, l), out in zip(BWD_PLAN.get(host, ()), touts):
            land[n], land2[n] = out

    gathered("pre", comm_only("ag_pre", gather_tasks("pre")))

    n_a, n_c = W['a_ln_g'].shape[0], W['c_ln_g'].shape[0]
    sh_rows = jnp.concatenate([W['a_ln_g'], W['a_ln_b'], W['c_w_dw'].reshape(n_c * CONV_W, -1), W['c_b_dw'],
                               W['c_ln_g'], W['c_ln_b']], axis=0)
    n_sh = sh_rows.shape[0]
    sh_full = all_gather_xy("ag_small", _pad_rows(sh_rows, 8)).transpose(1, 0, 2).reshape(-1, D)[:n_sh]
    o = 0
    a_ln_g_f, o = sh_full[o:o + n_a], o + n_a
    a_ln_b_f, o = sh_full[o:o + n_a], o + n_a
    c_w_dw_f, o = sh_full[o:o + n_c * CONV_W].reshape(n_c, CONV_W, D), o + n_c * CONV_W
    c_b_dw_f, o = sh_full[o:o + n_c], o + n_c
    c_ln_g_f, o = sh_full[o:o + n_c], o + n_c
    c_ln_b_f, o = sh_full[o:o + n_c], o + n_c

    onehot, colmask = _rpb_tables()
    n_dr = 2 * NA_ROWS - 1
    rpb2 = jnp.pad(W['b_rpb'][0].reshape(H * n_dr, 2 * NA_COLS - 1), ((0, 0), (0, 1)))
    toep = rpb_expand("rpb_expand", rpb2, onehot, colmask).reshape(H, n_dr, GRID_W, GRID_W)
    bias = na_bias_tables(toep)
    na_classes = (0, 1, 2) if seq // QN > 2 else (0, 2)

    def mixer_params(i):
        mixer, j = i % 3, i // 3
        if mixer == 0:
            return dict(ln_g=a_ln_g_f[j:j + 1], ln_b=a_ln_b_f[j:j + 1], ws=W['a_w_s'][j].astype(BF16),
                        bfull=jnp.repeat(W['a_b_s'][j].T, CHUNK, axis=1))
        if mixer == 2:
            return dict(wdw=_pad_rows(c_w_dw_f[j], 32), bdw=c_b_dw_f[j:j + 1], ln_g=c_ln_g_f[j:j + 1],
                        ln_b=c_ln_b_f[j:j + 1])
        return {}

    def fwd(fn, host, *args):
        res, touts = fn(host, *args, tasks=gather_tasks(host))
        gathered(host, touts)
        return res

    def bwd(fn, host, *args, extra=(), **kw):
        tasks = scatter_tasks(host)
        res, touts = fn(host, *args, tasks=tasks + list(extra), **kw)
        scattered(host, touts[:len(tasks)])
        return (res, touts[len(tasks):]) if extra else res

    h = jnp.concatenate([x[0], ctx[0]], axis=0)
    saved = []
    for i in range(L):
        mixer, j = i % 3, i // 3
        n_in, n_out = MIXER_IN[mixer], MIXER_OUT[mixer]
        if i == L - 1:
            h = h[:seq]
        Ti = h.shape[0]
        tm = _tile(Ti, 768)
        tmh = _tile(Ti, 384)
        mod = mod_all[i]
        mp = mixer_params(i)
        s = dict(h0=h, mp=mp)
        hm = nm_fwd(f"nm1_{i}", h, W['g_mix'][i:i + 1], mod, 0, 1, seq)
        s['hm'] = hm
        u = fwd(mm_cols, f"in_{i}", hm, Wg[(n_in, j)], BF16, tm)
        if mixer == 0:
            p = gmlp_fwd(f"a_mid_{i}", u, mp['ln_g'], mp['ln_b'], mp['ws'], mp['bfull'])
        elif mixer == 1:
            p = jnp.concatenate([fwd(na_fwd, f"b_na_{i}", u, bias, seq), ctx_attn_fwd(f"b_ctx_{i}", u, seq)], axis=0)
        else:
            y, yc, p = conv_fwd(f"c_mid_{i}", u, mp['wdw'], mp['bdw'], mp['ln_g'], mp['ln_b'], seq)
            s.update(y=y, yc=yc)
        s.update(u=u, p=p)
        m1, h = fwd(mm_rows_residual, f"out_{i}", p, Wg[(n_out, j)], h, mod, 2, seq, tm)
        s.update(m1=m1, h1=h)
        hf = nm_fwd(f"nm2_{i}", h, W['g_ffn'][i:i + 1], mod, 3, 4, seq)
        a, b, act = fwd(mm_ffn_up, f"ffn_up_{i}", hf, Wg[('ffn_w1', i)], Wg[('ffn_w3', i)], tmh)
        m2, h = fwd(mm_rows_residual, f"ffn_down_{i}", act, Wg[('ffn_w2', i)], h, mod, 5, seq, tm)
        s.update(hf=hf, a=a, b=b, act=act, m2=m2)
        saved.append(s)

    dh, st_loss = loss_head("loss_head", h, loss_target[0], W['g_final'].reshape(1, D))

    dmod_lat, dmod_ctx, dmod_tot = [None] * L, [None] * L, [None] * L
    dg_mix, dg_ffn = [None] * L, [None] * L
    small = {}
    gate2_done = None

    def set_dmod(i, st_n1, st_g1, st_n2, st_g2):
        for dst, r_n, r_g in ((dmod_lat, (1, 2), 0), (dmod_ctx, (3, 4), 1), (dmod_tot, (5, 6), 2)):
            dst[i] = jnp.concatenate([st_n1[r_n[0]:r_n[0] + 1], st_n1[r_n[1]:r_n[1] + 1], st_g1[r_g:r_g + 1],
                                      st_n2[r_n[0]:r_n[0] + 1], st_n2[r_n[1]:r_n[1] + 1], st_g2[r_g:r_g + 1]], axis=0)

    packs = {}

    def small_pack(part):
        layers = [0] if part == 'layer0' else list(range(1, L))
        a_parts = [small[('a', j)] for j in range(n_a) if (3 * j in layers)]
        cat = lambda xs: jnp.concatenate(xs, axis=0)
        entries = [('dmod_lat', cat([dmod_lat[i] for i in layers])), ('dmod_ctx', cat([dmod_ctx[i] for i in layers])),
                   ('ada_b', cat([dmod_tot[i] for i in layers])),
                   ('g_mix', cat([dg_mix[i] for i in layers])), ('g_ffn', cat([dg_ffn[i] for i in layers]))]
        if a_parts:
            entries += [('a_ln_g', cat([p[0] for p in a_parts])), ('a_ln_b', cat([p[1] for p in a_parts])),
                        ('a_w_s', cat([p[2] for p in a_parts])), ('a_b_s', cat([p[3] for p in a_parts]))]
        if part == 'rest':
            c_parts = [small[('c', j)] for j in range(n_c)]
            entries += [('b_rpb', small[('b', 0)]),
                        ('c_w_dw', cat([p[0] for p in c_parts])), ('c_b_dw', cat([p[1] for p in c_parts])),
                        ('c_ln_g', cat([p[2] for p in c_parts])), ('c_ln_b', cat([p[3] for p in c_parts])),
                        ('g_final', st_loss[0:1]), ('loss', st_loss[1:2])]
        offsets, o = {}, 0
        for n, arr in entries:
            offsets[n] = (o, arr.shape[0])
            o += arr.shape[0]
        return _pad_rows(cat([arr for _, arr in entries]), 64), offsets

    def bwd_gather(fn, host, part, *args):
        pack, offsets = small_pack(part)
        res, touts = bwd(fn, host, *args, extra=[GatherAllTask(pack)])
        packs[part] = (touts[0][0], offsets)
        return res

    for i in reversed(range(L)):
        mixer, j = i % 3, i // 3
        n_in, n_out = MIXER_IN[mixer], MIXER_OUT[mixer]
        s = saved[i]
        mp = s['mp']
        mod = mod_all[i]
        if i == L - 2:
            dh = jnp.concatenate([dh, jnp.zeros((ctx_rows, D), F32)], axis=0)
        Ti = dh.shape[0]
        tm = _tile(Ti, 768)
        tmh = _tile(Ti, 384)
        dm2, st_g2 = gate_bwd(f"gate2_bwd_{i}", dh, s['m2'], mod, 5, seq) if gate2_done is None else gate2_done
        da, db = bwd(mm_rows_dgrad, f"ffn_down_dx_{i}", dm2, Wg[('ffn_w2', i)], BF16, tm, ffn_ab=(s['a'], s['b']))
        dW[('ffn_w2', i)] = bwd(mm_wgrad_rows, f"ffn_w2_dw_{i}", s['act'], dm2, tm)
        dW[('ffn_w1', i)] = bwd(mm_wgrad_cols, f"ffn_w1_dw_{i}", s['hf'], da, tm)
        dW[('ffn_w3', i)] = bwd(mm_wgrad_cols, f"ffn_w3_dw_{i}", s['hf'], db, tm)
        ffn_dx_args = (f"ffn_up_dx_{i}", [da, db], [Wg[('ffn_w1', i)], Wg[('ffn_w3', i)]], tm)
        dhf = bwd_gather(mm_cols_dgrad, ffn_dx_args[0], 'rest', *ffn_dx_args[1:]) if i == 0 else \
            bwd(mm_cols_dgrad, *ffn_dx_args)
        dh, st_n2, dm1, st_g1 = nm_bwd(f"nm2_bwd_{i}", s['h1'], dhf, dh, W['g_ffn'][i:i + 1], mod, 4, seq,
                                       (s['m1'], mod, 2))
        dp = bwd(mm_rows_dgrad, f"out_dx_{i}", dm1, Wg[(n_out, j)], F32 if mixer == 2 else BF16, tm)[0]
        dW[(n_out, j)] = bwd(mm_wgrad_rows, f"out_dw_{i}", s['p'], dm1, tm)
        if mixer == 0:
            du, dws, dbs, st_a = gmlp_bwd(f"a_mid_bwd_{i}", s['u'], dp, mp['ln_g'], mp['ln_b'], mp['ws'], mp['bfull'])
            small[('a', j)] = (st_a[0:1], st_a[1:2], dws.reshape(-1, D), dbs.T.reshape(1, D))
        elif mixer == 1:
            dq, dk, dv, dkc, dvc, dbias = na_bwd(f"b_na_bwd_{i}", s['u'], bias, dp, seq)
            dqc, dkc, dvc = ctx_attn_bwd(f"b_ctx_bwd_{i}", s['u'], dp, dkc, dvc, seq)
            du = jnp.concatenate([jnp.concatenate([dq, dk.astype(BF16), dv.astype(BF16)], axis=1),
                                  jnp.concatenate([dqc, dkc, dvc], axis=1)], axis=0)
            blocks = dbias.reshape(3, H, Q_ROWS, GRID_W, K_ROWS, GRID_W).transpose(1, 0, 2, 4, 3, 5)
            folded = rpb_fold(f"rpb_fold_{i}", blocks, na_classes)
            drpb = rpb_reduce(f"rpb_reduce_{i}", folded.reshape(H * n_dr, GRID_W * GRID_W), onehot.T)
            small[('b', j)] = _rows_of(drpb[:, :2 * NA_COLS - 1].reshape(-1), D)
        else:
            dyc, st_c = conv_bwd_norm(f"c_norm_bwd_{i}", dp, s['yc'], mp['ln_g'], mp['ln_b'])
            du, dwdw = conv_bwd_taps(f"c_taps_bwd_{i}", dyc, s['y'], s['u'], mp['wdw'], seq)
            small[('c', j)] = (dwdw, st_c[2:3], st_c[0:1], st_c[1:2])
        if i == 0:
            dg_mix[0], dg_ffn[0] = jnp.zeros((1, D), F32), st_n2[0:1]
            set_dmod(0, jnp.zeros((8, D), F32), st_g1, st_n2, st_g2)
            dW[(n_in, j)] = bwd_gather(mm_wgrad_cols, f"in_dw_{i}", 'layer0', s['hm'], du, tm)
        else:
            dW[(n_in, j)] = bwd(mm_wgrad_cols, f"in_dw_{i}", s['hm'], du, tm)
        dhm = bwd(mm_cols_dgrad, f"in_dx_{i}", [du], [Wg[(n_in, j)]], tm)
        below = 1 <= i <= L - 2
        gate_below = (saved[i - 1]['m2'], mod_all[i - 1], 5) if below else (s['m1'], mod, 2)
        dh, st_n1, dm_below, st_below = nm_bwd(f"nm1_bwd_{i}", s['h0'], dhm, dh, W['g_mix'][i:i + 1], mod, 1, seq,
                                               gate_below)
        gate2_done = (dm_below, st_below) if below else None
        if i > 0:
            dg_mix[i], dg_ffn[i] = st_n1[0:1], st_n2[0:1]
            set_dmod(i, st_n1, st_g1, st_n2, st_g2)
    grad_x = dh[:seq].reshape(1, seq, D)

    late = _pad_rows(jnp.concatenate([st_n1[1:7], st_n1[0:1]], axis=0), 8)
    touts = comm_only("ag_small_late", [GatherAllTask(late)] + scatter_tasks("rs_post"))
    gathered_late = touts[0][0]
    scattered("rs_post", touts[1:])
    sums = {part: reduce_8("reduce_small_" + part, packs[part][0]) for part in ('layer0', 'rest')}
    sums_late = reduce_8("reduce_small_late", gathered_late)
    late_rows = {'dmod_ctx': sums_late[2:4], 'ada_b': sums_late[4:6], 'g_mix': sums_late[6:7]}
    out = {}

    def rows_of(n, arrays, axis):
        found = []
        for part in ('layer0', 'rest'):
            if n in packs[part][1]:
                lo, cnt = packs[part][1][n]
                found.append(lax.slice_in_dim(arrays[part], lo, lo + cnt, axis=axis))
        return jnp.concatenate(found, axis=axis)

    def summed(n):
        rows = rows_of(n, sums, 0)
        if n in late_rows:
            rows = jnp.concatenate([late_rows[n], rows[late_rows[n].shape[0]:]], axis=0)
        return rows

    loss = (0.5 / D) * jnp.sum(summed('loss'))

    dm_lat = rows_of('dmod_lat', {part: packs[part][0] for part in packs}, 1)
    dm_lat = jnp.concatenate([gathered_late[:, 0:2], dm_lat[:, 2:]], axis=1)
    dm_lat = dm_lat.reshape(8, L, 6 * D).transpose(1, 0, 2)
    dm_ctx = summed('dmod_ctx').reshape(L, 1, 6 * D)
    dm16 = jnp.concatenate([dm_lat, dm_ctx, jnp.zeros((L, 7, 6 * D), F32)], axis=1)
    dm16 = lax.dynamic_slice_in_dim(dm16, s_idx * Cs, Cs, axis=2)
    dmc8 = jnp.concatenate([dm16[:, 8:9], jnp.zeros((L, 7, Cs), F32)], axis=1)
    g_ada, d_ada, m_ada, v_ada, ds_part = ada_bwd_adam("ada_bwd_adam", craw16.T, dm16, dmc8, ada_w,
                                                       Mo['ada_w'], Vo['ada_w'])
    ds_all = all_gather_8("ag_ds_ctx", ds_part)
    cc = cctx_adam("cctx_adam", ds_all, W['c_ctx'].reshape(1, D), Mo['c_ctx'].reshape(1, D),
                   Vo['c_ctx'].reshape(1, D))
    out.update({'c_ctx': tuple(t.reshape(D) for t in cc), 'ada_w': (g_ada, d_ada, m_ada, v_ada)})

    for n in BIG_NAMES:
        out[n], _ = adam_parts("adam_" + n, land[n], land2[n], W[n], Mo[n], Vo[n])

    def own_cols(full):
        w = full.shape[-1] // N_SHARD
        return lax.dynamic_slice_in_dim(full, s_idx * w, w, axis=full.ndim - 1)

    small_g = {
        'ada_b': summed('ada_b').reshape(L, 6 * D), 'g_mix': summed('g_mix'), 'g_ffn': summed('g_ffn'),
        'a_ln_g': own_cols(summed('a_ln_g')), 'a_ln_b': own_cols(summed('a_ln_b')),
        'a_w_s': summed('a_w_s').reshape(n_a, G, CHUNK, CHUNK), 'a_b_s': summed('a_b_s').reshape(n_a, G, CHUNK),
        'b_rpb': summed('b_rpb').reshape(-1)[:H * n_dr * (2 * NA_COLS - 1)].reshape(W['b_rpb'].shape),
        'c_w_dw': own_cols(summed('c_w_dw').reshape(n_c, 32, D)[:, :CONV_W]),
        'c_b_dw': own_cols(summed('c_b_dw')), 'c_ln_g': own_cols(summed('c_ln_g')),
        'c_ln_b': own_cols(summed('c_ln_b')), 'g_final': summed('g_final').reshape(D),
    }

    def packed(d):
        flat = jnp.concatenate([d[n].reshape(-1) for n in SMALL_NAMES])
        return _pad_rows(_rows_of(flat, SMALL_PACK_COLS), 8)

    res = adam_flat("adam_small", packed(small_g), packed(W), packed(Mo), packed(Vo))
    o = 0
    for n in SMALL_NAMES:
        size, shape = W[n].size, W[n].shape
        out[n] = (small_g[n],) + tuple(r.reshape(-1)[o:o + size].reshape(shape) for r in res)
        o += size

    return (loss, grad_x) + tuple(out[n][k] for k in range(4) for n in WEIGHT_NAMES)


def kernel(x, c, ctx, c_ctx, ada_w, ada_b, g_mix, g_ffn, ffn_w1, ffn_w3, ffn_w2, a_w_in, a_ln_g, a_ln_b, a_w_s, a_b_s, a_w_out, b_w_qkv, b_rpb, b_w_out, c_w_pw1, c_w_dw, c_b_dw, c_ln_g, c_ln_b, c_w_pw2, g_final, loss_target, m_c_ctx, m_ada_w, m_ada_b, m_g_mix, m_g_ffn, m_ffn_w1, m_ffn_w3, m_ffn_w2, m_a_w_in, m_a_ln_g, m_a_ln_b, m_a_w_s, m_a_b_s, m_a_w_out, m_b_w_qkv, m_b_rpb, m_b_w_out, m_c_w_pw1, m_c_w_dw, m_c_b_dw, m_c_ln_g, m_c_ln_b, m_c_w_pw2, m_g_final, v_c_ctx, v_ada_w, v_ada_b, v_g_mix, v_g_ffn, v_ffn_w1, v_ffn_w3, v_ffn_w2, v_a_w_in, v_a_ln_g, v_a_ln_b, v_a_w_s, v_a_b_s, v_a_w_out, v_b_w_qkv, v_b_rpb, v_b_w_out, v_c_w_pw1, v_c_w_dw, v_c_b_dw, v_c_ln_g, v_c_ln_b, v_c_w_pw2, v_g_final):
    W = dict(zip(WEIGHT_NAMES, (c_ctx, ada_w, ada_b, g_mix, g_ffn, ffn_w1, ffn_w3, ffn_w2, a_w_in, a_ln_g, a_ln_b, a_w_s, a_b_s, a_w_out, b_w_qkv, b_rpb, b_w_out, c_w_pw1, c_w_dw, c_b_dw, c_ln_g, c_ln_b, c_w_pw2, g_final)))
    Mo = dict(zip(WEIGHT_NAMES, (m_c_ctx, m_ada_w, m_ada_b, m_g_mix, m_g_ffn, m_ffn_w1, m_ffn_w3, m_ffn_w2, m_a_w_in, m_a_ln_g, m_a_ln_b, m_a_w_s, m_a_b_s, m_a_w_out, m_b_w_qkv, m_b_rpb, m_b_w_out, m_c_w_pw1, m_c_w_dw, m_c_b_dw, m_c_ln_g, m_c_ln_b, m_c_w_pw2, m_g_final)))
    Vo = dict(zip(WEIGHT_NAMES, (v_c_ctx, v_ada_w, v_ada_b, v_g_mix, v_g_ffn, v_ffn_w1, v_ffn_w3, v_ffn_w2, v_a_w_in, v_a_ln_g, v_a_ln_b, v_a_w_s, v_a_b_s, v_a_w_out, v_b_w_qkv, v_b_rpb, v_b_w_out, v_c_w_pw1, v_c_w_dw, v_c_b_dw, v_c_ln_g, v_c_ln_b, v_c_w_pw2, v_g_final)))
    return _step(W, Mo, Vo, x, c, ctx, loss_target)
```
